```python
import numpy as np
import jax
import jax.numpy as jnp
from jax import lax

D_MODEL = 1024
BATCH = 4
SEQ = 8192
DEPTH = 2

GRID_W = 64
CTX_LEN = 256
HEAD_DIM = 64
ATTN_HEADS = 8
ATTN_KV_HEADS = 2
GROUP = ATTN_HEADS // ATTN_KV_HEADS
ATTN_DIM = ATTN_HEADS * HEAD_DIM
KV_DIM = ATTN_KV_HEADS * HEAD_DIM
RW_HEADS = 8
RW_DIM = RW_HEADS * HEAD_DIM
D_MIX = ATTN_DIM + RW_DIM
W_LORA = 64
A_LORA = 64
G_LORA = 128
CONV_W = 3
N_EXPERTS = 32
TOP_K = 4
D_EXPERT = D_MODEL
SWIGLU_LIMIT = 7.0
SWIGLU_ALPHA = 1.702
ROPE_THETA = 10000.0
ROPE_PAIRS = HEAD_DIM // 4
Q_BLOCK = 128
MOE_BLOCK = 256
NORM_EPS = 1e-6
GN_EPS = 64e-5
ATTN_SCALE = HEAD_DIM ** -0.5
SPLITS = (KV_DIM, KV_DIM, RW_DIM, RW_DIM, 2 * W_LORA, 2 * A_LORA, ATTN_DIM, RW_DIM, G_LORA)
STATE_COLS = 2 * KV_DIM + 2 * RW_DIM + 2 * W_LORA + 2 * A_LORA
IN_COLS = STATE_COLS + ATTN_DIM + RW_DIM + G_LORA

kernel_name = 'hybrid_gqa_rwkv7_moe_dit'


def rms_norm(x, g):
    xf = x.astype(jnp.float32)
    y = xf * lax.rsqrt(jnp.mean(xf * xf, axis=-1, keepdims=True) + NORM_EPS)
    return (y * g.astype(jnp.float32)).astype(x.dtype)


def heads(x, n):
    return x.reshape(x.shape[:-1] + (n, x.shape[-1] // n))


def split_cols(p, sizes):
    return jnp.split(p, np.cumsum(sizes)[:-1].tolist(), axis=-1)


def adaln(cvec, w_mod, b_mod):
    return jnp.split(jax.nn.silu(cvec) @ w_mod + b_mod, 6, axis=-1)


def modulate(h, shift, scale):
    return h * (1 + scale) + shift


def short_conv(u, w):
    T = u.shape[1]
    pad = CONV_W // 2
    up = jnp.pad(u, ((0, 0), (pad, pad), (0, 0)))
    return sum(up[:, i:i + T] * w[i] for i in range(CONV_W))


def axial_rope(rows):
    row = jnp.repeat(jnp.arange(rows), GRID_W)
    col = jnp.tile(jnp.arange(GRID_W), rows)
    pos = jnp.stack([row, col], axis=-1).astype(jnp.float32)
    freqs = ROPE_THETA ** (-jnp.arange(ROPE_PAIRS, dtype=jnp.float32) / ROPE_PAIRS)
    ang = pos[:, :, None] * freqs
    return jnp.cos(ang), jnp.sin(ang)


def apply_rope(x, cos, sin):
    B, T, H, Dh = x.shape
    xa = x.reshape(B, T, H, 2, 2, ROPE_PAIRS).astype(jnp.float32)
    x1, x2 = xa[..., 0, :], xa[..., 1, :]
    c, s = cos[None, :, None], sin[None, :, None]
    out = jnp.stack([x1 * c - x2 * s, x2 * c + x1 * s], axis=-2)
    return out.reshape(B, T, H, Dh).astype(x.dtype)


def gqa(q, k, v):
    B, Q = q.shape[:2]
    qg = q.reshape(B, Q, ATTN_KV_HEADS, GROUP, HEAD_DIM)
    s = jnp.einsum('bqhgd,bshd->bhgqs', qg, k, preferred_element_type=jnp.float32) * ATTN_SCALE
    p = jax.nn.softmax(s, axis=-1).astype(v.dtype)
    o = jnp.einsum('bhgqs,bshd->bqhgd', p, v)
    return o.reshape(B, Q, ATTN_DIM)


def latent_attention(q, k, v, kc, vc):
    B, T = q.shape[:2]
    k_all = jnp.concatenate([kc, k], axis=1)
    v_all = jnp.concatenate([vc, v], axis=1)
    nb = T // Q_BLOCK
    qb = jnp.moveaxis(q.reshape(B, nb, Q_BLOCK, ATTN_HEADS, HEAD_DIM), 1, 0)
    o = lax.map(lambda qi: gqa(qi, k_all, v_all), qb)
    return jnp.moveaxis(o, 0, 1).reshape(B, T, ATTN_DIM)


def rwkv_state_inputs(k_raw, v_raw, xw, xa, conv_w, w0, w_up, a0, a_up, k_k, k_a):
    B, T, _ = k_raw.shape
    k = short_conv(k_raw, conv_w[:, RW_DIM:2 * RW_DIM])
    v = short_conv(v_raw, conv_w[:, 2 * RW_DIM:])
    wl = w0 + jnp.einsum('btdl,dlc->btdc', jnp.tanh(xw.reshape(B, T, 2, W_LORA)), w_up)
    decay = jnp.exp(-jnp.exp(-jax.nn.softplus(-wl.astype(jnp.float32)) - 0.5))
    a = jax.nn.sigmoid(a0 + jnp.einsum('btdl,dlc->btdc', xa.reshape(B, T, 2, A_LORA), a_up))
    kk = heads(k * k_k, RW_HEADS).astype(jnp.float32)
    kk = kk * lax.rsqrt(jnp.maximum(jnp.sum(kk * kk, axis=-1, keepdims=True), 1e-24))
    k_dir = k[:, :, None, :] * (1 + (a - 1) * k_a)
    return (heads(k_dir, RW_HEADS), heads(v, RW_HEADS), heads(decay, RW_HEADS), heads(a, RW_HEADS), kk)


def wkv_scan(w, k, v, a_op, b_op, s0, r, reverse):
    xs = (w, k, v, a_op, b_op) + (() if r is None else (r,))
    xs = tuple(jnp.moveaxis(t.astype(jnp.float32), 1, 0) for t in xs)

    def step(S, xt):
        w_t, k_t, v_t, a_t, b_t = xt[:5]
        S = (S * w_t[:, :, None, :]
             + jnp.einsum('bhij,bhj->bhi', S, a_t)[..., None] * b_t[:, :, None, :]
             + v_t[..., None] * k_t[:, :, None, :])
        out = None if r is None else jnp.einsum('bhij,bhj->bhi', S, xt[5])
        return S, out

    S, out = lax.scan(step, s0, xs, reverse=reverse)
    return S, (None if r is None else jnp.moveaxis(out, 0, 1))


def scan_dir(st, d, s0, r, reverse):
    k_dir, v, decay, a, kk = st
    return wkv_scan(decay[:, :, d], k_dir[:, :, d], v, -kk, kk * a[:, :, d], s0, r, reverse)


def rwkv_readout(o_f, o_b, r, st, xg, g_up, r_k, ln_x_g, ln_x_b):
    k_dir, v = st[0], st[1]
    B, T = v.shape[:2]
    o = o_f + o_b
    mu = jnp.mean(o, axis=-1, keepdims=True)
    var = jnp.mean(jnp.square(o - mu), axis=-1, keepdims=True)
    gn = ((o - mu) * lax.rsqrt(var + GN_EPS)).reshape(B, T, RW_DIM) * ln_x_g + ln_x_b
    k_bonus = jnp.mean(k_dir.astype(jnp.float32), axis=2)
    bonus = jnp.sum(r.astype(jnp.float32) * k_bonus * r_k, axis=-1, keepdims=True) * v.astype(jnp.float32)
    g = jax.nn.sigmoid(xg) @ g_up
    return ((gn + bonus.reshape(B, T, RW_DIM)) * g).astype(v.dtype)


def token_mixer(n_lat, n_ctx, w_in, w_out, q_norm_g, k_norm_g, conv_w, w0, w_up, a0, a_up, g_up,
                k_k, k_a, r_k, ln_x_g, ln_x_b, cos, sin, last):
    B = n_lat.shape[0]
    k_at, v_at, k_rw, v_rw, xw, xa, q_at, r_rw, xg = split_cols(n_lat @ w_in, SPLITS)
    if last:
        ctx_parts = split_cols(n_ctx @ w_in[:, :STATE_COLS], SPLITS[:6])
    else:
        ctx_parts = split_cols(n_ctx @ w_in, SPLITS)
    ck_at, cv_at, ck_rw, cv_rw, cxw, cxa = ctx_parts[:6]

    q = apply_rope(rms_norm(heads(q_at, ATTN_HEADS), q_norm_g), cos, sin)
    k = apply_rope(rms_norm(heads(k_at, ATTN_KV_HEADS), k_norm_g), cos, sin)
    kc = rms_norm(heads(ck_at, ATTN_KV_HEADS), k_norm_g)
    vc = heads(cv_at, ATTN_KV_HEADS)
    o_at = latent_attention(q, k, heads(v_at, ATTN_KV_HEADS), kc, vc)

    rw = (conv_w, w0, w_up, a0, a_up, k_k, k_a)
    st_c = rwkv_state_inputs(ck_rw, cv_rw, cxw, cxa, *rw)
    st_l = rwkv_state_inputs(k_rw, v_rw, xw, xa, *rw)
    r = heads(short_conv(r_rw, conv_w[:, :RW_DIM]), RW_HEADS)
    r_c = None if last else heads(short_conv(ctx_parts[7], conv_w[:, :RW_DIM]), RW_HEADS)
    s0 = jnp.zeros((B, RW_HEADS, HEAD_DIM, HEAD_DIM), jnp.float32)
    s_cf, o_cf = scan_dir(st_c, 0, s0, r_c, False)
    s_cb, o_cb = scan_dir(st_c, 1, s0, r_c, True)
    _, o_lf = scan_dir(st_l, 0, s_cf, r, False)
    _, o_lb = scan_dir(st_l, 1, s_cb, r, True)
    o_rw = rwkv_readout(o_lf, o_lb, r, st_l, xg, g_up, r_k, ln_x_g, ln_x_b)

    y_lat = jnp.concatenate([o_at, o_rw.astype(o_at.dtype)], axis=-1) @ w_out
    if last:
        return y_lat, None
    q_c = rms_norm(heads(ctx_parts[6], ATTN_HEADS), q_norm_g)
    o_at_c = gqa(q_c, kc, vc)
    o_rw_c = rwkv_readout(o_cf, o_cb, r_c, st_c, ctx_parts[8], g_up, r_k, ln_x_g, ln_x_b)
    y_ctx = jnp.concatenate([o_at_c, o_rw_c.astype(o_at_c.dtype)], axis=-1) @ w_out
    return y_lat, y_ctx


def moe_ffn(h, router_w, router_b, e_w1, e_b1, e_w2, e_b2):
    N, D = h.shape
    logits = (h @ router_w + router_b).astype(jnp.float32)
    top_v, top_e = lax.top_k(logits, TOP_K)
    gates = jax.nn.softmax(top_v, axis=-1)
    flat_e = top_e.reshape(-1)
    flat_tok = jnp.arange(N * TOP_K, dtype=jnp.int32) // TOP_K
    flat_g = gates.reshape(-1)
    order = jnp.argsort(flat_e)
    e_sorted = flat_e[order]
    counts = jnp.bincount(flat_e, length=N_EXPERTS)
    padded = (counts + MOE_BLOCK - 1) // MOE_BLOCK * MOE_BLOCK
    start = jnp.cumsum(counts) - counts
    pstart = jnp.cumsum(padded) - padded
    dest = pstart[e_sorted] + jnp.arange(N * TOP_K, dtype=jnp.int32) - start[e_sorted]
    n_blocks = (N * TOP_K + MOE_BLOCK - 1) // MOE_BLOCK + N_EXPERTS
    rows = n_blocks * MOE_BLOCK
    slot_tok = jnp.full((rows,), N, jnp.int32).at[dest].set(flat_tok[order])
    slot_gate = jnp.zeros((rows,), jnp.float32).at[dest].set(flat_g[order])
    block_e = jnp.minimum(jnp.searchsorted(jnp.cumsum(padded), jnp.arange(n_blocks) * MOE_BLOCK, side='right'),
                          N_EXPERTS - 1)
    h_pad = jnp.concatenate([h, jnp.zeros((1, D), h.dtype)], axis=0)

    def run_block(args):
        toks, e = args
        u = h_pad[toks] @ e_w1[e] + e_b1[e]
        x_glu, x_lin = jnp.split(u, 2, axis=-1)
        x_glu = jnp.minimum(x_glu, SWIGLU_LIMIT)
        x_lin = jnp.clip(x_lin, -SWIGLU_LIMIT, SWIGLU_LIMIT)
        act = x_glu * jax.nn.sigmoid(SWIGLU_ALPHA * x_glu) * (x_lin + 1)
        return act @ e_w2[e] + e_b2[e]

    out = lax.map(run_block, (slot_tok.reshape(n_blocks, MOE_BLOCK), block_e)).reshape(rows, D)
    out = out * slot_gate[:, None].astype(out.dtype)
    y = jnp.zeros((N + 1, D), out.dtype).at[slot_tok].add(out)
    return y[:N]


def setup_inputs(seed: int = 0) -> dict:
    key = jax.random.key(seed)
    ks = jax.random.split(key, 30)
    L, D, E, F = DEPTH, D_MODEL, N_EXPERTS, D_EXPERT

    def nrm(k, shape, scale):
        return scale * jax.random.normal(k, shape, jnp.float32)

    return {
        'x': nrm(ks[0], (BATCH, SEQ, D), 1.0),
        'c': nrm(ks[1], (BATCH, D), 1.0),
        'ctx': nrm(ks[2], (BATCH, CTX_LEN, D), 1.0),
        'c_ctx': nrm(ks[3], (D,), 1.0),
        'norm_mix_g': 1.0 + nrm(ks[4], (L, D), 0.05),
        'norm_ffn_g': 1.0 + nrm(ks[5], (L, D), 0.05),
        'w_mod': nrm(ks[6], (L, D, 6 * D), 0.5 * D ** -0.5),
        'b_mod': nrm(ks[7], (L, 6 * D), 0.02),
        'w_in': nrm(ks[8], (L, D, IN_COLS), D ** -0.5),
        'w_out': nrm(ks[9], (L, D_MIX, D), D_MIX ** -0.5),
        'q_norm_g': 1.0 + nrm(ks[10], (L, HEAD_DIM), 0.05),
        'k_norm_g': 1.0 + nrm(ks[11], (L, HEAD_DIM), 0.05),
        'conv_w': nrm(ks[12], (L, CONV_W, 3 * RW_DIM), 0.1) + jax.nn.one_hot(CONV_W // 2, CONV_W)[None, :, None],
        'w0': jax.random.uniform(ks[13], (L, 2, RW_DIM), jnp.float32, -6.0, 1.0),
        'w_up': nrm(ks[14], (L, 2, W_LORA, RW_DIM), 0.5 * W_LORA ** -0.5),
        'a0': nrm(ks[15], (L, 2, RW_DIM), 0.5),
        'a_up': nrm(ks[16], (L, 2, A_LORA, RW_DIM), A_LORA ** -0.5),
        'g_up': nrm(ks[17], (L, G_LORA, RW_DIM), G_LORA ** -0.5),
        'k_k': 0.85 + nrm(ks[18], (L, RW_DIM), 0.05),
        'k_a': 1.0 + nrm(ks[19], (L, RW_DIM), 0.05),
        'r_k': nrm(ks[20], (L, RW_HEADS, HEAD_DIM), 0.1),
        'ln_x_g': 1.0 + nrm(ks[21], (L, RW_DIM), 0.05),
        'ln_x_b': nrm(ks[22], (L, RW_DIM), 0.02),
        'router_w': nrm(ks[23], (L, D, E), D ** -0.5),
        'router_b': nrm(ks[24], (L, E), 0.01),
        'e_w1': nrm(ks[25], (L, E, D, 2 * F), D ** -0.5),
        'e_b1': nrm(ks[26], (L, E, 2 * F), 0.01),
        'e_w2': nrm(ks[27], (L, E, F, D), F ** -0.5),
        'e_b2': nrm(ks[28], (L, E, D), 0.01),
        'norm_final_g': 1.0 + nrm(ks[29], (D,), 0.05),
    }


def reference(x, c, ctx, c_ctx, norm_mix_g, norm_ffn_g, w_mod, b_mod, w_in, w_out, q_norm_g, k_norm_g,
              conv_w, w0, w_up, a0, a_up, g_up, k_k, k_a, r_k, ln_x_g, ln_x_b,
              router_w, router_b, e_w1, e_b1, e_w2, e_b2, norm_final_g):
    B, T, D = x.shape
    C = ctx.shape[1]
    ROWS = T // GRID_W
    cos, sin = axial_rope(ROWS)
    x_lat, x_ctx = x, ctx
    for l in range(DEPTH):
        last = l == DEPTH - 1
        sh_m, sc_m, g_m, sh_f, sc_f, g_f = [p[:, None, :] for p in adaln(c, w_mod[l], b_mod[l])]
        csh_m, csc_m, cg_m, csh_f, csc_f, cg_f = adaln(c_ctx, w_mod[l], b_mod[l])
        y_lat, y_ctx = token_mixer(
            modulate(rms_norm(x_lat, norm_mix_g[l]), sh_m, sc_m),
            modulate(rms_norm(x_ctx, norm_mix_g[l]), csh_m, csc_m),
            w_in[l], w_out[l], q_norm_g[l], k_norm_g[l], conv_w[l], w0[l], w_up[l], a0[l], a_up[l], g_up[l],
            k_k[l], k_a[l], r_k[l], ln_x_g[l], ln_x_b[l], cos, sin, last)
        x_lat = x_lat + g_m * y_lat
        h_lat = modulate(rms_norm(x_lat, norm_ffn_g[l]), sh_f, sc_f).reshape(B * T, D)
        moe_w = (router_w[l], router_b[l], e_w1[l], e_b1[l], e_w2[l], e_b2[l])
        if last:
            x_lat = x_lat + g_f * moe_ffn(h_lat, *moe_w).reshape(B, T, D)
        else:
            x_ctx = x_ctx + cg_m * y_ctx
            h_ctx = modulate(rms_norm(x_ctx, norm_ffn_g[l]), csh_f, csc_f).reshape(B * C, D)
            f = moe_ffn(jnp.concatenate([h_lat, h_ctx], axis=0), *moe_w)
            x_lat = x_lat + g_f * f[:B * T].reshape(B, T, D)
            x_ctx = x_ctx + cg_f * f[B * T:].reshape(B, C, D)
    return rms_norm(x_lat, norm_final_g)
```

```python
import functools

import numpy as np
import jax
import jax.numpy as jnp
from jax import lax
from jax.experimental import pallas as pl
from jax.experimental.pallas import tpu as pltpu

F32 = jnp.float32
BF16 = jnp.bfloat16

GRID_W = 64
HEAD_DIM = 64
ATTN_HEADS = 8
ATTN_KV_HEADS = 2
GROUP = ATTN_HEADS // ATTN_KV_HEADS
ATTN_DIM = ATTN_HEADS * HEAD_DIM
KV_DIM = ATTN_KV_HEADS * HEAD_DIM
RW_HEADS = 8
RW_DIM = RW_HEADS * HEAD_DIM
W_LORA = 64
A_LORA = 64
G_LORA = 128
CONV_W = 3
N_EXPERTS = 32
TOP_K = 4
SWIGLU_LIMIT = 7.0
SWIGLU_ALPHA = 1.702
ROPE_THETA = 10000.0
ROPE_PAIRS = HEAD_DIM // 4
MOE_BLOCK = 256
NORM_EPS = 1e-6
GN_EPS = 64e-5
ATTN_SCALE = HEAD_DIM ** -0.5
SPLITS = (KV_DIM, KV_DIM, RW_DIM, RW_DIM, 2 * W_LORA, 2 * A_LORA, ATTN_DIM, RW_DIM, G_LORA)
STATE_COLS = 2 * KV_DIM + 2 * RW_DIM + 2 * W_LORA + 2 * A_LORA

LANES = 128
SUBLANES = 8
QUAD_LANES = 256
VMEM_LIMIT = 48 * 1024 * 1024


def _row_tile(n, pref):
    t = pref
    while n % t:
        t //= 2
    return t


def _adaln_kernel(c_ref, w_ref, b_ref, o_ref):
    c = c_ref[...]
    s = c * jax.nn.sigmoid(c)
    o_ref[...] = jnp.dot(s, w_ref[...], preferred_element_type=F32,
                         precision=lax.Precision.HIGHEST) + b_ref[...]


def adaln_all(cvecs, w_mod, b_mod):
    R, D = cvecs.shape
    N = w_mod.shape[1]
    tn = 512
    return pl.pallas_call(
        _adaln_kernel,
        out_shape=jax.ShapeDtypeStruct((R, N), F32),
        grid=(N // tn,),
        in_specs=[pl.BlockSpec((R, D), lambda j: (0, 0)),
                  pl.BlockSpec((D, tn), lambda j: (0, j)),
                  pl.BlockSpec((1, tn), lambda j: (0, j))],
        out_specs=pl.BlockSpec((R, tn), lambda j: (0, j)),
        compiler_params=pltpu.CompilerParams(dimension_semantics=("arbitrary",),
                                             vmem_limit_bytes=VMEM_LIMIT),
        name="adaln",
    )(cvecs, w_mod, b_mod.reshape(1, N))


def _norm_mod(x, g, sc, sh):
    ms = jnp.mean(x * x, axis=-1, keepdims=True)
    y = x * lax.rsqrt(ms + NORM_EPS) * g
    return y * (1.0 + sc) + sh


def _norm_mod_matmul_kernel(x_ref, g_ref, sc_ref, sh_ref, w_ref, o_ref):
    h = _norm_mod(x_ref[0], g_ref[...], sc_ref[0], sh_ref[0])
    o_ref[0] = jnp.dot(h.astype(BF16), w_ref[...], preferred_element_type=F32)


def norm_mod_matmul(x, g, sc, sh, w):
    B, T, D = x.shape
    N = w.shape[1]
    tm = _row_tile(T, 256)
    return pl.pallas_call(
        _norm_mod_matmul_kernel,
        out_shape=jax.ShapeDtypeStruct((B, T, N), F32),
        grid=(B, T // tm),
        in_specs=[pl.BlockSpec((1, tm, D), lambda b, i: (b, i, 0)),
                  pl.BlockSpec((1, D), lambda b, i: (0, 0)),
                  pl.BlockSpec((1, 1, D), lambda b, i: (b, 0, 0)),
                  pl.BlockSpec((1, 1, D), lambda b, i: (b, 0, 0)),
                  pl.BlockSpec((D, N), lambda b, i: (0, 0))],
        out_specs=pl.BlockSpec((1, tm, N), lambda b, i: (b, i, 0)),
        compiler_params=pltpu.CompilerParams(dimension_semantics=("arbitrary", "arbitrary"),
                                             vmem_limit_bytes=VMEM_LIMIT),
        name="norm_mod_matmul",
    )(x, g.reshape(1, D), sc, sh, w)


def _matmul_res_kernel(a_ref, w_ref, res_ref, gate_ref, o_ref):
    y = jnp.dot(a_ref[0].astype(BF16), w_ref[...], preferred_element_type=F32)
    o_ref[0] = res_ref[0] + gate_ref[0] * y


def matmul_res(a, w, res, gate):
    B, T, K = a.shape
    N = w.shape[1]
    tm = _row_tile(T, 256)
    return pl.pallas_call(
        _matmul_res_kernel,
        out_shape=jax.ShapeDtypeStruct((B, T, N), F32),
        grid=(B, T // tm),
        in_specs=[pl.BlockSpec((1, tm, K), lambda b, i: (b, i, 0)),
                  pl.BlockSpec((K, N), lambda b, i: (0, 0)),
                  pl.BlockSpec((1, tm, N), lambda b, i: (b, i, 0)),
                  pl.BlockSpec((1, 1, N), lambda b, i: (b, 0, 0))],
        out_specs=pl.BlockSpec((1, tm, N), lambda b, i: (b, i, 0)),
        compiler_params=pltpu.CompilerParams(dimension_semantics=("arbitrary", "arbitrary"),
                                             vmem_limit_bytes=VMEM_LIMIT),
        name="matmul_res",
    )(a, w, res, gate)


def _attn_kernel(q_ref, kt_ref, v_ref, o_ref, m_ref, l_ref, acc_ref, *, tq, tk, nk):
    q = q_ref[0].reshape(2 * tq, HEAD_DIM)
    m_ref[...] = jnp.full(m_ref.shape, -jnp.inf, F32)
    l_ref[...] = jnp.zeros(l_ref.shape, F32)
    acc_ref[...] = jnp.zeros(acc_ref.shape, F32)

    def body(j, carry):
        ks = pl.multiple_of(j * tk, tk)
        kt = kt_ref[0, 0, :, pl.ds(ks, tk)]
        vv = v_ref[0, 0, pl.ds(ks, tk), :]
        s = jnp.dot(q, kt, preferred_element_type=F32)
        m_old = m_ref[...]
        m_new = jnp.maximum(m_old, jnp.max(s, axis=-1, keepdims=True))
        alpha = jnp.exp(m_old - m_new)
        p = jnp.exp(s - m_new)
        l_ref[...] = alpha * l_ref[...] + jnp.sum(p, axis=-1, keepdims=True)
        acc_ref[...] = alpha * acc_ref[...] + jnp.dot(p.astype(BF16), vv, preferred_element_type=F32)
        m_ref[...] = m_new
        return carry

    lax.fori_loop(0, nk, body, 0)
    o = acc_ref[...] / l_ref[...]
    o_ref[0] = jnp.concatenate([o[:tq], o[tq:]], axis=-1)


def attention(q, kt, v):
    B, H, T, _ = q.shape
    S = kt.shape[-1]
    tq = _row_tile(T, 256)
    tk = 768 if S % 768 == 0 else _row_tile(S, 512)
    nk = S // tk
    pairs = H // 2
    pairs_per_kv = GROUP // 2
    kern = functools.partial(_attn_kernel, tq=tq, tk=tk, nk=nk)
    return pl.pallas_call(
        kern,
        out_shape=jax.ShapeDtypeStruct((B, T, H * HEAD_DIM), F32),
        grid=(B, pairs, T // tq),
        in_specs=[pl.BlockSpec((1, 2, tq, HEAD_DIM), lambda b, p, i: (b, p, i, 0)),
                  pl.BlockSpec((1, 1, HEAD_DIM, S), lambda b, p, i: (b, p // pairs_per_kv, 0, 0)),
                  pl.BlockSpec((1, 1, S, HEAD_DIM), lambda b, p, i: (b, p // pairs_per_kv, 0, 0))],
        out_specs=pl.BlockSpec((1, tq, 2 * HEAD_DIM), lambda b, p, i: (b, i, p)),
        scratch_shapes=[pltpu.VMEM((2 * tq, 1), F32), pltpu.VMEM((2 * tq, 1), F32),
                        pltpu.VMEM((2 * tq, HEAD_DIM), F32)],
        compiler_params=pltpu.CompilerParams(dimension_semantics=("arbitrary",) * 3,
                                             vmem_limit_bytes=VMEM_LIMIT),
        name="attention",
    )(q, kt, v)


def _wkv_kernel(w_ref, k_ref, v_ref, a_ref, b_ref, r_ref, o_ref, s_ref, *, tc, nquads):
    @pl.when(pl.program_id(1) == 0)
    def _():
        s_ref[...] = jnp.zeros(s_ref.shape, F32)

    rows_all = 2 * nquads * HEAD_DIM
    shp = (rows_all, QUAD_LANES)
    lane = lax.broadcasted_iota(jnp.int32, shp, 1)
    sub = lax.broadcasted_iota(jnp.int32, shp, 0)
    eye = (lane % HEAD_DIM == sub % HEAD_DIM).astype(F32)
    ka = lax.broadcasted_iota(jnp.int32, (QUAD_LANES, QUAD_LANES), 0) // HEAD_DIM
    kb = lax.broadcasted_iota(jnp.int32, (QUAD_LANES, QUAD_LANES), 1) // HEAD_DIM
    ones_bd = (ka == kb).astype(BF16)

    def seg_sum(x_bf16):
        return jnp.dot(x_bf16, ones_bd, preferred_element_type=F32)

    def bcast_rows(blks, j):
        parts = []
        for d in range(2):
            for q in range(nquads):
                row = blks[d][j:j + 1, q * QUAD_LANES:(q + 1) * QUAD_LANES]
                parts.append(jnp.broadcast_to(row, (HEAD_DIM, QUAD_LANES)))
        return jnp.concatenate(parts, axis=0)

    def group(g, carry):
        t0 = pl.multiple_of(g * SUBLANES, SUBLANES)
        blks = [[ref[d, 0, pl.ds(t0, SUBLANES), :] for d in range(2)]
                for ref in (w_ref, k_ref, v_ref, a_ref, b_ref, r_ref)]
        S = s_ref[...]
        out_rows = [[] for _ in range(2 * nquads)]
        for j in range(SUBLANES):
            wt, kt, vt, at, bt, rt = (bcast_rows(x, j) for x in blks)
            sa_in = S * at
            hi = sa_in.astype(BF16)
            lo = (sa_in - hi.astype(F32)).astype(BF16)
            sa = seg_sum(hi) + seg_sum(lo)
            vcol = seg_sum((eye * vt).astype(BF16))
            S = S * wt + sa * bt + vcol * kt
            ob = seg_sum((S * rt).astype(BF16)) * eye
            for c in range(2 * nquads):
                out_rows[c].append(jnp.sum(ob[c * HEAD_DIM:(c + 1) * HEAD_DIM], axis=0, keepdims=True))
        s_ref[...] = S
        for d in range(2):
            for q in range(nquads):
                o_ref[d, 0, pl.ds(t0, SUBLANES), q * QUAD_LANES:(q + 1) * QUAD_LANES] = (
                    jnp.concatenate(out_rows[d * nquads + q], axis=0))
        return carry

    lax.fori_loop(0, tc // SUBLANES, group, 0)


def wkv_scan(w, k, v, a, b, r):
    _, B, S, C = w.shape
    nquads = C // QUAD_LANES
    tc = _row_tile(S, 128)
    spec = pl.BlockSpec((2, 1, tc, C), lambda bb, t: (0, bb, t, 0))
    kern = functools.partial(_wkv_kernel, tc=tc, nquads=nquads)
    return pl.pallas_call(
        kern,
        out_shape=jax.ShapeDtypeStruct((2, B, S, C), F32),
        grid=(B, S // tc),
        in_specs=[spec] * 6,
        out_specs=spec,
        scratch_shapes=[pltpu.VMEM((2 * nquads * HEAD_DIM, QUAD_LANES), F32)],
        compiler_params=pltpu.CompilerParams(dimension_semantics=("arbitrary", "arbitrary"),
                                             vmem_limit_bytes=VMEM_LIMIT),
        name="wkv_scan",
    )(w, k, v, a, b, r)


def _moe_kernel(be_ref, hs_ref, w1_ref, b1_ref, w2_ref, b2_ref, g_ref, o_ref, *, f):
    u = jnp.dot(hs_ref[...], w1_ref[0], preferred_element_type=F32) + b1_ref[0]
    x_glu = jnp.minimum(u[:, :f], SWIGLU_LIMIT)
    x_lin = jnp.clip(u[:, f:], -SWIGLU_LIMIT, SWIGLU_LIMIT)
    act = x_glu * jax.nn.sigmoid(SWIGLU_ALPHA * x_glu) * (x_lin + 1.0)
    y = jnp.dot(act.astype(BF16), w2_ref[0], preferred_element_type=F32) + b2_ref[0]
    o_ref[...] = y * g_ref[...]


def moe_blocks(block_e, hs, w1, b1, w2, b2, gate):
    rows, D = hs.shape
    E, _, F2 = w1.shape
    f = F2 // 2
    tm = MOE_BLOCK
    nb = rows // tm
    kern = functools.partial(_moe_kernel, f=f)
    grid_spec = pltpu.PrefetchScalarGridSpec(
        num_scalar_prefetch=1,
        grid=(nb,),
        in_specs=[pl.BlockSpec((tm, D), lambda i, be: (i, 0)),
                  pl.BlockSpec((1, D, F2), lambda i, be: (be[i], 0, 0)),
                  pl.BlockSpec((1, 1, F2), lambda i, be: (be[i], 0, 0)),
                  pl.BlockSpec((1, f, D), lambda i, be: (be[i], 0, 0)),
                  pl.BlockSpec((1, 1, D), lambda i, be: (be[i], 0, 0)),
                  pl.BlockSpec((tm, 1), lambda i, be: (i, 0))],
        out_specs=pl.BlockSpec((tm, D), lambda i, be: (i, 0)),
    )
    return pl.pallas_call(
        kern,
        out_shape=jax.ShapeDtypeStruct((rows, D), F32),
        grid_spec=grid_spec,
        compiler_params=pltpu.CompilerParams(dimension_semantics=("arbitrary",),
                                             vmem_limit_bytes=VMEM_LIMIT),
        name="moe_blocks",
    )(block_e, hs, w1, b1.reshape(E, 1, F2), w2, b2.reshape(E, 1, D), gate)


def _ffn_pre_kernel(x_ref, g_ref, sc_ref, sh_ref, rw_ref, rb_ref, h_ref, lg_ref):
    h = _norm_mod(x_ref[0], g_ref[...], sc_ref[0], sh_ref[0])
    h_ref[0] = h.astype(BF16)
    lg_ref[0] = jnp.dot(h, rw_ref[...], preferred_element_type=F32,
                        precision=lax.Precision.HIGHEST) + rb_ref[...]


def ffn_pre(x, g, sc, sh, router_w, router_b):
    B, T, D = x.shape
    E = router_w.shape[1]
    tm = _row_tile(T, 256)
    return pl.pallas_call(
        _ffn_pre_kernel,
        out_shape=(jax.ShapeDtypeStruct((B, T, D), BF16), jax.ShapeDtypeStruct((B, T, E), F32)),
        grid=(B, T // tm),
        in_specs=[pl.BlockSpec((1, tm, D), lambda b, i: (b, i, 0)),
                  pl.BlockSpec((1, D), lambda b, i: (0, 0)),
                  pl.BlockSpec((1, 1, D), lambda b, i: (b, 0, 0)),
                  pl.BlockSpec((1, 1, D), lambda b, i: (b, 0, 0)),
                  pl.BlockSpec((D, E), lambda b, i: (0, 0)),
                  pl.BlockSpec((1, E), lambda b, i: (0, 0))],
        out_specs=(pl.BlockSpec((1, tm, D), lambda b, i: (b, i, 0)),
                   pl.BlockSpec((1, tm, E), lambda b, i: (b, i, 0))),
        compiler_params=pltpu.CompilerParams(dimension_semantics=("arbitrary", "arbitrary"),
                                             vmem_limit_bytes=VMEM_LIMIT),
        name="ffn_pre",
    )(x, g.reshape(1, D), sc, sh, router_w, router_b.reshape(1, E))


def _rms_kernel(x_ref, g_ref, o_ref):
    x = x_ref[0]
    ms = jnp.mean(x * x, axis=-1, keepdims=True)
    o_ref[0] = x * lax.rsqrt(ms + NORM_EPS) * g_ref[...]


def final_norm(x, g):
    B, T, D = x.shape
    tm = _row_tile(T, 512)
    return pl.pallas_call(
        _rms_kernel,
        out_shape=jax.ShapeDtypeStruct((B, T, D), F32),
        grid=(B, T // tm),
        in_specs=[pl.BlockSpec((1, tm, D), lambda b, i: (b, i, 0)),
                  pl.BlockSpec((1, D), lambda b, i: (0, 0))],
        out_specs=pl.BlockSpec((1, tm, D), lambda b, i: (b, i, 0)),
        compiler_params=pltpu.CompilerParams(dimension_semantics=("arbitrary", "arbitrary"),
                                             vmem_limit_bytes=VMEM_LIMIT),
        name="final_norm",
    )(x, g.reshape(1, D))


def _heads(x, n):
    return x.reshape(x.shape[:-1] + (n, x.shape[-1] // n))


def _split_cols(p, sizes):
    return jnp.split(p, np.cumsum(sizes)[:-1].tolist(), axis=-1)


def _rms(x, g):
    return x * lax.rsqrt(jnp.mean(x * x, axis=-1, keepdims=True) + NORM_EPS) * g


def _short_conv(u, w):
    T = u.shape[1]
    pad = CONV_W // 2
    up = jnp.pad(u, ((0, 0), (pad, pad), (0, 0)))
    return sum(up[:, i:i + T] * w[i] for i in range(CONV_W))


def _axial_rope(rows):
    row = jnp.repeat(jnp.arange(rows), GRID_W)
    col = jnp.tile(jnp.arange(GRID_W), rows)
    pos = jnp.stack([row, col], axis=-1).astype(F32)
    freqs = ROPE_THETA ** (-jnp.arange(ROPE_PAIRS, dtype=F32) / ROPE_PAIRS)
    ang = pos[:, :, None] * freqs
    return jnp.cos(ang), jnp.sin(ang)


def _apply_rope(x, cos, sin):
    B, T, H, Dh = x.shape
    xa = x.reshape(B, T, H, 2, 2, ROPE_PAIRS)
    x1, x2 = xa[..., 0, :], xa[..., 1, :]
    c, s = cos[None, :, None], sin[None, :, None]
    out = jnp.stack([x1 * c - x2 * s, x2 * c + x1 * s], axis=-2)
    return out.reshape(B, T, H, Dh)


def _state_inputs(k_raw, v_raw, xw, xa, conv_w, w0, w_up, a0, a_up, k_k, k_a):
    B, T, _ = k_raw.shape
    hp = lax.Precision.HIGHEST
    k = _short_conv(k_raw, conv_w[:, RW_DIM:2 * RW_DIM])
    v = _short_conv(v_raw, conv_w[:, 2 * RW_DIM:])
    wl = w0 + jnp.einsum('btdl,dlc->btdc', jnp.tanh(xw.reshape(B, T, 2, W_LORA)), w_up, precision=hp)
    decay = jnp.exp(-jnp.exp(-jax.nn.softplus(-wl) - 0.5))
    a = jax.nn.sigmoid(a0 + jnp.einsum('btdl,dlc->btdc', xa.reshape(B, T, 2, A_LORA), a_up, precision=hp))
    kk = _heads(k * k_k, RW_HEADS)
    kk = kk * lax.rsqrt(jnp.maximum(jnp.sum(kk * kk, axis=-1, keepdims=True), 1e-24))
    kk = kk.reshape(B, T, RW_DIM)
    k_dir = k[:, :, None, :] * (1 + (a - 1) * k_a)
    return k_dir, v, decay, a, kk


def _readout(o, r, k_dir, v, xg, g_up, r_k, ln_x_g, ln_x_b):
    B, T, _ = o.shape
    oh = _heads(o, RW_HEADS)
    mu = jnp.mean(oh, axis=-1, keepdims=True)
    var = jnp.mean(jnp.square(oh - mu), axis=-1, keepdims=True)
    gn = ((oh - mu) * lax.rsqrt(var + GN_EPS)).reshape(B, T, RW_DIM) * ln_x_g + ln_x_b
    k_bonus = jnp.mean(k_dir, axis=2)
    bonus = jnp.sum(_heads(r * k_bonus * r_k.reshape(-1), RW_HEADS), axis=-1, keepdims=True) * _heads(v, RW_HEADS)
    g = jnp.dot(jax.nn.sigmoid(xg), g_up, precision=lax.Precision.HIGHEST)
    return (gn + bonus.reshape(B, T, RW_DIM)) * g


def _token_mixer(p_lat, p_ctx, q_norm_g, k_norm_g, conv_w, w0, w_up, a0, a_up, g_up,
                 k_k, k_a, r_k, ln_x_g, ln_x_b, cos, sin, last):
    B, T, _ = p_lat.shape
    C = p_ctx.shape[1]
    k_at, v_at, k_rw, v_rw, xw, xa, q_at, r_rw, xg = _split_cols(p_lat, SPLITS)
    ctx_parts = _split_cols(p_ctx, SPLITS[:6] if last else SPLITS)
    ck_at, cv_at, ck_rw, cv_rw, cxw, cxa = ctx_parts[:6]

    q = _apply_rope(_rms(_heads(q_at, ATTN_HEADS), q_norm_g), cos, sin) * ATTN_SCALE
    k = _apply_rope(_rms(_heads(k_at, ATTN_KV_HEADS), k_norm_g), cos, sin)
    kc = _rms(_heads(ck_at, ATTN_KV_HEADS), k_norm_g)
    vc = _heads(cv_at, ATTN_KV_HEADS)
    k_all = jnp.concatenate([kc, k], axis=1)
    v_all = jnp.concatenate([vc, _heads(v_at, ATTN_KV_HEADS)], axis=1)
    qh = jnp.transpose(q, (0, 2, 1, 3)).astype(BF16)
    kt_all = jnp.transpose(k_all, (0, 2, 3, 1)).astype(BF16)
    vh_all = jnp.transpose(v_all, (0, 2, 1, 3)).astype(BF16)
    o_at = attention(qh, kt_all, vh_all)

    rw = (conv_w, w0, w_up, a0, a_up, k_k, k_a)
    kd_c, v_c, dec_c, a_c, kk_c = _state_inputs(ck_rw, cv_rw, cxw, cxa, *rw)
    kd_l, v_l, dec_l, a_l, kk_l = _state_inputs(k_rw, v_rw, xw, xa, *rw)
    r_l = _short_conv(r_rw, conv_w[:, :RW_DIM])
    r_c = jnp.zeros((B, C, RW_DIM), F32) if last else _short_conv(ctx_parts[7], conv_w[:, :RW_DIM])

    def seq(xc, xl, d):
        if d == 0:
            return jnp.concatenate([xc, xl], axis=1)
        return jnp.flip(jnp.concatenate([xl, xc], axis=1), axis=1)

    def both(fc, fl):
        return jnp.stack([seq(fc(0), fl(0), 0), seq(fc(1), fl(1), 1)], axis=0)

    w_s = both(lambda d: dec_c[:, :, d], lambda d: dec_l[:, :, d])
    k_s = both(lambda d: kd_c[:, :, d], lambda d: kd_l[:, :, d])
    v_s = both(lambda d: v_c, lambda d: v_l)
    a_s = both(lambda d: -kk_c, lambda d: -kk_l)
    b_s = both(lambda d: kk_c * a_c[:, :, d], lambda d: kk_l * a_l[:, :, d])
    r_s = both(lambda d: r_c, lambda d: r_l)
    o_s = wkv_scan(w_s, k_s, v_s, a_s, b_s, r_s)
    o_f = o_s[0]
    o_b = jnp.flip(o_s[1], axis=1)
    o_lat = o_f[:, C:] + o_b[:, :T]
    o_rw = _readout(o_lat, r_l, kd_l, v_l, xg, g_up, r_k, ln_x_g, ln_x_b)
    cat_lat = jnp.concatenate([o_at, o_rw], axis=-1)
    if last:
        return cat_lat, None

    q_c = _rms(_heads(ctx_parts[6], ATTN_HEADS), q_norm_g) * ATTN_SCALE
    o_at_c = attention(jnp.transpose(q_c, (0, 2, 1, 3)).astype(BF16),
                       jnp.transpose(kc, (0, 2, 3, 1)).astype(BF16),
                       jnp.transpose(vc, (0, 2, 1, 3)).astype(BF16))
    o_ctx = o_f[:, :C] + o_b[:, T:]
    o_rw_c = _readout(o_ctx, r_c, kd_c, v_c, ctx_parts[8], g_up, r_k, ln_x_g, ln_x_b)
    cat_ctx = jnp.concatenate([o_at_c, o_rw_c], axis=-1)
    return cat_lat, cat_ctx


def _moe_ffn(h, logits, e_w1, e_b1, e_w2, e_b2):
    N, D = h.shape
    top_v, top_e = lax.top_k(logits, TOP_K)
    gates = jax.nn.softmax(top_v, axis=-1)
    flat_e = top_e.reshape(-1)
    flat_tok = jnp.arange(N * TOP_K, dtype=jnp.int32) // TOP_K
    flat_g = gates.reshape(-1)
    order = jnp.argsort(flat_e)
    e_sorted = flat_e[order]
    counts = jnp.bincount(flat_e, length=N_EXPERTS)
    padded = (counts + MOE_BLOCK - 1) // MOE_BLOCK * MOE_BLOCK
    start = jnp.cumsum(counts) - counts
    pstart = jnp.cumsum(padded) - padded
    dest = pstart[e_sorted] + jnp.arange(N * TOP_K, dtype=jnp.int32) - start[e_sorted]
    n_blocks = (N * TOP_K + MOE_BLOCK - 1) // MOE_BLOCK + N_EXPERTS
    rows = n_blocks * MOE_BLOCK
    slot_tok = jnp.full((rows,), N, jnp.int32).at[dest].set(flat_tok[order])
    slot_gate = jnp.zeros((rows,), F32).at[dest].set(flat_g[order])
    block_e = jnp.minimum(jnp.searchsorted(jnp.cumsum(padded), jnp.arange(n_blocks) * MOE_BLOCK, side='right'),
                          N_EXPERTS - 1).astype(jnp.int32)
    h_pad = jnp.concatenate([h, jnp.zeros((1, D), h.dtype)], axis=0)
    hs = h_pad[slot_tok]
    out = moe_blocks(block_e, hs, e_w1, e_b1, e_w2, e_b2, slot_gate[:, None])
    y = jnp.zeros((N + 1, D), F32).at[slot_tok].add(out)
    return y[:N]


def kernel(x, c, ctx, c_ctx, norm_mix_g, norm_ffn_g, w_mod, b_mod, w_in, w_out, q_norm_g, k_norm_g,
           conv_w, w0, w_up, a0, a_up, g_up, k_k, k_a, r_k, ln_x_g, ln_x_b,
           router_w, router_b, e_w1, e_b1, e_w2, e_b2, norm_final_g):
    B, T, D = x.shape
    C = ctx.shape[1]
    depth = w_in.shape[0]
    cos, sin = _axial_rope(T // GRID_W)
    rpad = -(-(B + 1) // 8) * 8
    cvecs = jnp.zeros((rpad, D), F32).at[:B].set(c).at[B].set(c_ctx)
    x_lat, x_ctx = x, ctx
    for l in range(depth):
        last = l == depth - 1
        mods = adaln_all(cvecs, w_mod[l], b_mod[l])
        m_lat = [m[:, None, :] for m in jnp.split(mods[:B], 6, axis=-1)]
        m_ctx = [jnp.broadcast_to(m[None, :, :], (B, 1, D)) for m in jnp.split(mods[B:B + 1], 6, axis=-1)]
        sh_m, sc_m, g_m, sh_f, sc_f, g_f = m_lat
        csh_m, csc_m, cg_m, csh_f, csc_f, cg_f = m_ctx
        w_in_l = w_in[l].astype(BF16)
        p_lat = norm_mod_matmul(x_lat, norm_mix_g[l], sc_m, sh_m, w_in_l)
        p_ctx = norm_mod_matmul(x_ctx, norm_mix_g[l], csc_m, csh_m,
                                w_in_l[:, :STATE_COLS] if last else w_in_l)
        cat_lat, cat_ctx = _token_mixer(
            p_lat, p_ctx, q_norm_g[l], k_norm_g[l], conv_w[l], w0[l], w_up[l], a0[l], a_up[l], g_up[l],
            k_k[l], k_a[l], r_k[l], ln_x_g[l], ln_x_b[l], cos, sin, last)
        w_out_l = w_out[l].astype(BF16)
        x_lat = matmul_res(cat_lat, w_out_l, x_lat, g_m)
        h_lat, lg_lat = ffn_pre(x_lat, norm_ffn_g[l], sc_f, sh_f, router_w[l], router_b[l])
        e_w1_l = e_w1[l].astype(BF16)
        e_w2_l = e_w2[l].astype(BF16)
        if last:
            f = _moe_ffn(h_lat.reshape(B * T, D), lg_lat.reshape(B * T, -1), e_w1_l, e_b1[l], e_w2_l, e_b2[l])
            x_lat = x_lat + g_f * f.reshape(B, T, D)
        else:
            x_ctx = matmul_res(cat_ctx, w_out_l, x_ctx, cg_m)
            h_ctx, lg_ctx = ffn_pre(x_ctx, norm_ffn_g[l], csc_f, csh_f, router_w[l], router_b[l])
            h_all = jnp.concatenate([h_lat.reshape(B * T, D), h_ctx.reshape(B * C, D)], axis=0)
            lg_all = jnp.concatenate([lg_lat.reshape(B * T, -1), lg_ctx.reshape(B * C, -1)], axis=0)
            f = _moe_ffn(h_all, lg_all, e_w1_l, e_b1[l], e_w2_l, e_b2[l])
            x_lat = x_lat + g_f * f[:B * T].reshape(B, T, D)
            x_ctx = x_ctx + cg_f * f[B * T:].reshape(B, C, D)
    return final_norm(x_lat, norm_final_g)
```

```python
import functools
import math

import numpy as np
import jax
import jax.numpy as jnp
from jax import lax
from jax.experimental import pallas as pl
from jax.experimental.pallas import tpu as pltpu

F32 = jnp.float32
BF16 = jnp.bfloat16
HIGHEST = lax.Precision.HIGHEST

GRID_W = 64
HEAD_DIM = 64
ATTN_HEADS = 8
ATTN_KV_HEADS = 2
GROUP = ATTN_HEADS // ATTN_KV_HEADS
ATTN_DIM = ATTN_HEADS * HEAD_DIM
KV_DIM = ATTN_KV_HEADS * HEAD_DIM
RW_HEADS = 8
RW_DIM = RW_HEADS * HEAD_DIM
W_LORA = 64
A_LORA = 64
G_LORA = 128
CONV_W = 3
N_EXPERTS = 32
TOP_K = 4
SWIGLU_LIMIT = 7.0
SWIGLU_ALPHA = 1.702
ROPE_THETA = 10000.0
ROPE_PAIRS = HEAD_DIM // 4
MOE_BLOCK = 256
NORM_EPS = 1e-6
GN_EPS = 64e-5
ATTN_SCALE = HEAD_DIM ** -0.5
EXP_M05 = math.exp(-0.5)
COL_K_AT = 0
COL_V_AT = KV_DIM
COL_K_RW = 2 * KV_DIM
COL_V_RW = COL_K_RW + RW_DIM
COL_XW = COL_V_RW + RW_DIM
COL_XA = COL_XW + 2 * W_LORA
STATE_COLS = COL_XA + 2 * A_LORA
COL_Q_AT = STATE_COLS
COL_R_RW = COL_Q_AT + ATTN_DIM
COL_XG = COL_R_RW + RW_DIM

LANES = 128
SUBLANES = 8
QUAD_LANES = 256
VMEM_LIMIT = 48 * 1024 * 1024


def _row_tile(n, pref):
    t = pref
    while n % t:
        t //= 2
    return t


def _params(n_axes):
    return pltpu.CompilerParams(dimension_semantics=("arbitrary",) * n_axes, vmem_limit_bytes=VMEM_LIMIT)


def _ones_bd(n):
    ra = lax.broadcasted_iota(jnp.int32, (n, n), 0) // HEAD_DIM
    rb = lax.broadcasted_iota(jnp.int32, (n, n), 1) // HEAD_DIM
    return (ra == rb).astype(BF16)


def _seg_sum(x_bf16, ones_bd):
    return jnp.dot(x_bf16, ones_bd, preferred_element_type=F32)


def _seg_sum2(x, ones_bd):
    hi = x.astype(BF16)
    lo = (x - hi.astype(F32)).astype(BF16)
    return _seg_sum(hi, ones_bd) + _seg_sum(lo, ones_bd)


def _adaln_kernel(c_ref, w_ref, b_ref, o_ref):
    c = c_ref[...]
    s = c * jax.nn.sigmoid(c)
    o_ref[...] = jnp.dot(s, w_ref[...], preferred_element_type=F32, precision=HIGHEST) + b_ref[...]


def adaln_all(cvecs, w_mod, b_mod):
    R, D = cvecs.shape
    N = w_mod.shape[1]
    tn = 512
    return pl.pallas_call(
        _adaln_kernel,
        out_shape=jax.ShapeDtypeStruct((R, N), F32),
        grid=(N // tn,),
        in_specs=[pl.BlockSpec((R, D), lambda j: (0, 0)),
                  pl.BlockSpec((D, tn), lambda j: (0, j)),
                  pl.BlockSpec((1, tn), lambda j: (0, j))],
        out_specs=pl.BlockSpec((R, tn), lambda j: (0, j)),
        compiler_params=_params(1),
        name="adaln",
    )(cvecs, w_mod, b_mod.reshape(1, N))


def _norm_mod(x, g, sc, sh):
    ms = jnp.mean(x * x, axis=-1, keepdims=True)
    y = x * lax.rsqrt(ms + NORM_EPS) * g
    return y * (1.0 + sc) + sh


def _norm_mod_matmul_kernel(x_ref, g_ref, sc_ref, sh_ref, w_ref, o_ref):
    h = _norm_mod(x_ref[0], g_ref[...], sc_ref[0], sh_ref[0])
    o_ref[0] = jnp.dot(h.astype(BF16), w_ref[...], preferred_element_type=F32)


def norm_mod_matmul(x, g, sc, sh, w):
    B, T, D = x.shape
    N = w.shape[1]
    tm = _row_tile(T, 256)
    return pl.pallas_call(
        _norm_mod_matmul_kernel,
        out_shape=jax.ShapeDtypeStruct((B, T, N), F32),
        grid=(B, T // tm),
        in_specs=[pl.BlockSpec((1, tm, D), lambda b, i: (b, i, 0)),
                  pl.BlockSpec((1, D), lambda b, i: (0, 0)),
                  pl.BlockSpec((1, 1, D), lambda b, i: (b, 0, 0)),
                  pl.BlockSpec((1, 1, D), lambda b, i: (b, 0, 0)),
                  pl.BlockSpec((D, N), lambda b, i: (0, 0))],
        out_specs=pl.BlockSpec((1, tm, N), lambda b, i: (b, i, 0)),
        compiler_params=_params(2),
        name="norm_mod_matmul",
    )(x, g.reshape(1, D), sc, sh, w)


def _pair_rms(x, g):
    lane = lax.broadcasted_iota(jnp.int32, x.shape, 1)
    first = lane < HEAD_DIM
    xx = x * x
    s0 = jnp.sum(jnp.where(first, xx, 0.0), axis=1, keepdims=True)
    s1 = jnp.sum(jnp.where(first, 0.0, xx), axis=1, keepdims=True)
    ms = jnp.where(first, s0, s1) * (1.0 / HEAD_DIM)
    return x * lax.rsqrt(ms + NORM_EPS) * g


def _pair_rope(x, cos, sin):
    lane = lax.broadcasted_iota(jnp.int32, x.shape, 1)
    partner = jnp.where(lane % (2 * ROPE_PAIRS) < ROPE_PAIRS,
                        pltpu.roll(x, LANES - ROPE_PAIRS, 1), pltpu.roll(x, ROPE_PAIRS, 1))
    return x * cos + partner * sin


def _rope_tables(T):
    rows = T // GRID_W
    row = jnp.repeat(jnp.arange(rows), GRID_W)
    col = jnp.tile(jnp.arange(GRID_W), rows)
    pos = jnp.stack([row, col], axis=-1).astype(F32)
    freqs = ROPE_THETA ** (-jnp.arange(ROPE_PAIRS, dtype=F32) / ROPE_PAIRS)
    ang = pos[:, :, None] * freqs
    cos, sin = jnp.cos(ang), jnp.sin(ang)
    cos_h = jnp.concatenate([cos[:, 0], cos[:, 0], cos[:, 1], cos[:, 1]], axis=-1)
    sin_h = jnp.concatenate([-sin[:, 0], sin[:, 0], -sin[:, 1], sin[:, 1]], axis=-1)
    return jnp.tile(cos_h, (1, 2)), jnp.tile(sin_h, (1, 2))


def _kv_prep_kernel(*refs, rope):
    if rope:
        p_ref, g_ref, cos_ref, sin_ref, kt_ref, v_ref = refs
    else:
        p_ref, g_ref, kt_ref, v_ref = refs
    x = p_ref[0]
    k = _pair_rms(x[:, :KV_DIM], g_ref[...])
    if rope:
        k = _pair_rope(k, cos_ref[...], sin_ref[...])
    v = x[:, KV_DIM:]
    lane = lax.broadcasted_iota(jnp.int32, k.shape, 1)
    first = lane < HEAD_DIM
    kr = pltpu.roll(k, HEAD_DIM, 1)
    vr = pltpu.roll(v, HEAD_DIM, 1)
    kt_ref[0, 0] = jnp.where(first, k, kr).T.astype(BF16)
    kt_ref[0, 1] = jnp.where(first, kr, k).T.astype(BF16)
    v_ref[0, 0] = jnp.where(first, v, vr).astype(BF16)
    v_ref[0, 1] = jnp.where(first, vr, v).astype(BF16)


def kv_prep(p, k_norm_g, cos=None, sin=None):
    B, T, _ = p.shape
    tm = _row_tile(T, 256)
    rope = cos is not None
    g2 = jnp.tile(k_norm_g, 2).reshape(1, LANES)
    in_specs = [pl.BlockSpec((1, tm, 2 * KV_DIM), lambda b, i: (b, i, 0)),
                pl.BlockSpec((1, LANES), lambda b, i: (0, 0))]
    args = [p, g2]
    if rope:
        in_specs += [pl.BlockSpec((tm, LANES), lambda b, i: (i, 0))] * 2
        args += [cos, sin]
    return pl.pallas_call(
        functools.partial(_kv_prep_kernel, rope=rope),
        out_shape=(jax.ShapeDtypeStruct((B, ATTN_KV_HEADS, LANES, T), BF16),
                   jax.ShapeDtypeStruct((B, ATTN_KV_HEADS, T, LANES), BF16)),
        grid=(B, T // tm),
        in_specs=in_specs,
        out_specs=(pl.BlockSpec((1, ATTN_KV_HEADS, LANES, tm), lambda b, i: (b, 0, 0, i)),
                   pl.BlockSpec((1, ATTN_KV_HEADS, tm, LANES), lambda b, i: (b, 0, i, 0))),
        compiler_params=_params(2),
        name="kv_prep",
    )(*args)


def _attn_kernel(*refs, tq, tkl, nkl, rope, has_lat):
    refs = list(refs)
    p_ref, g_ref = refs[:2]
    refs = refs[2:]
    if rope:
        cos_ref, sin_ref = refs[:2]
        refs = refs[2:]
    ktc_ref, vc_ref = refs[:2]
    refs = refs[2:]
    if has_lat:
        ktl_ref, vl_ref = refs[:2]
        refs = refs[2:]
    o_ref, m_ref, l_ref, acc_ref = refs

    q = _pair_rms(p_ref[0], g_ref[...])
    if rope:
        q = _pair_rope(q, cos_ref[...], sin_ref[...])
    q = q * ATTN_SCALE
    lane = lax.broadcasted_iota(jnp.int32, (tq, LANES), 1)
    first = lane < HEAD_DIM
    qq = jnp.concatenate([jnp.where(first, q, 0.0), jnp.where(first, 0.0, q)], axis=0).astype(BF16)

    m_ref[...] = jnp.full(m_ref.shape, -jnp.inf, F32)
    l_ref[...] = jnp.zeros(l_ref.shape, F32)
    acc_ref[...] = jnp.zeros(acc_ref.shape, F32)

    def chunk(kt, vv):
        s = jnp.dot(qq, kt, preferred_element_type=F32)
        m_old = m_ref[...]
        m_new = jnp.maximum(m_old, jnp.max(s, axis=-1, keepdims=True))
        alpha = jnp.exp(m_old - m_new)
        p = jnp.exp(s - pltpu.repeat(m_new, s.shape[1] // LANES, axis=1))
        l_ref[...] = alpha * l_ref[...] + jnp.sum(p, axis=-1, keepdims=True)
        acc_ref[...] = alpha * acc_ref[...] + jnp.dot(p.astype(BF16), vv, preferred_element_type=F32)
        m_ref[...] = m_new

    chunk(ktc_ref[0, 0], vc_ref[0, 0])
    if has_lat:
        def body(j, carry):
            ks = pl.multiple_of(j * tkl, tkl)
            chunk(ktl_ref[0, 0, :, pl.ds(ks, tkl)], vl_ref[0, 0, pl.ds(ks, tkl), :])
            return carry
        lax.fori_loop(0, nkl, body, 0)

    o = acc_ref[...] / l_ref[...]
    o_ref[0] = jnp.where(first, o[:tq], o[tq:])


def attention(p, q_norm_g, ktc, vc, ktl=None, vl=None, cos=None, sin=None):
    B, T, _ = p.shape
    C = ktc.shape[-1]
    rope = cos is not None
    has_lat = ktl is not None
    tq = _row_tile(T, 256)
    pairs = ATTN_HEADS // 2
    pairs_per_kv = GROUP // 2
    qblk0 = COL_Q_AT // LANES
    g2 = jnp.tile(q_norm_g, 2).reshape(1, LANES)
    in_specs = [pl.BlockSpec((1, tq, LANES), lambda b, h, i: (b, i, qblk0 + h)),
                pl.BlockSpec((1, LANES), lambda b, h, i: (0, 0))]
    args = [p, g2]
    if rope:
        in_specs += [pl.BlockSpec((tq, LANES), lambda b, h, i: (i, 0))] * 2
        args += [cos, sin]
    in_specs += [pl.BlockSpec((1, 1, LANES, C), lambda b, h, i: (b, h // pairs_per_kv, 0, 0)),
                 pl.BlockSpec((1, 1, C, LANES), lambda b, h, i: (b, h // pairs_per_kv, 0, 0))]
    args += [ktc, vc]
    tkl = nkl = 0
    if has_lat:
        S = ktl.shape[-1]
        tkl = _row_tile(S, 1024)
        nkl = S // tkl
        in_specs += [pl.BlockSpec((1, 1, LANES, S), lambda b, h, i: (b, h // pairs_per_kv, 0, 0)),
                     pl.BlockSpec((1, 1, S, LANES), lambda b, h, i: (b, h // pairs_per_kv, 0, 0))]
        args += [ktl, vl]
    kern = functools.partial(_attn_kernel, tq=tq, tkl=tkl, nkl=nkl, rope=rope, has_lat=has_lat)
    return pl.pallas_call(
        kern,
        out_shape=jax.ShapeDtypeStruct((B, T, ATTN_DIM), F32),
        grid=(B, pairs, T // tq),
        in_specs=in_specs,
        out_specs=pl.BlockSpec((1, tq, LANES), lambda b, h, i: (b, i, h)),
        scratch_shapes=[pltpu.VMEM((2 * tq, LANES), F32)] * 3,
        compiler_params=_params(3),
        name="attention",
    )(*args)


def _wkv_prep_kernel(p_ref, pp_ref, pn_ref, cw_ref, w0_ref, wup_ref, a0_ref, aup_ref, kk_ref, ka_ref,
                     w2_ref, k2_ref, b2_ref, v_ref, kkn_ref, r_ref, *, tm, has_r):
    i = pl.program_id(1)
    n = pl.num_programs(1)
    x = p_ref[0]
    prev = pp_ref[0][SUBLANES - 1:SUBLANES, :] * (i > 0).astype(F32)
    nxt = pn_ref[0][0:1, :] * (i < n - 1).astype(F32)
    row = lax.broadcasted_iota(jnp.int32, (tm, RW_DIM), 0)

    def conv(col, wcol):
        u = x[:, col:col + RW_DIM]
        up = jnp.where(row == 0, prev[:, col:col + RW_DIM], pltpu.roll(u, 1, 0))
        un = jnp.where(row == tm - 1, nxt[:, col:col + RW_DIM], pltpu.roll(u, tm - 1, 0))
        w = cw_ref[:, wcol:wcol + RW_DIM]
        return up * w[0:1] + u * w[1:2] + un * w[2:3]

    k = conv(COL_K_RW, RW_DIM)
    v = conv(COL_V_RW, 2 * RW_DIM)
    v_ref[0] = v
    r_ref[0] = conv(COL_R_RW, 0) if has_r else jnp.zeros((tm, RW_DIM), F32)
    kk = k * kk_ref[...]
    ss = _seg_sum2(kk * kk, _ones_bd(RW_DIM))
    kk = kk * lax.rsqrt(jnp.maximum(ss, 1e-24))
    kkn_ref[0] = kk
    xw = jnp.tanh(x[:, COL_XW:COL_XW + 2 * W_LORA])
    xa = x[:, COL_XA:COL_XA + 2 * A_LORA]
    for d in range(2):
        wl = w0_ref[d:d + 1, :] + jnp.dot(xw[:, d * W_LORA:(d + 1) * W_LORA], wup_ref[d],
                                          preferred_element_type=F32, precision=HIGHEST)
        decay = jnp.exp(-EXP_M05 * jax.nn.sigmoid(wl))
        a = jax.nn.sigmoid(a0_ref[d:d + 1, :] + jnp.dot(xa[:, d * A_LORA:(d + 1) * A_LORA], aup_ref[d],
                                                        preferred_element_type=F32, precision=HIGHEST))
        sl = slice(d * RW_DIM, (d + 1) * RW_DIM)
        w2_ref[0, :, sl] = decay
        k2_ref[0, :, sl] = k * (1.0 + (a - 1.0) * ka_ref[...])
        b2_ref[0, :, sl] = kk * a


def wkv_prep(p, conv_w, w0, w_up, a0, a_up, k_k, k_a, has_r):
    B, T, NP = p.shape
    tm = _row_tile(T, 256)
    nb8 = tm // SUBLANES
    last8 = T // SUBLANES - 1
    full = lambda a: pl.BlockSpec(a.shape, lambda b, i: (0,) * a.ndim)
    kk2, ka2 = k_k.reshape(1, RW_DIM), k_a.reshape(1, RW_DIM)
    wide = jax.ShapeDtypeStruct((B, T, 2 * RW_DIM), F32)
    narrow = jax.ShapeDtypeStruct((B, T, RW_DIM), F32)
    wspec = pl.BlockSpec((1, tm, 2 * RW_DIM), lambda b, i: (b, i, 0))
    nspec = pl.BlockSpec((1, tm, RW_DIM), lambda b, i: (b, i, 0))
    return pl.pallas_call(
        functools.partial(_wkv_prep_kernel, tm=tm, has_r=has_r),
        out_shape=(wide, wide, wide, narrow, narrow, narrow),
        grid=(B, T // tm),
        in_specs=[pl.BlockSpec((1, tm, NP), lambda b, i: (b, i, 0)),
                  pl.BlockSpec((1, SUBLANES, NP), lambda b, i: (b, jnp.maximum(i * nb8 - 1, 0), 0)),
                  pl.BlockSpec((1, SUBLANES, NP), lambda b, i: (b, jnp.minimum((i + 1) * nb8, last8), 0)),
                  full(conv_w), full(w0), full(w_up), full(a0), full(a_up), full(kk2), full(ka2)],
        out_specs=(wspec, wspec, wspec, nspec, nspec, nspec),
        compiler_params=_params(2),
        name="wkv_prep",
    )(p, p, p, conv_w, w0, w_up, a0, a_up, kk2, ka2)


def _wkv_kernel(wf_ref, wb_ref, kf_ref, kb_ref, bf_ref, bb_ref, vf_ref, vb_ref, af_ref, ab_ref,
                rf_ref, rb_ref, s0_ref, of_ref, ob_ref, st_ref, s_ref, *, tc, nquads):
    tb = pl.program_id(1)

    @pl.when(tb == 0)
    def _():
        s_ref[...] = s0_ref[0]

    rows_all = 2 * nquads * HEAD_DIM
    shp = (rows_all, QUAD_LANES)
    lane = lax.broadcasted_iota(jnp.int32, shp, 1)
    sub = lax.broadcasted_iota(jnp.int32, shp, 0)
    eye = (lane % HEAD_DIM == sub % HEAD_DIM).astype(F32)
    ones_bd = _ones_bd(QUAD_LANES)
    ng = tc // SUBLANES

    def bcast_rows(blks, j):
        parts = []
        for d in range(2):
            jj = j if d == 0 else SUBLANES - 1 - j
            for q in range(nquads):
                row = blks[d][jj:jj + 1, q * QUAD_LANES:(q + 1) * QUAD_LANES]
                parts.append(jnp.broadcast_to(row, (HEAD_DIM, QUAD_LANES)))
        return jnp.concatenate(parts, axis=0)

    def group(g, carry):
        tf = pl.multiple_of(g * SUBLANES, SUBLANES)
        tr = pl.multiple_of((ng - 1 - g) * SUBLANES, SUBLANES)
        ld = lambda fr, br: [fr[0, pl.ds(tf, SUBLANES), :], br[0, pl.ds(tr, SUBLANES), :]]
        wq, kq, bq, vq, rq = (ld(f, b) for f, b in ((wf_ref, wb_ref), (kf_ref, kb_ref), (bf_ref, bb_ref),
                                                    (vf_ref, vb_ref), (rf_ref, rb_ref)))
        aq = [-x for x in ld(af_ref, ab_ref)]
        S = s_ref[...]
        out_rows = [[] for _ in range(2 * nquads)]
        for j in range(SUBLANES):
            wt, kt, vt, at, bt, rt = (bcast_rows(x, j) for x in (wq, kq, vq, aq, bq, rq))
            sa = _seg_sum2(S * at, ones_bd)
            vcol = _seg_sum((eye * vt).astype(BF16), ones_bd)
            S = S * wt + sa * bt + vcol * kt
            ob = _seg_sum((S * rt).astype(BF16), ones_bd) * eye
            for c in range(2 * nquads):
                out_rows[c].append(jnp.sum(ob[c * HEAD_DIM:(c + 1) * HEAD_DIM], axis=0, keepdims=True))
        s_ref[...] = S
        for q in range(nquads):
            sl = slice(q * QUAD_LANES, (q + 1) * QUAD_LANES)
            of_ref[0, pl.ds(tf, SUBLANES), sl] = jnp.concatenate(out_rows[q], axis=0)
            ob_ref[0, pl.ds(tr, SUBLANES), sl] = jnp.concatenate(out_rows[nquads + q][::-1], axis=0)
        return carry

    lax.fori_loop(0, ng, group, 0)

    @pl.when(tb == pl.num_programs(1) - 1)
    def _():
        st_ref[0] = s_ref[...]


def wkv_scan(w2, k2, b2, v, kk, r, s0):
    B, T, C = v.shape
    nquads = C // QUAD_LANES
    tc = _row_tile(T, 128)
    nt = T // tc
    fwd = lambda col: pl.BlockSpec((1, tc, C), lambda b, t: (b, t, col))
    bwd = lambda col: pl.BlockSpec((1, tc, C), lambda b, t: (b, nt - 1 - t, col))
    sspec = pl.BlockSpec((1,) + s0.shape[1:], lambda b, t: (b, 0, 0))
    kern = functools.partial(_wkv_kernel, tc=tc, nquads=nquads)
    return pl.pallas_call(
        kern,
        out_shape=(jax.ShapeDtypeStruct((B, T, C), F32), jax.ShapeDtypeStruct((B, T, C), F32),
                   jax.ShapeDtypeStruct(s0.shape, F32)),
        grid=(B, nt),
        in_specs=[fwd(0), bwd(1), fwd(0), bwd(1), fwd(0), bwd(1), fwd(0), bwd(0), fwd(0), bwd(0),
                  fwd(0), bwd(0), sspec],
        out_specs=(fwd(0), bwd(0), sspec),
        scratch_shapes=[pltpu.VMEM(s0.shape[1:], F32)],
        compiler_params=_params(2),
        name="wkv_scan",
    )(w2, w2, k2, k2, b2, b2, v, v, kk, kk, r, r, s0)


def _mixer_out_kernel(of_ref, ob_ref, r_ref, v_ref, k2_ref, xg_ref, oat_ref, x_ref, gm_ref,
                      gup_ref, rk_ref, lng_ref, lnb_ref, wo_ref, o_ref):
    ones = _ones_bd(RW_DIM)
    inv = 1.0 / HEAD_DIM
    o = of_ref[0] + ob_ref[0]
    mu = _seg_sum2(o, ones) * inv
    dv = o - mu
    var = _seg_sum2(dv * dv, ones) * inv
    gn = dv * lax.rsqrt(var + GN_EPS) * lng_ref[...] + lnb_ref[...]
    k2 = k2_ref[0]
    k_bonus = (k2[:, :RW_DIM] + k2[:, RW_DIM:]) * 0.5
    bonus = _seg_sum2(r_ref[0] * k_bonus * rk_ref[...], ones) * v_ref[0]
    g = jnp.dot(jax.nn.sigmoid(xg_ref[0]), gup_ref[...], preferred_element_type=F32, precision=HIGHEST)
    y = (gn + bonus) * g
    acc = jnp.dot(oat_ref[0].astype(BF16), wo_ref[:ATTN_DIM, :], preferred_element_type=F32)
    acc = acc + jnp.dot(y.astype(BF16), wo_ref[ATTN_DIM:, :], preferred_element_type=F32)
    o_ref[0] = x_ref[0] + gm_ref[0] * acc


def mixer_out(o_f, o_b, r, v, k2, p, o_at, x, g_m, g_up, r_k, ln_x_g, ln_x_b, w_out):
    B, T, D = x.shape
    tm = _row_tile(T, 256)
    nspec = pl.BlockSpec((1, tm, RW_DIM), lambda b, i: (b, i, 0))
    full = lambda a: pl.BlockSpec(a.shape, lambda b, i: (0,) * a.ndim)
    rk2, lng2, lnb2 = r_k.reshape(1, RW_DIM), ln_x_g.reshape(1, RW_DIM), ln_x_b.reshape(1, RW_DIM)
    return pl.pallas_call(
        _mixer_out_kernel,
        out_shape=jax.ShapeDtypeStruct((B, T, D), F32),
        grid=(B, T // tm),
        in_specs=[nspec, nspec, nspec, nspec,
                  pl.BlockSpec((1, tm, 2 * RW_DIM), lambda b, i: (b, i, 0)),
                  pl.BlockSpec((1, tm, G_LORA), lambda b, i: (b, i, COL_XG // G_LORA)),
                  pl.BlockSpec((1, tm, ATTN_DIM), lambda b, i: (b, i, 0)),
                  pl.BlockSpec((1, tm, D), lambda b, i: (b, i, 0)),
                  pl.BlockSpec((1, 1, D), lambda b, i: (b, 0, 0)),
                  full(g_up), full(rk2), full(lng2), full(lnb2), full(w_out)],
        out_specs=pl.BlockSpec((1, tm, D), lambda b, i: (b, i, 0)),
        compiler_params=_params(2),
        name="mixer_out",
    )(o_f, o_b, r, v, k2, p, o_at, x, g_m, g_up, rk2, lng2, lnb2, w_out)


def _moe_kernel(be_ref, hs_ref, w1_ref, b1_ref, w2_ref, b2_ref, g_ref, o_ref, *, f):
    u = jnp.dot(hs_ref[...], w1_ref[0], preferred_element_type=F32) + b1_ref[0]
    x_glu = jnp.minimum(u[:, :f], SWIGLU_LIMIT)
    x_lin = jnp.clip(u[:, f:], -SWIGLU_LIMIT, SWIGLU_LIMIT)
    act = x_glu * jax.nn.sigmoid(SWIGLU_ALPHA * x_glu) * (x_lin + 1.0)
    y = jnp.dot(act.astype(BF16), w2_ref[0], preferred_element_type=F32) + b2_ref[0]
    o_ref[...] = y * g_ref[...]


def moe_blocks(block_e, hs, w1, b1, w2, b2, gate):
    rows, D = hs.shape
    E, _, F2 = w1.shape
    f = F2 // 2
    tm = MOE_BLOCK
    nb = rows // tm
    kern = functools.partial(_moe_kernel, f=f)
    grid_spec = pltpu.PrefetchScalarGridSpec(
        num_scalar_prefetch=1,
        grid=(nb,),
        in_specs=[pl.BlockSpec((tm, D), lambda i, be: (i, 0)),
                  pl.BlockSpec((1, D, F2), lambda i, be: (be[i], 0, 0)),
                  pl.BlockSpec((1, 1, F2), lambda i, be: (be[i], 0, 0)),
                  pl.BlockSpec((1, f, D), lambda i, be: (be[i], 0, 0)),
                  pl.BlockSpec((1, 1, D), lambda i, be: (be[i], 0, 0)),
                  pl.BlockSpec((tm, 1), lambda i, be: (i, 0))],
        out_specs=pl.BlockSpec((tm, D), lambda i, be: (i, 0)),
    )
    return pl.pallas_call(
        kern,
        out_shape=jax.ShapeDtypeStruct((rows, D), F32),
        grid_spec=grid_spec,
        compiler_params=_params(1),
        name="moe_blocks",
    )(block_e, hs, w1, b1.reshape(E, 1, F2), w2, b2.reshape(E, 1, D), gate)


def _ffn_pre_kernel(x_ref, g_ref, sc_ref, sh_ref, rw_ref, rb_ref, h_ref, lg_ref):
    h = _norm_mod(x_ref[0], g_ref[...], sc_ref[0], sh_ref[0])
    h_ref[0] = h.astype(BF16)
    lg_ref[0] = jnp.dot(h, rw_ref[...], preferred_element_type=F32, precision=HIGHEST) + rb_ref[...]


def ffn_pre(x, g, sc, sh, router_w, router_b):
    B, T, D = x.shape
    E = router_w.shape[1]
    tm = _row_tile(T, 256)
    return pl.pallas_call(
        _ffn_pre_kernel,
        out_shape=(jax.ShapeDtypeStruct((B, T, D), BF16), jax.ShapeDtypeStruct((B, T, E), F32)),
        grid=(B, T // tm),
        in_specs=[pl.BlockSpec((1, tm, D), lambda b, i: (b, i, 0)),
                  pl.BlockSpec((1, D), lambda b, i: (0, 0)),
                  pl.BlockSpec((1, 1, D), lambda b, i: (b, 0, 0)),
                  pl.BlockSpec((1, 1, D), lambda b, i: (b, 0, 0)),
                  pl.BlockSpec((D, E), lambda b, i: (0, 0)),
                  pl.BlockSpec((1, E), lambda b, i: (0, 0))],
        out_specs=(pl.BlockSpec((1, tm, D), lambda b, i: (b, i, 0)),
                   pl.BlockSpec((1, tm, E), lambda b, i: (b, i, 0))),
        compiler_params=_params(2),
        name="ffn_pre",
    )(x, g.reshape(1, D), sc, sh, router_w, router_b.reshape(1, E))


def _rms_kernel(x_ref, g_ref, o_ref):
    x = x_ref[0]
    ms = jnp.mean(x * x, axis=-1, keepdims=True)
    o_ref[0] = x * lax.rsqrt(ms + NORM_EPS) * g_ref[...]


def final_norm(x, g):
    B, T, D = x.shape
    tm = _row_tile(T, 512)
    return pl.pallas_call(
        _rms_kernel,
        out_shape=jax.ShapeDtypeStruct((B, T, D), F32),
        grid=(B, T // tm),
        in_specs=[pl.BlockSpec((1, tm, D), lambda b, i: (b, i, 0)),
                  pl.BlockSpec((1, D), lambda b, i: (0, 0))],
        out_specs=pl.BlockSpec((1, tm, D), lambda b, i: (b, i, 0)),
        compiler_params=_params(2),
        name="final_norm",
    )(x, g.reshape(1, D))


def _moe_ffn(h, logits, e_w1, e_b1, e_w2, e_b2):
    N, D = h.shape
    top_v, top_e = lax.top_k(logits, TOP_K)
    gates = jax.nn.softmax(top_v, axis=-1)
    flat_e = top_e.reshape(-1)
    flat_tok = jnp.arange(N * TOP_K, dtype=jnp.int32) // TOP_K
    flat_g = gates.reshape(-1)
    order = jnp.argsort(flat_e)
    e_sorted = flat_e[order]
    counts = jnp.bincount(flat_e, length=N_EXPERTS)
    padded = (counts + MOE_BLOCK - 1) // MOE_BLOCK * MOE_BLOCK
    start = jnp.cumsum(counts) - counts
    pstart = jnp.cumsum(padded) - padded
    dest = pstart[e_sorted] + jnp.arange(N * TOP_K, dtype=jnp.int32) - start[e_sorted]
    n_blocks = (N * TOP_K + MOE_BLOCK - 1) // MOE_BLOCK + N_EXPERTS
    rows = n_blocks * MOE_BLOCK
    slot_tok = jnp.full((rows,), N, jnp.int32).at[dest].set(flat_tok[order])
    slot_gate = jnp.zeros((rows,), F32).at[dest].set(flat_g[order])
    block_e = jnp.minimum(jnp.searchsorted(jnp.cumsum(padded), jnp.arange(n_blocks) * MOE_BLOCK, side='right'),
                          N_EXPERTS - 1).astype(jnp.int32)
    h_pad = jnp.concatenate([h, jnp.zeros((1, D), h.dtype)], axis=0)
    hs = h_pad[slot_tok]
    out = moe_blocks(block_e, hs, e_w1, e_b1, e_w2, e_b2, slot_gate[:, None])
    y = jnp.zeros((N + 1, D), F32).at[slot_tok].add(out)
    return y[:N]


def kernel(x, c, ctx, c_ctx, norm_mix_g, norm_ffn_g, w_mod, b_mod, w_in, w_out, q_norm_g, k_norm_g,
           conv_w, w0, w_up, a0, a_up, g_up, k_k, k_a, r_k, ln_x_g, ln_x_b,
           router_w, router_b, e_w1, e_b1, e_w2, e_b2, norm_final_g):
    B, T, D = x.shape
    C = ctx.shape[1]
    depth = w_in.shape[0]
    cos, sin = _rope_tables(T)
    rpad = -(-(B + 1) // SUBLANES) * SUBLANES
    cvecs = jnp.zeros((rpad, D), F32).at[:B].set(c).at[B].set(c_ctx)
    s_zero = jnp.zeros((B, 2 * RW_DIM * HEAD_DIM // QUAD_LANES, QUAD_LANES), F32)
    x_lat, x_ctx = x, ctx
    for l in range(depth):
        last = l == depth - 1
        mods = adaln_all(cvecs, w_mod[l], b_mod[l])
        m_lat = [m[:, None, :] for m in jnp.split(mods[:B], 6, axis=-1)]
        m_ctx = [jnp.broadcast_to(m[None, :, :], (B, 1, D)) for m in jnp.split(mods[B:B + 1], 6, axis=-1)]
        sh_m, sc_m, g_m, sh_f, sc_f, g_f = m_lat
        csh_m, csc_m, cg_m, csh_f, csc_f, cg_f = m_ctx
        w_in_l = w_in[l].astype(BF16)
        w_out_l = w_out[l].astype(BF16)
        p_lat = norm_mod_matmul(x_lat, norm_mix_g[l], sc_m, sh_m, w_in_l)
        p_ctx = norm_mod_matmul(x_ctx, norm_mix_g[l], csc_m, csh_m,
                                w_in_l[:, :STATE_COLS] if last else w_in_l)

        ktc, vc = kv_prep(p_ctx, k_norm_g[l])
        ktl, vl = kv_prep(p_lat, k_norm_g[l], cos, sin)
        o_at = attention(p_lat, q_norm_g[l], ktc, vc, ktl, vl, cos, sin)

        rw = (conv_w[l], w0[l], w_up[l], a0[l], a_up[l], k_k[l], k_a[l])
        w2c, k2c, b2c, v_c, kk_c, r_c = wkv_prep(p_ctx, *rw, has_r=not last)
        w2l, k2l, b2l, v_l, kk_l, r_l = wkv_prep(p_lat, *rw, has_r=True)
        of_c, ob_c, s_c = wkv_scan(w2c, k2c, b2c, v_c, kk_c, r_c, s_zero)
        of_l, ob_l, _ = wkv_scan(w2l, k2l, b2l, v_l, kk_l, r_l, s_c)

        ro = (g_up[l], r_k[l], ln_x_g[l], ln_x_b[l], w_out_l)
        x_lat = mixer_out(of_l, ob_l, r_l, v_l, k2l, p_lat, o_at, x_lat, g_m, *ro)
        h_lat, lg_lat = ffn_pre(x_lat, norm_ffn_g[l], sc_f, sh_f, router_w[l], router_b[l])
        e_w1_l = e_w1[l].astype(BF16)
        e_w2_l = e_w2[l].astype(BF16)
        if last:
            f = _moe_ffn(h_lat.reshape(B * T, D), lg_lat.reshape(B * T, -1), e_w1_l, e_b1[l], e_w2_l, e_b2[l])
            x_lat = x_lat + g_f * f.reshape(B, T, D)
        else:
            o_at_c = attention(p_ctx, q_norm_g[l], ktc, vc)
            x_ctx = mixer_out(of_c, ob_c, r_c, v_c, k2c, p_ctx, o_at_c, x_ctx, cg_m, *ro)
            h_ctx, lg_ctx = ffn_pre(x_ctx, norm_ffn_g[l], csc_f, csh_f, router_w[l], router_b[l])
            h_all = jnp.concatenate([h_lat.reshape(B * T, D), h_ctx.reshape(B * C, D)], axis=0)
            lg_all = jnp.concatenate([lg_lat.reshape(B * T, -1), lg_ctx.reshape(B * C, -1)], axis=0)
            f = _moe_ffn(h_all, lg_all, e_w1_l, e_b1[l], e_w2_l, e_b2[l])
            x_lat = x_lat + g_f * f[:B * T].reshape(B, T, D)
            x_ctx = x_ctx + cg_f * f[B * T:].reshape(B, C, D)
    return final_norm(x_lat, norm_final_g)
```

```python
import functools
import math

import numpy as np
import jax
import jax.numpy as jnp
from jax import lax
from jax.experimental import pallas as pl
from jax.experimental.pallas import tpu as pltpu

F32 = jnp.float32
BF16 = jnp.bfloat16
HIGHEST = lax.Precision.HIGHEST

GRID_W = 64
HEAD_DIM = 64
ATTN_HEADS = 8
ATTN_KV_HEADS = 2
GROUP = ATTN_HEADS // ATTN_KV_HEADS
ATTN_DIM = ATTN_HEADS * HEAD_DIM
KV_DIM = ATTN_KV_HEADS * HEAD_DIM
RW_HEADS = 8
RW_DIM = RW_HEADS * HEAD_DIM
W_LORA = 64
A_LORA = 64
G_LORA = 128
CONV_W = 3
N_EXPERTS = 32
TOP_K = 4
SWIGLU_LIMIT = 7.0
SWIGLU_ALPHA = 1.702
ROPE_THETA = 10000.0
ROPE_PAIRS = HEAD_DIM // 4
MOE_BLOCK = 256
NORM_EPS = 1e-6
GN_EPS = 64e-5
ATTN_SCALE = HEAD_DIM ** -0.5
EXP_M05 = math.exp(-0.5)
COL_K_AT = 0
COL_V_AT = KV_DIM
COL_K_RW = 2 * KV_DIM
COL_V_RW = COL_K_RW + RW_DIM
COL_XW = COL_V_RW + RW_DIM
COL_XA = COL_XW + 2 * W_LORA
STATE_COLS = COL_XA + 2 * A_LORA
COL_Q_AT = STATE_COLS
COL_R_RW = COL_Q_AT + ATTN_DIM
COL_XG = COL_R_RW + RW_DIM

LANES = 128
SUBLANES = 8
QUAD_LANES = 256
VMEM_LIMIT = 48 * 1024 * 1024


def _row_tile(n, pref):
    t = pref
    while n % t:
        t //= 2
    return t


def _params(n_axes):
    return pltpu.CompilerParams(dimension_semantics=("arbitrary",) * n_axes, vmem_limit_bytes=VMEM_LIMIT)


def _ones_bd(n):
    ra = lax.broadcasted_iota(jnp.int32, (n, n), 0) // HEAD_DIM
    rb = lax.broadcasted_iota(jnp.int32, (n, n), 1) // HEAD_DIM
    return (ra == rb).astype(BF16)


def _seg_sum(x_bf16, ones_bd):
    return jnp.dot(x_bf16, ones_bd, preferred_element_type=F32)


def _seg_sum2(x, ones_bd):
    hi = x.astype(BF16)
    lo = (x - hi.astype(F32)).astype(BF16)
    return _seg_sum(hi, ones_bd) + _seg_sum(lo, ones_bd)


def _adaln_kernel(c_ref, w_ref, b_ref, o_ref):
    c = c_ref[...]
    s = c * jax.nn.sigmoid(c)
    o_ref[...] = jnp.dot(s, w_ref[...], preferred_element_type=F32, precision=HIGHEST) + b_ref[...]


def adaln_all(cvecs, w_mod, b_mod):
    R, D = cvecs.shape
    N = w_mod.shape[1]
    tn = 512
    return pl.pallas_call(
        _adaln_kernel,
        out_shape=jax.ShapeDtypeStruct((R, N), F32),
        grid=(N // tn,),
        in_specs=[pl.BlockSpec((R, D), lambda j: (0, 0)),
                  pl.BlockSpec((D, tn), lambda j: (0, j)),
                  pl.BlockSpec((1, tn), lambda j: (0, j))],
        out_specs=pl.BlockSpec((R, tn), lambda j: (0, j)),
        compiler_params=_params(1),
        name="adaln",
    )(cvecs, w_mod, b_mod.reshape(1, N))


def _norm_mod(x, g, sc, sh):
    ms = jnp.mean(x * x, axis=-1, keepdims=True)
    y = x * lax.rsqrt(ms + NORM_EPS) * g
    return y * (1.0 + sc) + sh


def _norm_mod_matmul_kernel(x_ref, g_ref, sc_ref, sh_ref, w_ref, o_ref):
    h = _norm_mod(x_ref[0], g_ref[...], sc_ref[0], sh_ref[0])
    o_ref[0] = jnp.dot(h.astype(BF16), w_ref[...], preferred_element_type=F32)


def norm_mod_matmul(x, g, sc, sh, w):
    B, T, D = x.shape
    N = w.shape[1]
    tm = _row_tile(T, 256)
    return pl.pallas_call(
        _norm_mod_matmul_kernel,
        out_shape=jax.ShapeDtypeStruct((B, T, N), F32),
        grid=(B, T // tm),
        in_specs=[pl.BlockSpec((1, tm, D), lambda b, i: (b, i, 0)),
                  pl.BlockSpec((1, D), lambda b, i: (0, 0)),
                  pl.BlockSpec((1, 1, D), lambda b, i: (b, 0, 0)),
                  pl.BlockSpec((1, 1, D), lambda b, i: (b, 0, 0)),
                  pl.BlockSpec((D, N), lambda b, i: (0, 0))],
        out_specs=pl.BlockSpec((1, tm, N), lambda b, i: (b, i, 0)),
        compiler_params=_params(2),
        name="norm_mod_matmul",
    )(x, g.reshape(1, D), sc, sh, w)


def _pair_rms(x, g):
    lane = lax.broadcasted_iota(jnp.int32, x.shape, 1)
    first = lane < HEAD_DIM
    xx = x * x
    s0 = jnp.sum(jnp.where(first, xx, 0.0), axis=1, keepdims=True)
    s1 = jnp.sum(jnp.where(first, 0.0, xx), axis=1, keepdims=True)
    ms = jnp.where(first, s0, s1) * (1.0 / HEAD_DIM)
    return x * lax.rsqrt(ms + NORM_EPS) * g


def _pair_rope(x, cos, sin):
    lane = lax.broadcasted_iota(jnp.int32, x.shape, 1)
    partner = jnp.where(lane % (2 * ROPE_PAIRS) < ROPE_PAIRS,
                        pltpu.roll(x, LANES - ROPE_PAIRS, 1), pltpu.roll(x, ROPE_PAIRS, 1))
    return x * cos + partner * sin


def _rope_tables(T):
    rows = T // GRID_W
    row = jnp.repeat(jnp.arange(rows), GRID_W)
    col = jnp.tile(jnp.arange(GRID_W), rows)
    pos = jnp.stack([row, col], axis=-1).astype(F32)
    freqs = ROPE_THETA ** (-jnp.arange(ROPE_PAIRS, dtype=F32) / ROPE_PAIRS)
    ang = pos[:, :, None] * freqs
    cos, sin = jnp.cos(ang), jnp.sin(ang)
    cos_h = jnp.concatenate([cos[:, 0], cos[:, 0], cos[:, 1], cos[:, 1]], axis=-1)
    sin_h = jnp.concatenate([-sin[:, 0], sin[:, 0], -sin[:, 1], sin[:, 1]], axis=-1)
    return jnp.tile(cos_h, (1, 2)), jnp.tile(sin_h, (1, 2))


def _kv_prep_kernel(*refs, rope):
    if rope:
        p_ref, g_ref, cos_ref, sin_ref, kt_ref, v_ref = refs
    else:
        p_ref, g_ref, kt_ref, v_ref = refs
    x = p_ref[0]
    k = _pair_rms(x[:, :KV_DIM], g_ref[...])
    if rope:
        k = _pair_rope(k, cos_ref[...], sin_ref[...])
    v = x[:, KV_DIM:]
    lane = lax.broadcasted_iota(jnp.int32, k.shape, 1)
    first = lane < HEAD_DIM
    kr = pltpu.roll(k, HEAD_DIM, 1)
    vr = pltpu.roll(v, HEAD_DIM, 1)
    kt_ref[0, 0] = jnp.where(first, k, kr).T.astype(BF16)
    kt_ref[0, 1] = jnp.where(first, kr, k).T.astype(BF16)
    v_ref[0, 0] = jnp.where(first, v, vr).astype(BF16)
    v_ref[0, 1] = jnp.where(first, vr, v).astype(BF16)


def kv_prep(p, k_norm_g, cos=None, sin=None):
    B, T, _ = p.shape
    tm = _row_tile(T, 256)
    rope = cos is not None
    g2 = jnp.tile(k_norm_g, 2).reshape(1, LANES)
    in_specs = [pl.BlockSpec((1, tm, 2 * KV_DIM), lambda b, i: (b, i, 0)),
                pl.BlockSpec((1, LANES), lambda b, i: (0, 0))]
    args = [p, g2]
    if rope:
        in_specs += [pl.BlockSpec((tm, LANES), lambda b, i: (i, 0))] * 2
        args += [cos, sin]
    return pl.pallas_call(
        functools.partial(_kv_prep_kernel, rope=rope),
        out_shape=(jax.ShapeDtypeStruct((B, ATTN_KV_HEADS, LANES, T), BF16),
                   jax.ShapeDtypeStruct((B, ATTN_KV_HEADS, T, LANES), BF16)),
        grid=(B, T // tm),
        in_specs=in_specs,
        out_specs=(pl.BlockSpec((1, ATTN_KV_HEADS, LANES, tm), lambda b, i: (b, 0, 0, i)),
                   pl.BlockSpec((1, ATTN_KV_HEADS, tm, LANES), lambda b, i: (b, 0, i, 0))),
        compiler_params=_params(2),
        name="kv_prep",
    )(*args)


def _attn_kernel(*refs, tq, tkl, nkl, rope, has_lat):
    refs = list(refs)
    p_ref, g_ref = refs[:2]
    refs = refs[2:]
    if rope:
        cos_ref, sin_ref = refs[:2]
        refs = refs[2:]
    ktc_ref, vc_ref = refs[:2]
    refs = refs[2:]
    if has_lat:
        ktl_ref, vl_ref = refs[:2]
        refs = refs[2:]
    o_ref, m_ref, l_ref, acc_ref = refs

    q = _pair_rms(p_ref[0], g_ref[...])
    if rope:
        q = _pair_rope(q, cos_ref[...], sin_ref[...])
    q = q * ATTN_SCALE
    lane = lax.broadcasted_iota(jnp.int32, (tq, LANES), 1)
    first = lane < HEAD_DIM
    qq = jnp.concatenate([jnp.where(first, q, 0.0), jnp.where(first, 0.0, q)], axis=0).astype(BF16)

    m_ref[...] = jnp.full(m_ref.shape, -jnp.inf, F32)
    l_ref[...] = jnp.zeros(l_ref.shape, F32)
    acc_ref[...] = jnp.zeros(acc_ref.shape, F32)

    def chunk(kt, vv):
        s = jnp.dot(qq, kt, preferred_element_type=F32)
        m_old = m_ref[...]
        m_new = jnp.maximum(m_old, jnp.max(s, axis=-1, keepdims=True))
        alpha = jnp.exp(m_old - m_new)
        p = jnp.exp(s - jnp.concatenate([m_new] * (s.shape[1] // LANES), axis=1))
        l_ref[...] = alpha * l_ref[...] + jnp.sum(p, axis=-1, keepdims=True)
        acc_ref[...] = alpha * acc_ref[...] + jnp.dot(p.astype(BF16), vv, preferred_element_type=F32)
        m_ref[...] = m_new

    chunk(ktc_ref[0, 0], vc_ref[0, 0])
    if has_lat:
        def body(j, carry):
            ks = pl.multiple_of(j * tkl, tkl)
            chunk(ktl_ref[0, 0, :, pl.ds(ks, tkl)], vl_ref[0, 0, pl.ds(ks, tkl), :])
            return carry
        lax.fori_loop(0, nkl, body, 0)

    o = acc_ref[...] / l_ref[...]
    o_ref[0] = jnp.where(first, o[:tq], o[tq:])


def attention(p, q_norm_g, ktc, vc, ktl=None, vl=None, cos=None, sin=None):
    B, T, _ = p.shape
    C = ktc.shape[-1]
    rope = cos is not None
    has_lat = ktl is not None
    tq = _row_tile(T, 256)
    pairs = ATTN_HEADS // 2
    pairs_per_kv = GROUP // 2
    qblk0 = COL_Q_AT // LANES
    g2 = jnp.tile(q_norm_g, 2).reshape(1, LANES)
    in_specs = [pl.BlockSpec((1, tq, LANES), lambda b, h, i: (b, i, qblk0 + h)),
                pl.BlockSpec((1, LANES), lambda b, h, i: (0, 0))]
    args = [p, g2]
    if rope:
        in_specs += [pl.BlockSpec((tq, LANES), lambda b, h, i: (i, 0))] * 2
        args += [cos, sin]
    in_specs += [pl.BlockSpec((1, 1, LANES, C), lambda b, h, i: (b, h // pairs_per_kv, 0, 0)),
                 pl.BlockSpec((1, 1, C, LANES), lambda b, h, i: (b, h // pairs_per_kv, 0, 0))]
    args += [ktc, vc]
    tkl = nkl = 0
    if has_lat:
        S = ktl.shape[-1]
        tkl = _row_tile(S, 1024)
        nkl = S // tkl
        in_specs += [pl.BlockSpec((1, 1, LANES, S), lambda b, h, i: (b, h // pairs_per_kv, 0, 0)),
                     pl.BlockSpec((1, 1, S, LANES), lambda b, h, i: (b, h // pairs_per_kv, 0, 0))]
        args += [ktl, vl]
    kern = functools.partial(_attn_kernel, tq=tq, tkl=tkl, nkl=nkl, rope=rope, has_lat=has_lat)
    return pl.pallas_call(
        kern,
        out_shape=jax.ShapeDtypeStruct((B, T, ATTN_DIM), F32),
        grid=(B, pairs, T // tq),
        in_specs=in_specs,
        out_specs=pl.BlockSpec((1, tq, LANES), lambda b, h, i: (b, i, h)),
        scratch_shapes=[pltpu.VMEM((2 * tq, LANES), F32)] * 3,
        compiler_params=_params(3),
        name="attention",
    )(*args)


def _wkv_prep_kernel(p_ref, pp_ref, pn_ref, cw_ref, w0_ref, wup_ref, a0_ref, aup_ref, kk_ref, ka_ref,
                     w2_ref, k2_ref, b2_ref, v_ref, kkn_ref, r_ref, *, tm, has_r):
    i = pl.program_id(1)
    n = pl.num_programs(1)
    x = p_ref[0]
    prev = pp_ref[0][SUBLANES - 1:SUBLANES, :] * (i > 0).astype(F32)
    nxt = pn_ref[0][0:1, :] * (i < n - 1).astype(F32)
    row = lax.broadcasted_iota(jnp.int32, (tm, RW_DIM), 0)

    def conv(col, wcol):
        u = x[:, col:col + RW_DIM]
        up = jnp.where(row == 0, prev[:, col:col + RW_DIM], pltpu.roll(u, 1, 0))
        un = jnp.where(row == tm - 1, nxt[:, col:col + RW_DIM], pltpu.roll(u, tm - 1, 0))
        w = cw_ref[:, wcol:wcol + RW_DIM]
        return up * w[0:1] + u * w[1:2] + un * w[2:3]

    k = conv(COL_K_RW, RW_DIM)
    v = conv(COL_V_RW, 2 * RW_DIM)
    v_ref[0] = v
    r_ref[0] = conv(COL_R_RW, 0) if has_r else jnp.zeros((tm, RW_DIM), F32)
    kk = k * kk_ref[...]
    ss = _seg_sum2(kk * kk, _ones_bd(RW_DIM))
    kk = kk * lax.rsqrt(jnp.maximum(ss, 1e-24))
    kkn_ref[0] = kk
    xw = jnp.tanh(x[:, COL_XW:COL_XW + 2 * W_LORA])
    xa = x[:, COL_XA:COL_XA + 2 * A_LORA]
    for d in range(2):
        wl = w0_ref[d:d + 1, :] + jnp.dot(xw[:, d * W_LORA:(d + 1) * W_LORA], wup_ref[d],
                                          preferred_element_type=F32, precision=HIGHEST)
        decay = jnp.exp(-EXP_M05 * jax.nn.sigmoid(wl))
        a = jax.nn.sigmoid(a0_ref[d:d + 1, :] + jnp.dot(xa[:, d * A_LORA:(d + 1) * A_LORA], aup_ref[d],
                                                        preferred_element_type=F32, precision=HIGHEST))
        sl = slice(d * RW_DIM, (d + 1) * RW_DIM)
        w2_ref[0, :, sl] = decay
        k2_ref[0, :, sl] = k * (1.0 + (a - 1.0) * ka_ref[...])
        b2_ref[0, :, sl] = kk * a


def wkv_prep(p, conv_w, w0, w_up, a0, a_up, k_k, k_a, has_r):
    B, T, NP = p.shape
    tm = _row_tile(T, 256)
    nb8 = tm // SUBLANES
    last8 = T // SUBLANES - 1
    full = lambda a: pl.BlockSpec(a.shape, lambda b, i: (0,) * a.ndim)
    kk2, ka2 = k_k.reshape(1, RW_DIM), k_a.reshape(1, RW_DIM)
    wide = jax.ShapeDtypeStruct((B, T, 2 * RW_DIM), F32)
    narrow = jax.ShapeDtypeStruct((B, T, RW_DIM), F32)
    wspec = pl.BlockSpec((1, tm, 2 * RW_DIM), lambda b, i: (b, i, 0))
    nspec = pl.BlockSpec((1, tm, RW_DIM), lambda b, i: (b, i, 0))
    return pl.pallas_call(
        functools.partial(_wkv_prep_kernel, tm=tm, has_r=has_r),
        out_shape=(wide, wide, wide, narrow, narrow, narrow),
        grid=(B, T // tm),
        in_specs=[pl.BlockSpec((1, tm, NP), lambda b, i: (b, i, 0)),
                  pl.BlockSpec((1, SUBLANES, NP), lambda b, i: (b, jnp.maximum(i * nb8 - 1, 0), 0)),
                  pl.BlockSpec((1, SUBLANES, NP), lambda b, i: (b, jnp.minimum((i + 1) * nb8, last8), 0)),
                  full(conv_w), full(w0), full(w_up), full(a0), full(a_up), full(kk2), full(ka2)],
        out_specs=(wspec, wspec, wspec, nspec, nspec, nspec),
        compiler_params=_params(2),
        name="wkv_prep",
    )(p, p, p, conv_w, w0, w_up, a0, a_up, kk2, ka2)


def _wkv_kernel(wf_ref, wb_ref, kf_ref, kb_ref, bf_ref, bb_ref, vf_ref, vb_ref, af_ref, ab_ref,
                rf_ref, rb_ref, s0_ref, of_ref, ob_ref, st_ref, s_ref, *, tc, nquads):
    tb = pl.program_id(1)

    @pl.when(tb == 0)
    def _():
        s_ref[...] = s0_ref[0]

    rows_all = 2 * nquads * HEAD_DIM
    shp = (rows_all, QUAD_LANES)
    lane = lax.broadcasted_iota(jnp.int32, shp, 1)
    sub = lax.broadcasted_iota(jnp.int32, shp, 0)
    eye = (lane % HEAD_DIM == sub % HEAD_DIM).astype(F32)
    ones_bd = _ones_bd(QUAD_LANES)
    ng = tc // SUBLANES

    def bcast_rows(blks, j):
        parts = []
        for d in range(2):
            jj = j if d == 0 else SUBLANES - 1 - j
            for q in range(nquads):
                row = blks[d][jj:jj + 1, q * QUAD_LANES:(q + 1) * QUAD_LANES]
                parts.append(jnp.broadcast_to(row, (HEAD_DIM, QUAD_LANES)))
        return jnp.concatenate(parts, axis=0)

    def group(g, carry):
        tf = pl.multiple_of(g * SUBLANES, SUBLANES)
        tr = pl.multiple_of((ng - 1 - g) * SUBLANES, SUBLANES)
        ld = lambda fr, br: [fr[0, pl.ds(tf, SUBLANES), :], br[0, pl.ds(tr, SUBLANES), :]]
        wq, kq, bq, vq, rq = (ld(f, b) for f, b in ((wf_ref, wb_ref), (kf_ref, kb_ref), (bf_ref, bb_ref),
                                                    (vf_ref, vb_ref), (rf_ref, rb_ref)))
        aq = [-x for x in ld(af_ref, ab_ref)]
        S = s_ref[...]
        out_rows = [[] for _ in range(2 * nquads)]
        for j in range(SUBLANES):
            wt, kt, vt, at, bt, rt = (bcast_rows(x, j) for x in (wq, kq, vq, aq, bq, rq))
            sa = _seg_sum2(S * at, ones_bd)
            vcol = _seg_sum((eye * vt).astype(BF16), ones_bd)
            S = S * wt + sa * bt + vcol * kt
            ob = _seg_sum((S * rt).astype(BF16), ones_bd) * eye
            for c in range(2 * nquads):
                out_rows[c].append(jnp.sum(ob[c * HEAD_DIM:(c + 1) * HEAD_DIM], axis=0, keepdims=True))
        s_ref[...] = S
        for q in range(nquads):
            sl = slice(q * QUAD_LANES, (q + 1) * QUAD_LANES)
            of_ref[0, pl.ds(tf, SUBLANES), sl] = jnp.concatenate(out_rows[q], axis=0)
            ob_ref[0, pl.ds(tr, SUBLANES), sl] = jnp.concatenate(out_rows[nquads + q][::-1], axis=0)
        return carry

    lax.fori_loop(0, ng, group, 0)

    @pl.when(tb == pl.num_programs(1) - 1)
    def _():
        st_ref[0] = s_ref[...]


def wkv_scan(w2, k2, b2, v, kk, r, s0):
    B, T, C = v.shape
    nquads = C // QUAD_LANES
    tc = _row_tile(T, 128)
    nt = T // tc
    fwd = lambda col: pl.BlockSpec((1, tc, C), lambda b, t: (b, t, col))
    bwd = lambda col: pl.BlockSpec((1, tc, C), lambda b, t: (b, nt - 1 - t, col))
    sspec = pl.BlockSpec((1,) + s0.shape[1:], lambda b, t: (b, 0, 0))
    kern = functools.partial(_wkv_kernel, tc=tc, nquads=nquads)
    return pl.pallas_call(
        kern,
        out_shape=(jax.ShapeDtypeStruct((B, T, C), F32), jax.ShapeDtypeStruct((B, T, C), F32),
                   jax.ShapeDtypeStruct(s0.shape, F32)),
        grid=(B, nt),
        in_specs=[fwd(0), bwd(1), fwd(0), bwd(1), fwd(0), bwd(1), fwd(0), bwd(0), fwd(0), bwd(0),
                  fwd(0), bwd(0), sspec],
        out_specs=(fwd(0), bwd(0), sspec),
        scratch_shapes=[pltpu.VMEM(s0.shape[1:], F32)],
        compiler_params=_params(2),
        name="wkv_scan",
    )(w2, w2, k2, k2, b2, b2, v, v, kk, kk, r, r, s0)


def _mixer_out_kernel(of_ref, ob_ref, r_ref, v_ref, k2_ref, xg_ref, oat_ref, x_ref, gm_ref,
                      gup_ref, rk_ref, lng_ref, lnb_ref, wo_ref, o_ref):
    ones = _ones_bd(RW_DIM)
    inv = 1.0 / HEAD_DIM
    o = of_ref[0] + ob_ref[0]
    mu = _seg_sum2(o, ones) * inv
    dv = o - mu
    var = _seg_sum2(dv * dv, ones) * inv
    gn = dv * lax.rsqrt(var + GN_EPS) * lng_ref[...] + lnb_ref[...]
    k2 = k2_ref[0]
    k_bonus = (k2[:, :RW_DIM] + k2[:, RW_DIM:]) * 0.5
    bonus = _seg_sum2(r_ref[0] * k_bonus * rk_ref[...], ones) * v_ref[0]
    g = jnp.dot(jax.nn.sigmoid(xg_ref[0]), gup_ref[...], preferred_element_type=F32, precision=HIGHEST)
    y = (gn + bonus) * g
    acc = jnp.dot(oat_ref[0].astype(BF16), wo_ref[:ATTN_DIM, :], preferred_element_type=F32)
    acc = acc + jnp.dot(y.astype(BF16), wo_ref[ATTN_DIM:, :], preferred_element_type=F32)
    o_ref[0] = x_ref[0] + gm_ref[0] * acc


def mixer_out(o_f, o_b, r, v, k2, p, o_at, x, g_m, g_up, r_k, ln_x_g, ln_x_b, w_out):
    B, T, D = x.shape
    tm = _row_tile(T, 256)
    nspec = pl.BlockSpec((1, tm, RW_DIM), lambda b, i: (b, i, 0))
    full = lambda a: pl.BlockSpec(a.shape, lambda b, i: (0,) * a.ndim)
    rk2, lng2, lnb2 = r_k.reshape(1, RW_DIM), ln_x_g.reshape(1, RW_DIM), ln_x_b.reshape(1, RW_DIM)
    return pl.pallas_call(
        _mixer_out_kernel,
        out_shape=jax.ShapeDtypeStruct((B, T, D), F32),
        grid=(B, T // tm),
        in_specs=[nspec, nspec, nspec, nspec,
                  pl.BlockSpec((1, tm, 2 * RW_DIM), lambda b, i: (b, i, 0)),
                  pl.BlockSpec((1, tm, G_LORA), lambda b, i: (b, i, COL_XG // G_LORA)),
                  pl.BlockSpec((1, tm, ATTN_DIM), lambda b, i: (b, i, 0)),
                  pl.BlockSpec((1, tm, D), lambda b, i: (b, i, 0)),
                  pl.BlockSpec((1, 1, D), lambda b, i: (b, 0, 0)),
                  full(g_up), full(rk2), full(lng2), full(lnb2), full(w_out)],
        out_specs=pl.BlockSpec((1, tm, D), lambda b, i: (b, i, 0)),
        compiler_params=_params(2),
        name="mixer_out",
    )(o_f, o_b, r, v, k2, p, o_at, x, g_m, g_up, rk2, lng2, lnb2, w_out)


def _start_row_copies(idx_ref, n, src_hbm, dst_ref, sem):
    def body(r, carry):
        pltpu.make_async_copy(src_hbm.at[pl.ds(idx_ref[0, 0, r], 1)], dst_ref.at[pl.ds(r, 1)], sem).start()
        return carry
    lax.fori_loop(0, n, body, 0, unroll=8)


def _wait_row_copies(n, src_hbm, dst_ref, sem):
    pltpu.make_async_copy(src_hbm.at[pl.ds(0, n)], dst_ref.at[pl.ds(0, n)], sem).wait()


def _moe_kernel(be_ref, tok_ref, tokn_ref, h_hbm, w1_ref, b1_ref, w2_ref, b2_ref, o_ref, buf, sems, *, f, tm):
    i = pl.program_id(0)
    slot = i % 2

    @pl.when(i == 0)
    def _():
        _start_row_copies(tok_ref, tm, h_hbm, buf.at[0], sems.at[0])

    @pl.when(i + 1 < pl.num_programs(0))
    def _():
        _start_row_copies(tokn_ref, tm, h_hbm, buf.at[1 - slot], sems.at[1 - slot])

    _wait_row_copies(tm, h_hbm, buf.at[slot], sems.at[slot])
    u = jnp.dot(buf[slot].astype(BF16), w1_ref[0], preferred_element_type=F32) + b1_ref[0]
    x_glu = jnp.minimum(u[:, :f], SWIGLU_LIMIT)
    x_lin = jnp.clip(u[:, f:], -SWIGLU_LIMIT, SWIGLU_LIMIT)
    act = x_glu * jax.nn.sigmoid(SWIGLU_ALPHA * x_glu) * (x_lin + 1.0)
    o_ref[...] = jnp.dot(act.astype(BF16), w2_ref[0], preferred_element_type=F32) + b2_ref[0]


def moe_blocks(block_e, slot_tok, h, w1, b1, w2, b2):
    N, D = h.shape
    E, _, F2 = w1.shape
    f = F2 // 2
    tm = MOE_BLOCK
    rows = slot_tok.shape[0]
    nb = rows // tm
    tok3 = slot_tok.reshape(nb, 1, tm)
    kern = functools.partial(_moe_kernel, f=f, tm=tm)
    grid_spec = pltpu.PrefetchScalarGridSpec(
        num_scalar_prefetch=1,
        grid=(nb,),
        in_specs=[pl.BlockSpec((1, 1, tm), lambda i, be: (i, 0, 0), memory_space=pltpu.SMEM),
                  pl.BlockSpec((1, 1, tm), lambda i, be: (jnp.minimum(i + 1, nb - 1), 0, 0),
                               memory_space=pltpu.SMEM),
                  pl.BlockSpec(memory_space=pl.ANY),
                  pl.BlockSpec((1, D, F2), lambda i, be: (be[i], 0, 0)),
                  pl.BlockSpec((1, 1, F2), lambda i, be: (be[i], 0, 0)),
                  pl.BlockSpec((1, f, D), lambda i, be: (be[i], 0, 0)),
                  pl.BlockSpec((1, 1, D), lambda i, be: (be[i], 0, 0))],
        out_specs=pl.BlockSpec((tm, D), lambda i, be: (i, 0)),
        scratch_shapes=[pltpu.VMEM((2, tm, D), F32), pltpu.SemaphoreType.DMA((2,))],
    )
    return pl.pallas_call(
        kern,
        out_shape=jax.ShapeDtypeStruct((rows, D), F32),
        grid_spec=grid_spec,
        compiler_params=_params(1),
        name="moe_blocks",
    )(block_e, tok3, tok3, h, w1, b1.reshape(E, 1, F2), w2, b2.reshape(E, 1, D))


def _moe_combine_kernel(sl_ref, sln_ref, out_hbm, gate_ref, x_ref, gf_ref, o_ref, buf, sems, *, tb):
    b, i = pl.program_id(0), pl.program_id(1)
    step = b * pl.num_programs(1) + i
    nsteps = pl.num_programs(0) * pl.num_programs(1)
    slot = step % 2
    n = TOP_K * tb

    @pl.when(step == 0)
    def _():
        _start_row_copies(sl_ref, n, out_hbm, buf.at[0], sems.at[0])

    @pl.when(step + 1 < nsteps)
    def _():
        _start_row_copies(sln_ref, n, out_hbm, buf.at[1 - slot], sems.at[1 - slot])

    _wait_row_copies(n, out_hbm, buf.at[slot], sems.at[slot])
    gate = gate_ref[0]
    y = jnp.zeros(x_ref.shape[1:], F32)
    for k in range(TOP_K):
        y = y + gate[:, k:k + 1] * buf[slot, k * tb:(k + 1) * tb, :]
    o_ref[0] = x_ref[0] + gf_ref[0] * y


def moe_combine(out, slots, gates, x, g_f):
    B, T, D = x.shape
    tb = _row_tile(T, 64)
    nt = T // tb
    n = TOP_K * tb
    sl3 = jnp.transpose(slots.reshape(B * nt, tb, TOP_K), (0, 2, 1)).reshape(B * nt, 1, n)
    last = B * nt - 1
    return pl.pallas_call(
        functools.partial(_moe_combine_kernel, tb=tb),
        out_shape=jax.ShapeDtypeStruct((B, T, D), F32),
        grid=(B, nt),
        in_specs=[pl.BlockSpec((1, 1, n), lambda b, i: (b * nt + i, 0, 0), memory_space=pltpu.SMEM),
                  pl.BlockSpec((1, 1, n), lambda b, i: (jnp.minimum(b * nt + i + 1, last), 0, 0),
                               memory_space=pltpu.SMEM),
                  pl.BlockSpec(memory_space=pl.ANY),
                  pl.BlockSpec((1, tb, LANES), lambda b, i: (b, i, 0)),
                  pl.BlockSpec((1, tb, D), lambda b, i: (b, i, 0)),
                  pl.BlockSpec((1, 1, D), lambda b, i: (b, 0, 0))],
        out_specs=pl.BlockSpec((1, tb, D), lambda b, i: (b, i, 0)),
        scratch_shapes=[pltpu.VMEM((2, n, D), F32), pltpu.SemaphoreType.DMA((2,))],
        compiler_params=_params(2),
        name="moe_combine",
    )(sl3, sl3, out, gates, x, g_f)


def _ffn_pre_kernel(x_ref, g_ref, sc_ref, sh_ref, rw_ref, rb_ref, h_ref, te_ref, tg_ref):
    h = _norm_mod(x_ref[0], g_ref[...], sc_ref[0], sh_ref[0])
    h_ref[0] = h
    cur = jnp.dot(h, rw_ref[...], preferred_element_type=F32, precision=HIGHEST) + rb_ref[...]
    n_e = cur.shape[1]
    lane_e = lax.broadcasted_iota(jnp.int32, cur.shape, 1).astype(F32)
    lane = lax.broadcasted_iota(jnp.int32, te_ref.shape[1:], 1)
    vals, idxs = [], []
    for k in range(TOP_K):
        m = jnp.max(cur, axis=1, keepdims=True)
        idx = jnp.min(jnp.where(cur == m, lane_e, float(n_e)), axis=1, keepdims=True)
        vals.append(m)
        idxs.append(idx)
        cur = jnp.where(lane_e == idx, -jnp.inf, cur)
    ex = [jnp.exp(v - vals[0]) for v in vals]
    denom = ex[0]
    for e in ex[1:]:
        denom = denom + e
    te = jnp.zeros(lane.shape, F32)
    tg = jnp.zeros(lane.shape, F32)
    for k in range(TOP_K):
        te = jnp.where(lane == k, idxs[k], te)
        tg = jnp.where(lane == k, ex[k] / denom, tg)
    te_ref[0] = te.astype(jnp.int32)
    tg_ref[0] = tg


def ffn_pre(x, g, sc, sh, router_w, router_b):
    B, T, D = x.shape
    E = router_w.shape[1]
    tm = _row_tile(T, 256)
    return pl.pallas_call(
        _ffn_pre_kernel,
        out_shape=(jax.ShapeDtypeStruct((B, T, D), F32), jax.ShapeDtypeStruct((B, T, LANES), jnp.int32),
                   jax.ShapeDtypeStruct((B, T, LANES), F32)),
        grid=(B, T // tm),
        in_specs=[pl.BlockSpec((1, tm, D), lambda b, i: (b, i, 0)),
                  pl.BlockSpec((1, D), lambda b, i: (0, 0)),
                  pl.BlockSpec((1, 1, D), lambda b, i: (b, 0, 0)),
                  pl.BlockSpec((1, 1, D), lambda b, i: (b, 0, 0)),
                  pl.BlockSpec((D, E), lambda b, i: (0, 0)),
                  pl.BlockSpec((1, E), lambda b, i: (0, 0))],
        out_specs=(pl.BlockSpec((1, tm, D), lambda b, i: (b, i, 0)),
                   pl.BlockSpec((1, tm, LANES), lambda b, i: (b, i, 0)),
                   pl.BlockSpec((1, tm, LANES), lambda b, i: (b, i, 0))),
        compiler_params=_params(2),
        name="ffn_pre",
    )(x, g.reshape(1, D), sc, sh, router_w, router_b.reshape(1, E))


def _rms_kernel(x_ref, g_ref, o_ref):
    x = x_ref[0]
    ms = jnp.mean(x * x, axis=-1, keepdims=True)
    o_ref[0] = x * lax.rsqrt(ms + NORM_EPS) * g_ref[...]


def final_norm(x, g):
    B, T, D = x.shape
    tm = _row_tile(T, 512)
    return pl.pallas_call(
        _rms_kernel,
        out_shape=jax.ShapeDtypeStruct((B, T, D), F32),
        grid=(B, T // tm),
        in_specs=[pl.BlockSpec((1, tm, D), lambda b, i: (b, i, 0)),
                  pl.BlockSpec((1, D), lambda b, i: (0, 0))],
        out_specs=pl.BlockSpec((1, tm, D), lambda b, i: (b, i, 0)),
        compiler_params=_params(2),
        name="final_norm",
    )(x, g.reshape(1, D))


def _route(top_e):
    N = top_e.shape[0]
    n4 = N * TOP_K
    flat_e = top_e.reshape(-1)
    order = jnp.argsort(flat_e)
    e_sorted = flat_e[order]
    counts = jnp.bincount(flat_e, length=N_EXPERTS)
    padded = (counts + MOE_BLOCK - 1) // MOE_BLOCK * MOE_BLOCK
    start = jnp.cumsum(counts) - counts
    pstart = jnp.cumsum(padded) - padded
    dest = (pstart[e_sorted] + jnp.arange(n4, dtype=jnp.int32) - start[e_sorted]).astype(jnp.int32)
    n_blocks = (n4 + MOE_BLOCK - 1) // MOE_BLOCK + N_EXPERTS
    rows = n_blocks * MOE_BLOCK
    slot_tok = jnp.zeros((rows,), jnp.int32).at[dest].set((order // TOP_K).astype(jnp.int32))
    slot_flat = jnp.zeros((n4,), jnp.int32).at[order].set(dest)
    block_e = jnp.minimum(jnp.searchsorted(jnp.cumsum(padded), jnp.arange(n_blocks) * MOE_BLOCK, side='right'),
                          N_EXPERTS - 1).astype(jnp.int32)
    return slot_tok, block_e, slot_flat.reshape(N, TOP_K)


def kernel(x, c, ctx, c_ctx, norm_mix_g, norm_ffn_g, w_mod, b_mod, w_in, w_out, q_norm_g, k_norm_g,
           conv_w, w0, w_up, a0, a_up, g_up, k_k, k_a, r_k, ln_x_g, ln_x_b,
           router_w, router_b, e_w1, e_b1, e_w2, e_b2, norm_final_g):
    B, T, D = x.shape
    C = ctx.shape[1]
    depth = w_in.shape[0]
    cos, sin = _rope_tables(T)
    rpad = -(-(B + 1) // SUBLANES) * SUBLANES
    cvecs = jnp.zeros((rpad, D), F32).at[:B].set(c).at[B].set(c_ctx)
    s_zero = jnp.zeros((B, 2 * RW_DIM * HEAD_DIM // QUAD_LANES, QUAD_LANES), F32)
    x_lat, x_ctx = x, ctx
    for l in range(depth):
        last = l == depth - 1
        mods = adaln_all(cvecs, w_mod[l], b_mod[l])
        m_lat = [m[:, None, :] for m in jnp.split(mods[:B], 6, axis=-1)]
        m_ctx = [jnp.broadcast_to(m[None, :, :], (B, 1, D)) for m in jnp.split(mods[B:B + 1], 6, axis=-1)]
        sh_m, sc_m, g_m, sh_f, sc_f, g_f = m_lat
        csh_m, csc_m, cg_m, csh_f, csc_f, cg_f = m_ctx
        w_in_l = w_in[l].astype(BF16)
        w_out_l = w_out[l].astype(BF16)
        p_lat = norm_mod_matmul(x_lat, norm_mix_g[l], sc_m, sh_m, w_in_l)
        p_ctx = norm_mod_matmul(x_ctx, norm_mix_g[l], csc_m, csh_m,
                                w_in_l[:, :STATE_COLS] if last else w_in_l)

        ktc, vc = kv_prep(p_ctx, k_norm_g[l])
        ktl, vl = kv_prep(p_lat, k_norm_g[l], cos, sin)
        o_at = attention(p_lat, q_norm_g[l], ktc, vc, ktl, vl, cos, sin)

        rw = (conv_w[l], w0[l], w_up[l], a0[l], a_up[l], k_k[l], k_a[l])
        w2c, k2c, b2c, v_c, kk_c, r_c = wkv_prep(p_ctx, *rw, has_r=not last)
        w2l, k2l, b2l, v_l, kk_l, r_l = wkv_prep(p_lat, *rw, has_r=True)
        of_c, ob_c, s_c = wkv_scan(w2c, k2c, b2c, v_c, kk_c, r_c, s_zero)
        of_l, ob_l, _ = wkv_scan(w2l, k2l, b2l, v_l, kk_l, r_l, s_c)

        ro = (g_up[l], r_k[l], ln_x_g[l], ln_x_b[l], w_out_l)
        x_lat = mixer_out(of_l, ob_l, r_l, v_l, k2l, p_lat, o_at, x_lat, g_m, *ro)
        h_lat, te_lat, tg_lat = ffn_pre(x_lat, norm_ffn_g[l], sc_f, sh_f, router_w[l], router_b[l])
        e_w1_l = e_w1[l].astype(BF16)
        e_w2_l = e_w2[l].astype(BF16)
        if last:
            slot_tok, block_e, slot_flat = _route(te_lat.reshape(B * T, LANES)[:, :TOP_K])
            out = moe_blocks(block_e, slot_tok, h_lat.reshape(B * T, D), e_w1_l, e_b1[l], e_w2_l, e_b2[l])
            x_lat = moe_combine(out, slot_flat.reshape(B, T, TOP_K), tg_lat, x_lat, g_f)
        else:
            o_at_c = attention(p_ctx, q_norm_g[l], ktc, vc)
            x_ctx = mixer_out(of_c, ob_c, r_c, v_c, k2c, p_ctx, o_at_c, x_ctx, cg_m, *ro)
            h_ctx, te_ctx, tg_ctx = ffn_pre(x_ctx, norm_ffn_g[l], csc_f, csh_f, router_w[l], router_b[l])
            h_all = jnp.concatenate([h_lat.reshape(B * T, D), h_ctx.reshape(B * C, D)], axis=0)
            te_all = jnp.concatenate([te_lat.reshape(B * T, LANES), te_ctx.reshape(B * C, LANES)], axis=0)
            slot_tok, block_e, slot_flat = _route(te_all[:, :TOP_K])
            out = moe_blocks(block_e, slot_tok, h_all, e_w1_l, e_b1[l], e_w2_l, e_b2[l])
            x_lat = moe_combine(out, slot_flat[:B * T].reshape(B, T, TOP_K), tg_lat, x_lat, g_f)
            x_ctx = moe_combine(out, slot_flat[B * T:].reshape(B, C, TOP_K), tg_ctx, x_ctx, cg_f)
    return final_norm(x_lat, norm_final_g)
```

```python
import functools
import math

import numpy as np
import jax
import jax.numpy as jnp
from jax import lax
from jax.experimental import pallas as pl
from jax.experimental.pallas import tpu as pltpu

F32 = jnp.float32
BF16 = jnp.bfloat16
HIGHEST = lax.Precision.HIGHEST

GRID_W = 64
HEAD_DIM = 64
ATTN_HEADS = 8
ATTN_KV_HEADS = 2
GROUP = ATTN_HEADS // ATTN_KV_HEADS
ATTN_DIM = ATTN_HEADS * HEAD_DIM
KV_DIM = ATTN_KV_HEADS * HEAD_DIM
RW_HEADS = 8
RW_DIM = RW_HEADS * HEAD_DIM
W_LORA = 64
A_LORA = 64
G_LORA = 128
CONV_W = 3
N_EXPERTS = 32
TOP_K = 4
SWIGLU_LIMIT = 7.0
SWIGLU_ALPHA = 1.702
ROPE_THETA = 10000.0
ROPE_PAIRS = HEAD_DIM // 4
MOE_BLOCK = 256
NORM_EPS = 1e-6
GN_EPS = 64e-5
ATTN_SCALE = HEAD_DIM ** -0.5
EXP_M05 = math.exp(-0.5)
COL_K_AT = 0
COL_V_AT = KV_DIM
COL_K_RW = 2 * KV_DIM
COL_V_RW = COL_K_RW + RW_DIM
COL_XW = COL_V_RW + RW_DIM
COL_XA = COL_XW + 2 * W_LORA
STATE_COLS = COL_XA + 2 * A_LORA
COL_Q_AT = STATE_COLS
COL_R_RW = COL_Q_AT + ATTN_DIM
COL_XG = COL_R_RW + RW_DIM

LANES = 128
SUBLANES = 8
QUAD_LANES = 256
VMEM_LIMIT = 48 * 1024 * 1024


def _row_tile(n, pref):
    t = pref
    while n % t:
        t //= 2
    return t


def _params(n_axes):
    return pltpu.CompilerParams(dimension_semantics=("arbitrary",) * n_axes, vmem_limit_bytes=VMEM_LIMIT)


def _ones_bd(n):
    ra = lax.broadcasted_iota(jnp.int32, (n, n), 0) // HEAD_DIM
    rb = lax.broadcasted_iota(jnp.int32, (n, n), 1) // HEAD_DIM
    return (ra == rb).astype(BF16)


def _seg_sum(x_bf16, ones_bd):
    return jnp.dot(x_bf16, ones_bd, preferred_element_type=F32)


def _seg_sum2(x, ones_bd):
    hi = x.astype(BF16)
    lo = (x - hi.astype(F32)).astype(BF16)
    return _seg_sum(hi, ones_bd) + _seg_sum(lo, ones_bd)


def _adaln_kernel(c_ref, w_ref, b_ref, o_ref):
    c = c_ref[...]
    s = c * jax.nn.sigmoid(c)
    o_ref[...] = jnp.dot(s, w_ref[...], preferred_element_type=F32, precision=HIGHEST) + b_ref[...]


def adaln_all(cvecs, w_mod, b_mod):
    R, D = cvecs.shape
    N = w_mod.shape[1]
    tn = 512
    return pl.pallas_call(
        _adaln_kernel,
        out_shape=jax.ShapeDtypeStruct((R, N), F32),
        grid=(N // tn,),
        in_specs=[pl.BlockSpec((R, D), lambda j: (0, 0)),
                  pl.BlockSpec((D, tn), lambda j: (0, j)),
                  pl.BlockSpec((1, tn), lambda j: (0, j))],
        out_specs=pl.BlockSpec((R, tn), lambda j: (0, j)),
        compiler_params=_params(1),
        name="adaln",
    )(cvecs, w_mod, b_mod.reshape(1, N))


def _norm_mod(x, g, sc, sh):
    ms = jnp.mean(x * x, axis=-1, keepdims=True)
    y = x * lax.rsqrt(ms + NORM_EPS) * g
    return y * (1.0 + sc) + sh


def _norm_mod_matmul_kernel(x_ref, g_ref, sc_ref, sh_ref, w_ref, o_ref):
    h = _norm_mod(x_ref[0], g_ref[...], sc_ref[0], sh_ref[0])
    o_ref[0] = jnp.dot(h.astype(BF16), w_ref[...], preferred_element_type=F32)


def norm_mod_matmul(x, g, sc, sh, w):
    B, T, D = x.shape
    N = w.shape[1]
    tm = _row_tile(T, 256)
    return pl.pallas_call(
        _norm_mod_matmul_kernel,
        out_shape=jax.ShapeDtypeStruct((B, T, N), F32),
        grid=(B, T // tm),
        in_specs=[pl.BlockSpec((1, tm, D), lambda b, i: (b, i, 0)),
                  pl.BlockSpec((1, D), lambda b, i: (0, 0)),
                  pl.BlockSpec((1, 1, D), lambda b, i: (b, 0, 0)),
                  pl.BlockSpec((1, 1, D), lambda b, i: (b, 0, 0)),
                  pl.BlockSpec((D, N), lambda b, i: (0, 0))],
        out_specs=pl.BlockSpec((1, tm, N), lambda b, i: (b, i, 0)),
        compiler_params=_params(2),
        name="norm_mod_matmul",
    )(x, g.reshape(1, D), sc, sh, w)


def _pair_rms(x, g):
    lane = lax.broadcasted_iota(jnp.int32, x.shape, 1)
    first = lane < HEAD_DIM
    xx = x * x
    s0 = jnp.sum(jnp.where(first, xx, 0.0), axis=1, keepdims=True)
    s1 = jnp.sum(jnp.where(first, 0.0, xx), axis=1, keepdims=True)
    ms = jnp.where(first, s0, s1) * (1.0 / HEAD_DIM)
    return x * lax.rsqrt(ms + NORM_EPS) * g


def _pair_rope(x, cos, sin):
    lane = lax.broadcasted_iota(jnp.int32, x.shape, 1)
    partner = jnp.where(lane % (2 * ROPE_PAIRS) < ROPE_PAIRS,
                        pltpu.roll(x, LANES - ROPE_PAIRS, 1), pltpu.roll(x, ROPE_PAIRS, 1))
    return x * cos + partner * sin


def _rope_tables(T):
    rows = T // GRID_W
    row = jnp.repeat(jnp.arange(rows), GRID_W)
    col = jnp.tile(jnp.arange(GRID_W), rows)
    pos = jnp.stack([row, col], axis=-1).astype(F32)
    freqs = ROPE_THETA ** (-jnp.arange(ROPE_PAIRS, dtype=F32) / ROPE_PAIRS)
    ang = pos[:, :, None] * freqs
    cos, sin = jnp.cos(ang), jnp.sin(ang)
    cos_h = jnp.concatenate([cos[:, 0], cos[:, 0], cos[:, 1], cos[:, 1]], axis=-1)
    sin_h = jnp.concatenate([-sin[:, 0], sin[:, 0], -sin[:, 1], sin[:, 1]], axis=-1)
    return jnp.tile(cos_h, (1, 2)), jnp.tile(sin_h, (1, 2))


def _kv_prep_kernel(*refs, rope):
    if rope:
        p_ref, g_ref, cos_ref, sin_ref, kt_ref, v_ref = refs
    else:
        p_ref, g_ref, kt_ref, v_ref = refs
    x = p_ref[0]
    k = _pair_rms(x[:, :KV_DIM], g_ref[...])
    if rope:
        k = _pair_rope(k, cos_ref[...], sin_ref[...])
    v = x[:, KV_DIM:]
    lane = lax.broadcasted_iota(jnp.int32, k.shape, 1)
    first = lane < HEAD_DIM
    kr = pltpu.roll(k, HEAD_DIM, 1)
    vr = pltpu.roll(v, HEAD_DIM, 1)
    kt_ref[0, 0] = jnp.where(first, k, kr).T.astype(BF16)
    kt_ref[0, 1] = jnp.where(first, kr, k).T.astype(BF16)
    v_ref[0, 0] = jnp.where(first, v, vr).astype(BF16)
    v_ref[0, 1] = jnp.where(first, vr, v).astype(BF16)


def kv_prep(p, k_norm_g, cos=None, sin=None):
    B, T, _ = p.shape
    tm = _row_tile(T, 256)
    rope = cos is not None
    g2 = jnp.tile(k_norm_g, 2).reshape(1, LANES)
    in_specs = [pl.BlockSpec((1, tm, 2 * KV_DIM), lambda b, i: (b, i, 0)),
                pl.BlockSpec((1, LANES), lambda b, i: (0, 0))]
    args = [p, g2]
    if rope:
        in_specs += [pl.BlockSpec((tm, LANES), lambda b, i: (i, 0))] * 2
        args += [cos, sin]
    return pl.pallas_call(
        functools.partial(_kv_prep_kernel, rope=rope),
        out_shape=(jax.ShapeDtypeStruct((B, ATTN_KV_HEADS, LANES, T), BF16),
                   jax.ShapeDtypeStruct((B, ATTN_KV_HEADS, T, LANES), BF16)),
        grid=(B, T // tm),
        in_specs=in_specs,
        out_specs=(pl.BlockSpec((1, ATTN_KV_HEADS, LANES, tm), lambda b, i: (b, 0, 0, i)),
                   pl.BlockSpec((1, ATTN_KV_HEADS, tm, LANES), lambda b, i: (b, 0, i, 0))),
        compiler_params=_params(2),
        name="kv_prep",
    )(*args)


def _attn_kernel(*refs, tq, tkl, nkl, rope, has_lat):
    refs = list(refs)
    p_ref, g_ref = refs[:2]
    refs = refs[2:]
    if rope:
        cos_ref, sin_ref = refs[:2]
        refs = refs[2:]
    ktc_ref, vc_ref = refs[:2]
    refs = refs[2:]
    if has_lat:
        ktl_ref, vl_ref = refs[:2]
        refs = refs[2:]
    o_ref, m_ref, l_ref, acc_ref = refs

    q = _pair_rms(p_ref[0], g_ref[...])
    if rope:
        q = _pair_rope(q, cos_ref[...], sin_ref[...])
    q = q * ATTN_SCALE
    lane = lax.broadcasted_iota(jnp.int32, (tq, LANES), 1)
    first = lane < HEAD_DIM
    qq = jnp.concatenate([jnp.where(first, q, 0.0), jnp.where(first, 0.0, q)], axis=0).astype(BF16)

    m_ref[...] = jnp.full(m_ref.shape, -jnp.inf, F32)
    l_ref[...] = jnp.zeros(l_ref.shape, F32)
    acc_ref[...] = jnp.zeros(acc_ref.shape, F32)

    def chunk(kt, vv):
        s = jnp.dot(qq, kt, preferred_element_type=F32)
        m_old = m_ref[...]
        m_new = jnp.maximum(m_old, jnp.max(s, axis=-1, keepdims=True))
        alpha = jnp.exp(m_old - m_new)
        p = jnp.exp(s - jnp.concatenate([m_new] * (s.shape[1] // LANES), axis=1))
        l_ref[...] = alpha * l_ref[...] + jnp.sum(p, axis=-1, keepdims=True)
        acc_ref[...] = alpha * acc_ref[...] + jnp.dot(p.astype(BF16), vv, preferred_element_type=F32)
        m_ref[...] = m_new

    chunk(ktc_ref[0, 0], vc_ref[0, 0])
    if has_lat:
        def body(j, carry):
            ks = pl.multiple_of(j * tkl, tkl)
            chunk(ktl_ref[0, 0, :, pl.ds(ks, tkl)], vl_ref[0, 0, pl.ds(ks, tkl), :])
            return carry
        lax.fori_loop(0, nkl, body, 0)

    o = acc_ref[...] / l_ref[...]
    o_ref[0] = jnp.where(first, o[:tq], o[tq:])


def attention(p, q_norm_g, ktc, vc, ktl=None, vl=None, cos=None, sin=None):
    B, T, _ = p.shape
    C = ktc.shape[-1]
    rope = cos is not None
    has_lat = ktl is not None
    tq = _row_tile(T, 256)
    pairs = ATTN_HEADS // 2
    pairs_per_kv = GROUP // 2
    qblk0 = COL_Q_AT // LANES
    g2 = jnp.tile(q_norm_g, 2).reshape(1, LANES)
    in_specs = [pl.BlockSpec((1, tq, LANES), lambda b, h, i: (b, i, qblk0 + h)),
                pl.BlockSpec((1, LANES), lambda b, h, i: (0, 0))]
    args = [p, g2]
    if rope:
        in_specs += [pl.BlockSpec((tq, LANES), lambda b, h, i: (i, 0))] * 2
        args += [cos, sin]
    in_specs += [pl.BlockSpec((1, 1, LANES, C), lambda b, h, i: (b, h // pairs_per_kv, 0, 0)),
                 pl.BlockSpec((1, 1, C, LANES), lambda b, h, i: (b, h // pairs_per_kv, 0, 0))]
    args += [ktc, vc]
    tkl = nkl = 0
    if has_lat:
        S = ktl.shape[-1]
        tkl = _row_tile(S, 1024)
        nkl = S // tkl
        in_specs += [pl.BlockSpec((1, 1, LANES, S), lambda b, h, i: (b, h // pairs_per_kv, 0, 0)),
                     pl.BlockSpec((1, 1, S, LANES), lambda b, h, i: (b, h // pairs_per_kv, 0, 0))]
        args += [ktl, vl]
    kern = functools.partial(_attn_kernel, tq=tq, tkl=tkl, nkl=nkl, rope=rope, has_lat=has_lat)
    return pl.pallas_call(
        kern,
        out_shape=jax.ShapeDtypeStruct((B, T, ATTN_DIM), F32),
        grid=(B, pairs, T // tq),
        in_specs=in_specs,
        out_specs=pl.BlockSpec((1, tq, LANES), lambda b, h, i: (b, i, h)),
        scratch_shapes=[pltpu.VMEM((2 * tq, LANES), F32)] * 3,
        compiler_params=_params(3),
        name="attention",
    )(*args)


def _wkv_prep_kernel(p_ref, pp_ref, pn_ref, cw_ref, w0_ref, wup_ref, a0_ref, aup_ref, kk_ref, ka_ref,
                     w2_ref, k2_ref, b2_ref, v_ref, kkn_ref, r_ref, *, tm, has_r):
    i = pl.program_id(1)
    n = pl.num_programs(1)
    x = p_ref[0]
    prev = pp_ref[0][SUBLANES - 1:SUBLANES, :] * (i > 0).astype(F32)
    nxt = pn_ref[0][0:1, :] * (i < n - 1).astype(F32)
    row = lax.broadcasted_iota(jnp.int32, (tm, RW_DIM), 0)

    def conv(col, wcol):
        u = x[:, col:col + RW_DIM]
        up = jnp.where(row == 0, prev[:, col:col + RW_DIM], pltpu.roll(u, 1, 0))
        un = jnp.where(row == tm - 1, nxt[:, col:col + RW_DIM], pltpu.roll(u, tm - 1, 0))
        w = cw_ref[:, wcol:wcol + RW_DIM]
        return up * w[0:1] + u * w[1:2] + un * w[2:3]

    k = conv(COL_K_RW, RW_DIM)
    v = conv(COL_V_RW, 2 * RW_DIM)
    v_ref[0] = v
    r_ref[0] = conv(COL_R_RW, 0) if has_r else jnp.zeros((tm, RW_DIM), F32)
    kk = k * kk_ref[...]
    ss = _seg_sum2(kk * kk, _ones_bd(RW_DIM))
    kk = kk * lax.rsqrt(jnp.maximum(ss, 1e-24))
    kkn_ref[0] = kk
    xw = jnp.tanh(x[:, COL_XW:COL_XW + 2 * W_LORA])
    xa = x[:, COL_XA:COL_XA + 2 * A_LORA]
    for d in range(2):
        wl = w0_ref[d:d + 1, :] + jnp.dot(xw[:, d * W_LORA:(d + 1) * W_LORA], wup_ref[d],
                                          preferred_element_type=F32, precision=HIGHEST)
        decay = -EXP_M05 * jax.nn.sigmoid(wl)
        a = jax.nn.sigmoid(a0_ref[d:d + 1, :] + jnp.dot(xa[:, d * A_LORA:(d + 1) * A_LORA], aup_ref[d],
                                                        preferred_element_type=F32, precision=HIGHEST))
        sl = slice(d * RW_DIM, (d + 1) * RW_DIM)
        w2_ref[0, :, sl] = decay
        k2_ref[0, :, sl] = k * (1.0 + (a - 1.0) * ka_ref[...])
        b2_ref[0, :, sl] = kk * a


def wkv_prep(p, conv_w, w0, w_up, a0, a_up, k_k, k_a, has_r):
    B, T, NP = p.shape
    tm = _row_tile(T, 256)
    nb8 = tm // SUBLANES
    last8 = T // SUBLANES - 1
    full = lambda a: pl.BlockSpec(a.shape, lambda b, i: (0,) * a.ndim)
    kk2, ka2 = k_k.reshape(1, RW_DIM), k_a.reshape(1, RW_DIM)
    wide = jax.ShapeDtypeStruct((B, T, 2 * RW_DIM), F32)
    narrow = jax.ShapeDtypeStruct((B, T, RW_DIM), F32)
    wspec = pl.BlockSpec((1, tm, 2 * RW_DIM), lambda b, i: (b, i, 0))
    nspec = pl.BlockSpec((1, tm, RW_DIM), lambda b, i: (b, i, 0))
    return pl.pallas_call(
        functools.partial(_wkv_prep_kernel, tm=tm, has_r=has_r),
        out_shape=(wide, wide, wide, narrow, narrow, narrow),
        grid=(B, T // tm),
        in_specs=[pl.BlockSpec((1, tm, NP), lambda b, i: (b, i, 0)),
                  pl.BlockSpec((1, SUBLANES, NP), lambda b, i: (b, jnp.maximum(i * nb8 - 1, 0), 0)),
                  pl.BlockSpec((1, SUBLANES, NP), lambda b, i: (b, jnp.minimum((i + 1) * nb8, last8), 0)),
                  full(conv_w), full(w0), full(w_up), full(a0), full(a_up), full(kk2), full(ka2)],
        out_specs=(wspec, wspec, wspec, nspec, nspec, nspec),
        compiler_params=_params(2),
        name="wkv_prep",
    )(p, p, p, conv_w, w0, w_up, a0, a_up, kk2, ka2)


def _wkv_kernel(wf_ref, wb_ref, kf_ref, kb_ref, bf_ref, bb_ref, vf_ref, vb_ref, af_ref, ab_ref,
                rf_ref, rb_ref, s0_ref, of_ref, ob_ref, st_ref, s_ref, *, tc, nquads):
    tb = pl.program_id(1)

    @pl.when(tb == 0)
    def _():
        s_ref[...] = s0_ref[0]

    rows_all = 2 * nquads * HEAD_DIM
    shp = (rows_all, QUAD_LANES)
    lane = lax.broadcasted_iota(jnp.int32, shp, 1)
    sub = lax.broadcasted_iota(jnp.int32, shp, 0)
    eye = (lane % HEAD_DIM == sub % HEAD_DIM).astype(F32)
    ones_bd = _ones_bd(QUAD_LANES)
    ng = tc // SUBLANES

    def bcast_rows(blks, j):
        parts = []
        for d in range(2):
            jj = j if d == 0 else SUBLANES - 1 - j
            for q in range(nquads):
                row = blks[d][jj:jj + 1, q * QUAD_LANES:(q + 1) * QUAD_LANES]
                parts.append(jnp.broadcast_to(row, (HEAD_DIM, QUAD_LANES)))
        return jnp.concatenate(parts, axis=0)

    def group(g, carry):
        tf = pl.multiple_of(g * SUBLANES, SUBLANES)
        tr = pl.multiple_of((ng - 1 - g) * SUBLANES, SUBLANES)
        ld = lambda fr, br: [fr[0, pl.ds(tf, SUBLANES), :], br[0, pl.ds(tr, SUBLANES), :]]
        wq, kq, bq, vq, rq = (ld(f, b) for f, b in ((wf_ref, wb_ref), (kf_ref, kb_ref), (bf_ref, bb_ref),
                                                    (vf_ref, vb_ref), (rf_ref, rb_ref)))
        aq = [-x for x in ld(af_ref, ab_ref)]
        S = s_ref[...]
        out_rows = [[] for _ in range(2 * nquads)]
        for j in range(SUBLANES):
            wt, kt, vt, at, bt, rt = (bcast_rows(x, j) for x in (wq, kq, vq, aq, bq, rq))
            sa = _seg_sum2(S * at, ones_bd)
            vcol = _seg_sum((eye * vt).astype(BF16), ones_bd)
            S = S * wt + sa * bt + vcol * kt
            ob = _seg_sum((S * rt).astype(BF16), ones_bd) * eye
            for c in range(2 * nquads):
                out_rows[c].append(jnp.sum(ob[c * HEAD_DIM:(c + 1) * HEAD_DIM], axis=0, keepdims=True))
        s_ref[...] = S
        for q in range(nquads):
            sl = slice(q * QUAD_LANES, (q + 1) * QUAD_LANES)
            of_ref[0, pl.ds(tf, SUBLANES), sl] = jnp.concatenate(out_rows[q], axis=0)
            ob_ref[0, pl.ds(tr, SUBLANES), sl] = jnp.concatenate(out_rows[nquads + q][::-1], axis=0)
        return carry

    lax.fori_loop(0, ng, group, 0)

    @pl.when(tb == pl.num_programs(1) - 1)
    def _():
        st_ref[0] = s_ref[...]


def wkv_scan(w2, k2, b2, v, kk, r, s0):
    B, T, C = v.shape
    nquads = C // QUAD_LANES
    tc = _row_tile(T, 128)
    nt = T // tc
    fwd = lambda col: pl.BlockSpec((1, tc, C), lambda b, t: (b, t, col))
    bwd = lambda col: pl.BlockSpec((1, tc, C), lambda b, t: (b, nt - 1 - t, col))
    sspec = pl.BlockSpec((1,) + s0.shape[1:], lambda b, t: (b, 0, 0))
    kern = functools.partial(_wkv_kernel, tc=tc, nquads=nquads)
    return pl.pallas_call(
        kern,
        out_shape=(jax.ShapeDtypeStruct((B, T, C), F32), jax.ShapeDtypeStruct((B, T, C), F32),
                   jax.ShapeDtypeStruct(s0.shape, F32)),
        grid=(B, nt),
        in_specs=[fwd(0), bwd(1), fwd(0), bwd(1), fwd(0), bwd(1), fwd(0), bwd(0), fwd(0), bwd(0),
                  fwd(0), bwd(0), sspec],
        out_specs=(fwd(0), bwd(0), sspec),
        scratch_shapes=[pltpu.VMEM(s0.shape[1:], F32)],
        compiler_params=_params(2),
        name="wkv_scan",
    )(w2, w2, k2, k2, b2, b2, v, v, kk, kk, r, r, s0)


CHUNK = 64


def _split_bf16(x):
    hi = x.astype(BF16)
    return hi, (x - hi.astype(F32)).astype(BF16)


def _mm3(a, b, nt=False):
    dn = (((1,), (1 if nt else 0,)), ((), ()))
    ah, al = _split_bf16(a)
    bh, bl = _split_bf16(b)
    m = a.shape[0]
    top = lax.dot_general(jnp.concatenate([ah, al], axis=0), bh, dn, preferred_element_type=F32)
    return top[:m] + top[m:] + lax.dot_general(ah, bl, dn, preferred_element_type=F32)


def _chunk_terms(lw, kd, bb, v, kk, r, backward):
    L = CHUNK
    row = lax.broadcasted_iota(jnp.int32, (L, QUAD_LANES), 0)
    col = lax.broadcasted_iota(jnp.int32, (L, QUAD_LANES), 1) % HEAD_DIM
    tr = lax.broadcasted_iota(jnp.int32, (L, L), 0)
    tc = lax.broadcasted_iota(jnp.int32, (L, L), 1)
    if backward:
        strict, incl, tri = col > row, col >= row, (tc >= tr).astype(F32)
    else:
        strict, incl, tri = col < row, col <= row, (tc <= tr).astype(F32)
    eye_l = (col == row).astype(F32)
    ra = lax.broadcasted_iota(jnp.int32, (QUAD_LANES, QUAD_LANES), 0)
    rb = lax.broadcasted_iota(jnp.int32, (QUAD_LANES, QUAD_LANES), 1)
    bd = ra // HEAD_DIM == rb // HEAD_DIM
    eye_q = (ra == rb).astype(F32)

    def bdiag(y):
        return jnp.where(bd, jnp.concatenate([y] * (QUAD_LANES // L), axis=0), 0.0)

    def bdm(x, y):
        return _mm3(x, bdiag(y))

    cs = _mm3(tri, lw)
    g_last = jnp.exp(cs[0:1] if backward else cs[L - 1:L])
    at = -kk * jnp.exp(cs - lw)
    g_inv = jnp.exp(-cs)
    bt = bb * g_inv
    kt = kd * g_inv
    rt = r * jnp.exp(cs)
    ar = jnp.concatenate([at, rt], axis=0)
    pb = _mm3(ar, bdiag(bt), nt=True)
    pk = _mm3(ar, bdiag(kt), nt=True)
    n = jnp.where(strict, pb[:L], 0.0)
    m_rb = jnp.where(incl, pb[L:], 0.0)
    m_ak = jnp.where(strict, pk[:L], 0.0)
    m_rk = jnp.where(incl, pk[L:], 0.0)
    p = n
    t_inv = eye_l + n
    for _ in range(int(math.log2(L)) - 1):
        p = bdm(p, p)
        t_inv = t_inv + bdm(t_inv, p)
    xy = bdm(jnp.concatenate([m_ak, m_rk], axis=0), v)
    w1 = bdm(t_inv, at)
    w2 = bdm(t_inv, xy[:L])
    q1 = rt + bdm(m_rb, w1)
    q2 = xy[L:] + bdm(m_rb, w2)
    g = jnp.where(bd, (eye_q + _mm3(w1.T, bt)) * g_last, 0.0)
    h = jnp.where(bd, _mm3(jnp.concatenate([w2, v], axis=0).T, jnp.concatenate([bt, kt], axis=0)) * g_last, 0.0)
    return q1, q2, g.T, h.T


def _wkv_chunk_kernel(lw_ref, k_ref, b_ref, v_ref, kk_ref, r_ref, q1_ref, q2_ref, gt_ref, ht_ref, *, nquads):
    for d in range(2):
        for q in range(nquads):
            sl = slice(q * QUAD_LANES, (q + 1) * QUAD_LANES)
            sd = slice(d * RW_DIM + q * QUAD_LANES, d * RW_DIM + (q + 1) * QUAD_LANES)
            q1, q2, gt, ht = _chunk_terms(lw_ref[0, :, sd], k_ref[0, :, sd], b_ref[0, :, sd],
                                          v_ref[0, :, sl], kk_ref[0, :, sl], r_ref[0, :, sl], backward=d == 1)
            q1_ref[0, d, :, sl] = q1
            q2_ref[0, d, :, sl] = q2
            gt_ref[0, d, 0, q] = gt
            ht_ref[0, d, 0, q] = ht


def wkv_chunk_prep(lw2, k2, b2, v, kk, r):
    B, T, C = v.shape
    nquads = C // QUAD_LANES
    nc = T // CHUNK
    wide = pl.BlockSpec((1, CHUNK, 2 * C), lambda b, c: (b, c, 0))
    narrow = pl.BlockSpec((1, CHUNK, C), lambda b, c: (b, c, 0))
    qspec = pl.BlockSpec((1, 2, CHUNK, C), lambda b, c: (b, 0, c, 0))
    gspec = pl.BlockSpec((1, 2, 1, nquads, QUAD_LANES, QUAD_LANES), lambda b, c: (b, 0, c, 0, 0, 0))
    qshape = jax.ShapeDtypeStruct((B, 2, T, C), F32)
    gshape = jax.ShapeDtypeStruct((B, 2, nc, nquads, QUAD_LANES, QUAD_LANES), F32)
    return pl.pallas_call(
        functools.partial(_wkv_chunk_kernel, nquads=nquads),
        out_shape=(qshape, qshape, gshape, gshape),
        grid=(B, nc),
        in_specs=[wide, wide, wide, narrow, narrow, narrow],
        out_specs=(qspec, qspec, gspec, gspec),
        compiler_params=_params(2),
        name="wkv_chunk_prep",
    )(lw2, k2, b2, v, kk, r)


def _wkv_seq_kernel(q1f_ref, q1b_ref, q2f_ref, q2b_ref, gtf_ref, gtb_ref, htf_ref, htb_ref, s0_ref,
                    of_ref, ob_ref, st_ref, z_ref, *, nquads):
    c = pl.program_id(1)

    @pl.when(c == 0)
    def _():
        z_ref[...] = s0_ref[0]

    dirs = ((q1f_ref, q2f_ref, gtf_ref, htf_ref, of_ref), (q1b_ref, q2b_ref, gtb_ref, htb_ref, ob_ref))
    for d, (q1_ref, q2_ref, gt_ref, ht_ref, o_ref) in enumerate(dirs):
        for q in range(nquads):
            sl = slice(q * QUAD_LANES, (q + 1) * QUAD_LANES)
            lhs = jnp.concatenate([q1_ref[0, 0, :, sl], gt_ref[0, 0, 0, q]], axis=0)
            res = _mm3(lhs, z_ref[d * nquads + q])
            o_ref[0, :, sl] = res[:CHUNK] + q2_ref[0, 0, :, sl]
            z_ref[d * nquads + q] = res[CHUNK:] + ht_ref[0, 0, 0, q]

    @pl.when(c == pl.num_programs(1) - 1)
    def _():
        st_ref[0] = z_ref[...]


def wkv_seq(q1, q2, gt, ht, s0):
    B, _, T, C = q1.shape
    nc, nquads = gt.shape[2], gt.shape[3]
    qf = pl.BlockSpec((1, 1, CHUNK, C), lambda b, c: (b, 0, c, 0))
    qb = pl.BlockSpec((1, 1, CHUNK, C), lambda b, c: (b, 1, nc - 1 - c, 0))
    gshape = (1, 1, 1, nquads, QUAD_LANES, QUAD_LANES)
    gf = pl.BlockSpec(gshape, lambda b, c: (b, 0, c, 0, 0, 0))
    gb = pl.BlockSpec(gshape, lambda b, c: (b, 1, nc - 1 - c, 0, 0, 0))
    sspec = pl.BlockSpec((1,) + s0.shape[1:], lambda b, c: (b, 0, 0, 0))
    return pl.pallas_call(
        functools.partial(_wkv_seq_kernel, nquads=nquads),
        out_shape=(jax.ShapeDtypeStruct((B, T, C), F32), jax.ShapeDtypeStruct((B, T, C), F32),
                   jax.ShapeDtypeStruct(s0.shape, F32)),
        grid=(B, nc),
        in_specs=[qf, qb, qf, qb, gf, gb, gf, gb, sspec],
        out_specs=(pl.BlockSpec((1, CHUNK, C), lambda b, c: (b, c, 0)),
                   pl.BlockSpec((1, CHUNK, C), lambda b, c: (b, nc - 1 - c, 0)), sspec),
        scratch_shapes=[pltpu.VMEM(s0.shape[1:], F32)],
        compiler_params=_params(2),
        name="wkv_seq",
    )(q1, q1, q2, q2, gt, gt, ht, ht, s0)


def wkv_chunked(lw2, k2, b2, v, kk, r, s0):
    q1, q2, gt, ht = wkv_chunk_prep(lw2, k2, b2, v, kk, r)
    return wkv_seq(q1, q2, gt, ht, s0)


def _mixer_out_kernel(of_ref, ob_ref, r_ref, v_ref, k2_ref, xg_ref, oat_ref, x_ref, gm_ref,
                      gup_ref, rk_ref, lng_ref, lnb_ref, wo_ref, o_ref):
    ones = _ones_bd(RW_DIM)
    inv = 1.0 / HEAD_DIM
    o = of_ref[0] + ob_ref[0]
    mu = _seg_sum2(o, ones) * inv
    dv = o - mu
    var = _seg_sum2(dv * dv, ones) * inv
    gn = dv * lax.rsqrt(var + GN_EPS) * lng_ref[...] + lnb_ref[...]
    k2 = k2_ref[0]
    k_bonus = (k2[:, :RW_DIM] + k2[:, RW_DIM:]) * 0.5
    bonus = _seg_sum2(r_ref[0] * k_bonus * rk_ref[...], ones) * v_ref[0]
    g = jnp.dot(jax.nn.sigmoid(xg_ref[0]), gup_ref[...], preferred_element_type=F32, precision=HIGHEST)
    y = (gn + bonus) * g
    acc = jnp.dot(oat_ref[0].astype(BF16), wo_ref[:ATTN_DIM, :], preferred_element_type=F32)
    acc = acc + jnp.dot(y.astype(BF16), wo_ref[ATTN_DIM:, :], preferred_element_type=F32)
    o_ref[0] = x_ref[0] + gm_ref[0] * acc


def mixer_out(o_f, o_b, r, v, k2, p, o_at, x, g_m, g_up, r_k, ln_x_g, ln_x_b, w_out):
    B, T, D = x.shape
    tm = _row_tile(T, 256)
    nspec = pl.BlockSpec((1, tm, RW_DIM), lambda b, i: (b, i, 0))
    full = lambda a: pl.BlockSpec(a.shape, lambda b, i: (0,) * a.ndim)
    rk2, lng2, lnb2 = r_k.reshape(1, RW_DIM), ln_x_g.reshape(1, RW_DIM), ln_x_b.reshape(1, RW_DIM)
    return pl.pallas_call(
        _mixer_out_kernel,
        out_shape=jax.ShapeDtypeStruct((B, T, D), F32),
        grid=(B, T // tm),
        in_specs=[nspec, nspec, nspec, nspec,
                  pl.BlockSpec((1, tm, 2 * RW_DIM), lambda b, i: (b, i, 0)),
                  pl.BlockSpec((1, tm, G_LORA), lambda b, i: (b, i, COL_XG // G_LORA)),
                  pl.BlockSpec((1, tm, ATTN_DIM), lambda b, i: (b, i, 0)),
                  pl.BlockSpec((1, tm, D), lambda b, i: (b, i, 0)),
                  pl.BlockSpec((1, 1, D), lambda b, i: (b, 0, 0)),
                  full(g_up), full(rk2), full(lng2), full(lnb2), full(w_out)],
        out_specs=pl.BlockSpec((1, tm, D), lambda b, i: (b, i, 0)),
        compiler_params=_params(2),
        name="mixer_out",
    )(o_f, o_b, r, v, k2, p, o_at, x, g_m, g_up, rk2, lng2, lnb2, w_out)


def _start_row_copies(idx_ref, n, src_hbm, dst_ref, sem):
    def body(r, carry):
        pltpu.make_async_copy(src_hbm.at[pl.ds(idx_ref[0, 0, r], 1)], dst_ref.at[pl.ds(r, 1)], sem).start()
        return carry
    lax.fori_loop(0, n, body, 0, unroll=8)


def _wait_row_copies(n, src_hbm, dst_ref, sem):
    pltpu.make_async_copy(src_hbm.at[pl.ds(0, n)], dst_ref.at[pl.ds(0, n)], sem).wait()


def _moe_kernel(be_ref, tok_ref, tokn_ref, h_hbm, w1_ref, b1_ref, w2_ref, b2_ref, o_ref, buf, sems, *, f, tm):
    i = pl.program_id(0)
    slot = i % 2

    @pl.when(i == 0)
    def _():
        _start_row_copies(tok_ref, tm, h_hbm, buf.at[0], sems.at[0])

    @pl.when(i + 1 < pl.num_programs(0))
    def _():
        _start_row_copies(tokn_ref, tm, h_hbm, buf.at[1 - slot], sems.at[1 - slot])

    _wait_row_copies(tm, h_hbm, buf.at[slot], sems.at[slot])
    u = jnp.dot(buf[slot].astype(BF16), w1_ref[0], preferred_element_type=F32) + b1_ref[0]
    x_glu = jnp.minimum(u[:, :f], SWIGLU_LIMIT)
    x_lin = jnp.clip(u[:, f:], -SWIGLU_LIMIT, SWIGLU_LIMIT)
    act = x_glu * jax.nn.sigmoid(SWIGLU_ALPHA * x_glu) * (x_lin + 1.0)
    o_ref[...] = jnp.dot(act.astype(BF16), w2_ref[0], preferred_element_type=F32) + b2_ref[0]


def moe_blocks(block_e, slot_tok, h, w1, b1, w2, b2):
    N, D = h.shape
    E, _, F2 = w1.shape
    f = F2 // 2
    tm = MOE_BLOCK
    rows = slot_tok.shape[0]
    nb = rows // tm
    tok3 = slot_tok.reshape(nb, 1, tm)
    kern = functools.partial(_moe_kernel, f=f, tm=tm)
    grid_spec = pltpu.PrefetchScalarGridSpec(
        num_scalar_prefetch=1,
        grid=(nb,),
        in_specs=[pl.BlockSpec((1, 1, tm), lambda i, be: (i, 0, 0), memory_space=pltpu.SMEM),
                  pl.BlockSpec((1, 1, tm), lambda i, be: (jnp.minimum(i + 1, nb - 1), 0, 0),
                               memory_space=pltpu.SMEM),
                  pl.BlockSpec(memory_space=pl.ANY),
                  pl.BlockSpec((1, D, F2), lambda i, be: (be[i], 0, 0)),
                  pl.BlockSpec((1, 1, F2), lambda i, be: (be[i], 0, 0)),
                  pl.BlockSpec((1, f, D), lambda i, be: (be[i], 0, 0)),
                  pl.BlockSpec((1, 1, D), lambda i, be: (be[i], 0, 0))],
        out_specs=pl.BlockSpec((tm, D), lambda i, be: (i, 0)),
        scratch_shapes=[pltpu.VMEM((2, tm, D), F32), pltpu.SemaphoreType.DMA((2,))],
    )
    return pl.pallas_call(
        kern,
        out_shape=jax.ShapeDtypeStruct((rows, D), F32),
        grid_spec=grid_spec,
        compiler_params=_params(1),
        name="moe_blocks",
    )(block_e, tok3, tok3, h, w1, b1.reshape(E, 1, F2), w2, b2.reshape(E, 1, D))


def _moe_combine_kernel(sl_ref, sln_ref, out_hbm, gate_ref, x_ref, gf_ref, o_ref, buf, sems, *, tb):
    b, i = pl.program_id(0), pl.program_id(1)
    step = b * pl.num_programs(1) + i
    nsteps = pl.num_programs(0) * pl.num_programs(1)
    slot = step % 2
    n = TOP_K * tb

    @pl.when(step == 0)
    def _():
        _start_row_copies(sl_ref, n, out_hbm, buf.at[0], sems.at[0])

    @pl.when(step + 1 < nsteps)
    def _():
        _start_row_copies(sln_ref, n, out_hbm, buf.at[1 - slot], sems.at[1 - slot])

    _wait_row_copies(n, out_hbm, buf.at[slot], sems.at[slot])
    gate = gate_ref[0]
    y = jnp.zeros(x_ref.shape[1:], F32)
    for k in range(TOP_K):
        y = y + gate[:, k:k + 1] * buf[slot, k * tb:(k + 1) * tb, :]
    o_ref[0] = x_ref[0] + gf_ref[0] * y


def moe_combine(out, slots, gates, x, g_f):
    B, T, D = x.shape
    tb = _row_tile(T, 64)
    nt = T // tb
    n = TOP_K * tb
    sl3 = jnp.transpose(slots.reshape(B * nt, tb, TOP_K), (0, 2, 1)).reshape(B * nt, 1, n)
    last = B * nt - 1
    return pl.pallas_call(
        functools.partial(_moe_combine_kernel, tb=tb),
        out_shape=jax.ShapeDtypeStruct((B, T, D), F32),
        grid=(B, nt),
        in_specs=[pl.BlockSpec((1, 1, n), lambda b, i: (b * nt + i, 0, 0), memory_space=pltpu.SMEM),
                  pl.BlockSpec((1, 1, n), lambda b, i: (jnp.minimum(b * nt + i + 1, last), 0, 0),
                               memory_space=pltpu.SMEM),
                  pl.BlockSpec(memory_space=pl.ANY),
                  pl.BlockSpec((1, tb, LANES), lambda b, i: (b, i, 0)),
                  pl.BlockSpec((1, tb, D), lambda b, i: (b, i, 0)),
                  pl.BlockSpec((1, 1, D), lambda b, i: (b, 0, 0))],
        out_specs=pl.BlockSpec((1, tb, D), lambda b, i: (b, i, 0)),
        scratch_shapes=[pltpu.VMEM((2, n, D), F32), pltpu.SemaphoreType.DMA((2,))],
        compiler_params=_params(2),
        name="moe_combine",
    )(sl3, sl3, out, gates, x, g_f)


def _ffn_pre_kernel(x_ref, g_ref, sc_ref, sh_ref, rw_ref, rb_ref, h_ref, te_ref, tg_ref):
    h = _norm_mod(x_ref[0], g_ref[...], sc_ref[0], sh_ref[0])
    h_ref[0] = h
    cur = jnp.dot(h, rw_ref[...], preferred_element_type=F32, precision=HIGHEST) + rb_ref[...]
    n_e = cur.shape[1]
    lane_e = lax.broadcasted_iota(jnp.int32, cur.shape, 1).astype(F32)
    lane = lax.broadcasted_iota(jnp.int32, te_ref.shape[1:], 1)
    vals, idxs = [], []
    for k in range(TOP_K):
        m = jnp.max(cur, axis=1, keepdims=True)
        idx = jnp.min(jnp.where(cur == m, lane_e, float(n_e)), axis=1, keepdims=True)
        vals.append(m)
        idxs.append(idx)
        cur = jnp.where(lane_e == idx, -jnp.inf, cur)
    ex = [jnp.exp(v - vals[0]) for v in vals]
    denom = ex[0]
    for e in ex[1:]:
        denom = denom + e
    te = jnp.zeros(lane.shape, F32)
    tg = jnp.zeros(lane.shape, F32)
    for k in range(TOP_K):
        te = jnp.where(lane == k, idxs[k], te)
        tg = jnp.where(lane == k, ex[k] / denom, tg)
    te_ref[0] = te.astype(jnp.int32)
    tg_ref[0] = tg


def ffn_pre(x, g, sc, sh, router_w, router_b):
    B, T, D = x.shape
    E = router_w.shape[1]
    tm = _row_tile(T, 256)
    return pl.pallas_call(
        _ffn_pre_kernel,
        out_shape=(jax.ShapeDtypeStruct((B, T, D), F32), jax.ShapeDtypeStruct((B, T, LANES), jnp.int32),
                   jax.ShapeDtypeStruct((B, T, LANES), F32)),
        grid=(B, T // tm),
        in_specs=[pl.BlockSpec((1, tm, D), lambda b, i: (b, i, 0)),
                  pl.BlockSpec((1, D), lambda b, i: (0, 0)),
                  pl.BlockSpec((1, 1, D), lambda b, i: (b, 0, 0)),
                  pl.BlockSpec((1, 1, D), lambda b, i: (b, 0, 0)),
                  pl.BlockSpec((D, E), lambda b, i: (0, 0)),
                  pl.BlockSpec((1, E), lambda b, i: (0, 0))],
        out_specs=(pl.BlockSpec((1, tm, D), lambda b, i: (b, i, 0)),
                   pl.BlockSpec((1, tm, LANES), lambda b, i: (b, i, 0)),
                   pl.BlockSpec((1, tm, LANES), lambda b, i: (b, i, 0))),
        compiler_params=_params(2),
        name="ffn_pre",
    )(x, g.reshape(1, D), sc, sh, router_w, router_b.reshape(1, E))


def _rms_kernel(x_ref, g_ref, o_ref):
    x = x_ref[0]
    ms = jnp.mean(x * x, axis=-1, keepdims=True)
    o_ref[0] = x * lax.rsqrt(ms + NORM_EPS) * g_ref[...]


def final_norm(x, g):
    B, T, D = x.shape
    tm = _row_tile(T, 512)
    return pl.pallas_call(
        _rms_kernel,
        out_shape=jax.ShapeDtypeStruct((B, T, D), F32),
        grid=(B, T // tm),
        in_specs=[pl.BlockSpec((1, tm, D), lambda b, i: (b, i, 0)),
                  pl.BlockSpec((1, D), lambda b, i: (0, 0))],
        out_specs=pl.BlockSpec((1, tm, D), lambda b, i: (b, i, 0)),
        compiler_params=_params(2),
        name="final_norm",
    )(x, g.reshape(1, D))


def _route(top_e):
    N = top_e.shape[0]
    n4 = N * TOP_K
    flat_e = top_e.reshape(-1)
    order = jnp.argsort(flat_e)
    e_sorted = flat_e[order]
    counts = jnp.bincount(flat_e, length=N_EXPERTS)
    padded = (counts + MOE_BLOCK - 1) // MOE_BLOCK * MOE_BLOCK
    start = jnp.cumsum(counts) - counts
    pstart = jnp.cumsum(padded) - padded
    dest = (pstart[e_sorted] + jnp.arange(n4, dtype=jnp.int32) - start[e_sorted]).astype(jnp.int32)
    n_blocks = (n4 + MOE_BLOCK - 1) // MOE_BLOCK + N_EXPERTS
    rows = n_blocks * MOE_BLOCK
    slot_tok = jnp.zeros((rows,), jnp.int32).at[dest].set((order // TOP_K).astype(jnp.int32))
    slot_flat = jnp.zeros((n4,), jnp.int32).at[order].set(dest)
    block_e = jnp.minimum(jnp.searchsorted(jnp.cumsum(padded), jnp.arange(n_blocks) * MOE_BLOCK, side='right'),
                          N_EXPERTS - 1).astype(jnp.int32)
    return slot_tok, block_e, slot_flat.reshape(N, TOP_K)


def kernel(x, c, ctx, c_ctx, norm_mix_g, norm_ffn_g, w_mod, b_mod, w_in, w_out, q_norm_g, k_norm_g,
           conv_w, w0, w_up, a0, a_up, g_up, k_k, k_a, r_k, ln_x_g, ln_x_b,
           router_w, router_b, e_w1, e_b1, e_w2, e_b2, norm_final_g):
    B, T, D = x.shape
    C = ctx.shape[1]
    depth = w_in.shape[0]
    cos, sin = _rope_tables(T)
    rpad = -(-(B + 1) // SUBLANES) * SUBLANES
    cvecs = jnp.zeros((rpad, D), F32).at[:B].set(c).at[B].set(c_ctx)
    s_zero = jnp.zeros((B, 2 * RW_DIM // QUAD_LANES, QUAD_LANES, QUAD_LANES), F32)
    x_lat, x_ctx = x, ctx
    for l in range(depth):
        last = l == depth - 1
        mods = adaln_all(cvecs, w_mod[l], b_mod[l])
        m_lat = [m[:, None, :] for m in jnp.split(mods[:B], 6, axis=-1)]
        m_ctx = [jnp.broadcast_to(m[None, :, :], (B, 1, D)) for m in jnp.split(mods[B:B + 1], 6, axis=-1)]
        sh_m, sc_m, g_m, sh_f, sc_f, g_f = m_lat
        csh_m, csc_m, cg_m, csh_f, csc_f, cg_f = m_ctx
        w_in_l = w_in[l].astype(BF16)
        w_out_l = w_out[l].astype(BF16)
        p_lat = norm_mod_matmul(x_lat, norm_mix_g[l], sc_m, sh_m, w_in_l)
        p_ctx = norm_mod_matmul(x_ctx, norm_mix_g[l], csc_m, csh_m,
                                w_in_l[:, :STATE_COLS] if last else w_in_l)

        ktc, vc = kv_prep(p_ctx, k_norm_g[l])
        ktl, vl = kv_prep(p_lat, k_norm_g[l], cos, sin)
        o_at = attention(p_lat, q_norm_g[l], ktc, vc, ktl, vl, cos, sin)

        rw = (conv_w[l], w0[l], w_up[l], a0[l], a_up[l], k_k[l], k_a[l])
        w2c, k2c, b2c, v_c, kk_c, r_c = wkv_prep(p_ctx, *rw, has_r=not last)
        w2l, k2l, b2l, v_l, kk_l, r_l = wkv_prep(p_lat, *rw, has_r=True)
        of_c, ob_c, s_c = wkv_chunked(w2c, k2c, b2c, v_c, kk_c, r_c, s_zero)
        of_l, ob_l, _ = wkv_chunked(w2l, k2l, b2l, v_l, kk_l, r_l, s_c)

        ro = (g_up[l], r_k[l], ln_x_g[l], ln_x_b[l], w_out_l)
        x_lat = mixer_out(of_l, ob_l, r_l, v_l, k2l, p_lat, o_at, x_lat, g_m, *ro)
        h_lat, te_lat, tg_lat = ffn_pre(x_lat, norm_ffn_g[l], sc_f, sh_f, router_w[l], router_b[l])
        e_w1_l = e_w1[l].astype(BF16)
        e_w2_l = e_w2[l].astype(BF16)
        if last:
            slot_tok, block_e, slot_flat = _route(te_lat.reshape(B * T, LANES)[:, :TOP_K])
            out = moe_blocks(block_e, slot_tok, h_lat.reshape(B * T, D), e_w1_l, e_b1[l], e_w2_l, e_b2[l])
            x_lat = moe_combine(out, slot_flat.reshape(B, T, TOP_K), tg_lat, x_lat, g_f)
        else:
            o_at_c = attention(p_ctx, q_norm_g[l], ktc, vc)
            x_ctx = mixer_out(of_c, ob_c, r_c, v_c, k2c, p_ctx, o_at_c, x_ctx, cg_m, *ro)
            h_ctx, te_ctx, tg_ctx = ffn_pre(x_ctx, norm_ffn_g[l], csc_f, csh_f, router_w[l], router_b[l])
            h_all = jnp.concatenate([h_lat.reshape(B * T, D), h_ctx.reshape(B * C, D)], axis=0)
            te_all = jnp.concatenate([te_lat.reshape(B * T, LANES), te_ctx.reshape(B * C, LANES)], axis=0)
            slot_tok, block_e, slot_flat = _route(te_all[:, :TOP_K])
            out = moe_blocks(block_e, slot_tok, h_all, e_w1_l, e_b1[l], e_w2_l, e_b2[l])
            x_lat = moe_combine(out, slot_flat[:B * T].reshape(B, T, TOP_K), tg_lat, x_lat, g_f)
            x_ctx = moe_combine(out, slot_flat[B * T:].reshape(B, C, TOP_K), tg_ctx, x_ctx, cg_f)
    return final_norm(x_lat, norm_final_g)
```

```python
import functools
import math

import numpy as np
import jax
import jax.numpy as jnp
from jax import lax
from jax.experimental import pallas as pl
from jax.experimental.pallas import tpu as pltpu

F32 = jnp.float32
BF16 = jnp.bfloat16
HIGHEST = lax.Precision.HIGHEST

GRID_W = 64
HEAD_DIM = 64
ATTN_HEADS = 8
ATTN_KV_HEADS = 2
GROUP = ATTN_HEADS // ATTN_KV_HEADS
ATTN_DIM = ATTN_HEADS * HEAD_DIM
KV_DIM = ATTN_KV_HEADS * HEAD_DIM
RW_HEADS = 8
RW_DIM = RW_HEADS * HEAD_DIM
W_LORA = 64
A_LORA = 64
G_LORA = 128
CONV_W = 3
N_EXPERTS = 32
TOP_K = 4
SWIGLU_LIMIT = 7.0
SWIGLU_ALPHA = 1.702
ROPE_THETA = 10000.0
ROPE_PAIRS = HEAD_DIM // 4
MOE_BLOCK = 256
NORM_EPS = 1e-6
GN_EPS = 64e-5
ATTN_SCALE = HEAD_DIM ** -0.5
EXP_M05 = math.exp(-0.5)
COL_K_AT = 0
COL_V_AT = KV_DIM
COL_K_RW = 2 * KV_DIM
COL_V_RW = COL_K_RW + RW_DIM
COL_XW = COL_V_RW + RW_DIM
COL_XA = COL_XW + 2 * W_LORA
STATE_COLS = COL_XA + 2 * A_LORA
COL_Q_AT = STATE_COLS
COL_R_RW = COL_Q_AT + ATTN_DIM
COL_XG = COL_R_RW + RW_DIM

LANES = 128
SUBLANES = 8
QUAD_LANES = 256
VMEM_LIMIT = 48 * 1024 * 1024


def _row_tile(n, pref):
    t = pref
    while n % t:
        t //= 2
    return t


def _params(n_axes):
    return pltpu.CompilerParams(dimension_semantics=("arbitrary",) * n_axes, vmem_limit_bytes=VMEM_LIMIT)


def _ones_bd(n):
    ra = lax.broadcasted_iota(jnp.int32, (n, n), 0) // HEAD_DIM
    rb = lax.broadcasted_iota(jnp.int32, (n, n), 1) // HEAD_DIM
    return (ra == rb).astype(BF16)


def _seg_sum(x_bf16, ones_bd):
    return jnp.dot(x_bf16, ones_bd, preferred_element_type=F32)


def _seg_sum2(x, ones_bd):
    hi = x.astype(BF16)
    lo = (x - hi.astype(F32)).astype(BF16)
    return _seg_sum(hi, ones_bd) + _seg_sum(lo, ones_bd)


def _adaln_kernel(c_ref, w_ref, b_ref, o_ref):
    c = c_ref[...]
    s = c * jax.nn.sigmoid(c)
    o_ref[...] = jnp.dot(s, w_ref[...], preferred_element_type=F32, precision=HIGHEST) + b_ref[...]


def adaln_all(cvecs, w_mod, b_mod):
    R, D = cvecs.shape
    N = w_mod.shape[1]
    tn = 512
    return pl.pallas_call(
        _adaln_kernel,
        out_shape=jax.ShapeDtypeStruct((R, N), F32),
        grid=(N // tn,),
        in_specs=[pl.BlockSpec((R, D), lambda j: (0, 0)),
                  pl.BlockSpec((D, tn), lambda j: (0, j)),
                  pl.BlockSpec((1, tn), lambda j: (0, j))],
        out_specs=pl.BlockSpec((R, tn), lambda j: (0, j)),
        compiler_params=_params(1),
        name="adaln",
    )(cvecs, w_mod, b_mod.reshape(1, N))


def _norm_mod(x, g, sc, sh):
    ms = jnp.mean(x * x, axis=-1, keepdims=True)
    y = x * lax.rsqrt(ms + NORM_EPS) * g
    return y * (1.0 + sc) + sh


def _norm_mod_matmul_kernel(x_ref, g_ref, sc_ref, sh_ref, w_ref, o_ref):
    h = _norm_mod(x_ref[0], g_ref[...], sc_ref[0], sh_ref[0])
    o_ref[0] = jnp.dot(h.astype(BF16), w_ref[...], preferred_element_type=F32)


def norm_mod_matmul(x, g, sc, sh, w):
    B, T, D = x.shape
    N = w.shape[1]
    tm = _row_tile(T, 256)
    return pl.pallas_call(
        _norm_mod_matmul_kernel,
        out_shape=jax.ShapeDtypeStruct((B, T, N), F32),
        grid=(B, T // tm),
        in_specs=[pl.BlockSpec((1, tm, D), lambda b, i: (b, i, 0)),
                  pl.BlockSpec((1, D), lambda b, i: (0, 0)),
                  pl.BlockSpec((1, 1, D), lambda b, i: (b, 0, 0)),
                  pl.BlockSpec((1, 1, D), lambda b, i: (b, 0, 0)),
                  pl.BlockSpec((D, N), lambda b, i: (0, 0))],
        out_specs=pl.BlockSpec((1, tm, N), lambda b, i: (b, i, 0)),
        compiler_params=_params(2),
        name="norm_mod_matmul",
    )(x, g.reshape(1, D), sc, sh, w)


def _pair_rms(x, g):
    lane = lax.broadcasted_iota(jnp.int32, x.shape, 1)
    first = lane < HEAD_DIM
    xx = x * x
    s0 = jnp.sum(jnp.where(first, xx, 0.0), axis=1, keepdims=True)
    s1 = jnp.sum(jnp.where(first, 0.0, xx), axis=1, keepdims=True)
    ms = jnp.where(first, s0, s1) * (1.0 / HEAD_DIM)
    return x * lax.rsqrt(ms + NORM_EPS) * g


def _pair_rope(x, cos, sin):
    lane = lax.broadcasted_iota(jnp.int32, x.shape, 1)
    partner = jnp.where(lane % (2 * ROPE_PAIRS) < ROPE_PAIRS,
                        pltpu.roll(x, LANES - ROPE_PAIRS, 1), pltpu.roll(x, ROPE_PAIRS, 1))
    return x * cos + partner * sin


def _rope_tables(T):
    rows = T // GRID_W
    row = jnp.repeat(jnp.arange(rows), GRID_W)
    col = jnp.tile(jnp.arange(GRID_W), rows)
    pos = jnp.stack([row, col], axis=-1).astype(F32)
    freqs = ROPE_THETA ** (-jnp.arange(ROPE_PAIRS, dtype=F32) / ROPE_PAIRS)
    ang = pos[:, :, None] * freqs
    cos, sin = jnp.cos(ang), jnp.sin(ang)
    cos_h = jnp.concatenate([cos[:, 0], cos[:, 0], cos[:, 1], cos[:, 1]], axis=-1)
    sin_h = jnp.concatenate([-sin[:, 0], sin[:, 0], -sin[:, 1], sin[:, 1]], axis=-1)
    return jnp.tile(cos_h, (1, 2)), jnp.tile(sin_h, (1, 2))


def _kv_prep_kernel(*refs, rope):
    if rope:
        p_ref, g_ref, cos_ref, sin_ref, kt_ref, v_ref = refs
    else:
        p_ref, g_ref, kt_ref, v_ref = refs
    x = p_ref[0]
    k = _pair_rms(x[:, :KV_DIM], g_ref[...])
    if rope:
        k = _pair_rope(k, cos_ref[...], sin_ref[...])
    v = x[:, KV_DIM:]
    lane = lax.broadcasted_iota(jnp.int32, k.shape, 1)
    first = lane < HEAD_DIM
    kr = pltpu.roll(k, HEAD_DIM, 1)
    vr = pltpu.roll(v, HEAD_DIM, 1)
    kt_ref[0, 0] = jnp.where(first, k, kr).T.astype(BF16)
    kt_ref[0, 1] = jnp.where(first, kr, k).T.astype(BF16)
    v_ref[0, 0] = jnp.where(first, v, vr).astype(BF16)
    v_ref[0, 1] = jnp.where(first, vr, v).astype(BF16)


def kv_prep(p, k_norm_g, cos=None, sin=None):
    B, T, _ = p.shape
    tm = _row_tile(T, 256)
    rope = cos is not None
    g2 = jnp.tile(k_norm_g, 2).reshape(1, LANES)
    in_specs = [pl.BlockSpec((1, tm, 2 * KV_DIM), lambda b, i: (b, i, 0)),
                pl.BlockSpec((1, LANES), lambda b, i: (0, 0))]
    args = [p, g2]
    if rope:
        in_specs += [pl.BlockSpec((tm, LANES), lambda b, i: (i, 0))] * 2
        args += [cos, sin]
    return pl.pallas_call(
        functools.partial(_kv_prep_kernel, rope=rope),
        out_shape=(jax.ShapeDtypeStruct((B, ATTN_KV_HEADS, LANES, T), BF16),
                   jax.ShapeDtypeStruct((B, ATTN_KV_HEADS, T, LANES), BF16)),
        grid=(B, T // tm),
        in_specs=in_specs,
        out_specs=(pl.BlockSpec((1, ATTN_KV_HEADS, LANES, tm), lambda b, i: (b, 0, 0, i)),
                   pl.BlockSpec((1, ATTN_KV_HEADS, tm, LANES), lambda b, i: (b, 0, i, 0))),
        compiler_params=_params(2),
        name="kv_prep",
    )(*args)


def _attn_kernel(*refs, tq, tkl, nkl, rope, has_lat):
    refs = list(refs)
    p_ref, g_ref = refs[:2]
    refs = refs[2:]
    if rope:
        cos_ref, sin_ref = refs[:2]
        refs = refs[2:]
    ktc_ref, vc_ref = refs[:2]
    refs = refs[2:]
    if has_lat:
        ktl_ref, vl_ref = refs[:2]
        refs = refs[2:]
    o_ref, m_ref, l_ref, acc_ref = refs

    q = _pair_rms(p_ref[0], g_ref[...])
    if rope:
        q = _pair_rope(q, cos_ref[...], sin_ref[...])
    q = q * ATTN_SCALE
    lane = lax.broadcasted_iota(jnp.int32, (tq, LANES), 1)
    first = lane < HEAD_DIM
    qq = jnp.concatenate([jnp.where(first, q, 0.0), jnp.where(first, 0.0, q)], axis=0).astype(BF16)

    m_ref[...] = jnp.full(m_ref.shape, -jnp.inf, F32)
    l_ref[...] = jnp.zeros(l_ref.shape, F32)
    acc_ref[...] = jnp.zeros(acc_ref.shape, F32)

    def chunk(kt, vv):
        s = jnp.dot(qq, kt, preferred_element_type=F32)
        m_old = m_ref[...]
        m_new = jnp.maximum(m_old, jnp.max(s, axis=-1, keepdims=True))
        alpha = jnp.exp(m_old - m_new)
        p = jnp.exp(s - jnp.concatenate([m_new] * (s.shape[1] // LANES), axis=1))
        l_ref[...] = alpha * l_ref[...] + jnp.sum(p, axis=-1, keepdims=True)
        acc_ref[...] = alpha * acc_ref[...] + jnp.dot(p.astype(BF16), vv, preferred_element_type=F32)
        m_ref[...] = m_new

    chunk(ktc_ref[0, 0], vc_ref[0, 0])
    if has_lat:
        def body(j, carry):
            ks = pl.multiple_of(j * tkl, tkl)
            chunk(ktl_ref[0, 0, :, pl.ds(ks, tkl)], vl_ref[0, 0, pl.ds(ks, tkl), :])
            return carry
        lax.fori_loop(0, nkl, body, 0)

    o = acc_ref[...] / l_ref[...]
    o_ref[0] = jnp.where(first, o[:tq], o[tq:])


def attention(p, q_norm_g, ktc, vc, ktl=None, vl=None, cos=None, sin=None):
    B, T, _ = p.shape
    C = ktc.shape[-1]
    rope = cos is not None
    has_lat = ktl is not None
    tq = _row_tile(T, 256)
    pairs = ATTN_HEADS // 2
    pairs_per_kv = GROUP // 2
    qblk0 = COL_Q_AT // LANES
    g2 = jnp.tile(q_norm_g, 2).reshape(1, LANES)
    in_specs = [pl.BlockSpec((1, tq, LANES), lambda b, h, i: (b, i, qblk0 + h)),
                pl.BlockSpec((1, LANES), lambda b, h, i: (0, 0))]
    args = [p, g2]
    if rope:
        in_specs += [pl.BlockSpec((tq, LANES), lambda b, h, i: (i, 0))] * 2
        args += [cos, sin]
    in_specs += [pl.BlockSpec((1, 1, LANES, C), lambda b, h, i: (b, h // pairs_per_kv, 0, 0)),
                 pl.BlockSpec((1, 1, C, LANES), lambda b, h, i: (b, h // pairs_per_kv, 0, 0))]
    args += [ktc, vc]
    tkl = nkl = 0
    if has_lat:
        S = ktl.shape[-1]
        tkl = _row_tile(S, 1024)
        nkl = S // tkl
        in_specs += [pl.BlockSpec((1, 1, LANES, S), lambda b, h, i: (b, h // pairs_per_kv, 0, 0)),
                     pl.BlockSpec((1, 1, S, LANES), lambda b, h, i: (b, h // pairs_per_kv, 0, 0))]
        args += [ktl, vl]
    kern = functools.partial(_attn_kernel, tq=tq, tkl=tkl, nkl=nkl, rope=rope, has_lat=has_lat)
    return pl.pallas_call(
        kern,
        out_shape=jax.ShapeDtypeStruct((B, T, ATTN_DIM), F32),
        grid=(B, pairs, T // tq),
        in_specs=in_specs,
        out_specs=pl.BlockSpec((1, tq, LANES), lambda b, h, i: (b, i, h)),
        scratch_shapes=[pltpu.VMEM((2 * tq, LANES), F32)] * 3,
        compiler_params=_params(3),
        name="attention",
    )(*args)


def _wkv_prep_kernel(p_ref, pp_ref, pn_ref, cw_ref, w0_ref, wup_ref, a0_ref, aup_ref, kk_ref, ka_ref,
                     w2_ref, k2_ref, b2_ref, v_ref, kkn_ref, r_ref, *, tm, has_r):
    i = pl.program_id(1)
    n = pl.num_programs(1)
    x = p_ref[0]
    prev = pp_ref[0][SUBLANES - 1:SUBLANES, :] * (i > 0).astype(F32)
    nxt = pn_ref[0][0:1, :] * (i < n - 1).astype(F32)
    row = lax.broadcasted_iota(jnp.int32, (tm, RW_DIM), 0)

    def conv(col, wcol):
        u = x[:, col:col + RW_DIM]
        up = jnp.where(row == 0, prev[:, col:col + RW_DIM], pltpu.roll(u, 1, 0))
        un = jnp.where(row == tm - 1, nxt[:, col:col + RW_DIM], pltpu.roll(u, tm - 1, 0))
        w = cw_ref[:, wcol:wcol + RW_DIM]
        return up * w[0:1] + u * w[1:2] + un * w[2:3]

    k = conv(COL_K_RW, RW_DIM)
    v = conv(COL_V_RW, 2 * RW_DIM)
    v_ref[0] = v
    r_ref[0] = conv(COL_R_RW, 0) if has_r else jnp.zeros((tm, RW_DIM), F32)
    kk = k * kk_ref[...]
    ss = _seg_sum2(kk * kk, _ones_bd(RW_DIM))
    kk = kk * lax.rsqrt(jnp.maximum(ss, 1e-24))
    kkn_ref[0] = kk
    xw = jnp.tanh(x[:, COL_XW:COL_XW + 2 * W_LORA])
    xa = x[:, COL_XA:COL_XA + 2 * A_LORA]
    for d in range(2):
        wl = w0_ref[d:d + 1, :] + jnp.dot(xw[:, d * W_LORA:(d + 1) * W_LORA], wup_ref[d],
                                          preferred_element_type=F32, precision=HIGHEST)
        decay = -EXP_M05 * jax.nn.sigmoid(wl)
        a = jax.nn.sigmoid(a0_ref[d:d + 1, :] + jnp.dot(xa[:, d * A_LORA:(d + 1) * A_LORA], aup_ref[d],
                                                        preferred_element_type=F32, precision=HIGHEST))
        sl = slice(d * RW_DIM, (d + 1) * RW_DIM)
        w2_ref[0, :, sl] = decay
        k2_ref[0, :, sl] = k * (1.0 + (a - 1.0) * ka_ref[...])
        b2_ref[0, :, sl] = kk * a


def wkv_prep(p, conv_w, w0, w_up, a0, a_up, k_k, k_a, has_r):
    B, T, NP = p.shape
    tm = _row_tile(T, 256)
    nb8 = tm // SUBLANES
    last8 = T // SUBLANES - 1
    full = lambda a: pl.BlockSpec(a.shape, lambda b, i: (0,) * a.ndim)
    kk2, ka2 = k_k.reshape(1, RW_DIM), k_a.reshape(1, RW_DIM)
    wide = jax.ShapeDtypeStruct((B, T, 2 * RW_DIM), F32)
    narrow = jax.ShapeDtypeStruct((B, T, RW_DIM), F32)
    wspec = pl.BlockSpec((1, tm, 2 * RW_DIM), lambda b, i: (b, i, 0))
    nspec = pl.BlockSpec((1, tm, RW_DIM), lambda b, i: (b, i, 0))
    return pl.pallas_call(
        functools.partial(_wkv_prep_kernel, tm=tm, has_r=has_r),
        out_shape=(wide, wide, wide, narrow, narrow, narrow),
        grid=(B, T // tm),
        in_specs=[pl.BlockSpec((1, tm, NP), lambda b, i: (b, i, 0)),
                  pl.BlockSpec((1, SUBLANES, NP), lambda b, i: (b, jnp.maximum(i * nb8 - 1, 0), 0)),
                  pl.BlockSpec((1, SUBLANES, NP), lambda b, i: (b, jnp.minimum((i + 1) * nb8, last8), 0)),
                  full(conv_w), full(w0), full(w_up), full(a0), full(a_up), full(kk2), full(ka2)],
        out_specs=(wspec, wspec, wspec, nspec, nspec, nspec),
        compiler_params=_params(2),
        name="wkv_prep",
    )(p, p, p, conv_w, w0, w_up, a0, a_up, kk2, ka2)


def _wkv_kernel(wf_ref, wb_ref, kf_ref, kb_ref, bf_ref, bb_ref, vf_ref, vb_ref, af_ref, ab_ref,
                rf_ref, rb_ref, s0_ref, of_ref, ob_ref, st_ref, s_ref, *, tc, nquads):
    tb = pl.program_id(1)

    @pl.when(tb == 0)
    def _():
        s_ref[...] = s0_ref[0]

    rows_all = 2 * nquads * HEAD_DIM
    shp = (rows_all, QUAD_LANES)
    lane = lax.broadcasted_iota(jnp.int32, shp, 1)
    sub = lax.broadcasted_iota(jnp.int32, shp, 0)
    eye = (lane % HEAD_DIM == sub % HEAD_DIM).astype(F32)
    ones_bd = _ones_bd(QUAD_LANES)
    ng = tc // SUBLANES

    def bcast_rows(blks, j):
        parts = []
        for d in range(2):
            jj = j if d == 0 else SUBLANES - 1 - j
            for q in range(nquads):
                row = blks[d][jj:jj + 1, q * QUAD_LANES:(q + 1) * QUAD_LANES]
                parts.append(jnp.broadcast_to(row, (HEAD_DIM, QUAD_LANES)))
        return jnp.concatenate(parts, axis=0)

    def group(g, carry):
        tf = pl.multiple_of(g * SUBLANES, SUBLANES)
        tr = pl.multiple_of((ng - 1 - g) * SUBLANES, SUBLANES)
        ld = lambda fr, br: [fr[0, pl.ds(tf, SUBLANES), :], br[0, pl.ds(tr, SUBLANES), :]]
        wq, kq, bq, vq, rq = (ld(f, b) for f, b in ((wf_ref, wb_ref), (kf_ref, kb_ref), (bf_ref, bb_ref),
                                                    (vf_ref, vb_ref), (rf_ref, rb_ref)))
        aq = [-x for x in ld(af_ref, ab_ref)]
        S = s_ref[...]
        out_rows = [[] for _ in range(2 * nquads)]
        for j in range(SUBLANES):
            wt, kt, vt, at, bt, rt = (bcast_rows(x, j) for x in (wq, kq, vq, aq, bq, rq))
            sa = _seg_sum2(S * at, ones_bd)
            vcol = _seg_sum((eye * vt).astype(BF16), ones_bd)
            S = S * wt + sa * bt + vcol * kt
            ob = _seg_sum((S * rt).astype(BF16), ones_bd) * eye
            for c in range(2 * nquads):
                out_rows[c].append(jnp.sum(ob[c * HEAD_DIM:(c + 1) * HEAD_DIM], axis=0, keepdims=True))
        s_ref[...] = S
        for q in range(nquads):
            sl = slice(q * QUAD_LANES, (q + 1) * QUAD_LANES)
            of_ref[0, pl.ds(tf, SUBLANES), sl] = jnp.concatenate(out_rows[q], axis=0)
            ob_ref[0, pl.ds(tr, SUBLANES), sl] = jnp.concatenate(out_rows[nquads + q][::-1], axis=0)
        return carry

    lax.fori_loop(0, ng, group, 0)

    @pl.when(tb == pl.num_programs(1) - 1)
    def _():
        st_ref[0] = s_ref[...]


def wkv_scan(w2, k2, b2, v, kk, r, s0):
    B, T, C = v.shape
    nquads = C // QUAD_LANES
    tc = _row_tile(T, 128)
    nt = T // tc
    fwd = lambda col: pl.BlockSpec((1, tc, C), lambda b, t: (b, t, col))
    bwd = lambda col: pl.BlockSpec((1, tc, C), lambda b, t: (b, nt - 1 - t, col))
    sspec = pl.BlockSpec((1,) + s0.shape[1:], lambda b, t: (b, 0, 0))
    kern = functools.partial(_wkv_kernel, tc=tc, nquads=nquads)
    return pl.pallas_call(
        kern,
        out_shape=(jax.ShapeDtypeStruct((B, T, C), F32), jax.ShapeDtypeStruct((B, T, C), F32),
                   jax.ShapeDtypeStruct(s0.shape, F32)),
        grid=(B, nt),
        in_specs=[fwd(0), bwd(1), fwd(0), bwd(1), fwd(0), bwd(1), fwd(0), bwd(0), fwd(0), bwd(0),
                  fwd(0), bwd(0), sspec],
        out_specs=(fwd(0), bwd(0), sspec),
        scratch_shapes=[pltpu.VMEM(s0.shape[1:], F32)],
        compiler_params=_params(2),
        name="wkv_scan",
    )(w2, w2, k2, k2, b2, b2, v, v, kk, kk, r, r, s0)


CHUNK = 64


def _split_bf16(x):
    hi = x.astype(BF16)
    return hi, (x - hi.astype(F32)).astype(BF16)


def _mm3(a, b, nt=False, passes=3):
    dn = (((1,), (1 if nt else 0,)), ((), ()))
    if passes == 1:
        return lax.dot_general(a.astype(BF16), b.astype(BF16), dn, preferred_element_type=F32)
    ah, al = _split_bf16(a)
    m = a.shape[0]
    if passes == 2:
        top = lax.dot_general(jnp.concatenate([ah, al], axis=0), b.astype(BF16), dn, preferred_element_type=F32)
        return top[:m] + top[m:]
    bh, bl = _split_bf16(b)
    top = lax.dot_general(jnp.concatenate([ah, al], axis=0), bh, dn, preferred_element_type=F32)
    return top[:m] + top[m:] + lax.dot_general(ah, bl, dn, preferred_element_type=F32)


def _chunk_terms(chains):
    L = CHUNK
    nch = range(len(chains))
    row = lax.broadcasted_iota(jnp.int32, (L, QUAD_LANES), 0)
    col = lax.broadcasted_iota(jnp.int32, (L, QUAD_LANES), 1) % HEAD_DIM
    tr = lax.broadcasted_iota(jnp.int32, (L, L), 0)
    tc = lax.broadcasted_iota(jnp.int32, (L, L), 1)
    masks = {True: (col > row, col >= row, (tc >= tr).astype(F32)),
             False: (col < row, col <= row, (tc <= tr).astype(F32))}
    eye_l = (col == row).astype(F32)
    ra = lax.broadcasted_iota(jnp.int32, (QUAD_LANES, QUAD_LANES), 0)
    rb = lax.broadcasted_iota(jnp.int32, (QUAD_LANES, QUAD_LANES), 1)
    bd = ra // HEAD_DIM == rb // HEAD_DIM
    eye_q = (ra == rb).astype(F32)

    def bdiag(y):
        return jnp.where(bd, jnp.concatenate([y] * (QUAD_LANES // L), axis=0), 0.0)

    def bdm(x, y, passes):
        ys = [bdiag(y[:, i:i + QUAD_LANES]) for i in range(0, y.shape[1], QUAD_LANES)]
        return _mm3(x, ys[0] if len(ys) == 1 else jnp.concatenate(ys, axis=1), passes=passes)

    lw, kd, bb, v, kk, r, back = (list(x) for x in zip(*chains))
    strict = [masks[b][0] for b in back]
    incl = [masks[b][1] for b in back]
    cs = [_mm3(masks[back[c]][2], lw[c]) for c in nch]
    g_last = [jnp.exp(cs[c][0:1] if back[c] else cs[c][L - 1:L]) for c in nch]
    at = [-kk[c] * jnp.exp(cs[c] - lw[c]) for c in nch]
    g_inv = [jnp.exp(-cs[c]) for c in nch]
    bt = [bb[c] * g_inv[c] for c in nch]
    kt = [kd[c] * g_inv[c] for c in nch]
    rt = [r[c] * jnp.exp(cs[c]) for c in nch]
    ar = [jnp.concatenate([at[c], rt[c]], axis=0) for c in nch]
    pb = [_mm3(ar[c], bdiag(bt[c]), nt=True) for c in nch]
    pk = [_mm3(ar[c], bdiag(kt[c]), nt=True) for c in nch]
    n = [jnp.where(strict[c], pb[c][:L], 0.0) for c in nch]
    m_rb = [jnp.where(incl[c], pb[c][L:], 0.0) for c in nch]
    m_ak = [jnp.where(strict[c], pk[c][:L], 0.0) for c in nch]
    m_rk = [jnp.where(incl[c], pk[c][L:], 0.0) for c in nch]
    steps = int(math.log2(L)) - 1
    p = [bdm(n[c], n[c], 3) for c in nch]
    t_inv = [eye_l + n[c] for c in nch]
    for k in range(steps):
        if k + 1 < steps:
            both = [bdm(jnp.concatenate([p[c], t_inv[c]], axis=0), p[c], 1) for c in nch]
            p = [both[c][:L] for c in nch]
            t_inv = [t_inv[c] + both[c][L:] for c in nch]
        else:
            t_inv = [t_inv[c] + bdm(t_inv[c], p[c], 1) for c in nch]
    xy = [bdm(jnp.concatenate([m_ak[c], m_rk[c]], axis=0), v[c], 1) for c in nch]
    w = [bdm(t_inv[c], jnp.concatenate([at[c], xy[c][:L]], axis=1), 2) for c in nch]
    qq = [bdm(m_rb[c], w[c], 1) for c in nch]
    q1 = [rt[c] + qq[c][:, :QUAD_LANES] for c in nch]
    q2 = [xy[c][L:] + qq[c][:, QUAD_LANES:] for c in nch]
    g = [jnp.where(bd, (eye_q + _mm3(w[c][:, :QUAD_LANES].T, bt[c], passes=2)) * g_last[c], 0.0) for c in nch]
    h = [jnp.where(bd, _mm3(jnp.concatenate([w[c][:, QUAD_LANES:], v[c]], axis=0).T,
                            jnp.concatenate([bt[c], kt[c]], axis=0), passes=2) * g_last[c], 0.0) for c in nch]
    return [(q1[c], q2[c], g[c].T, h[c].T) for c in nch]


def _wkv_chunk_kernel(lw_ref, k_ref, b_ref, v_ref, kk_ref, r_ref, q1_ref, q2_ref, gt_ref, ht_ref, *, nquads):
    chains = []
    for d in range(2):
        for q in range(nquads):
            sl = slice(q * QUAD_LANES, (q + 1) * QUAD_LANES)
            sd = slice(d * RW_DIM + q * QUAD_LANES, d * RW_DIM + (q + 1) * QUAD_LANES)
            chains.append((lw_ref[0, :, sd], k_ref[0, :, sd], b_ref[0, :, sd],
                           v_ref[0, :, sl], kk_ref[0, :, sl], r_ref[0, :, sl], d == 1))
    res = _chunk_terms(chains)
    for d in range(2):
        for q in range(nquads):
            sl = slice(q * QUAD_LANES, (q + 1) * QUAD_LANES)
            q1, q2, gt, ht = res[d * nquads + q]
            q1_ref[0, d, :, sl] = q1
            q2_ref[0, d, :, sl] = q2
            gt_ref[0, d, 0, q] = gt
            ht_ref[0, d, 0, q] = ht


def wkv_chunk_prep(lw2, k2, b2, v, kk, r):
    B, T, C = v.shape
    nquads = C // QUAD_LANES
    nc = T // CHUNK
    wide = pl.BlockSpec((1, CHUNK, 2 * C), lambda b, c: (b, c, 0))
    narrow = pl.BlockSpec((1, CHUNK, C), lambda b, c: (b, c, 0))
    qspec = pl.BlockSpec((1, 2, CHUNK, C), lambda b, c: (b, 0, c, 0))
    gspec = pl.BlockSpec((1, 2, 1, nquads, QUAD_LANES, QUAD_LANES), lambda b, c: (b, 0, c, 0, 0, 0))
    qshape = jax.ShapeDtypeStruct((B, 2, T, C), F32)
    gshape = jax.ShapeDtypeStruct((B, 2, nc, nquads, QUAD_LANES, QUAD_LANES), F32)
    return pl.pallas_call(
        functools.partial(_wkv_chunk_kernel, nquads=nquads),
        out_shape=(qshape, qshape, gshape, gshape),
        grid=(B, nc),
        in_specs=[wide, wide, wide, narrow, narrow, narrow],
        out_specs=(qspec, qspec, gspec, gspec),
        compiler_params=_params(2),
        name="wkv_chunk_prep",
    )(lw2, k2, b2, v, kk, r)


def _wkv_seq_kernel(q1f_ref, q1b_ref, q2f_ref, q2b_ref, gtf_ref, gtb_ref, htf_ref, htb_ref, s0_ref,
                    of_ref, ob_ref, st_ref, z_ref, *, nquads):
    c = pl.program_id(1)

    @pl.when(c == 0)
    def _():
        z_ref[...] = s0_ref[0]

    dirs = ((q1f_ref, q2f_ref, gtf_ref, htf_ref, of_ref), (q1b_ref, q2b_ref, gtb_ref, htb_ref, ob_ref))
    chains = [(d, q, slice(q * QUAD_LANES, (q + 1) * QUAD_LANES)) for d in range(2) for q in range(nquads)]
    lhs = [jnp.concatenate([dirs[d][0][0, 0, :, sl], dirs[d][2][0, 0, 0, q]], axis=0) for d, q, sl in chains]
    res = [_mm3(lhs[i], z_ref[d * nquads + q]) for i, (d, q, sl) in enumerate(chains)]
    for i, (d, q, sl) in enumerate(chains):
        dirs[d][4][0, :, sl] = res[i][:CHUNK] + dirs[d][1][0, 0, :, sl]
        z_ref[d * nquads + q] = res[i][CHUNK:] + dirs[d][3][0, 0, 0, q]

    @pl.when(c == pl.num_programs(1) - 1)
    def _():
        st_ref[0] = z_ref[...]


def wkv_seq(q1, q2, gt, ht, s0):
    B, _, T, C = q1.shape
    nc, nquads = gt.shape[2], gt.shape[3]
    qf = pl.BlockSpec((1, 1, CHUNK, C), lambda b, c: (b, 0, c, 0))
    qb = pl.BlockSpec((1, 1, CHUNK, C), lambda b, c: (b, 1, nc - 1 - c, 0))
    gshape = (1, 1, 1, nquads, QUAD_LANES, QUAD_LANES)
    gf = pl.BlockSpec(gshape, lambda b, c: (b, 0, c, 0, 0, 0))
    gb = pl.BlockSpec(gshape, lambda b, c: (b, 1, nc - 1 - c, 0, 0, 0))
    sspec = pl.BlockSpec((1,) + s0.shape[1:], lambda b, c: (b, 0, 0, 0))
    return pl.pallas_call(
        functools.partial(_wkv_seq_kernel, nquads=nquads),
        out_shape=(jax.ShapeDtypeStruct((B, T, C), F32), jax.ShapeDtypeStruct((B, T, C), F32),
                   jax.ShapeDtypeStruct(s0.shape, F32)),
        grid=(B, nc),
        in_specs=[qf, qb, qf, qb, gf, gb, gf, gb, sspec],
        out_specs=(pl.BlockSpec((1, CHUNK, C), lambda b, c: (b, c, 0)),
                   pl.BlockSpec((1, CHUNK, C), lambda b, c: (b, nc - 1 - c, 0)), sspec),
        scratch_shapes=[pltpu.VMEM(s0.shape[1:], F32)],
        compiler_params=_params(2),
        name="wkv_seq",
    )(q1, q1, q2, q2, gt, gt, ht, ht, s0)


def wkv_chunked(lw2, k2, b2, v, kk, r, s0):
    q1, q2, gt, ht = wkv_chunk_prep(lw2, k2, b2, v, kk, r)
    return wkv_seq(q1, q2, gt, ht, s0)


def _mixer_out_kernel(of_ref, ob_ref, r_ref, v_ref, k2_ref, xg_ref, oat_ref, x_ref, gm_ref,
                      gup_ref, rk_ref, lng_ref, lnb_ref, wo_ref, o_ref):
    ones = _ones_bd(RW_DIM)
    inv = 1.0 / HEAD_DIM
    o = of_ref[0] + ob_ref[0]
    mu = _seg_sum2(o, ones) * inv
    dv = o - mu
    var = _seg_sum2(dv * dv, ones) * inv
    gn = dv * lax.rsqrt(var + GN_EPS) * lng_ref[...] + lnb_ref[...]
    k2 = k2_ref[0]
    k_bonus = (k2[:, :RW_DIM] + k2[:, RW_DIM:]) * 0.5
    bonus = _seg_sum2(r_ref[0] * k_bonus * rk_ref[...], ones) * v_ref[0]
    g = jnp.dot(jax.nn.sigmoid(xg_ref[0]), gup_ref[...], preferred_element_type=F32, precision=HIGHEST)
    y = (gn + bonus) * g
    acc = jnp.dot(oat_ref[0].astype(BF16), wo_ref[:ATTN_DIM, :], preferred_element_type=F32)
    acc = acc + jnp.dot(y.astype(BF16), wo_ref[ATTN_DIM:, :], preferred_element_type=F32)
    o_ref[0] = x_ref[0] + gm_ref[0] * acc


def mixer_out(o_f, o_b, r, v, k2, p, o_at, x, g_m, g_up, r_k, ln_x_g, ln_x_b, w_out):
    B, T, D = x.shape
    tm = _row_tile(T, 256)
    nspec = pl.BlockSpec((1, tm, RW_DIM), lambda b, i: (b, i, 0))
    full = lambda a: pl.BlockSpec(a.shape, lambda b, i: (0,) * a.ndim)
    rk2, lng2, lnb2 = r_k.reshape(1, RW_DIM), ln_x_g.reshape(1, RW_DIM), ln_x_b.reshape(1, RW_DIM)
    return pl.pallas_call(
        _mixer_out_kernel,
        out_shape=jax.ShapeDtypeStruct((B, T, D), F32),
        grid=(B, T // tm),
        in_specs=[nspec, nspec, nspec, nspec,
                  pl.BlockSpec((1, tm, 2 * RW_DIM), lambda b, i: (b, i, 0)),
                  pl.BlockSpec((1, tm, G_LORA), lambda b, i: (b, i, COL_XG // G_LORA)),
                  pl.BlockSpec((1, tm, ATTN_DIM), lambda b, i: (b, i, 0)),
                  pl.BlockSpec((1, tm, D), lambda b, i: (b, i, 0)),
                  pl.BlockSpec((1, 1, D), lambda b, i: (b, 0, 0)),
                  full(g_up), full(rk2), full(lng2), full(lnb2), full(w_out)],
        out_specs=pl.BlockSpec((1, tm, D), lambda b, i: (b, i, 0)),
        compiler_params=_params(2),
        name="mixer_out",
    )(o_f, o_b, r, v, k2, p, o_at, x, g_m, g_up, rk2, lng2, lnb2, w_out)


def _start_row_copies(idx_ref, n, src_hbm, dst_ref, sem):
    def body(r, carry):
        pltpu.make_async_copy(src_hbm.at[pl.ds(idx_ref[0, 0, r], 1)], dst_ref.at[pl.ds(r, 1)], sem).start()
        return carry
    lax.fori_loop(0, n, body, 0, unroll=8)


def _wait_row_copies(n, src_hbm, dst_ref, sem):
    pltpu.make_async_copy(src_hbm.at[pl.ds(0, n)], dst_ref.at[pl.ds(0, n)], sem).wait()


def _moe_kernel(be_ref, tok_ref, tokn_ref, h_hbm, w1_ref, b1_ref, w2_ref, b2_ref, o_ref, buf, sems, *, f, tm):
    i = pl.program_id(0)
    slot = i % 2

    @pl.when(i == 0)
    def _():
        _start_row_copies(tok_ref, tm, h_hbm, buf.at[0], sems.at[0])

    @pl.when(i + 1 < pl.num_programs(0))
    def _():
        _start_row_copies(tokn_ref, tm, h_hbm, buf.at[1 - slot], sems.at[1 - slot])

    _wait_row_copies(tm, h_hbm, buf.at[slot], sems.at[slot])
    u = jnp.dot(buf[slot].astype(BF16), w1_ref[0], preferred_element_type=F32) + b1_ref[0]
    x_glu = jnp.minimum(u[:, :f], SWIGLU_LIMIT)
    x_lin = jnp.clip(u[:, f:], -SWIGLU_LIMIT, SWIGLU_LIMIT)
    act = x_glu * jax.nn.sigmoid(SWIGLU_ALPHA * x_glu) * (x_lin + 1.0)
    o_ref[...] = jnp.dot(act.astype(BF16), w2_ref[0], preferred_element_type=F32) + b2_ref[0]


def moe_blocks(block_e, slot_tok, h, w1, b1, w2, b2):
    N, D = h.shape
    E, _, F2 = w1.shape
    f = F2 // 2
    tm = MOE_BLOCK
    rows = slot_tok.shape[0]
    nb = rows // tm
    tok3 = slot_tok.reshape(nb, 1, tm)
    kern = functools.partial(_moe_kernel, f=f, tm=tm)
    grid_spec = pltpu.PrefetchScalarGridSpec(
        num_scalar_prefetch=1,
        grid=(nb,),
        in_specs=[pl.BlockSpec((1, 1, tm), lambda i, be: (i, 0, 0), memory_space=pltpu.SMEM),
                  pl.BlockSpec((1, 1, tm), lambda i, be: (jnp.minimum(i + 1, nb - 1), 0, 0),
                               memory_space=pltpu.SMEM),
                  pl.BlockSpec(memory_space=pl.ANY),
                  pl.BlockSpec((1, D, F2), lambda i, be: (be[i], 0, 0)),
                  pl.BlockSpec((1, 1, F2), lambda i, be: (be[i], 0, 0)),
                  pl.BlockSpec((1, f, D), lambda i, be: (be[i], 0, 0)),
                  pl.BlockSpec((1, 1, D), lambda i, be: (be[i], 0, 0))],
        out_specs=pl.BlockSpec((tm, D), lambda i, be: (i, 0)),
        scratch_shapes=[pltpu.VMEM((2, tm, D), F32), pltpu.SemaphoreType.DMA((2,))],
    )
    return pl.pallas_call(
        kern,
        out_shape=jax.ShapeDtypeStruct((rows, D), F32),
        grid_spec=grid_spec,
        compiler_params=_params(1),
        name="moe_blocks",
    )(block_e, tok3, tok3, h, w1, b1.reshape(E, 1, F2), w2, b2.reshape(E, 1, D))


def _moe_combine_kernel(sl_ref, sln_ref, out_hbm, gate_ref, x_ref, gf_ref, o_ref, buf, sems, *, tb):
    b, i = pl.program_id(0), pl.program_id(1)
    step = b * pl.num_programs(1) + i
    nsteps = pl.num_programs(0) * pl.num_programs(1)
    slot = step % 2
    n = TOP_K * tb

    @pl.when(step == 0)
    def _():
        _start_row_copies(sl_ref, n, out_hbm, buf.at[0], sems.at[0])

    @pl.when(step + 1 < nsteps)
    def _():
        _start_row_copies(sln_ref, n, out_hbm, buf.at[1 - slot], sems.at[1 - slot])

    _wait_row_copies(n, out_hbm, buf.at[slot], sems.at[slot])
    gate = gate_ref[0]
    y = jnp.zeros(x_ref.shape[1:], F32)
    for k in range(TOP_K):
        y = y + gate[:, k:k + 1] * buf[slot, k * tb:(k + 1) * tb, :]
    o_ref[0] = x_ref[0] + gf_ref[0] * y


def moe_combine(out, slots, gates, x, g_f):
    B, T, D = x.shape
    tb = _row_tile(T, 64)
    nt = T // tb
    n = TOP_K * tb
    sl3 = jnp.transpose(slots.reshape(B * nt, tb, TOP_K), (0, 2, 1)).reshape(B * nt, 1, n)
    last = B * nt - 1
    return pl.pallas_call(
        functools.partial(_moe_combine_kernel, tb=tb),
        out_shape=jax.ShapeDtypeStruct((B, T, D), F32),
        grid=(B, nt),
        in_specs=[pl.BlockSpec((1, 1, n), lambda b, i: (b * nt + i, 0, 0), memory_space=pltpu.SMEM),
                  pl.BlockSpec((1, 1, n), lambda b, i: (jnp.minimum(b * nt + i + 1, last), 0, 0),
                               memory_space=pltpu.SMEM),
                  pl.BlockSpec(memory_space=pl.ANY),
                  pl.BlockSpec((1, tb, LANES), lambda b, i: (b, i, 0)),
                  pl.BlockSpec((1, tb, D), lambda b, i: (b, i, 0)),
                  pl.BlockSpec((1, 1, D), lambda b, i: (b, 0, 0))],
        out_specs=pl.BlockSpec((1, tb, D), lambda b, i: (b, i, 0)),
        scratch_shapes=[pltpu.VMEM((2, n, D), F32), pltpu.SemaphoreType.DMA((2,))],
        compiler_params=_params(2),
        name="moe_combine",
    )(sl3, sl3, out, gates, x, g_f)


def _ffn_pre_kernel(x_ref, g_ref, sc_ref, sh_ref, rw_ref, rb_ref, h_ref, te_ref, tg_ref):
    h = _norm_mod(x_ref[0], g_ref[...], sc_ref[0], sh_ref[0])
    h_ref[0] = h
    cur = jnp.dot(h, rw_ref[...], preferred_element_type=F32, precision=HIGHEST) + rb_ref[...]
    n_e = cur.shape[1]
    lane_e = lax.broadcasted_iota(jnp.int32, cur.shape, 1).astype(F32)
    lane = lax.broadcasted_iota(jnp.int32, te_ref.shape[1:], 1)
    vals, idxs = [], []
    for k in range(TOP_K):
        m = jnp.max(cur, axis=1, keepdims=True)
        idx = jnp.min(jnp.where(cur == m, lane_e, float(n_e)), axis=1, keepdims=True)
        vals.append(m)
        idxs.append(idx)
        cur = jnp.where(lane_e == idx, -jnp.inf, cur)
    ex = [jnp.exp(v - vals[0]) for v in vals]
    denom = ex[0]
    for e in ex[1:]:
        denom = denom + e
    te = jnp.zeros(lane.shape, F32)
    tg = jnp.zeros(lane.shape, F32)
    for k in range(TOP_K):
        te = jnp.where(lane == k, idxs[k], te)
        tg = jnp.where(lane == k, ex[k] / denom, tg)
    te_ref[0] = te.astype(jnp.int32)
    tg_ref[0] = tg


def ffn_pre(x, g, sc, sh, router_w, router_b):
    B, T, D = x.shape
    E = router_w.shape[1]
    tm = _row_tile(T, 256)
    return pl.pallas_call(
        _ffn_pre_kernel,
        out_shape=(jax.ShapeDtypeStruct((B, T, D), F32), jax.ShapeDtypeStruct((B, T, LANES), jnp.int32),
                   jax.ShapeDtypeStruct((B, T, LANES), F32)),
        grid=(B, T // tm),
        in_specs=[pl.BlockSpec((1, tm, D), lambda b, i: (b, i, 0)),
                  pl.BlockSpec((1, D), lambda b, i: (0, 0)),
                  pl.BlockSpec((1, 1, D), lambda b, i: (b, 0, 0)),
                  pl.BlockSpec((1, 1, D), lambda b, i: (b, 0, 0)),
                  pl.BlockSpec((D, E), lambda b, i: (0, 0)),
                  pl.BlockSpec((1, E), lambda b, i: (0, 0))],
        out_specs=(pl.BlockSpec((1, tm, D), lambda b, i: (b, i, 0)),
                   pl.BlockSpec((1, tm, LANES), lambda b, i: (b, i, 0)),
                   pl.BlockSpec((1, tm, LANES), lambda b, i: (b, i, 0))),
        compiler_params=_params(2),
        name="ffn_pre",
    )(x, g.reshape(1, D), sc, sh, router_w, router_b.reshape(1, E))


def _rms_kernel(x_ref, g_ref, o_ref):
    x = x_ref[0]
    ms = jnp.mean(x * x, axis=-1, keepdims=True)
    o_ref[0] = x * lax.rsqrt(ms + NORM_EPS) * g_ref[...]


def final_norm(x, g):
    B, T, D = x.shape
    tm = _row_tile(T, 512)
    return pl.pallas_call(
        _rms_kernel,
        out_shape=jax.ShapeDtypeStruct((B, T, D), F32),
        grid=(B, T // tm),
        in_specs=[pl.BlockSpec((1, tm, D), lambda b, i: (b, i, 0)),
                  pl.BlockSpec((1, D), lambda b, i: (0, 0))],
        out_specs=pl.BlockSpec((1, tm, D), lambda b, i: (b, i, 0)),
        compiler_params=_params(2),
        name="final_norm",
    )(x, g.reshape(1, D))


def _route(top_e):
    N = top_e.shape[0]
    n4 = N * TOP_K
    flat_e = top_e.reshape(-1)
    order = jnp.argsort(flat_e)
    e_sorted = flat_e[order]
    counts = jnp.bincount(flat_e, length=N_EXPERTS)
    padded = (counts + MOE_BLOCK - 1) // MOE_BLOCK * MOE_BLOCK
    start = jnp.cumsum(counts) - counts
    pstart = jnp.cumsum(padded) - padded
    dest = (pstart[e_sorted] + jnp.arange(n4, dtype=jnp.int32) - start[e_sorted]).astype(jnp.int32)
    n_blocks = (n4 + MOE_BLOCK - 1) // MOE_BLOCK + N_EXPERTS
    rows = n_blocks * MOE_BLOCK
    block_e = jnp.minimum(jnp.searchsorted(jnp.cumsum(padded), jnp.arange(n_blocks) * MOE_BLOCK, side='right'),
                          N_EXPERTS - 1).astype(jnp.int32)
    slot = jnp.arange(rows, dtype=jnp.int32)
    slot_e = jnp.repeat(block_e, MOE_BLOCK)
    off = slot - pstart[slot_e].astype(jnp.int32)
    src = jnp.minimum(start[slot_e].astype(jnp.int32) + off, n4 - 1)
    slot_tok = jnp.where(off < counts[slot_e], order[src] // TOP_K, 0).astype(jnp.int32)
    slot_flat = dest[jnp.argsort(order)]
    return slot_tok, block_e, slot_flat.reshape(N, TOP_K)


def kernel(x, c, ctx, c_ctx, norm_mix_g, norm_ffn_g, w_mod, b_mod, w_in, w_out, q_norm_g, k_norm_g,
           conv_w, w0, w_up, a0, a_up, g_up, k_k, k_a, r_k, ln_x_g, ln_x_b,
           router_w, router_b, e_w1, e_b1, e_w2, e_b2, norm_final_g):
    B, T, D = x.shape
    C = ctx.shape[1]
    depth = w_in.shape[0]
    cos, sin = _rope_tables(T)
    rpad = -(-(B + 1) // SUBLANES) * SUBLANES
    cvecs = jnp.zeros((rpad, D), F32).at[:B].set(c).at[B].set(c_ctx)
    s_zero = jnp.zeros((B, 2 * RW_DIM // QUAD_LANES, QUAD_LANES, QUAD_LANES), F32)
    x_lat, x_ctx = x, ctx
    for l in range(depth):
        last = l == depth - 1
        mods = adaln_all(cvecs, w_mod[l], b_mod[l])
        m_lat = [m[:, None, :] for m in jnp.split(mods[:B], 6, axis=-1)]
        m_ctx = [jnp.broadcast_to(m[None, :, :], (B, 1, D)) for m in jnp.split(mods[B:B + 1], 6, axis=-1)]
        sh_m, sc_m, g_m, sh_f, sc_f, g_f = m_lat
        csh_m, csc_m, cg_m, csh_f, csc_f, cg_f = m_ctx
        w_in_l = w_in[l].astype(BF16)
        w_out_l = w_out[l].astype(BF16)
        p_lat = norm_mod_matmul(x_lat, norm_mix_g[l], sc_m, sh_m, w_in_l)
        p_ctx = norm_mod_matmul(x_ctx, norm_mix_g[l], csc_m, csh_m,
                                w_in_l[:, :STATE_COLS] if last else w_in_l)

        ktc, vc = kv_prep(p_ctx, k_norm_g[l])
        ktl, vl = kv_prep(p_lat, k_norm_g[l], cos, sin)
        o_at = attention(p_lat, q_norm_g[l], ktc, vc, ktl, vl, cos, sin)

        rw = (conv_w[l], w0[l], w_up[l], a0[l], a_up[l], k_k[l], k_a[l])
        w2c, k2c, b2c, v_c, kk_c, r_c = wkv_prep(p_ctx, *rw, has_r=not last)
        w2l, k2l, b2l, v_l, kk_l, r_l = wkv_prep(p_lat, *rw, has_r=True)
        of_c, ob_c, s_c = wkv_chunked(w2c, k2c, b2c, v_c, kk_c, r_c, s_zero)
        of_l, ob_l, _ = wkv_chunked(w2l, k2l, b2l, v_l, kk_l, r_l, s_c)

        ro = (g_up[l], r_k[l], ln_x_g[l], ln_x_b[l], w_out_l)
        x_lat = mixer_out(of_l, ob_l, r_l, v_l, k2l, p_lat, o_at, x_lat, g_m, *ro)
        h_lat, te_lat, tg_lat = ffn_pre(x_lat, norm_ffn_g[l], sc_f, sh_f, router_w[l], router_b[l])
        e_w1_l = e_w1[l].astype(BF16)
        e_w2_l = e_w2[l].astype(BF16)
        if last:
            slot_tok, block_e, slot_flat = _route(te_lat.reshape(B * T, LANES)[:, :TOP_K])
            out = moe_blocks(block_e, slot_tok, h_lat.reshape(B * T, D), e_w1_l, e_b1[l], e_w2_l, e_b2[l])
            x_lat = moe_combine(out, slot_flat.reshape(B, T, TOP_K), tg_lat, x_lat, g_f)
        else:
            o_at_c = attention(p_ctx, q_norm_g[l], ktc, vc)
            x_ctx = mixer_out(of_c, ob_c, r_c, v_c, k2c, p_ctx, o_at_c, x_ctx, cg_m, *ro)
            h_ctx, te_ctx, tg_ctx = ffn_pre(x_ctx, norm_ffn_g[l], csc_f, csh_f, router_w[l], router_b[l])
            h_all = jnp.concatenate([h_lat.reshape(B * T, D), h_ctx.reshape(B * C, D)], axis=0)
            te_all = jnp.concatenate([te_lat.reshape(B * T, LANES), te_ctx.reshape(B * C, LANES)], axis=0)
            slot_tok, block_e, slot_flat = _route(te_all[:, :TOP_K])
            out = moe_blocks(block_e, slot_tok, h_all, e_w1_l, e_b1[l], e_w2_l, e_b2[l])
            x_lat = moe_combine(out, slot_flat[:B * T].reshape(B, T, TOP_K), tg_lat, x_lat, g_f)
            x_ctx = moe_combine(out, slot_flat[B * T:].reshape(B, C, TOP_K), tg_ctx, x_ctx, cg_f)
    return final_norm(x_lat, norm_final_g)
```

```python
import functools
import math

import numpy as np
import jax
import jax.numpy as jnp
from jax import lax
from jax.experimental import pallas as pl
from jax.experimental.pallas import tpu as pltpu

F32 = jnp.float32
BF16 = jnp.bfloat16
HIGHEST = lax.Precision.HIGHEST

GRID_W = 64
HEAD_DIM = 64
ATTN_HEADS = 8
ATTN_KV_HEADS = 2
GROUP = ATTN_HEADS // ATTN_KV_HEADS
ATTN_DIM = ATTN_HEADS * HEAD_DIM
KV_DIM = ATTN_KV_HEADS * HEAD_DIM
RW_HEADS = 8
RW_DIM = RW_HEADS * HEAD_DIM
W_LORA = 64
A_LORA = 64
G_LORA = 128
CONV_W = 3
N_EXPERTS = 32
TOP_K = 4
SWIGLU_LIMIT = 7.0
SWIGLU_ALPHA = 1.702
ROPE_THETA = 10000.0
ROPE_PAIRS = HEAD_DIM // 4
MOE_BLOCK = 256
NORM_EPS = 1e-6
GN_EPS = 64e-5
ATTN_SCALE = HEAD_DIM ** -0.5
EXP_M05 = math.exp(-0.5)
COL_K_AT = 0
COL_V_AT = KV_DIM
COL_K_RW = 2 * KV_DIM
COL_V_RW = COL_K_RW + RW_DIM
COL_XW = COL_V_RW + RW_DIM
COL_XA = COL_XW + 2 * W_LORA
STATE_COLS = COL_XA + 2 * A_LORA
COL_Q_AT = STATE_COLS
COL_R_RW = COL_Q_AT + ATTN_DIM
COL_XG = COL_R_RW + RW_DIM

LANES = 128
SUBLANES = 8
QUAD_LANES = 256
VMEM_LIMIT = 48 * 1024 * 1024


def _row_tile(n, pref):
    t = pref
    while n % t:
        t //= 2
    return t


def _params(n_axes):
    return pltpu.CompilerParams(dimension_semantics=("arbitrary",) * n_axes, vmem_limit_bytes=VMEM_LIMIT)


def _ones_bd(n):
    ra = lax.broadcasted_iota(jnp.int32, (n, n), 0) // HEAD_DIM
    rb = lax.broadcasted_iota(jnp.int32, (n, n), 1) // HEAD_DIM
    return (ra == rb).astype(BF16)


def _seg_sum(x_bf16, ones_bd):
    return jnp.dot(x_bf16, ones_bd, preferred_element_type=F32)


def _seg_sum2(x, ones_bd):
    hi = x.astype(BF16)
    lo = (x - hi.astype(F32)).astype(BF16)
    return _seg_sum(hi, ones_bd) + _seg_sum(lo, ones_bd)


def _adaln_kernel(c_ref, w_ref, b_ref, o_ref):
    c = c_ref[...]
    s = c * jax.nn.sigmoid(c)
    o_ref[...] = jnp.dot(s, w_ref[...], preferred_element_type=F32, precision=HIGHEST) + b_ref[...]


def adaln_all(cvecs, w_mod, b_mod):
    R, D = cvecs.shape
    N = w_mod.shape[1]
    tn = 512
    return pl.pallas_call(
        _adaln_kernel,
        out_shape=jax.ShapeDtypeStruct((R, N), F32),
        grid=(N // tn,),
        in_specs=[pl.BlockSpec((R, D), lambda j: (0, 0)),
                  pl.BlockSpec((D, tn), lambda j: (0, j)),
                  pl.BlockSpec((1, tn), lambda j: (0, j))],
        out_specs=pl.BlockSpec((R, tn), lambda j: (0, j)),
        compiler_params=_params(1),
        name="adaln",
    )(cvecs, w_mod, b_mod.reshape(1, N))


def _norm_mod(x, g, sc, sh):
    ms = jnp.mean(x * x, axis=-1, keepdims=True)
    y = x * lax.rsqrt(ms + NORM_EPS) * g
    return y * (1.0 + sc) + sh


def _norm_mod_matmul_kernel(x_ref, g_ref, sc_ref, sh_ref, w_ref, o_ref):
    h = _norm_mod(x_ref[0], g_ref[...], sc_ref[0], sh_ref[0])
    o_ref[0] = jnp.dot(h.astype(BF16), w_ref[...], preferred_element_type=F32)


def norm_mod_matmul(x, g, sc, sh, w):
    B, T, D = x.shape
    N = w.shape[1]
    tm = _row_tile(T, 256)
    return pl.pallas_call(
        _norm_mod_matmul_kernel,
        out_shape=jax.ShapeDtypeStruct((B, T, N), F32),
        grid=(B, T // tm),
        in_specs=[pl.BlockSpec((1, tm, D), lambda b, i: (b, i, 0)),
                  pl.BlockSpec((1, D), lambda b, i: (0, 0)),
                  pl.BlockSpec((1, 1, D), lambda b, i: (b, 0, 0)),
                  pl.BlockSpec((1, 1, D), lambda b, i: (b, 0, 0)),
                  pl.BlockSpec((D, N), lambda b, i: (0, 0))],
        out_specs=pl.BlockSpec((1, tm, N), lambda b, i: (b, i, 0)),
        compiler_params=_params(2),
        name="norm_mod_matmul",
    )(x, g.reshape(1, D), sc, sh, w)


def _pair_rms(x, g):
    lane = lax.broadcasted_iota(jnp.int32, x.shape, 1)
    first = lane < HEAD_DIM
    xx = x * x
    s0 = jnp.sum(jnp.where(first, xx, 0.0), axis=1, keepdims=True)
    s1 = jnp.sum(jnp.where(first, 0.0, xx), axis=1, keepdims=True)
    ms = jnp.where(first, s0, s1) * (1.0 / HEAD_DIM)
    return x * lax.rsqrt(ms + NORM_EPS) * g


def _pair_rope(x, cos, sin):
    lane = lax.broadcasted_iota(jnp.int32, x.shape, 1)
    partner = jnp.where(lane % (2 * ROPE_PAIRS) < ROPE_PAIRS,
                        pltpu.roll(x, LANES - ROPE_PAIRS, 1), pltpu.roll(x, ROPE_PAIRS, 1))
    return x * cos + partner * sin


def _rope_tables(T):
    rows = T // GRID_W
    row = jnp.repeat(jnp.arange(rows), GRID_W)
    col = jnp.tile(jnp.arange(GRID_W), rows)
    pos = jnp.stack([row, col], axis=-1).astype(F32)
    freqs = ROPE_THETA ** (-jnp.arange(ROPE_PAIRS, dtype=F32) / ROPE_PAIRS)
    ang = pos[:, :, None] * freqs
    cos, sin = jnp.cos(ang), jnp.sin(ang)
    cos_h = jnp.concatenate([cos[:, 0], cos[:, 0], cos[:, 1], cos[:, 1]], axis=-1)
    sin_h = jnp.concatenate([-sin[:, 0], sin[:, 0], -sin[:, 1], sin[:, 1]], axis=-1)
    return jnp.tile(cos_h, (1, 2)), jnp.tile(sin_h, (1, 2))


def _kv_prep_kernel(*refs, rope):
    if rope:
        p_ref, g_ref, cos_ref, sin_ref, kt_ref, v_ref = refs
    else:
        p_ref, g_ref, kt_ref, v_ref = refs
    x = p_ref[0]
    k = _pair_rms(x[:, :KV_DIM], g_ref[...])
    if rope:
        k = _pair_rope(k, cos_ref[...], sin_ref[...])
    v = x[:, KV_DIM:]
    lane = lax.broadcasted_iota(jnp.int32, k.shape, 1)
    first = lane < HEAD_DIM
    kr = pltpu.roll(k, HEAD_DIM, 1)
    vr = pltpu.roll(v, HEAD_DIM, 1)
    kt_ref[0, 0] = jnp.where(first, k, kr).T.astype(BF16)
    kt_ref[0, 1] = jnp.where(first, kr, k).T.astype(BF16)
    ones = jnp.ones(v.shape, F32)
    v_ref[0, 0] = jnp.concatenate([jnp.where(first, v, vr), ones], axis=1).astype(BF16)
    v_ref[0, 1] = jnp.concatenate([jnp.where(first, vr, v), ones], axis=1).astype(BF16)


def kv_prep(p, k_norm_g, cos=None, sin=None):
    B, T, _ = p.shape
    tm = _row_tile(T, 256)
    rope = cos is not None
    g2 = jnp.tile(k_norm_g, 2).reshape(1, LANES)
    in_specs = [pl.BlockSpec((1, tm, 2 * KV_DIM), lambda b, i: (b, i, 0)),
                pl.BlockSpec((1, LANES), lambda b, i: (0, 0))]
    args = [p, g2]
    if rope:
        in_specs += [pl.BlockSpec((tm, LANES), lambda b, i: (i, 0))] * 2
        args += [cos, sin]
    return pl.pallas_call(
        functools.partial(_kv_prep_kernel, rope=rope),
        out_shape=(jax.ShapeDtypeStruct((B, ATTN_KV_HEADS, LANES, T), BF16),
                   jax.ShapeDtypeStruct((B, ATTN_KV_HEADS, T, 2 * LANES), BF16)),
        grid=(B, T // tm),
        in_specs=in_specs,
        out_specs=(pl.BlockSpec((1, ATTN_KV_HEADS, LANES, tm), lambda b, i: (b, 0, 0, i)),
                   pl.BlockSpec((1, ATTN_KV_HEADS, tm, 2 * LANES), lambda b, i: (b, 0, i, 0))),
        compiler_params=_params(2),
        name="kv_prep",
    )(*args)


def _attn_kernel(*refs, tq, tkl, nkl, rope, has_lat, rg):
    refs = list(refs)
    p_ref, g_ref = refs[:2]
    refs = refs[2:]
    if rope:
        cos_ref, sin_ref = refs[:2]
        refs = refs[2:]
    ktc_ref, vc_ref = refs[:2]
    refs = refs[2:]
    if has_lat:
        ktl_ref, vl_ref = refs[:2]
        refs = refs[2:]
    o_ref, m_ref, acc_ref = refs

    q = _pair_rms(p_ref[0], g_ref[...])
    if rope:
        q = _pair_rope(q, cos_ref[...], sin_ref[...])
    q = q * ATTN_SCALE
    lane = lax.broadcasted_iota(jnp.int32, (tq, LANES), 1)
    first = lane < HEAD_DIM
    qq = jnp.concatenate([jnp.where(first, q, 0.0), jnp.where(first, 0.0, q)], axis=0).astype(BF16)

    m_ref[...] = jnp.full(m_ref.shape, -jnp.inf, F32)
    acc_ref[...] = jnp.zeros(acc_ref.shape, F32)
    groups = [slice(g * rg, (g + 1) * rg) for g in range(2 * tq // rg)]

    def chunk(kt, vv):
        s = [jnp.dot(qq[g], kt, preferred_element_type=F32) for g in groups]
        m_old = [m_ref[g] for g in groups]
        m_new = [jnp.maximum(mo, jnp.max(x, axis=-1, keepdims=True)) for mo, x in zip(m_old, s)]
        alpha = [jnp.exp(mo - mn) for mo, mn in zip(m_old, m_new)]
        p = [jnp.exp(x - jnp.concatenate([mn] * (x.shape[1] // LANES), axis=1)).astype(BF16)
             for x, mn in zip(s, m_new)]
        pv = [jnp.dot(x, vv, preferred_element_type=F32) for x in p]
        for g, al, mn, x in zip(groups, alpha, m_new, pv):
            acc_ref[g] = jnp.concatenate([al, al], axis=1) * acc_ref[g] + x
            m_ref[g] = mn

    chunk(ktc_ref[0, 0], vc_ref[0, 0])
    if has_lat:
        def body(j, carry):
            ks = pl.multiple_of(j * tkl, tkl)
            chunk(ktl_ref[0, 0, :, pl.ds(ks, tkl)], vl_ref[0, 0, pl.ds(ks, tkl), :])
            return carry
        lax.fori_loop(0, nkl, body, 0)

    acc = acc_ref[...]
    o = acc[:, :LANES] / acc[:, LANES:]
    o_ref[0] = jnp.where(first, o[:tq], o[tq:])


def attention(p, q_norm_g, ktc, vc, ktl=None, vl=None, cos=None, sin=None):
    B, T, _ = p.shape
    C = ktc.shape[-1]
    rope = cos is not None
    has_lat = ktl is not None
    tq = _row_tile(T, 256)
    pairs = ATTN_HEADS // 2
    pairs_per_kv = GROUP // 2
    qblk0 = COL_Q_AT // LANES
    g2 = jnp.tile(q_norm_g, 2).reshape(1, LANES)
    in_specs = [pl.BlockSpec((1, tq, LANES), lambda b, h, i: (b, i, qblk0 + h)),
                pl.BlockSpec((1, LANES), lambda b, h, i: (0, 0))]
    args = [p, g2]
    if rope:
        in_specs += [pl.BlockSpec((tq, LANES), lambda b, h, i: (i, 0))] * 2
        args += [cos, sin]
    in_specs += [pl.BlockSpec((1, 1, LANES, C), lambda b, h, i: (b, h // pairs_per_kv, 0, 0)),
                 pl.BlockSpec((1, 1, C, 2 * LANES), lambda b, h, i: (b, h // pairs_per_kv, 0, 0))]
    args += [ktc, vc]
    tkl = nkl = 0
    if has_lat:
        S = ktl.shape[-1]
        tkl = _row_tile(S, 1024)
        nkl = S // tkl
        in_specs += [pl.BlockSpec((1, 1, LANES, S), lambda b, h, i: (b, h // pairs_per_kv, 0, 0)),
                     pl.BlockSpec((1, 1, S, 2 * LANES), lambda b, h, i: (b, h // pairs_per_kv, 0, 0))]
        args += [ktl, vl]
    kern = functools.partial(_attn_kernel, tq=tq, tkl=tkl, nkl=nkl, rope=rope, has_lat=has_lat,
                             rg=_row_tile(tq, ATTN_ROW_GROUP))
    return pl.pallas_call(
        kern,
        out_shape=jax.ShapeDtypeStruct((B, T, ATTN_DIM), F32),
        grid=(B, pairs, T // tq),
        in_specs=in_specs,
        out_specs=pl.BlockSpec((1, tq, LANES), lambda b, h, i: (b, i, h)),
        scratch_shapes=[pltpu.VMEM((2 * tq, LANES), F32), pltpu.VMEM((2 * tq, 2 * LANES), F32)],
        compiler_params=_params(3),
        name="attention",
    )(*args)


def _wkv_prep_kernel(p_ref, pp_ref, pn_ref, cw_ref, w0_ref, wup_ref, a0_ref, aup_ref, kk_ref, ka_ref,
                     w2_ref, k2_ref, b2_ref, v_ref, kkn_ref, r_ref, *, tm, has_r):
    i = pl.program_id(1)
    n = pl.num_programs(1)
    x = p_ref[0]
    prev = pp_ref[0][SUBLANES - 1:SUBLANES, :] * (i > 0).astype(F32)
    nxt = pn_ref[0][0:1, :] * (i < n - 1).astype(F32)
    row = lax.broadcasted_iota(jnp.int32, (tm, RW_DIM), 0)

    def conv(col, wcol):
        u = x[:, col:col + RW_DIM]
        up = jnp.where(row == 0, prev[:, col:col + RW_DIM], pltpu.roll(u, 1, 0))
        un = jnp.where(row == tm - 1, nxt[:, col:col + RW_DIM], pltpu.roll(u, tm - 1, 0))
        w = cw_ref[:, wcol:wcol + RW_DIM]
        return up * w[0:1] + u * w[1:2] + un * w[2:3]

    k = conv(COL_K_RW, RW_DIM)
    v = conv(COL_V_RW, 2 * RW_DIM)
    v_ref[0] = v
    r_ref[0] = conv(COL_R_RW, 0) if has_r else jnp.zeros((tm, RW_DIM), F32)
    kk = k * kk_ref[...]
    ss = _seg_sum2(kk * kk, _ones_bd(RW_DIM))
    kk = kk * lax.rsqrt(jnp.maximum(ss, 1e-24))
    kkn_ref[0] = kk
    xw = jnp.tanh(x[:, COL_XW:COL_XW + 2 * W_LORA])
    xa = x[:, COL_XA:COL_XA + 2 * A_LORA]
    for d in range(2):
        wl = w0_ref[d:d + 1, :] + jnp.dot(xw[:, d * W_LORA:(d + 1) * W_LORA], wup_ref[d],
                                          preferred_element_type=F32, precision=HIGHEST)
        decay = -EXP_M05 * jax.nn.sigmoid(wl)
        a = jax.nn.sigmoid(a0_ref[d:d + 1, :] + jnp.dot(xa[:, d * A_LORA:(d + 1) * A_LORA], aup_ref[d],
                                                        preferred_element_type=F32, precision=HIGHEST))
        sl = slice(d * RW_DIM, (d + 1) * RW_DIM)
        w2_ref[0, :, sl] = decay
        k2_ref[0, :, sl] = k * (1.0 + (a - 1.0) * ka_ref[...])
        b2_ref[0, :, sl] = kk * a


def wkv_prep(p, conv_w, w0, w_up, a0, a_up, k_k, k_a, has_r):
    B, T, NP = p.shape
    tm = _row_tile(T, 256)
    nb8 = tm // SUBLANES
    last8 = T // SUBLANES - 1
    full = lambda a: pl.BlockSpec(a.shape, lambda b, i: (0,) * a.ndim)
    kk2, ka2 = k_k.reshape(1, RW_DIM), k_a.reshape(1, RW_DIM)
    wide = jax.ShapeDtypeStruct((B, T, 2 * RW_DIM), F32)
    narrow = jax.ShapeDtypeStruct((B, T, RW_DIM), F32)
    wspec = pl.BlockSpec((1, tm, 2 * RW_DIM), lambda b, i: (b, i, 0))
    nspec = pl.BlockSpec((1, tm, RW_DIM), lambda b, i: (b, i, 0))
    return pl.pallas_call(
        functools.partial(_wkv_prep_kernel, tm=tm, has_r=has_r),
        out_shape=(wide, wide, wide, narrow, narrow, narrow),
        grid=(B, T // tm),
        in_specs=[pl.BlockSpec((1, tm, NP), lambda b, i: (b, i, 0)),
                  pl.BlockSpec((1, SUBLANES, NP), lambda b, i: (b, jnp.maximum(i * nb8 - 1, 0), 0)),
                  pl.BlockSpec((1, SUBLANES, NP), lambda b, i: (b, jnp.minimum((i + 1) * nb8, last8), 0)),
                  full(conv_w), full(w0), full(w_up), full(a0), full(a_up), full(kk2), full(ka2)],
        out_specs=(wspec, wspec, wspec, nspec, nspec, nspec),
        compiler_params=_params(2),
        name="wkv_prep",
    )(p, p, p, conv_w, w0, w_up, a0, a_up, kk2, ka2)


def _wkv_kernel(wf_ref, wb_ref, kf_ref, kb_ref, bf_ref, bb_ref, vf_ref, vb_ref, af_ref, ab_ref,
                rf_ref, rb_ref, s0_ref, of_ref, ob_ref, st_ref, s_ref, *, tc, nquads):
    tb = pl.program_id(1)

    @pl.when(tb == 0)
    def _():
        s_ref[...] = s0_ref[0]

    rows_all = 2 * nquads * HEAD_DIM
    shp = (rows_all, QUAD_LANES)
    lane = lax.broadcasted_iota(jnp.int32, shp, 1)
    sub = lax.broadcasted_iota(jnp.int32, shp, 0)
    eye = (lane % HEAD_DIM == sub % HEAD_DIM).astype(F32)
    ones_bd = _ones_bd(QUAD_LANES)
    ng = tc // SUBLANES

    def bcast_rows(blks, j):
        parts = []
        for d in range(2):
            jj = j if d == 0 else SUBLANES - 1 - j
            for q in range(nquads):
                row = blks[d][jj:jj + 1, q * QUAD_LANES:(q + 1) * QUAD_LANES]
                parts.append(jnp.broadcast_to(row, (HEAD_DIM, QUAD_LANES)))
        return jnp.concatenate(parts, axis=0)

    def group(g, carry):
        tf = pl.multiple_of(g * SUBLANES, SUBLANES)
        tr = pl.multiple_of((ng - 1 - g) * SUBLANES, SUBLANES)
        ld = lambda fr, br: [fr[0, pl.ds(tf, SUBLANES), :], br[0, pl.ds(tr, SUBLANES), :]]
        wq, kq, bq, vq, rq = (ld(f, b) for f, b in ((wf_ref, wb_ref), (kf_ref, kb_ref), (bf_ref, bb_ref),
                                                    (vf_ref, vb_ref), (rf_ref, rb_ref)))
        aq = [-x for x in ld(af_ref, ab_ref)]
        S = s_ref[...]
        out_rows = [[] for _ in range(2 * nquads)]
        for j in range(SUBLANES):
            wt, kt, vt, at, bt, rt = (bcast_rows(x, j) for x in (wq, kq, vq, aq, bq, rq))
            sa = _seg_sum2(S * at, ones_bd)
            vcol = _seg_sum((eye * vt).astype(BF16), ones_bd)
            S = S * wt + sa * bt + vcol * kt
            ob = _seg_sum((S * rt).astype(BF16), ones_bd) * eye
            for c in range(2 * nquads):
                out_rows[c].append(jnp.sum(ob[c * HEAD_DIM:(c + 1) * HEAD_DIM], axis=0, keepdims=True))
        s_ref[...] = S
        for q in range(nquads):
            sl = slice(q * QUAD_LANES, (q + 1) * QUAD_LANES)
            of_ref[0, pl.ds(tf, SUBLANES), sl] = jnp.concatenate(out_rows[q], axis=0)
            ob_ref[0, pl.ds(tr, SUBLANES), sl] = jnp.concatenate(out_rows[nquads + q][::-1], axis=0)
        return carry

    lax.fori_loop(0, ng, group, 0)

    @pl.when(tb == pl.num_programs(1) - 1)
    def _():
        st_ref[0] = s_ref[...]


def wkv_scan(w2, k2, b2, v, kk, r, s0):
    B, T, C = v.shape
    nquads = C // QUAD_LANES
    tc = _row_tile(T, 128)
    nt = T // tc
    fwd = lambda col: pl.BlockSpec((1, tc, C), lambda b, t: (b, t, col))
    bwd = lambda col: pl.BlockSpec((1, tc, C), lambda b, t: (b, nt - 1 - t, col))
    sspec = pl.BlockSpec((1,) + s0.shape[1:], lambda b, t: (b, 0, 0))
    kern = functools.partial(_wkv_kernel, tc=tc, nquads=nquads)
    return pl.pallas_call(
        kern,
        out_shape=(jax.ShapeDtypeStruct((B, T, C), F32), jax.ShapeDtypeStruct((B, T, C), F32),
                   jax.ShapeDtypeStruct(s0.shape, F32)),
        grid=(B, nt),
        in_specs=[fwd(0), bwd(1), fwd(0), bwd(1), fwd(0), bwd(1), fwd(0), bwd(0), fwd(0), bwd(0),
                  fwd(0), bwd(0), sspec],
        out_specs=(fwd(0), bwd(0), sspec),
        scratch_shapes=[pltpu.VMEM(s0.shape[1:], F32)],
        compiler_params=_params(2),
        name="wkv_scan",
    )(w2, w2, k2, k2, b2, b2, v, v, kk, kk, r, r, s0)


CHUNK = 64
ATTN_ROW_GROUP = 128


def _split_bf16(x):
    hi = x.astype(BF16)
    return hi, (x - hi.astype(F32)).astype(BF16)


def _mm3(a, b, nt=False, passes=3):
    dn = (((1,), (1 if nt else 0,)), ((), ()))
    if passes == 1:
        return lax.dot_general(a.astype(BF16), b.astype(BF16), dn, preferred_element_type=F32)
    ah, al = _split_bf16(a)
    m = a.shape[0]
    if passes == 2:
        top = lax.dot_general(jnp.concatenate([ah, al], axis=0), b.astype(BF16), dn, preferred_element_type=F32)
        return top[:m] + top[m:]
    bh, bl = _split_bf16(b)
    top = lax.dot_general(jnp.concatenate([ah, al], axis=0), bh, dn, preferred_element_type=F32)
    return top[:m] + top[m:] + lax.dot_general(ah, bl, dn, preferred_element_type=F32)


def _chunk_terms(chains):
    L = CHUNK
    nch = range(len(chains))
    row = lax.broadcasted_iota(jnp.int32, (L, QUAD_LANES), 0)
    col = lax.broadcasted_iota(jnp.int32, (L, QUAD_LANES), 1) % HEAD_DIM
    tr = lax.broadcasted_iota(jnp.int32, (L, L), 0)
    tc = lax.broadcasted_iota(jnp.int32, (L, L), 1)
    masks = {True: (col > row, col >= row, (tc >= tr).astype(F32)),
             False: (col < row, col <= row, (tc <= tr).astype(F32))}
    eye_l = (col == row).astype(F32)
    ra = lax.broadcasted_iota(jnp.int32, (QUAD_LANES, QUAD_LANES), 0)
    rb = lax.broadcasted_iota(jnp.int32, (QUAD_LANES, QUAD_LANES), 1)
    bd = ra // HEAD_DIM == rb // HEAD_DIM
    eye_q = (ra == rb).astype(F32)

    def bdiag(y):
        return jnp.where(bd, jnp.concatenate([y] * (QUAD_LANES // L), axis=0), 0.0)

    def bdm(x, y, passes):
        ys = [bdiag(y[:, i:i + QUAD_LANES]) for i in range(0, y.shape[1], QUAD_LANES)]
        return _mm3(x, ys[0] if len(ys) == 1 else jnp.concatenate(ys, axis=1), passes=passes)

    lw, kd, bb, v, kk, r, back = (list(x) for x in zip(*chains))
    strict = [masks[b][0] for b in back]
    incl = [masks[b][1] for b in back]
    cs = [_mm3(masks[back[c]][2], lw[c]) for c in nch]
    g_last = [jnp.exp(cs[c][0:1] if back[c] else cs[c][L - 1:L]) for c in nch]
    at = [-kk[c] * jnp.exp(cs[c] - lw[c]) for c in nch]
    g_inv = [jnp.exp(-cs[c]) for c in nch]
    bt = [bb[c] * g_inv[c] for c in nch]
    kt = [kd[c] * g_inv[c] for c in nch]
    rt = [r[c] * jnp.exp(cs[c]) for c in nch]
    ar = [jnp.concatenate([at[c], rt[c]], axis=0) for c in nch]
    pb = [_mm3(ar[c], bdiag(bt[c]), nt=True) for c in nch]
    pk = [_mm3(ar[c], bdiag(kt[c]), nt=True) for c in nch]
    n = [jnp.where(strict[c], pb[c][:L], 0.0) for c in nch]
    m_rb = [jnp.where(incl[c], pb[c][L:], 0.0) for c in nch]
    m_ak = [jnp.where(strict[c], pk[c][:L], 0.0) for c in nch]
    m_rk = [jnp.where(incl[c], pk[c][L:], 0.0) for c in nch]
    steps = int(math.log2(L)) - 1
    p = [bdm(n[c], n[c], 3) for c in nch]
    t_inv = [eye_l + n[c] for c in nch]
    for k in range(steps):
        if k + 1 < steps:
            both = [bdm(jnp.concatenate([p[c], t_inv[c]], axis=0), p[c], 1) for c in nch]
            p = [both[c][:L] for c in nch]
            t_inv = [t_inv[c] + both[c][L:] for c in nch]
        else:
            t_inv = [t_inv[c] + bdm(t_inv[c], p[c], 1) for c in nch]
    xy = [bdm(jnp.concatenate([m_ak[c], m_rk[c]], axis=0), v[c], 1) for c in nch]
    w = [bdm(t_inv[c], jnp.concatenate([at[c], xy[c][:L]], axis=1), 2) for c in nch]
    qq = [bdm(m_rb[c], w[c], 1) for c in nch]
    q1 = [rt[c] + qq[c][:, :QUAD_LANES] for c in nch]
    q2 = [xy[c][L:] + qq[c][:, QUAD_LANES:] for c in nch]
    g = [jnp.where(bd, (eye_q + _mm3(w[c][:, :QUAD_LANES].T, bt[c], passes=2)) * g_last[c], 0.0) for c in nch]
    h = [jnp.where(bd, _mm3(jnp.concatenate([w[c][:, QUAD_LANES:], v[c]], axis=0).T,
                            jnp.concatenate([bt[c], kt[c]], axis=0), passes=2) * g_last[c], 0.0) for c in nch]
    return [(q1[c], q2[c], g[c].T, h[c].T) for c in nch]


def _wkv_chunk_kernel(lw_ref, k_ref, b_ref, v_ref, kk_ref, r_ref, q1_ref, q2_ref, gt_ref, ht_ref, *, nquads):
    chains = []
    for d in range(2):
        for q in range(nquads):
            sl = slice(q * QUAD_LANES, (q + 1) * QUAD_LANES)
            sd = slice(d * RW_DIM + q * QUAD_LANES, d * RW_DIM + (q + 1) * QUAD_LANES)
            chains.append((lw_ref[0, :, sd], k_ref[0, :, sd], b_ref[0, :, sd],
                           v_ref[0, :, sl], kk_ref[0, :, sl], r_ref[0, :, sl], d == 1))
    res = _chunk_terms(chains)
    for d in range(2):
        for q in range(nquads):
            sl = slice(q * QUAD_LANES, (q + 1) * QUAD_LANES)
            q1, q2, gt, ht = res[d * nquads + q]
            q1_ref[0, d, :, sl] = q1
            q2_ref[0, d, :, sl] = q2
            gt_ref[0, d, 0, q] = gt
            ht_ref[0, d, 0, q] = ht


def wkv_chunk_prep(lw2, k2, b2, v, kk, r):
    B, T, C = v.shape
    nquads = C // QUAD_LANES
    nc = T // CHUNK
    wide = pl.BlockSpec((1, CHUNK, 2 * C), lambda b, c: (b, c, 0))
    narrow = pl.BlockSpec((1, CHUNK, C), lambda b, c: (b, c, 0))
    qspec = pl.BlockSpec((1, 2, CHUNK, C), lambda b, c: (b, 0, c, 0))
    gspec = pl.BlockSpec((1, 2, 1, nquads, QUAD_LANES, QUAD_LANES), lambda b, c: (b, 0, c, 0, 0, 0))
    qshape = jax.ShapeDtypeStruct((B, 2, T, C), F32)
    gshape = jax.ShapeDtypeStruct((B, 2, nc, nquads, QUAD_LANES, QUAD_LANES), F32)
    return pl.pallas_call(
        functools.partial(_wkv_chunk_kernel, nquads=nquads),
        out_shape=(qshape, qshape, gshape, gshape),
        grid=(B, nc),
        in_specs=[wide, wide, wide, narrow, narrow, narrow],
        out_specs=(qspec, qspec, gspec, gspec),
        compiler_params=_params(2),
        name="wkv_chunk_prep",
    )(lw2, k2, b2, v, kk, r)


def _wkv_seq_kernel(q1f_ref, q1b_ref, q2f_ref, q2b_ref, gtf_ref, gtb_ref, htf_ref, htb_ref, s0_ref,
                    of_ref, ob_ref, st_ref, z_ref, *, nquads):
    c = pl.program_id(1)

    @pl.when(c == 0)
    def _():
        z_ref[...] = s0_ref[0]

    dirs = ((q1f_ref, q2f_ref, gtf_ref, htf_ref, of_ref), (q1b_ref, q2b_ref, gtb_ref, htb_ref, ob_ref))
    chains = [(d, q, slice(q * QUAD_LANES, (q + 1) * QUAD_LANES)) for d in range(2) for q in range(nquads)]
    lhs = [jnp.concatenate([dirs[d][0][0, 0, :, sl], dirs[d][2][0, 0, 0, q]], axis=0) for d, q, sl in chains]
    res = [_mm3(lhs[i], z_ref[d * nquads + q]) for i, (d, q, sl) in enumerate(chains)]
    for i, (d, q, sl) in enumerate(chains):
        dirs[d][4][0, :, sl] = res[i][:CHUNK] + dirs[d][1][0, 0, :, sl]
        z_ref[d * nquads + q] = res[i][CHUNK:] + dirs[d][3][0, 0, 0, q]

    @pl.when(c == pl.num_programs(1) - 1)
    def _():
        st_ref[0] = z_ref[...]


def wkv_seq(q1, q2, gt, ht, s0):
    B, _, T, C = q1.shape
    nc, nquads = gt.shape[2], gt.shape[3]
    qf = pl.BlockSpec((1, 1, CHUNK, C), lambda b, c: (b, 0, c, 0))
    qb = pl.BlockSpec((1, 1, CHUNK, C), lambda b, c: (b, 1, nc - 1 - c, 0))
    gshape = (1, 1, 1, nquads, QUAD_LANES, QUAD_LANES)
    gf = pl.BlockSpec(gshape, lambda b, c: (b, 0, c, 0, 0, 0))
    gb = pl.BlockSpec(gshape, lambda b, c: (b, 1, nc - 1 - c, 0, 0, 0))
    sspec = pl.BlockSpec((1,) + s0.shape[1:], lambda b, c: (b, 0, 0, 0))
    return pl.pallas_call(
        functools.partial(_wkv_seq_kernel, nquads=nquads),
        out_shape=(jax.ShapeDtypeStruct((B, T, C), F32), jax.ShapeDtypeStruct((B, T, C), F32),
                   jax.ShapeDtypeStruct(s0.shape, F32)),
        grid=(B, nc),
        in_specs=[qf, qb, qf, qb, gf, gb, gf, gb, sspec],
        out_specs=(pl.BlockSpec((1, CHUNK, C), lambda b, c: (b, c, 0)),
                   pl.BlockSpec((1, CHUNK, C), lambda b, c: (b, nc - 1 - c, 0)), sspec),
        scratch_shapes=[pltpu.VMEM(s0.shape[1:], F32)],
        compiler_params=_params(2),
        name="wkv_seq",
    )(q1, q1, q2, q2, gt, gt, ht, ht, s0)


def wkv_chunked(lw2, k2, b2, v, kk, r, s0):
    q1, q2, gt, ht = wkv_chunk_prep(lw2, k2, b2, v, kk, r)
    return wkv_seq(q1, q2, gt, ht, s0)


def _mixer_out_kernel(of_ref, ob_ref, r_ref, v_ref, k2_ref, xg_ref, oat_ref, x_ref, gm_ref,
                      gup_ref, rk_ref, lng_ref, lnb_ref, wo_ref, o_ref):
    ones = _ones_bd(RW_DIM)
    inv = 1.0 / HEAD_DIM
    o = of_ref[0] + ob_ref[0]
    mu = _seg_sum2(o, ones) * inv
    dv = o - mu
    var = _seg_sum2(dv * dv, ones) * inv
    gn = dv * lax.rsqrt(var + GN_EPS) * lng_ref[...] + lnb_ref[...]
    k2 = k2_ref[0]
    k_bonus = (k2[:, :RW_DIM] + k2[:, RW_DIM:]) * 0.5
    bonus = _seg_sum2(r_ref[0] * k_bonus * rk_ref[...], ones) * v_ref[0]
    g = jnp.dot(jax.nn.sigmoid(xg_ref[0]), gup_ref[...], preferred_element_type=F32, precision=HIGHEST)
    y = (gn + bonus) * g
    acc = jnp.dot(oat_ref[0].astype(BF16), wo_ref[:ATTN_DIM, :], preferred_element_type=F32)
    acc = acc + jnp.dot(y.astype(BF16), wo_ref[ATTN_DIM:, :], preferred_element_type=F32)
    o_ref[0] = x_ref[0] + gm_ref[0] * acc


def mixer_out(o_f, o_b, r, v, k2, p, o_at, x, g_m, g_up, r_k, ln_x_g, ln_x_b, w_out):
    B, T, D = x.shape
    tm = _row_tile(T, 256)
    nspec = pl.BlockSpec((1, tm, RW_DIM), lambda b, i: (b, i, 0))
    full = lambda a: pl.BlockSpec(a.shape, lambda b, i: (0,) * a.ndim)
    rk2, lng2, lnb2 = r_k.reshape(1, RW_DIM), ln_x_g.reshape(1, RW_DIM), ln_x_b.reshape(1, RW_DIM)
    return pl.pallas_call(
        _mixer_out_kernel,
        out_shape=jax.ShapeDtypeStruct((B, T, D), F32),
        grid=(B, T // tm),
        in_specs=[nspec, nspec, nspec, nspec,
                  pl.BlockSpec((1, tm, 2 * RW_DIM), lambda b, i: (b, i, 0)),
                  pl.BlockSpec((1, tm, G_LORA), lambda b, i: (b, i, COL_XG // G_LORA)),
                  pl.BlockSpec((1, tm, ATTN_DIM), lambda b, i: (b, i, 0)),
                  pl.BlockSpec((1, tm, D), lambda b, i: (b, i, 0)),
                  pl.BlockSpec((1, 1, D), lambda b, i: (b, 0, 0)),
                  full(g_up), full(rk2), full(lng2), full(lnb2), full(w_out)],
        out_specs=pl.BlockSpec((1, tm, D), lambda b, i: (b, i, 0)),
        compiler_params=_params(2),
        name="mixer_out",
    )(o_f, o_b, r, v, k2, p, o_at, x, g_m, g_up, rk2, lng2, lnb2, w_out)


def _start_row_copies(idx_ref, n, src_hbm, dst_ref, sem):
    def body(r, carry):
        pltpu.make_async_copy(src_hbm.at[pl.ds(idx_ref[0, 0, r], 1)], dst_ref.at[pl.ds(r, 1)], sem).start()
        return carry
    lax.fori_loop(0, n, body, 0, unroll=8)


def _wait_row_copies(n, src_hbm, dst_ref, sem):
    pltpu.make_async_copy(src_hbm.at[pl.ds(0, n)], dst_ref.at[pl.ds(0, n)], sem).wait()


def _moe_kernel(be_ref, tok_ref, tokn_ref, h_hbm, w1_ref, b1_ref, w2_ref, b2_ref, o_ref, buf, sems, *, f, tm):
    i = pl.program_id(0)
    slot = i % 2

    @pl.when(i == 0)
    def _():
        _start_row_copies(tok_ref, tm, h_hbm, buf.at[0], sems.at[0])

    @pl.when(i + 1 < pl.num_programs(0))
    def _():
        _start_row_copies(tokn_ref, tm, h_hbm, buf.at[1 - slot], sems.at[1 - slot])

    _wait_row_copies(tm, h_hbm, buf.at[slot], sems.at[slot])
    u = jnp.dot(buf[slot].astype(BF16), w1_ref[0], preferred_element_type=F32) + b1_ref[0]
    x_glu = jnp.minimum(u[:, :f], SWIGLU_LIMIT)
    x_lin = jnp.clip(u[:, f:], -SWIGLU_LIMIT, SWIGLU_LIMIT)
    act = x_glu * jax.nn.sigmoid(SWIGLU_ALPHA * x_glu) * (x_lin + 1.0)
    o_ref[...] = jnp.dot(act.astype(BF16), w2_ref[0], preferred_element_type=F32) + b2_ref[0]


def moe_blocks(block_e, slot_tok, h, w1, b1, w2, b2):
    N, D = h.shape
    E, _, F2 = w1.shape
    f = F2 // 2
    tm = MOE_BLOCK
    rows = slot_tok.shape[0]
    nb = rows // tm
    tok3 = slot_tok.reshape(nb, 1, tm)
    kern = functools.partial(_moe_kernel, f=f, tm=tm)
    grid_spec = pltpu.PrefetchScalarGridSpec(
        num_scalar_prefetch=1,
        grid=(nb,),
        in_specs=[pl.BlockSpec((1, 1, tm), lambda i, be: (i, 0, 0), memory_space=pltpu.SMEM),
                  pl.BlockSpec((1, 1, tm), lambda i, be: (jnp.minimum(i + 1, nb - 1), 0, 0),
                               memory_space=pltpu.SMEM),
                  pl.BlockSpec(memory_space=pl.ANY),
                  pl.BlockSpec((1, D, F2), lambda i, be: (be[i], 0, 0)),
                  pl.BlockSpec((1, 1, F2), lambda i, be: (be[i], 0, 0)),
                  pl.BlockSpec((1, f, D), lambda i, be: (be[i], 0, 0)),
                  pl.BlockSpec((1, 1, D), lambda i, be: (be[i], 0, 0))],
        out_specs=pl.BlockSpec((tm, D), lambda i, be: (i, 0)),
        scratch_shapes=[pltpu.VMEM((2, tm, D), F32), pltpu.SemaphoreType.DMA((2,))],
    )
    return pl.pallas_call(
        kern,
        out_shape=jax.ShapeDtypeStruct((rows, D), F32),
        grid_spec=grid_spec,
        compiler_params=_params(1),
        name="moe_blocks",
    )(block_e, tok3, tok3, h, w1, b1.reshape(E, 1, F2), w2, b2.reshape(E, 1, D))


def _moe_combine_kernel(sl_ref, sln_ref, out_hbm, gate_ref, x_ref, gf_ref, o_ref, buf, sems, *, tb):
    b, i = pl.program_id(0), pl.program_id(1)
    step = b * pl.num_programs(1) + i
    nsteps = pl.num_programs(0) * pl.num_programs(1)
    slot = step % 2
    n = TOP_K * tb

    @pl.when(step == 0)
    def _():
        _start_row_copies(sl_ref, n, out_hbm, buf.at[0], sems.at[0])

    @pl.when(step + 1 < nsteps)
    def _():
        _start_row_copies(sln_ref, n, out_hbm, buf.at[1 - slot], sems.at[1 - slot])

    _wait_row_copies(n, out_hbm, buf.at[slot], sems.at[slot])
    gate = gate_ref[0]
    y = jnp.zeros(x_ref.shape[1:], F32)
    for k in range(TOP_K):
        y = y + gate[:, k:k + 1] * buf[slot, k * tb:(k + 1) * tb, :]
    o_ref[0] = x_ref[0] + gf_ref[0] * y


def moe_combine(out, slots, gates, x, g_f):
    B, T, D = x.shape
    tb = _row_tile(T, 64)
    nt = T // tb
    n = TOP_K * tb
    sl3 = jnp.transpose(slots.reshape(B * nt, tb, TOP_K), (0, 2, 1)).reshape(B * nt, 1, n)
    last = B * nt - 1
    return pl.pallas_call(
        functools.partial(_moe_combine_kernel, tb=tb),
        out_shape=jax.ShapeDtypeStruct((B, T, D), F32),
        grid=(B, nt),
        in_specs=[pl.BlockSpec((1, 1, n), lambda b, i: (b * nt + i, 0, 0), memory_space=pltpu.SMEM),
                  pl.BlockSpec((1, 1, n), lambda b, i: (jnp.minimum(b * nt + i + 1, last), 0, 0),
                               memory_space=pltpu.SMEM),
                  pl.BlockSpec(memory_space=pl.ANY),
                  pl.BlockSpec((1, tb, LANES), lambda b, i: (b, i, 0)),
                  pl.BlockSpec((1, tb, D), lambda b, i: (b, i, 0)),
                  pl.BlockSpec((1, 1, D), lambda b, i: (b, 0, 0))],
        out_specs=pl.BlockSpec((1, tb, D), lambda b, i: (b, i, 0)),
        scratch_shapes=[pltpu.VMEM((2, n, D), F32), pltpu.SemaphoreType.DMA((2,))],
        compiler_params=_params(2),
        name="moe_combine",
    )(sl3, sl3, out, gates, x, g_f)


def _ffn_pre_kernel(x_ref, g_ref, sc_ref, sh_ref, rw_ref, rb_ref, h_ref, te_ref, tg_ref):
    h = _norm_mod(x_ref[0], g_ref[...], sc_ref[0], sh_ref[0])
    h_ref[0] = h
    cur = jnp.dot(h, rw_ref[...], preferred_element_type=F32, precision=HIGHEST) + rb_ref[...]
    n_e = cur.shape[1]
    lane_e = lax.broadcasted_iota(jnp.int32, cur.shape, 1).astype(F32)
    lane = lax.broadcasted_iota(jnp.int32, te_ref.shape[1:], 1)
    vals, idxs = [], []
    for k in range(TOP_K):
        m = jnp.max(cur, axis=1, keepdims=True)
        idx = jnp.min(jnp.where(cur == m, lane_e, float(n_e)), axis=1, keepdims=True)
        vals.append(m)
        idxs.append(idx)
        cur = jnp.where(lane_e == idx, -jnp.inf, cur)
    ex = [jnp.exp(v - vals[0]) for v in vals]
    denom = ex[0]
    for e in ex[1:]:
        denom = denom + e
    te = jnp.zeros(lane.shape, F32)
    tg = jnp.zeros(lane.shape, F32)
    for k in range(TOP_K):
        te = jnp.where(lane == k, idxs[k], te)
        tg = jnp.where(lane == k, ex[k] / denom, tg)
    te_ref[0] = te.astype(jnp.int32)
    tg_ref[0] = tg


def ffn_pre(x, g, sc, sh, router_w, router_b):
    B, T, D = x.shape
    E = router_w.shape[1]
    tm = _row_tile(T, 256)
    return pl.pallas_call(
        _ffn_pre_kernel,
        out_shape=(jax.ShapeDtypeStruct((B, T, D), F32), jax.ShapeDtypeStruct((B, T, LANES), jnp.int32),
                   jax.ShapeDtypeStruct((B, T, LANES), F32)),
        grid=(B, T // tm),
        in_specs=[pl.BlockSpec((1, tm, D), lambda b, i: (b, i, 0)),
                  pl.BlockSpec((1, D), lambda b, i: (0, 0)),
                  pl.BlockSpec((1, 1, D), lambda b, i: (b, 0, 0)),
                  pl.BlockSpec((1, 1, D), lambda b, i: (b, 0, 0)),
                  pl.BlockSpec((D, E), lambda b, i: (0, 0)),
                  pl.BlockSpec((1, E), lambda b, i: (0, 0))],
        out_specs=(pl.BlockSpec((1, tm, D), lambda b, i: (b, i, 0)),
                   pl.BlockSpec((1, tm, LANES), lambda b, i: (b, i, 0)),
                   pl.BlockSpec((1, tm, LANES), lambda b, i: (b, i, 0))),
        compiler_params=_params(2),
        name="ffn_pre",
    )(x, g.reshape(1, D), sc, sh, router_w, router_b.reshape(1, E))


def _rms_kernel(x_ref, g_ref, o_ref):
    x = x_ref[0]
    ms = jnp.mean(x * x, axis=-1, keepdims=True)
    o_ref[0] = x * lax.rsqrt(ms + NORM_EPS) * g_ref[...]


def final_norm(x, g):
    B, T, D = x.shape
    tm = _row_tile(T, 512)
    return pl.pallas_call(
        _rms_kernel,
        out_shape=jax.ShapeDtypeStruct((B, T, D), F32),
        grid=(B, T // tm),
        in_specs=[pl.BlockSpec((1, tm, D), lambda b, i: (b, i, 0)),
                  pl.BlockSpec((1, D), lambda b, i: (0, 0))],
        out_specs=pl.BlockSpec((1, tm, D), lambda b, i: (b, i, 0)),
        compiler_params=_params(2),
        name="final_norm",
    )(x, g.reshape(1, D))


def _route(top_e):
    N = top_e.shape[0]
    n4 = N * TOP_K
    flat_e = top_e.reshape(-1)
    order = jnp.argsort(flat_e)
    e_sorted = flat_e[order]
    counts = jnp.bincount(flat_e, length=N_EXPERTS)
    padded = (counts + MOE_BLOCK - 1) // MOE_BLOCK * MOE_BLOCK
    start = jnp.cumsum(counts) - counts
    pstart = jnp.cumsum(padded) - padded
    dest = (pstart[e_sorted] + jnp.arange(n4, dtype=jnp.int32) - start[e_sorted]).astype(jnp.int32)
    n_blocks = (n4 + MOE_BLOCK - 1) // MOE_BLOCK + N_EXPERTS
    rows = n_blocks * MOE_BLOCK
    block_e = jnp.minimum(jnp.searchsorted(jnp.cumsum(padded), jnp.arange(n_blocks) * MOE_BLOCK, side='right'),
                          N_EXPERTS - 1).astype(jnp.int32)
    slot = jnp.arange(rows, dtype=jnp.int32)
    slot_e = jnp.repeat(block_e, MOE_BLOCK)
    off = slot - pstart[slot_e].astype(jnp.int32)
    src = jnp.minimum(start[slot_e].astype(jnp.int32) + off, n4 - 1)
    slot_tok = jnp.where(off < counts[slot_e], order[src] // TOP_K, 0).astype(jnp.int32)
    slot_flat = dest[jnp.argsort(order)]
    return slot_tok, block_e, slot_flat.reshape(N, TOP_K)


def kernel(x, c, ctx, c_ctx, norm_mix_g, norm_ffn_g, w_mod, b_mod, w_in, w_out, q_norm_g, k_norm_g,
           conv_w, w0, w_up, a0, a_up, g_up, k_k, k_a, r_k, ln_x_g, ln_x_b,
           router_w, router_b, e_w1, e_b1, e_w2, e_b2, norm_final_g):
    B, T, D = x.shape
    C = ctx.shape[1]
    depth = w_in.shape[0]
    cos, sin = _rope_tables(T)
    rpad = -(-(B + 1) // SUBLANES) * SUBLANES
    cvecs = jnp.zeros((rpad, D), F32).at[:B].set(c).at[B].set(c_ctx)
    s_zero = jnp.zeros((B, 2 * RW_DIM // QUAD_LANES, QUAD_LANES, QUAD_LANES), F32)
    x_lat, x_ctx = x, ctx
    for l in range(depth):
        last = l == depth - 1
        mods = adaln_all(cvecs, w_mod[l], b_mod[l])
        m_lat = [m[:, None, :] for m in jnp.split(mods[:B], 6, axis=-1)]
        m_ctx = [jnp.broadcast_to(m[None, :, :], (B, 1, D)) for m in jnp.split(mods[B:B + 1], 6, axis=-1)]
        sh_m, sc_m, g_m, sh_f, sc_f, g_f = m_lat
        csh_m, csc_m, cg_m, csh_f, csc_f, cg_f = m_ctx
        w_in_l = w_in[l].astype(BF16)
        w_out_l = w_out[l].astype(BF16)
        p_lat = norm_mod_matmul(x_lat, norm_mix_g[l], sc_m, sh_m, w_in_l)
        p_ctx = norm_mod_matmul(x_ctx, norm_mix_g[l], csc_m, csh_m,
                                w_in_l[:, :STATE_COLS] if last else w_in_l)

        ktc, vc = kv_prep(p_ctx, k_norm_g[l])
        ktl, vl = kv_prep(p_lat, k_norm_g[l], cos, sin)
        o_at = attention(p_lat, q_norm_g[l], ktc, vc, ktl, vl, cos, sin)

        rw = (conv_w[l], w0[l], w_up[l], a0[l], a_up[l], k_k[l], k_a[l])
        w2c, k2c, b2c, v_c, kk_c, r_c = wkv_prep(p_ctx, *rw, has_r=not last)
        w2l, k2l, b2l, v_l, kk_l, r_l = wkv_prep(p_lat, *rw, has_r=True)
        of_c, ob_c, s_c = wkv_chunked(w2c, k2c, b2c, v_c, kk_c, r_c, s_zero)
        of_l, ob_l, _ = wkv_chunked(w2l, k2l, b2l, v_l, kk_l, r_l, s_c)

        ro = (g_up[l], r_k[l], ln_x_g[l], ln_x_b[l], w_out_l)
        x_lat = mixer_out(of_l, ob_l, r_l, v_l, k2l, p_lat, o_at, x_lat, g_m, *ro)
        h_lat, te_lat, tg_lat = ffn_pre(x_lat, norm_ffn_g[l], sc_f, sh_f, router_w[l], router_b[l])
        e_w1_l = e_w1[l].astype(BF16)
        e_w2_l = e_w2[l].astype(BF16)
        if last:
            slot_tok, block_e, slot_flat = _route(te_lat.reshape(B * T, LANES)[:, :TOP_K])
            out = moe_blocks(block_e, slot_tok, h_lat.reshape(B * T, D), e_w1_l, e_b1[l], e_w2_l, e_b2[l])
            x_lat = moe_combine(out, slot_flat.reshape(B, T, TOP_K), tg_lat, x_lat, g_f)
        else:
            o_at_c = attention(p_ctx, q_norm_g[l], ktc, vc)
            x_ctx = mixer_out(of_c, ob_c, r_c, v_c, k2c, p_ctx, o_at_c, x_ctx, cg_m, *ro)
            h_ctx, te_ctx, tg_ctx = ffn_pre(x_ctx, norm_ffn_g[l], csc_f, csh_f, router_w[l], router_b[l])
            h_all = jnp.concatenate([h_lat.reshape(B * T, D), h_ctx.reshape(B * C, D)], axis=0)
            te_all = jnp.concatenate([te_lat.reshape(B * T, LANES), te_ctx.reshape(B * C, LANES)], axis=0)
            slot_tok, block_e, slot_flat = _route(te_all[:, :TOP_K])
            out = moe_blocks(block_e, slot_tok, h_all, e_w1_l, e_b1[l], e_w2_l, e_b2[l])
            x_lat = moe_combine(out, slot_flat[:B * T].reshape(B, T, TOP_K), tg_lat, x_lat, g_f)
            x_ctx = moe_combine(out, slot_flat[B * T:].reshape(B, C, TOP_K), tg_ctx, x_ctx, cg_f)
    return final_norm(x_lat, norm_final_g)
```

```python
import functools
import math

import numpy as np
import jax
import jax.numpy as jnp
from jax import lax
from jax.experimental import pallas as pl
from jax.experimental.pallas import tpu as pltpu

F32 = jnp.float32
BF16 = jnp.bfloat16
HIGHEST = lax.Precision.HIGHEST

GRID_W = 64
HEAD_DIM = 64
ATTN_HEADS = 8
ATTN_KV_HEADS = 2
GROUP = ATTN_HEADS // ATTN_KV_HEADS
ATTN_DIM = ATTN_HEADS * HEAD_DIM
KV_DIM = ATTN_KV_HEADS * HEAD_DIM
RW_HEADS = 8
RW_DIM = RW_HEADS * HEAD_DIM
W_LORA = 64
A_LORA = 64
G_LORA = 128
CONV_W = 3
N_EXPERTS = 32
TOP_K = 4
SWIGLU_LIMIT = 7.0
SWIGLU_ALPHA = 1.702
ROPE_THETA = 10000.0
ROPE_PAIRS = HEAD_DIM // 4
MOE_BLOCK = 256
NORM_EPS = 1e-6
GN_EPS = 64e-5
ATTN_SCALE = HEAD_DIM ** -0.5
EXP_M05 = math.exp(-0.5)
COL_K_AT = 0
COL_V_AT = KV_DIM
COL_K_RW = 2 * KV_DIM
COL_V_RW = COL_K_RW + RW_DIM
COL_XW = COL_V_RW + RW_DIM
COL_XA = COL_XW + 2 * W_LORA
STATE_COLS = COL_XA + 2 * A_LORA
COL_Q_AT = STATE_COLS
COL_R_RW = COL_Q_AT + ATTN_DIM
COL_XG = COL_R_RW + RW_DIM

LANES = 128
SUBLANES = 8
QUAD_LANES = 256
VMEM_LIMIT = 48 * 1024 * 1024


def _row_tile(n, pref):
    t = pref
    while n % t:
        t //= 2
    return t


def _params(n_axes):
    return pltpu.CompilerParams(dimension_semantics=("arbitrary",) * n_axes, vmem_limit_bytes=VMEM_LIMIT)


def _ones_bd(n):
    ra = lax.broadcasted_iota(jnp.int32, (n, n), 0) // HEAD_DIM
    rb = lax.broadcasted_iota(jnp.int32, (n, n), 1) // HEAD_DIM
    return (ra == rb).astype(BF16)


def _seg_sum(x_bf16, ones_bd):
    return jnp.dot(x_bf16, ones_bd, preferred_element_type=F32)


def _seg_sum2(x, ones_bd):
    hi = x.astype(BF16)
    lo = (x - hi.astype(F32)).astype(BF16)
    return _seg_sum(hi, ones_bd) + _seg_sum(lo, ones_bd)


def _adaln_kernel(c_ref, w_ref, b_ref, o_ref):
    c = c_ref[...]
    s = c * jax.nn.sigmoid(c)
    o_ref[...] = jnp.dot(s, w_ref[...], preferred_element_type=F32, precision=HIGHEST) + b_ref[...]


def adaln_all(cvecs, w_mod, b_mod):
    R, D = cvecs.shape
    N = w_mod.shape[1]
    tn = 512
    return pl.pallas_call(
        _adaln_kernel,
        out_shape=jax.ShapeDtypeStruct((R, N), F32),
        grid=(N // tn,),
        in_specs=[pl.BlockSpec((R, D), lambda j: (0, 0)),
                  pl.BlockSpec((D, tn), lambda j: (0, j)),
                  pl.BlockSpec((1, tn), lambda j: (0, j))],
        out_specs=pl.BlockSpec((R, tn), lambda j: (0, j)),
        compiler_params=_params(1),
        name="adaln",
    )(cvecs, w_mod, b_mod.reshape(1, N))


def _norm_mod(x, g, sc, sh):
    ms = jnp.mean(x * x, axis=-1, keepdims=True)
    y = x * lax.rsqrt(ms + NORM_EPS) * g
    return y * (1.0 + sc) + sh


def _norm_mod_matmul_kernel(x_ref, g_ref, sc_ref, sh_ref, w_ref, o_ref):
    h = _norm_mod(x_ref[0], g_ref[...], sc_ref[0], sh_ref[0])
    o_ref[0] = jnp.dot(h.astype(BF16), w_ref[...], preferred_element_type=F32)


def norm_mod_matmul(x, g, sc, sh, w):
    B, T, D = x.shape
    N = w.shape[1]
    tm = _row_tile(T, 256)
    return pl.pallas_call(
        _norm_mod_matmul_kernel,
        out_shape=jax.ShapeDtypeStruct((B, T, N), F32),
        grid=(B, T // tm),
        in_specs=[pl.BlockSpec((1, tm, D), lambda b, i: (b, i, 0)),
                  pl.BlockSpec((1, D), lambda b, i: (0, 0)),
                  pl.BlockSpec((1, 1, D), lambda b, i: (b, 0, 0)),
                  pl.BlockSpec((1, 1, D), lambda b, i: (b, 0, 0)),
                  pl.BlockSpec((D, N), lambda b, i: (0, 0))],
        out_specs=pl.BlockSpec((1, tm, N), lambda b, i: (b, i, 0)),
        compiler_params=_params(2),
        name="norm_mod_matmul",
    )(x, g.reshape(1, D), sc, sh, w)


def _pair_rms(x, g):
    lane = lax.broadcasted_iota(jnp.int32, x.shape, 1)
    first = lane < HEAD_DIM
    xx = x * x
    s0 = jnp.sum(jnp.where(first, xx, 0.0), axis=1, keepdims=True)
    s1 = jnp.sum(jnp.where(first, 0.0, xx), axis=1, keepdims=True)
    ms = jnp.where(first, s0, s1) * (1.0 / HEAD_DIM)
    return x * lax.rsqrt(ms + NORM_EPS) * g


def _pair_rope(x, cos, sin):
    lane = lax.broadcasted_iota(jnp.int32, x.shape, 1)
    partner = jnp.where(lane % (2 * ROPE_PAIRS) < ROPE_PAIRS,
                        pltpu.roll(x, LANES - ROPE_PAIRS, 1), pltpu.roll(x, ROPE_PAIRS, 1))
    return x * cos + partner * sin


def _rope_tables(T):
    rows = T // GRID_W
    row = jnp.repeat(jnp.arange(rows), GRID_W)
    col = jnp.tile(jnp.arange(GRID_W), rows)
    pos = jnp.stack([row, col], axis=-1).astype(F32)
    freqs = ROPE_THETA ** (-jnp.arange(ROPE_PAIRS, dtype=F32) / ROPE_PAIRS)
    ang = pos[:, :, None] * freqs
    cos, sin = jnp.cos(ang), jnp.sin(ang)
    cos_h = jnp.concatenate([cos[:, 0], cos[:, 0], cos[:, 1], cos[:, 1]], axis=-1)
    sin_h = jnp.concatenate([-sin[:, 0], sin[:, 0], -sin[:, 1], sin[:, 1]], axis=-1)
    return jnp.tile(cos_h, (1, 2)), jnp.tile(sin_h, (1, 2))


def _kv_prep_kernel(*refs, rope):
    if rope:
        p_ref, g_ref, cos_ref, sin_ref, kt_ref, v_ref = refs
    else:
        p_ref, g_ref, kt_ref, v_ref = refs
    x = p_ref[0]
    k = _pair_rms(x[:, :KV_DIM], g_ref[...])
    if rope:
        k = _pair_rope(k, cos_ref[...], sin_ref[...])
    v = x[:, KV_DIM:]
    lane = lax.broadcasted_iota(jnp.int32, k.shape, 1)
    first = lane < HEAD_DIM
    kr = pltpu.roll(k, HEAD_DIM, 1)
    vr = pltpu.roll(v, HEAD_DIM, 1)
    kt_ref[0, 0] = jnp.where(first, k, kr).T.astype(BF16)
    kt_ref[0, 1] = jnp.where(first, kr, k).T.astype(BF16)
    ones = jnp.ones(v.shape, F32)
    v_ref[0, 0] = jnp.concatenate([jnp.where(first, v, vr), ones], axis=1).astype(BF16)
    v_ref[0, 1] = jnp.concatenate([jnp.where(first, vr, v), ones], axis=1).astype(BF16)


def kv_prep(p, k_norm_g, cos=None, sin=None):
    B, T, _ = p.shape
    tm = _row_tile(T, 256)
    rope = cos is not None
    g2 = jnp.tile(k_norm_g, 2).reshape(1, LANES)
    in_specs = [pl.BlockSpec((1, tm, 2 * KV_DIM), lambda b, i: (b, i, 0)),
                pl.BlockSpec((1, LANES), lambda b, i: (0, 0))]
    args = [p, g2]
    if rope:
        in_specs += [pl.BlockSpec((tm, LANES), lambda b, i: (i, 0))] * 2
        args += [cos, sin]
    return pl.pallas_call(
        functools.partial(_kv_prep_kernel, rope=rope),
        out_shape=(jax.ShapeDtypeStruct((B, ATTN_KV_HEADS, LANES, T), BF16),
                   jax.ShapeDtypeStruct((B, ATTN_KV_HEADS, T, 2 * LANES), BF16)),
        grid=(B, T // tm),
        in_specs=in_specs,
        out_specs=(pl.BlockSpec((1, ATTN_KV_HEADS, LANES, tm), lambda b, i: (b, 0, 0, i)),
                   pl.BlockSpec((1, ATTN_KV_HEADS, tm, 2 * LANES), lambda b, i: (b, 0, i, 0))),
        compiler_params=_params(2),
        name="kv_prep",
    )(*args)


def _attn_kernel(*refs, tq, tkl, nkl, rope, has_lat, rg):
    refs = list(refs)
    p_ref, g_ref = refs[:2]
    refs = refs[2:]
    if rope:
        cos_ref, sin_ref = refs[:2]
        refs = refs[2:]
    ktc_ref, vc_ref = refs[:2]
    refs = refs[2:]
    if has_lat:
        ktl_ref, vl_ref = refs[:2]
        refs = refs[2:]
    o_ref, m_ref, acc_ref = refs

    q = _pair_rms(p_ref[0], g_ref[...])
    if rope:
        q = _pair_rope(q, cos_ref[...], sin_ref[...])
    q = q * ATTN_SCALE
    lane = lax.broadcasted_iota(jnp.int32, (tq, LANES), 1)
    first = lane < HEAD_DIM
    qq = jnp.concatenate([jnp.where(first, q, 0.0), jnp.where(first, 0.0, q)], axis=0).astype(BF16)

    m_ref[...] = jnp.full(m_ref.shape, -jnp.inf, F32)
    acc_ref[...] = jnp.zeros(acc_ref.shape, F32)
    groups = [slice(g * rg, (g + 1) * rg) for g in range(2 * tq // rg)]

    def chunk(kt, vv):
        s = [jnp.dot(qq[g], kt, preferred_element_type=F32) for g in groups]
        m_old = [m_ref[g] for g in groups]
        m_new = [jnp.maximum(mo, jnp.max(x, axis=-1, keepdims=True)) for mo, x in zip(m_old, s)]
        alpha = [jnp.exp(mo - mn) for mo, mn in zip(m_old, m_new)]
        p = [jnp.exp(x - jnp.concatenate([mn] * (x.shape[1] // LANES), axis=1)).astype(BF16)
             for x, mn in zip(s, m_new)]
        pv = [jnp.dot(x, vv, preferred_element_type=F32) for x in p]
        for g, al, mn, x in zip(groups, alpha, m_new, pv):
            acc_ref[g] = jnp.concatenate([al, al], axis=1) * acc_ref[g] + x
            m_ref[g] = mn

    chunk(ktc_ref[0, 0], vc_ref[0, 0])
    if has_lat:
        def body(j, carry):
            ks = pl.multiple_of(j * tkl, tkl)
            chunk(ktl_ref[0, 0, :, pl.ds(ks, tkl)], vl_ref[0, 0, pl.ds(ks, tkl), :])
            return carry
        lax.fori_loop(0, nkl, body, 0)

    acc = acc_ref[...]
    o = acc[:, :LANES] / acc[:, LANES:]
    o_ref[0] = jnp.where(first, o[:tq], o[tq:])


def attention(p, q_norm_g, ktc, vc, ktl=None, vl=None, cos=None, sin=None):
    B, T, _ = p.shape
    C = ktc.shape[-1]
    rope = cos is not None
    has_lat = ktl is not None
    tq = _row_tile(T, 256)
    pairs = ATTN_HEADS // 2
    pairs_per_kv = GROUP // 2
    qblk0 = COL_Q_AT // LANES
    g2 = jnp.tile(q_norm_g, 2).reshape(1, LANES)
    in_specs = [pl.BlockSpec((1, tq, LANES), lambda b, h, i: (b, i, qblk0 + h)),
                pl.BlockSpec((1, LANES), lambda b, h, i: (0, 0))]
    args = [p, g2]
    if rope:
        in_specs += [pl.BlockSpec((tq, LANES), lambda b, h, i: (i, 0))] * 2
        args += [cos, sin]
    in_specs += [pl.BlockSpec((1, 1, LANES, C), lambda b, h, i: (b, h // pairs_per_kv, 0, 0)),
                 pl.BlockSpec((1, 1, C, 2 * LANES), lambda b, h, i: (b, h // pairs_per_kv, 0, 0))]
    args += [ktc, vc]
    tkl = nkl = 0
    if has_lat:
        S = ktl.shape[-1]
        tkl = _row_tile(S, 1024)
        nkl = S // tkl
        in_specs += [pl.BlockSpec((1, 1, LANES, S), lambda b, h, i: (b, h // pairs_per_kv, 0, 0)),
                     pl.BlockSpec((1, 1, S, 2 * LANES), lambda b, h, i: (b, h // pairs_per_kv, 0, 0))]
        args += [ktl, vl]
    kern = functools.partial(_attn_kernel, tq=tq, tkl=tkl, nkl=nkl, rope=rope, has_lat=has_lat,
                             rg=_row_tile(tq, ATTN_ROW_GROUP))
    return pl.pallas_call(
        kern,
        out_shape=jax.ShapeDtypeStruct((B, T, ATTN_DIM), F32),
        grid=(B, pairs, T // tq),
        in_specs=in_specs,
        out_specs=pl.BlockSpec((1, tq, LANES), lambda b, h, i: (b, i, h)),
        scratch_shapes=[pltpu.VMEM((2 * tq, LANES), F32), pltpu.VMEM((2 * tq, 2 * LANES), F32)],
        compiler_params=_params(3),
        name="attention",
    )(*args)


def _wkv_prep_kernel(p_ref, pp_ref, pn_ref, cw_ref, w0_ref, wup_ref, a0_ref, aup_ref, kk_ref, ka_ref,
                     w2_ref, k2_ref, b2_ref, v_ref, kkn_ref, r_ref, *, tm, has_r):
    i = pl.program_id(1)
    n = pl.num_programs(1)
    x = p_ref[0]
    prev = pp_ref[0][SUBLANES - 1:SUBLANES, :] * (i > 0).astype(F32)
    nxt = pn_ref[0][0:1, :] * (i < n - 1).astype(F32)
    row = lax.broadcasted_iota(jnp.int32, (tm, RW_DIM), 0)

    def conv(col, wcol):
        u = x[:, col:col + RW_DIM]
        up = jnp.where(row == 0, prev[:, col:col + RW_DIM], pltpu.roll(u, 1, 0))
        un = jnp.where(row == tm - 1, nxt[:, col:col + RW_DIM], pltpu.roll(u, tm - 1, 0))
        w = cw_ref[:, wcol:wcol + RW_DIM]
        return up * w[0:1] + u * w[1:2] + un * w[2:3]

    k = conv(COL_K_RW, RW_DIM)
    v = conv(COL_V_RW, 2 * RW_DIM)
    v_ref[0] = v
    r_ref[0] = conv(COL_R_RW, 0) if has_r else jnp.zeros((tm, RW_DIM), F32)
    kk = k * kk_ref[...]
    ss = _seg_sum2(kk * kk, _ones_bd(RW_DIM))
    kk = kk * lax.rsqrt(jnp.maximum(ss, 1e-24))
    kkn_ref[0] = kk
    xw = jnp.tanh(x[:, COL_XW:COL_XW + 2 * W_LORA])
    xa = x[:, COL_XA:COL_XA + 2 * A_LORA]
    for d in range(2):
        wl = w0_ref[d:d + 1, :] + jnp.dot(xw[:, d * W_LORA:(d + 1) * W_LORA], wup_ref[d],
                                          preferred_element_type=F32, precision=HIGHEST)
        decay = -EXP_M05 * jax.nn.sigmoid(wl)
        a = jax.nn.sigmoid(a0_ref[d:d + 1, :] + jnp.dot(xa[:, d * A_LORA:(d + 1) * A_LORA], aup_ref[d],
                                                        preferred_element_type=F32, precision=HIGHEST))
        sl = slice(d * RW_DIM, (d + 1) * RW_DIM)
        w2_ref[0, :, sl] = decay
        k2_ref[0, :, sl] = k * (1.0 + (a - 1.0) * ka_ref[...])
        b2_ref[0, :, sl] = kk * a


def wkv_prep(p, conv_w, w0, w_up, a0, a_up, k_k, k_a, has_r):
    B, T, NP = p.shape
    tm = _row_tile(T, 256)
    nb8 = tm // SUBLANES
    last8 = T // SUBLANES - 1
    full = lambda a: pl.BlockSpec(a.shape, lambda b, i: (0,) * a.ndim)
    kk2, ka2 = k_k.reshape(1, RW_DIM), k_a.reshape(1, RW_DIM)
    wide = jax.ShapeDtypeStruct((B, T, 2 * RW_DIM), F32)
    narrow = jax.ShapeDtypeStruct((B, T, RW_DIM), F32)
    wspec = pl.BlockSpec((1, tm, 2 * RW_DIM), lambda b, i: (b, i, 0))
    nspec = pl.BlockSpec((1, tm, RW_DIM), lambda b, i: (b, i, 0))
    return pl.pallas_call(
        functools.partial(_wkv_prep_kernel, tm=tm, has_r=has_r),
        out_shape=(wide, wide, wide, narrow, narrow, narrow),
        grid=(B, T // tm),
        in_specs=[pl.BlockSpec((1, tm, NP), lambda b, i: (b, i, 0)),
                  pl.BlockSpec((1, SUBLANES, NP), lambda b, i: (b, jnp.maximum(i * nb8 - 1, 0), 0)),
                  pl.BlockSpec((1, SUBLANES, NP), lambda b, i: (b, jnp.minimum((i + 1) * nb8, last8), 0)),
                  full(conv_w), full(w0), full(w_up), full(a0), full(a_up), full(kk2), full(ka2)],
        out_specs=(wspec, wspec, wspec, nspec, nspec, nspec),
        compiler_params=_params(2),
        name="wkv_prep",
    )(p, p, p, conv_w, w0, w_up, a0, a_up, kk2, ka2)


def _wkv_kernel(wf_ref, wb_ref, kf_ref, kb_ref, bf_ref, bb_ref, vf_ref, vb_ref, af_ref, ab_ref,
                rf_ref, rb_ref, s0_ref, of_ref, ob_ref, st_ref, s_ref, *, tc, nquads):
    tb = pl.program_id(1)

    @pl.when(tb == 0)
    def _():
        s_ref[...] = s0_ref[0]

    rows_all = 2 * nquads * HEAD_DIM
    shp = (rows_all, QUAD_LANES)
    lane = lax.broadcasted_iota(jnp.int32, shp, 1)
    sub = lax.broadcasted_iota(jnp.int32, shp, 0)
    eye = (lane % HEAD_DIM == sub % HEAD_DIM).astype(F32)
    ones_bd = _ones_bd(QUAD_LANES)
    ng = tc // SUBLANES

    def bcast_rows(blks, j):
        parts = []
        for d in range(2):
            jj = j if d == 0 else SUBLANES - 1 - j
            for q in range(nquads):
                row = blks[d][jj:jj + 1, q * QUAD_LANES:(q + 1) * QUAD_LANES]
                parts.append(jnp.broadcast_to(row, (HEAD_DIM, QUAD_LANES)))
        return jnp.concatenate(parts, axis=0)

    def group(g, carry):
        tf = pl.multiple_of(g * SUBLANES, SUBLANES)
        tr = pl.multiple_of((ng - 1 - g) * SUBLANES, SUBLANES)
        ld = lambda fr, br: [fr[0, pl.ds(tf, SUBLANES), :], br[0, pl.ds(tr, SUBLANES), :]]
        wq, kq, bq, vq, rq = (ld(f, b) for f, b in ((wf_ref, wb_ref), (kf_ref, kb_ref), (bf_ref, bb_ref),
                                                    (vf_ref, vb_ref), (rf_ref, rb_ref)))
        aq = [-x for x in ld(af_ref, ab_ref)]
        S = s_ref[...]
        out_rows = [[] for _ in range(2 * nquads)]
        for j in range(SUBLANES):
            wt, kt, vt, at, bt, rt = (bcast_rows(x, j) for x in (wq, kq, vq, aq, bq, rq))
            sa = _seg_sum2(S * at, ones_bd)
            vcol = _seg_sum((eye * vt).astype(BF16), ones_bd)
            S = S * wt + sa * bt + vcol * kt
            ob = _seg_sum((S * rt).astype(BF16), ones_bd) * eye
            for c in range(2 * nquads):
                out_rows[c].append(jnp.sum(ob[c * HEAD_DIM:(c + 1) * HEAD_DIM], axis=0, keepdims=True))
        s_ref[...] = S
        for q in range(nquads):
            sl = slice(q * QUAD_LANES, (q + 1) * QUAD_LANES)
            of_ref[0, pl.ds(tf, SUBLANES), sl] = jnp.concatenate(out_rows[q], axis=0)
            ob_ref[0, pl.ds(tr, SUBLANES), sl] = jnp.concatenate(out_rows[nquads + q][::-1], axis=0)
        return carry

    lax.fori_loop(0, ng, group, 0)

    @pl.when(tb == pl.num_programs(1) - 1)
    def _():
        st_ref[0] = s_ref[...]


def wkv_scan(w2, k2, b2, v, kk, r, s0):
    B, T, C = v.shape
    nquads = C // QUAD_LANES
    tc = _row_tile(T, 128)
    nt = T // tc
    fwd = lambda col: pl.BlockSpec((1, tc, C), lambda b, t: (b, t, col))
    bwd = lambda col: pl.BlockSpec((1, tc, C), lambda b, t: (b, nt - 1 - t, col))
    sspec = pl.BlockSpec((1,) + s0.shape[1:], lambda b, t: (b, 0, 0))
    kern = functools.partial(_wkv_kernel, tc=tc, nquads=nquads)
    return pl.pallas_call(
        kern,
        out_shape=(jax.ShapeDtypeStruct((B, T, C), F32), jax.ShapeDtypeStruct((B, T, C), F32),
                   jax.ShapeDtypeStruct(s0.shape, F32)),
        grid=(B, nt),
        in_specs=[fwd(0), bwd(1), fwd(0), bwd(1), fwd(0), bwd(1), fwd(0), bwd(0), fwd(0), bwd(0),
                  fwd(0), bwd(0), sspec],
        out_specs=(fwd(0), bwd(0), sspec),
        scratch_shapes=[pltpu.VMEM(s0.shape[1:], F32)],
        compiler_params=_params(2),
        name="wkv_scan",
    )(w2, w2, k2, k2, b2, b2, v, v, kk, kk, r, r, s0)


CHUNK = 64
CHUNKS_PER_STEP = 2
ATTN_ROW_GROUP = 128


def _split_bf16(x):
    hi = x.astype(BF16)
    return hi, (x - hi.astype(F32)).astype(BF16)


def _mm3(a, b, nt=False, passes=3):
    dn = (((1,), (1 if nt else 0,)), ((), ()))
    if passes == 1:
        return lax.dot_general(a.astype(BF16), b.astype(BF16), dn, preferred_element_type=F32)
    ah, al = _split_bf16(a)
    m = a.shape[0]
    if passes == 2:
        top = lax.dot_general(jnp.concatenate([ah, al], axis=0), b.astype(BF16), dn, preferred_element_type=F32)
        return top[:m] + top[m:]
    bh, bl = _split_bf16(b)
    top = lax.dot_general(jnp.concatenate([ah, al], axis=0), bh, dn, preferred_element_type=F32)
    return top[:m] + top[m:] + lax.dot_general(ah, bl, dn, preferred_element_type=F32)


def _chunk_terms(chains):
    L = CHUNK
    nch = range(len(chains))
    row = lax.broadcasted_iota(jnp.int32, (L, QUAD_LANES), 0)
    col = lax.broadcasted_iota(jnp.int32, (L, QUAD_LANES), 1) % HEAD_DIM
    tr = lax.broadcasted_iota(jnp.int32, (L, L), 0)
    tc = lax.broadcasted_iota(jnp.int32, (L, L), 1)
    masks = {True: (col > row, col >= row, (tc >= tr).astype(F32)),
             False: (col < row, col <= row, (tc <= tr).astype(F32))}
    eye_l = (col == row).astype(F32)
    ra = lax.broadcasted_iota(jnp.int32, (QUAD_LANES, QUAD_LANES), 0)
    rb = lax.broadcasted_iota(jnp.int32, (QUAD_LANES, QUAD_LANES), 1)
    bd = ra // HEAD_DIM == rb // HEAD_DIM
    eye_q = (ra == rb).astype(F32)

    def bdiag(y):
        return jnp.where(bd, jnp.concatenate([y] * (QUAD_LANES // L), axis=0), 0.0)

    def bdm(x, y, passes):
        ys = [bdiag(y[:, i:i + QUAD_LANES]) for i in range(0, y.shape[1], QUAD_LANES)]
        return _mm3(x, ys[0] if len(ys) == 1 else jnp.concatenate(ys, axis=1), passes=passes)

    lw, kd, bb, v, kk, r, back = (list(x) for x in zip(*chains))
    strict = [masks[b][0] for b in back]
    incl = [masks[b][1] for b in back]
    cs = [_mm3(masks[back[c]][2], lw[c]) for c in nch]
    g_last = [jnp.exp(cs[c][0:1] if back[c] else cs[c][L - 1:L]) for c in nch]
    at = [-kk[c] * jnp.exp(cs[c] - lw[c]) for c in nch]
    g_inv = [jnp.exp(-cs[c]) for c in nch]
    bt = [bb[c] * g_inv[c] for c in nch]
    kt = [kd[c] * g_inv[c] for c in nch]
    rt = [r[c] * jnp.exp(cs[c]) for c in nch]
    ar = [jnp.concatenate([at[c], rt[c]], axis=0) for c in nch]
    pb = [_mm3(ar[c], bdiag(bt[c]), nt=True) for c in nch]
    pk = [_mm3(ar[c], bdiag(kt[c]), nt=True) for c in nch]
    n = [jnp.where(strict[c], pb[c][:L], 0.0) for c in nch]
    m_rb = [jnp.where(incl[c], pb[c][L:], 0.0) for c in nch]
    m_ak = [jnp.where(strict[c], pk[c][:L], 0.0) for c in nch]
    m_rk = [jnp.where(incl[c], pk[c][L:], 0.0) for c in nch]
    steps = int(math.log2(L)) - 1
    p = [bdm(n[c], n[c], 3) for c in nch]
    t_inv = [eye_l + n[c] for c in nch]
    for k in range(steps):
        if k + 1 < steps:
            both = [bdm(jnp.concatenate([p[c], t_inv[c]], axis=0), p[c], 1) for c in nch]
            p = [both[c][:L] for c in nch]
            t_inv = [t_inv[c] + both[c][L:] for c in nch]
        else:
            t_inv = [t_inv[c] + bdm(t_inv[c], p[c], 1) for c in nch]
    xy = [bdm(jnp.concatenate([m_ak[c], m_rk[c]], axis=0), v[c], 1) for c in nch]
    w = [bdm(t_inv[c], jnp.concatenate([at[c], xy[c][:L]], axis=1), 2) for c in nch]
    qq = [bdm(m_rb[c], w[c], 1) for c in nch]
    q1 = [rt[c] + qq[c][:, :QUAD_LANES] for c in nch]
    q2 = [xy[c][L:] + qq[c][:, QUAD_LANES:] for c in nch]
    g = [jnp.where(bd, (eye_q + _mm3(w[c][:, :QUAD_LANES].T, bt[c], passes=2)) * g_last[c], 0.0) for c in nch]
    h = [jnp.where(bd, _mm3(jnp.concatenate([w[c][:, QUAD_LANES:], v[c]], axis=0).T,
                            jnp.concatenate([bt[c], kt[c]], axis=0), passes=2) * g_last[c], 0.0) for c in nch]
    return [(q1[c], q2[c], g[c].T, h[c].T) for c in nch]


def _wkv_chunk_kernel(lw_ref, k_ref, b_ref, v_ref, kk_ref, r_ref, q1_ref, q2_ref, gt_ref, ht_ref, *, nquads):
    keys = [(j, d, q) for j in range(CHUNKS_PER_STEP) for d in range(2) for q in range(nquads)]
    chains = []
    for j, d, q in keys:
        rows = slice(j * CHUNK, (j + 1) * CHUNK)
        sl = slice(q * QUAD_LANES, (q + 1) * QUAD_LANES)
        sd = slice(d * RW_DIM + q * QUAD_LANES, d * RW_DIM + (q + 1) * QUAD_LANES)
        chains.append((lw_ref[0, rows, sd], k_ref[0, rows, sd], b_ref[0, rows, sd],
                       v_ref[0, rows, sl], kk_ref[0, rows, sl], r_ref[0, rows, sl], d == 1))
    res = _chunk_terms(chains)
    for (j, d, q), (q1, q2, gt, ht) in zip(keys, res):
        rows = slice(j * CHUNK, (j + 1) * CHUNK)
        sl = slice(q * QUAD_LANES, (q + 1) * QUAD_LANES)
        q1_ref[0, d, rows, sl] = q1
        q2_ref[0, d, rows, sl] = q2
        gt_ref[0, d, j, q] = gt
        ht_ref[0, d, j, q] = ht


def wkv_chunk_prep(lw2, k2, b2, v, kk, r):
    B, T, C = v.shape
    nquads = C // QUAD_LANES
    nc = T // CHUNK
    cps = CHUNKS_PER_STEP
    tr = cps * CHUNK
    wide = pl.BlockSpec((1, tr, 2 * C), lambda b, c: (b, c, 0))
    narrow = pl.BlockSpec((1, tr, C), lambda b, c: (b, c, 0))
    qspec = pl.BlockSpec((1, 2, tr, C), lambda b, c: (b, 0, c, 0))
    gspec = pl.BlockSpec((1, 2, cps, nquads, QUAD_LANES, QUAD_LANES), lambda b, c: (b, 0, c, 0, 0, 0))
    qshape = jax.ShapeDtypeStruct((B, 2, T, C), F32)
    gshape = jax.ShapeDtypeStruct((B, 2, nc, nquads, QUAD_LANES, QUAD_LANES), F32)
    return pl.pallas_call(
        functools.partial(_wkv_chunk_kernel, nquads=nquads),
        out_shape=(qshape, qshape, gshape, gshape),
        grid=(B, nc // cps),
        in_specs=[wide, wide, wide, narrow, narrow, narrow],
        out_specs=(qspec, qspec, gspec, gspec),
        compiler_params=_params(2),
        name="wkv_chunk_prep",
    )(lw2, k2, b2, v, kk, r)


def _wkv_seq_kernel(q1f_ref, q1b_ref, q2f_ref, q2b_ref, gtf_ref, gtb_ref, htf_ref, htb_ref, s0_ref,
                    of_ref, ob_ref, st_ref, z_ref, *, nquads):
    c = pl.program_id(1)

    @pl.when(c == 0)
    def _():
        z_ref[...] = s0_ref[0]

    dirs = ((q1f_ref, q2f_ref, gtf_ref, htf_ref, of_ref), (q1b_ref, q2b_ref, gtb_ref, htb_ref, ob_ref))
    chains = [(d, q, slice(q * QUAD_LANES, (q + 1) * QUAD_LANES)) for d in range(2) for q in range(nquads)]
    lhs = [jnp.concatenate([dirs[d][0][0, 0, :, sl], dirs[d][2][0, 0, 0, q]], axis=0) for d, q, sl in chains]
    res = [_mm3(lhs[i], z_ref[d * nquads + q]) for i, (d, q, sl) in enumerate(chains)]
    for i, (d, q, sl) in enumerate(chains):
        dirs[d][4][0, :, sl] = res[i][:CHUNK] + dirs[d][1][0, 0, :, sl]
        z_ref[d * nquads + q] = res[i][CHUNK:] + dirs[d][3][0, 0, 0, q]

    @pl.when(c == pl.num_programs(1) - 1)
    def _():
        st_ref[0] = z_ref[...]


def wkv_seq(q1, q2, gt, ht, s0):
    B, _, T, C = q1.shape
    nc, nquads = gt.shape[2], gt.shape[3]
    qf = pl.BlockSpec((1, 1, CHUNK, C), lambda b, c: (b, 0, c, 0))
    qb = pl.BlockSpec((1, 1, CHUNK, C), lambda b, c: (b, 1, nc - 1 - c, 0))
    gshape = (1, 1, 1, nquads, QUAD_LANES, QUAD_LANES)
    gf = pl.BlockSpec(gshape, lambda b, c: (b, 0, c, 0, 0, 0))
    gb = pl.BlockSpec(gshape, lambda b, c: (b, 1, nc - 1 - c, 0, 0, 0))
    sspec = pl.BlockSpec((1,) + s0.shape[1:], lambda b, c: (b, 0, 0, 0))
    return pl.pallas_call(
        functools.partial(_wkv_seq_kernel, nquads=nquads),
        out_shape=(jax.ShapeDtypeStruct((B, T, C), F32), jax.ShapeDtypeStruct((B, T, C), F32),
                   jax.ShapeDtypeStruct(s0.shape, F32)),
        grid=(B, nc),
        in_specs=[qf, qb, qf, qb, gf, gb, gf, gb, sspec],
        out_specs=(pl.BlockSpec((1, CHUNK, C), lambda b, c: (b, c, 0)),
                   pl.BlockSpec((1, CHUNK, C), lambda b, c: (b, nc - 1 - c, 0)), sspec),
        scratch_shapes=[pltpu.VMEM(s0.shape[1:], F32)],
        compiler_params=_params(2),
        name="wkv_seq",
    )(q1, q1, q2, q2, gt, gt, ht, ht, s0)


def wkv_chunked(lw2, k2, b2, v, kk, r, s0):
    q1, q2, gt, ht = wkv_chunk_prep(lw2, k2, b2, v, kk, r)
    return wkv_seq(q1, q2, gt, ht, s0)


def _mixer_out_kernel(of_ref, ob_ref, r_ref, v_ref, k2_ref, xg_ref, oat_ref, x_ref, gm_ref,
                      gup_ref, rk_ref, lng_ref, lnb_ref, wo_ref, o_ref):
    ones = _ones_bd(RW_DIM)
    inv = 1.0 / HEAD_DIM
    o = of_ref[0] + ob_ref[0]
    mu = _seg_sum2(o, ones) * inv
    dv = o - mu
    var = _seg_sum2(dv * dv, ones) * inv
    gn = dv * lax.rsqrt(var + GN_EPS) * lng_ref[...] + lnb_ref[...]
    k2 = k2_ref[0]
    k_bonus = (k2[:, :RW_DIM] + k2[:, RW_DIM:]) * 0.5
    bonus = _seg_sum2(r_ref[0] * k_bonus * rk_ref[...], ones) * v_ref[0]
    g = jnp.dot(jax.nn.sigmoid(xg_ref[0]), gup_ref[...], preferred_element_type=F32, precision=HIGHEST)
    y = (gn + bonus) * g
    acc = jnp.dot(oat_ref[0].astype(BF16), wo_ref[:ATTN_DIM, :], preferred_element_type=F32)
    acc = acc + jnp.dot(y.astype(BF16), wo_ref[ATTN_DIM:, :], preferred_element_type=F32)
    o_ref[0] = x_ref[0] + gm_ref[0] * acc


def mixer_out(o_f, o_b, r, v, k2, p, o_at, x, g_m, g_up, r_k, ln_x_g, ln_x_b, w_out):
    B, T, D = x.shape
    tm = _row_tile(T, 256)
    nspec = pl.BlockSpec((1, tm, RW_DIM), lambda b, i: (b, i, 0))
    full = lambda a: pl.BlockSpec(a.shape, lambda b, i: (0,) * a.ndim)
    rk2, lng2, lnb2 = r_k.reshape(1, RW_DIM), ln_x_g.reshape(1, RW_DIM), ln_x_b.reshape(1, RW_DIM)
    return pl.pallas_call(
        _mixer_out_kernel,
        out_shape=jax.ShapeDtypeStruct((B, T, D), F32),
        grid=(B, T // tm),
        in_specs=[nspec, nspec, nspec, nspec,
                  pl.BlockSpec((1, tm, 2 * RW_DIM), lambda b, i: (b, i, 0)),
                  pl.BlockSpec((1, tm, G_LORA), lambda b, i: (b, i, COL_XG // G_LORA)),
                  pl.BlockSpec((1, tm, ATTN_DIM), lambda b, i: (b, i, 0)),
                  pl.BlockSpec((1, tm, D), lambda b, i: (b, i, 0)),
                  pl.BlockSpec((1, 1, D), lambda b, i: (b, 0, 0)),
                  full(g_up), full(rk2), full(lng2), full(lnb2), full(w_out)],
        out_specs=pl.BlockSpec((1, tm, D), lambda b, i: (b, i, 0)),
        compiler_params=_params(2),
        name="mixer_out",
    )(o_f, o_b, r, v, k2, p, o_at, x, g_m, g_up, rk2, lng2, lnb2, w_out)


def _start_row_copies(idx_ref, n, src_hbm, dst_ref, sem):
    def body(r, carry):
        pltpu.make_async_copy(src_hbm.at[pl.ds(idx_ref[0, 0, r], 1)], dst_ref.at[pl.ds(r, 1)], sem).start()
        return carry
    lax.fori_loop(0, n, body, 0, unroll=8)


def _wait_row_copies(n, src_hbm, dst_ref, sem):
    pltpu.make_async_copy(src_hbm.at[pl.ds(0, n)], dst_ref.at[pl.ds(0, n)], sem).wait()


def _moe_kernel(be_ref, tok_ref, tokn_ref, h_hbm, w1f_ref, b1_ref, w2f_ref, b2_ref, o_ref, buf, sems,
                w1_ref, w2_ref, *, f, tm):
    i = pl.program_id(0)
    slot = i % 2

    @pl.when(jnp.logical_or(i == 0, be_ref[i] != be_ref[jnp.maximum(i - 1, 0)]))
    def _():
        w1_ref[...] = w1f_ref[...].astype(BF16)
        w2_ref[...] = w2f_ref[...].astype(BF16)

    @pl.when(i == 0)
    def _():
        _start_row_copies(tok_ref, tm, h_hbm, buf.at[0], sems.at[0])

    @pl.when(i + 1 < pl.num_programs(0))
    def _():
        _start_row_copies(tokn_ref, tm, h_hbm, buf.at[1 - slot], sems.at[1 - slot])

    _wait_row_copies(tm, h_hbm, buf.at[slot], sems.at[slot])
    u = jnp.dot(buf[slot].astype(BF16), w1_ref[0], preferred_element_type=F32) + b1_ref[0]
    x_glu = jnp.minimum(u[:, :f], SWIGLU_LIMIT)
    x_lin = jnp.clip(u[:, f:], -SWIGLU_LIMIT, SWIGLU_LIMIT)
    act = x_glu * jax.nn.sigmoid(SWIGLU_ALPHA * x_glu) * (x_lin + 1.0)
    o_ref[...] = jnp.dot(act.astype(BF16), w2_ref[0], preferred_element_type=F32) + b2_ref[0]


def moe_blocks(block_e, slot_tok, h, w1, b1, w2, b2):
    N, D = h.shape
    E, _, F2 = w1.shape
    f = F2 // 2
    tm = MOE_BLOCK
    rows = slot_tok.shape[0]
    nb = rows // tm
    tok3 = slot_tok.reshape(nb, 1, tm)
    kern = functools.partial(_moe_kernel, f=f, tm=tm)
    grid_spec = pltpu.PrefetchScalarGridSpec(
        num_scalar_prefetch=1,
        grid=(nb,),
        in_specs=[pl.BlockSpec((1, 1, tm), lambda i, be: (i, 0, 0), memory_space=pltpu.SMEM),
                  pl.BlockSpec((1, 1, tm), lambda i, be: (jnp.minimum(i + 1, nb - 1), 0, 0),
                               memory_space=pltpu.SMEM),
                  pl.BlockSpec(memory_space=pl.ANY),
                  pl.BlockSpec((1, D, F2), lambda i, be: (be[i], 0, 0)),
                  pl.BlockSpec((1, 1, F2), lambda i, be: (be[i], 0, 0)),
                  pl.BlockSpec((1, f, D), lambda i, be: (be[i], 0, 0)),
                  pl.BlockSpec((1, 1, D), lambda i, be: (be[i], 0, 0))],
        out_specs=pl.BlockSpec((tm, D), lambda i, be: (i, 0)),
        scratch_shapes=[pltpu.VMEM((2, tm, D), F32), pltpu.SemaphoreType.DMA((2,)),
                        pltpu.VMEM((1, D, F2), BF16), pltpu.VMEM((1, f, D), BF16)],
    )
    return pl.pallas_call(
        kern,
        out_shape=jax.ShapeDtypeStruct((rows, D), F32),
        grid_spec=grid_spec,
        compiler_params=_params(1),
        name="moe_blocks",
    )(block_e, tok3, tok3, h, w1, b1.reshape(E, 1, F2), w2, b2.reshape(E, 1, D))


def _moe_combine_kernel(sl_ref, sln_ref, out_hbm, gate_ref, x_ref, gf_ref, o_ref, buf, sems, *, tb):
    b, i = pl.program_id(0), pl.program_id(1)
    step = b * pl.num_programs(1) + i
    nsteps = pl.num_programs(0) * pl.num_programs(1)
    slot = step % 2
    n = TOP_K * tb

    @pl.when(step == 0)
    def _():
        _start_row_copies(sl_ref, n, out_hbm, buf.at[0], sems.at[0])

    @pl.when(step + 1 < nsteps)
    def _():
        _start_row_copies(sln_ref, n, out_hbm, buf.at[1 - slot], sems.at[1 - slot])

    _wait_row_copies(n, out_hbm, buf.at[slot], sems.at[slot])
    gate = gate_ref[0]
    y = jnp.zeros(x_ref.shape[1:], F32)
    for k in range(TOP_K):
        y = y + gate[:, k:k + 1] * buf[slot, k * tb:(k + 1) * tb, :]
    o_ref[0] = x_ref[0] + gf_ref[0] * y


def moe_combine(out, slots, gates, x, g_f):
    B, T, D = x.shape
    tb = _row_tile(T, 64)
    nt = T // tb
    n = TOP_K * tb
    sl3 = jnp.transpose(slots.reshape(B * nt, tb, TOP_K), (0, 2, 1)).reshape(B * nt, 1, n)
    last = B * nt - 1
    return pl.pallas_call(
        functools.partial(_moe_combine_kernel, tb=tb),
        out_shape=jax.ShapeDtypeStruct((B, T, D), F32),
        grid=(B, nt),
        in_specs=[pl.BlockSpec((1, 1, n), lambda b, i: (b * nt + i, 0, 0), memory_space=pltpu.SMEM),
                  pl.BlockSpec((1, 1, n), lambda b, i: (jnp.minimum(b * nt + i + 1, last), 0, 0),
                               memory_space=pltpu.SMEM),
                  pl.BlockSpec(memory_space=pl.ANY),
                  pl.BlockSpec((1, tb, LANES), lambda b, i: (b, i, 0)),
                  pl.BlockSpec((1, tb, D), lambda b, i: (b, i, 0)),
                  pl.BlockSpec((1, 1, D), lambda b, i: (b, 0, 0))],
        out_specs=pl.BlockSpec((1, tb, D), lambda b, i: (b, i, 0)),
        scratch_shapes=[pltpu.VMEM((2, n, D), F32), pltpu.SemaphoreType.DMA((2,))],
        compiler_params=_params(2),
        name="moe_combine",
    )(sl3, sl3, out, gates, x, g_f)


def _ffn_pre_kernel(x_ref, g_ref, sc_ref, sh_ref, rw_ref, rb_ref, h_ref, te_ref, tg_ref):
    h = _norm_mod(x_ref[0], g_ref[...], sc_ref[0], sh_ref[0])
    h_ref[0] = h
    cur = jnp.dot(h, rw_ref[...], preferred_element_type=F32, precision=HIGHEST) + rb_ref[...]
    n_e = cur.shape[1]
    lane_e = lax.broadcasted_iota(jnp.int32, cur.shape, 1).astype(F32)
    lane = lax.broadcasted_iota(jnp.int32, te_ref.shape[1:], 1)
    vals, idxs = [], []
    for k in range(TOP_K):
        m = jnp.max(cur, axis=1, keepdims=True)
        idx = jnp.min(jnp.where(cur == m, lane_e, float(n_e)), axis=1, keepdims=True)
        vals.append(m)
        idxs.append(idx)
        cur = jnp.where(lane_e == idx, -jnp.inf, cur)
    ex = [jnp.exp(v - vals[0]) for v in vals]
    denom = ex[0]
    for e in ex[1:]:
        denom = denom + e
    te = jnp.zeros(lane.shape, F32)
    tg = jnp.zeros(lane.shape, F32)
    for k in range(TOP_K):
        te = jnp.where(lane == k, idxs[k], te)
        tg = jnp.where(lane == k, ex[k] / denom, tg)
    te_ref[0] = te.astype(jnp.int32)
    tg_ref[0] = tg


def ffn_pre(x, g, sc, sh, router_w, router_b):
    B, T, D = x.shape
    E = router_w.shape[1]
    tm = _row_tile(T, 256)
    return pl.pallas_call(
        _ffn_pre_kernel,
        out_shape=(jax.ShapeDtypeStruct((B, T, D), F32), jax.ShapeDtypeStruct((B, T, LANES), jnp.int32),
                   jax.ShapeDtypeStruct((B, T, LANES), F32)),
        grid=(B, T // tm),
        in_specs=[pl.BlockSpec((1, tm, D), lambda b, i: (b, i, 0)),
                  pl.BlockSpec((1, D), lambda b, i: (0, 0)),
                  pl.BlockSpec((1, 1, D), lambda b, i: (b, 0, 0)),
                  pl.BlockSpec((1, 1, D), lambda b, i: (b, 0, 0)),
                  pl.BlockSpec((D, E), lambda b, i: (0, 0)),
                  pl.BlockSpec((1, E), lambda b, i: (0, 0))],
        out_specs=(pl.BlockSpec((1, tm, D), lambda b, i: (b, i, 0)),
                   pl.BlockSpec((1, tm, LANES), lambda b, i: (b, i, 0)),
                   pl.BlockSpec((1, tm, LANES), lambda b, i: (b, i, 0))),
        compiler_params=_params(2),
        name="ffn_pre",
    )(x, g.reshape(1, D), sc, sh, router_w, router_b.reshape(1, E))


def _rms_kernel(x_ref, g_ref, o_ref):
    x = x_ref[0]
    ms = jnp.mean(x * x, axis=-1, keepdims=True)
    o_ref[0] = x * lax.rsqrt(ms + NORM_EPS) * g_ref[...]


def final_norm(x, g):
    B, T, D = x.shape
    tm = _row_tile(T, 512)
    return pl.pallas_call(
        _rms_kernel,
        out_shape=jax.ShapeDtypeStruct((B, T, D), F32),
        grid=(B, T // tm),
        in_specs=[pl.BlockSpec((1, tm, D), lambda b, i: (b, i, 0)),
                  pl.BlockSpec((1, D), lambda b, i: (0, 0))],
        out_specs=pl.BlockSpec((1, tm, D), lambda b, i: (b, i, 0)),
        compiler_params=_params(2),
        name="final_norm",
    )(x, g.reshape(1, D))


def _route(top_e):
    N = top_e.shape[0]
    n4 = N * TOP_K
    flat_e = top_e.reshape(-1)
    order = jnp.argsort(flat_e)
    e_sorted = flat_e[order]
    counts = jnp.bincount(flat_e, length=N_EXPERTS)
    padded = (counts + MOE_BLOCK - 1) // MOE_BLOCK * MOE_BLOCK
    start = jnp.cumsum(counts) - counts
    pstart = jnp.cumsum(padded) - padded
    dest = (pstart[e_sorted] + jnp.arange(n4, dtype=jnp.int32) - start[e_sorted]).astype(jnp.int32)
    n_blocks = (n4 + MOE_BLOCK - 1) // MOE_BLOCK + N_EXPERTS
    rows = n_blocks * MOE_BLOCK
    ends = jnp.cumsum(padded)
    block_start = jnp.arange(n_blocks, dtype=ends.dtype) * MOE_BLOCK
    block_e = jnp.minimum(jnp.sum(ends[None, :] <= block_start[:, None], axis=1), N_EXPERTS - 1).astype(jnp.int32)
    slot = jnp.arange(rows, dtype=jnp.int32)
    slot_e = jnp.repeat(block_e, MOE_BLOCK)
    off = slot - pstart[slot_e].astype(jnp.int32)
    src = jnp.minimum(start[slot_e].astype(jnp.int32) + off, n4 - 1)
    slot_tok = jnp.where(off < counts[slot_e], order[src] // TOP_K, 0).astype(jnp.int32)
    slot_flat = dest[jnp.argsort(order)]
    return slot_tok, block_e, slot_flat.reshape(N, TOP_K)


def kernel(x, c, ctx, c_ctx, norm_mix_g, norm_ffn_g, w_mod, b_mod, w_in, w_out, q_norm_g, k_norm_g,
           conv_w, w0, w_up, a0, a_up, g_up, k_k, k_a, r_k, ln_x_g, ln_x_b,
           router_w, router_b, e_w1, e_b1, e_w2, e_b2, norm_final_g):
    B, T, D = x.shape
    C = ctx.shape[1]
    depth = w_in.shape[0]
    cos, sin = _rope_tables(T)
    rpad = -(-(B + 1) // SUBLANES) * SUBLANES
    cvecs = jnp.zeros((rpad, D), F32).at[:B].set(c).at[B].set(c_ctx)
    s_zero = jnp.zeros((B, 2 * RW_DIM // QUAD_LANES, QUAD_LANES, QUAD_LANES), F32)
    x_lat, x_ctx = x, ctx
    for l in range(depth):
        last = l == depth - 1
        mods = adaln_all(cvecs, w_mod[l], b_mod[l])
        m_lat = [m[:, None, :] for m in jnp.split(mods[:B], 6, axis=-1)]
        m_ctx = [jnp.broadcast_to(m[None, :, :], (B, 1, D)) for m in jnp.split(mods[B:B + 1], 6, axis=-1)]
        sh_m, sc_m, g_m, sh_f, sc_f, g_f = m_lat
        csh_m, csc_m, cg_m, csh_f, csc_f, cg_f = m_ctx
        w_in_l = w_in[l].astype(BF16)
        w_out_l = w_out[l].astype(BF16)
        p_lat = norm_mod_matmul(x_lat, norm_mix_g[l], sc_m, sh_m, w_in_l)
        p_ctx = norm_mod_matmul(x_ctx, norm_mix_g[l], csc_m, csh_m,
                                w_in_l[:, :STATE_COLS] if last else w_in_l)

        ktc, vc = kv_prep(p_ctx, k_norm_g[l])
        ktl, vl = kv_prep(p_lat, k_norm_g[l], cos, sin)
        o_at = attention(p_lat, q_norm_g[l], ktc, vc, ktl, vl, cos, sin)

        rw = (conv_w[l], w0[l], w_up[l], a0[l], a_up[l], k_k[l], k_a[l])
        w2c, k2c, b2c, v_c, kk_c, r_c = wkv_prep(p_ctx, *rw, has_r=not last)
        w2l, k2l, b2l, v_l, kk_l, r_l = wkv_prep(p_lat, *rw, has_r=True)
        of_c, ob_c, s_c = wkv_chunked(w2c, k2c, b2c, v_c, kk_c, r_c, s_zero)
        of_l, ob_l, _ = wkv_chunked(w2l, k2l, b2l, v_l, kk_l, r_l, s_c)

        ro = (g_up[l], r_k[l], ln_x_g[l], ln_x_b[l], w_out_l)
        x_lat = mixer_out(of_l, ob_l, r_l, v_l, k2l, p_lat, o_at, x_lat, g_m, *ro)
        h_lat, te_lat, tg_lat = ffn_pre(x_lat, norm_ffn_g[l], sc_f, sh_f, router_w[l], router_b[l])
        e_w1_l = e_w1[l]
        e_w2_l = e_w2[l]
        if last:
            slot_tok, block_e, slot_flat = _route(te_lat.reshape(B * T, LANES)[:, :TOP_K])
            out = moe_blocks(block_e, slot_tok, h_lat.reshape(B * T, D), e_w1_l, e_b1[l], e_w2_l, e_b2[l])
            x_lat = moe_combine(out, slot_flat.reshape(B, T, TOP_K), tg_lat, x_lat, g_f)
        else:
            o_at_c = attention(p_ctx, q_norm_g[l], ktc, vc)
            x_ctx = mixer_out(of_c, ob_c, r_c, v_c, k2c, p_ctx, o_at_c, x_ctx, cg_m, *ro)
            h_ctx, te_ctx, tg_ctx = ffn_pre(x_ctx, norm_ffn_g[l], csc_f, csh_f, router_w[l], router_b[l])
            h_all = jnp.concatenate([h_lat.reshape(B * T, D), h_ctx.reshape(B * C, D)], axis=0)
            te_all = jnp.concatenate([te_lat.reshape(B * T, LANES), te_ctx.reshape(B * C, LANES)], axis=0)
            slot_tok, block_e, slot_flat = _route(te_all[:, :TOP_K])
            out = moe_blocks(block_e, slot_tok, h_all, e_w1_l, e_b1[l], e_w2_l, e_b2[l])
            x_lat = moe_combine(out, slot_flat[:B * T].reshape(B, T, TOP_K), tg_lat, x_lat, g_f)
            x_ctx = moe_combine(out, slot_flat[B * T:].reshape(B, C, TOP_K), tg_ctx, x_ctx, cg_f)
    return final_norm(x_lat, norm_final_g)
```

```python
import functools
import math

import numpy as np
import jax
import jax.numpy as jnp
from jax import lax
from jax.experimental import pallas as pl
from jax.experimental.pallas import tpu as pltpu

F32 = jnp.float32
BF16 = jnp.bfloat16
HIGHEST = lax.Precision.HIGHEST

GRID_W = 64
HEAD_DIM = 64
ATTN_HEADS = 8
ATTN_KV_HEADS = 2
GROUP = ATTN_HEADS // ATTN_KV_HEADS
ATTN_DIM = ATTN_HEADS * HEAD_DIM
KV_DIM = ATTN_KV_HEADS * HEAD_DIM
RW_HEADS = 8
RW_DIM = RW_HEADS * HEAD_DIM
W_LORA = 64
A_LORA = 64
G_LORA = 128
CONV_W = 3
N_EXPERTS = 32
TOP_K = 4
SWIGLU_LIMIT = 7.0
SWIGLU_ALPHA = 1.702
ROPE_THETA = 10000.0
ROPE_PAIRS = HEAD_DIM // 4
MOE_BLOCK = 256
NORM_EPS = 1e-6
GN_EPS = 64e-5
ATTN_SCALE = HEAD_DIM ** -0.5
EXP_M05 = math.exp(-0.5)
COL_K_AT = 0
COL_V_AT = KV_DIM
COL_K_RW = 2 * KV_DIM
COL_V_RW = COL_K_RW + RW_DIM
COL_XW = COL_V_RW + RW_DIM
COL_XA = COL_XW + 2 * W_LORA
STATE_COLS = COL_XA + 2 * A_LORA
COL_Q_AT = STATE_COLS
COL_R_RW = COL_Q_AT + ATTN_DIM
COL_XG = COL_R_RW + RW_DIM

LANES = 128
SUBLANES = 8
QUAD_LANES = 256
VMEM_LIMIT = 48 * 1024 * 1024


def _row_tile(n, pref):
    t = pref
    while n % t:
        t //= 2
    return t


def _params(n_axes):
    return pltpu.CompilerParams(dimension_semantics=("arbitrary",) * n_axes, vmem_limit_bytes=VMEM_LIMIT)


def _ones_bd(n):
    ra = lax.broadcasted_iota(jnp.int32, (n, n), 0) // HEAD_DIM
    rb = lax.broadcasted_iota(jnp.int32, (n, n), 1) // HEAD_DIM
    return (ra == rb).astype(BF16)


def _seg_sum(x_bf16, ones_bd):
    return jnp.dot(x_bf16, ones_bd, preferred_element_type=F32)


def _seg_sum2(x, ones_bd):
    hi = x.astype(BF16)
    lo = (x - hi.astype(F32)).astype(BF16)
    return _seg_sum(hi, ones_bd) + _seg_sum(lo, ones_bd)


def _adaln_kernel(c_ref, w_ref, b_ref, o_ref):
    c = c_ref[...]
    s = c * jax.nn.sigmoid(c)
    o_ref[...] = jnp.dot(s, w_ref[...], preferred_element_type=F32, precision=HIGHEST) + b_ref[...]


def adaln_all(cvecs, w_mod, b_mod):
    R, D = cvecs.shape
    N = w_mod.shape[1]
    tn = 512
    return pl.pallas_call(
        _adaln_kernel,
        out_shape=jax.ShapeDtypeStruct((R, N), F32),
        grid=(N // tn,),
        in_specs=[pl.BlockSpec((R, D), lambda j: (0, 0)),
                  pl.BlockSpec((D, tn), lambda j: (0, j)),
                  pl.BlockSpec((1, tn), lambda j: (0, j))],
        out_specs=pl.BlockSpec((R, tn), lambda j: (0, j)),
        compiler_params=_params(1),
        name="adaln",
    )(cvecs, w_mod, b_mod.reshape(1, N))


def _norm_mod(x, g, sc, sh):
    ms = jnp.mean(x * x, axis=-1, keepdims=True)
    y = x * lax.rsqrt(ms + NORM_EPS) * g
    return y * (1.0 + sc) + sh


def _norm_mod_matmul_kernel(x_ref, g_ref, sc_ref, sh_ref, w_ref, o_ref):
    h = _norm_mod(x_ref[0], g_ref[...], sc_ref[0], sh_ref[0])
    o_ref[0] = jnp.dot(h.astype(BF16), w_ref[...], preferred_element_type=F32)


def norm_mod_matmul(x, g, sc, sh, w):
    B, T, D = x.shape
    N = w.shape[1]
    tm = _row_tile(T, 256)
    return pl.pallas_call(
        _norm_mod_matmul_kernel,
        out_shape=jax.ShapeDtypeStruct((B, T, N), F32),
        grid=(B, T // tm),
        in_specs=[pl.BlockSpec((1, tm, D), lambda b, i: (b, i, 0)),
                  pl.BlockSpec((1, D), lambda b, i: (0, 0)),
                  pl.BlockSpec((1, 1, D), lambda b, i: (b, 0, 0)),
                  pl.BlockSpec((1, 1, D), lambda b, i: (b, 0, 0)),
                  pl.BlockSpec((D, N), lambda b, i: (0, 0))],
        out_specs=pl.BlockSpec((1, tm, N), lambda b, i: (b, i, 0)),
        compiler_params=_params(2),
        name="norm_mod_matmul",
    )(x, g.reshape(1, D), sc, sh, w)


def _pair_rms(x, g):
    lane = lax.broadcasted_iota(jnp.int32, x.shape, 1)
    first = lane < HEAD_DIM
    xx = x * x
    s0 = jnp.sum(jnp.where(first, xx, 0.0), axis=1, keepdims=True)
    s1 = jnp.sum(jnp.where(first, 0.0, xx), axis=1, keepdims=True)
    ms = jnp.where(first, s0, s1) * (1.0 / HEAD_DIM)
    return x * lax.rsqrt(ms + NORM_EPS) * g


def _pair_rope(x, cos, sin):
    lane = lax.broadcasted_iota(jnp.int32, x.shape, 1)
    partner = jnp.where(lane % (2 * ROPE_PAIRS) < ROPE_PAIRS,
                        pltpu.roll(x, LANES - ROPE_PAIRS, 1), pltpu.roll(x, ROPE_PAIRS, 1))
    return x * cos + partner * sin


def _rope_tables(T):
    rows = T // GRID_W
    row = jnp.repeat(jnp.arange(rows), GRID_W)
    col = jnp.tile(jnp.arange(GRID_W), rows)
    pos = jnp.stack([row, col], axis=-1).astype(F32)
    freqs = ROPE_THETA ** (-jnp.arange(ROPE_PAIRS, dtype=F32) / ROPE_PAIRS)
    ang = pos[:, :, None] * freqs
    cos, sin = jnp.cos(ang), jnp.sin(ang)
    cos_h = jnp.concatenate([cos[:, 0], cos[:, 0], cos[:, 1], cos[:, 1]], axis=-1)
    sin_h = jnp.concatenate([-sin[:, 0], sin[:, 0], -sin[:, 1], sin[:, 1]], axis=-1)
    return jnp.tile(cos_h, (1, 2)), jnp.tile(sin_h, (1, 2))


def _kv_prep_kernel(*refs, rope):
    if rope:
        p_ref, g_ref, cos_ref, sin_ref, kt_ref, v_ref = refs
    else:
        p_ref, g_ref, kt_ref, v_ref = refs
    x = p_ref[0]
    k = _pair_rms(x[:, :KV_DIM], g_ref[...])
    if rope:
        k = _pair_rope(k, cos_ref[...], sin_ref[...])
    v = x[:, KV_DIM:]
    lane = lax.broadcasted_iota(jnp.int32, k.shape, 1)
    first = lane < HEAD_DIM
    kr = pltpu.roll(k, HEAD_DIM, 1)
    vr = pltpu.roll(v, HEAD_DIM, 1)
    kt_ref[0, 0] = jnp.where(first, k, kr).T.astype(BF16)
    kt_ref[0, 1] = jnp.where(first, kr, k).T.astype(BF16)
    ones = jnp.ones(v.shape, F32)
    v_ref[0, 0] = jnp.concatenate([jnp.where(first, v, vr), ones], axis=1).astype(BF16)
    v_ref[0, 1] = jnp.concatenate([jnp.where(first, vr, v), ones], axis=1).astype(BF16)


def kv_prep(p, k_norm_g, cos=None, sin=None):
    B, T, _ = p.shape
    tm = _row_tile(T, 256)
    rope = cos is not None
    g2 = jnp.tile(k_norm_g, 2).reshape(1, LANES)
    in_specs = [pl.BlockSpec((1, tm, 2 * KV_DIM), lambda b, i: (b, i, 0)),
                pl.BlockSpec((1, LANES), lambda b, i: (0, 0))]
    args = [p, g2]
    if rope:
        in_specs += [pl.BlockSpec((tm, LANES), lambda b, i: (i, 0))] * 2
        args += [cos, sin]
    return pl.pallas_call(
        functools.partial(_kv_prep_kernel, rope=rope),
        out_shape=(jax.ShapeDtypeStruct((B, ATTN_KV_HEADS, LANES, T), BF16),
                   jax.ShapeDtypeStruct((B, ATTN_KV_HEADS, T, 2 * LANES), BF16)),
        grid=(B, T // tm),
        in_specs=in_specs,
        out_specs=(pl.BlockSpec((1, ATTN_KV_HEADS, LANES, tm), lambda b, i: (b, 0, 0, i)),
                   pl.BlockSpec((1, ATTN_KV_HEADS, tm, 2 * LANES), lambda b, i: (b, 0, i, 0))),
        compiler_params=_params(2),
        name="kv_prep",
    )(*args)


def _attn_kernel(*refs, tq, tkl, nkl, rope, has_lat, rg):
    refs = list(refs)
    p_ref, g_ref = refs[:2]
    refs = refs[2:]
    if rope:
        cos_ref, sin_ref = refs[:2]
        refs = refs[2:]
    ktc_ref, vc_ref = refs[:2]
    refs = refs[2:]
    if has_lat:
        ktl_ref, vl_ref = refs[:2]
        refs = refs[2:]
    o_ref, m_ref, acc_ref = refs

    q = _pair_rms(p_ref[0], g_ref[...])
    if rope:
        q = _pair_rope(q, cos_ref[...], sin_ref[...])
    q = q * ATTN_SCALE
    lane = lax.broadcasted_iota(jnp.int32, (tq, LANES), 1)
    first = lane < HEAD_DIM
    qq = jnp.concatenate([jnp.where(first, q, 0.0), jnp.where(first, 0.0, q)], axis=0).astype(BF16)

    m_ref[...] = jnp.full(m_ref.shape, -jnp.inf, F32)
    acc_ref[...] = jnp.zeros(acc_ref.shape, F32)
    groups = [slice(g * rg, (g + 1) * rg) for g in range(2 * tq // rg)]

    def chunk(kt, vv):
        s = [jnp.dot(qq[g], kt, preferred_element_type=F32) for g in groups]
        m_old = [m_ref[g] for g in groups]
        m_new = [jnp.maximum(mo, jnp.max(x, axis=-1, keepdims=True)) for mo, x in zip(m_old, s)]
        alpha = [jnp.exp(mo - mn) for mo, mn in zip(m_old, m_new)]
        p = [jnp.exp(x - jnp.concatenate([mn] * (x.shape[1] // LANES), axis=1)).astype(BF16)
             for x, mn in zip(s, m_new)]
        pv = [jnp.dot(x, vv, preferred_element_type=F32) for x in p]
        for g, al, mn, x in zip(groups, alpha, m_new, pv):
            acc_ref[g] = jnp.concatenate([al, al], axis=1) * acc_ref[g] + x
            m_ref[g] = mn

    chunk(ktc_ref[0, 0], vc_ref[0, 0])
    if has_lat:
        def body(j, carry):
            ks = pl.multiple_of(j * tkl, tkl)
            chunk(ktl_ref[0, 0, :, pl.ds(ks, tkl)], vl_ref[0, 0, pl.ds(ks, tkl), :])
            return carry
        lax.fori_loop(0, nkl, body, 0)

    acc = acc_ref[...]
    o = acc[:, :LANES] / acc[:, LANES:]
    o_ref[0] = jnp.where(first, o[:tq], o[tq:])


def attention(p, q_norm_g, ktc, vc, ktl=None, vl=None, cos=None, sin=None):
    B, T, _ = p.shape
    C = ktc.shape[-1]
    rope = cos is not None
    has_lat = ktl is not None
    tq = _row_tile(T, 256)
    pairs = ATTN_HEADS // 2
    pairs_per_kv = GROUP // 2
    qblk0 = COL_Q_AT // LANES
    g2 = jnp.tile(q_norm_g, 2).reshape(1, LANES)
    in_specs = [pl.BlockSpec((1, tq, LANES), lambda b, h, i: (b, i, qblk0 + h)),
                pl.BlockSpec((1, LANES), lambda b, h, i: (0, 0))]
    args = [p, g2]
    if rope:
        in_specs += [pl.BlockSpec((tq, LANES), lambda b, h, i: (i, 0))] * 2
        args += [cos, sin]
    in_specs += [pl.BlockSpec((1, 1, LANES, C), lambda b, h, i: (b, h // pairs_per_kv, 0, 0)),
                 pl.BlockSpec((1, 1, C, 2 * LANES), lambda b, h, i: (b, h // pairs_per_kv, 0, 0))]
    args += [ktc, vc]
    tkl = nkl = 0
    if has_lat:
        S = ktl.shape[-1]
        tkl = _row_tile(S, 2048)
        nkl = S // tkl
        in_specs += [pl.BlockSpec((1, 1, LANES, S), lambda b, h, i: (b, h // pairs_per_kv, 0, 0)),
                     pl.BlockSpec((1, 1, S, 2 * LANES), lambda b, h, i: (b, h // pairs_per_kv, 0, 0))]
        args += [ktl, vl]
    kern = functools.partial(_attn_kernel, tq=tq, tkl=tkl, nkl=nkl, rope=rope, has_lat=has_lat,
                             rg=_row_tile(tq, ATTN_ROW_GROUP))
    return pl.pallas_call(
        kern,
        out_shape=jax.ShapeDtypeStruct((B, T, ATTN_DIM), F32),
        grid=(B, pairs, T // tq),
        in_specs=in_specs,
        out_specs=pl.BlockSpec((1, tq, LANES), lambda b, h, i: (b, i, h)),
        scratch_shapes=[pltpu.VMEM((2 * tq, LANES), F32), pltpu.VMEM((2 * tq, 2 * LANES), F32)],
        compiler_params=_params(3),
        name="attention",
    )(*args)


def _wkv_prep_kernel(p_ref, pp_ref, pn_ref, cw_ref, w0_ref, wup_ref, a0_ref, aup_ref, kk_ref, ka_ref,
                     w2_ref, k2_ref, b2_ref, v_ref, kkn_ref, r_ref, *, tm, has_r):
    i = pl.program_id(1)
    n = pl.num_programs(1)
    x = p_ref[0]
    prev = pp_ref[0][SUBLANES - 1:SUBLANES, :] * (i > 0).astype(F32)
    nxt = pn_ref[0][0:1, :] * (i < n - 1).astype(F32)
    row = lax.broadcasted_iota(jnp.int32, (tm, RW_DIM), 0)

    def conv(col, wcol):
        u = x[:, col:col + RW_DIM]
        up = jnp.where(row == 0, prev[:, col:col + RW_DIM], pltpu.roll(u, 1, 0))
        un = jnp.where(row == tm - 1, nxt[:, col:col + RW_DIM], pltpu.roll(u, tm - 1, 0))
        w = cw_ref[:, wcol:wcol + RW_DIM]
        return up * w[0:1] + u * w[1:2] + un * w[2:3]

    k = conv(COL_K_RW, RW_DIM)
    v = conv(COL_V_RW, 2 * RW_DIM)
    v_ref[0] = v
    r_ref[0] = conv(COL_R_RW, 0) if has_r else jnp.zeros((tm, RW_DIM), F32)
    kk = k * kk_ref[...]
    ss = _seg_sum2(kk * kk, _ones_bd(RW_DIM))
    kk = kk * lax.rsqrt(jnp.maximum(ss, 1e-24))
    kkn_ref[0] = kk
    xw = jnp.tanh(x[:, COL_XW:COL_XW + 2 * W_LORA])
    xa = x[:, COL_XA:COL_XA + 2 * A_LORA]
    for d in range(2):
        wl = w0_ref[d:d + 1, :] + jnp.dot(xw[:, d * W_LORA:(d + 1) * W_LORA], wup_ref[d],
                                          preferred_element_type=F32, precision=HIGHEST)
        decay = -EXP_M05 * jax.nn.sigmoid(wl)
        a = jax.nn.sigmoid(a0_ref[d:d + 1, :] + jnp.dot(xa[:, d * A_LORA:(d + 1) * A_LORA], aup_ref[d],
                                                        preferred_element_type=F32, precision=HIGHEST))
        sl = slice(d * RW_DIM, (d + 1) * RW_DIM)
        w2_ref[0, :, sl] = decay
        k2_ref[0, :, sl] = k * (1.0 + (a - 1.0) * ka_ref[...])
        b2_ref[0, :, sl] = kk * a


def wkv_prep(p, conv_w, w0, w_up, a0, a_up, k_k, k_a, has_r):
    B, T, NP = p.shape
    tm = _row_tile(T, 256)
    nb8 = tm // SUBLANES
    last8 = T // SUBLANES - 1
    full = lambda a: pl.BlockSpec(a.shape, lambda b, i: (0,) * a.ndim)
    kk2, ka2 = k_k.reshape(1, RW_DIM), k_a.reshape(1, RW_DIM)
    wide = jax.ShapeDtypeStruct((B, T, 2 * RW_DIM), F32)
    narrow = jax.ShapeDtypeStruct((B, T, RW_DIM), F32)
    wspec = pl.BlockSpec((1, tm, 2 * RW_DIM), lambda b, i: (b, i, 0))
    nspec = pl.BlockSpec((1, tm, RW_DIM), lambda b, i: (b, i, 0))
    return pl.pallas_call(
        functools.partial(_wkv_prep_kernel, tm=tm, has_r=has_r),
        out_shape=(wide, wide, wide, narrow, narrow, narrow),
        grid=(B, T // tm),
        in_specs=[pl.BlockSpec((1, tm, NP), lambda b, i: (b, i, 0)),
                  pl.BlockSpec((1, SUBLANES, NP), lambda b, i: (b, jnp.maximum(i * nb8 - 1, 0), 0)),
                  pl.BlockSpec((1, SUBLANES, NP), lambda b, i: (b, jnp.minimum((i + 1) * nb8, last8), 0)),
                  full(conv_w), full(w0), full(w_up), full(a0), full(a_up), full(kk2), full(ka2)],
        out_specs=(wspec, wspec, wspec, nspec, nspec, nspec),
        compiler_params=_params(2),
        name="wkv_prep",
    )(p, p, p, conv_w, w0, w_up, a0, a_up, kk2, ka2)


def _wkv_kernel(wf_ref, wb_ref, kf_ref, kb_ref, bf_ref, bb_ref, vf_ref, vb_ref, af_ref, ab_ref,
                rf_ref, rb_ref, s0_ref, of_ref, ob_ref, st_ref, s_ref, *, tc, nquads):
    tb = pl.program_id(1)

    @pl.when(tb == 0)
    def _():
        s_ref[...] = s0_ref[0]

    rows_all = 2 * nquads * HEAD_DIM
    shp = (rows_all, QUAD_LANES)
    lane = lax.broadcasted_iota(jnp.int32, shp, 1)
    sub = lax.broadcasted_iota(jnp.int32, shp, 0)
    eye = (lane % HEAD_DIM == sub % HEAD_DIM).astype(F32)
    ones_bd = _ones_bd(QUAD_LANES)
    ng = tc // SUBLANES

    def bcast_rows(blks, j):
        parts = []
        for d in range(2):
            jj = j if d == 0 else SUBLANES - 1 - j
            for q in range(nquads):
                row = blks[d][jj:jj + 1, q * QUAD_LANES:(q + 1) * QUAD_LANES]
                parts.append(jnp.broadcast_to(row, (HEAD_DIM, QUAD_LANES)))
        return jnp.concatenate(parts, axis=0)

    def group(g, carry):
        tf = pl.multiple_of(g * SUBLANES, SUBLANES)
        tr = pl.multiple_of((ng - 1 - g) * SUBLANES, SUBLANES)
        ld = lambda fr, br: [fr[0, pl.ds(tf, SUBLANES), :], br[0, pl.ds(tr, SUBLANES), :]]
        wq, kq, bq, vq, rq = (ld(f, b) for f, b in ((wf_ref, wb_ref), (kf_ref, kb_ref), (bf_ref, bb_ref),
                                                    (vf_ref, vb_ref), (rf_ref, rb_ref)))
        aq = [-x for x in ld(af_ref, ab_ref)]
        S = s_ref[...]
        out_rows = [[] for _ in range(2 * nquads)]
        for j in range(SUBLANES):
            wt, kt, vt, at, bt, rt = (bcast_rows(x, j) for x in (wq, kq, vq, aq, bq, rq))
            sa = _seg_sum2(S * at, ones_bd)
            vcol = _seg_sum((eye * vt).astype(BF16), ones_bd)
            S = S * wt + sa * bt + vcol * kt
            ob = _seg_sum((S * rt).astype(BF16), ones_bd) * eye
            for c in range(2 * nquads):
                out_rows[c].append(jnp.sum(ob[c * HEAD_DIM:(c + 1) * HEAD_DIM], axis=0, keepdims=True))
        s_ref[...] = S
        for q in range(nquads):
            sl = slice(q * QUAD_LANES, (q + 1) * QUAD_LANES)
            of_ref[0, pl.ds(tf, SUBLANES), sl] = jnp.concatenate(out_rows[q], axis=0)
            ob_ref[0, pl.ds(tr, SUBLANES), sl] = jnp.concatenate(out_rows[nquads + q][::-1], axis=0)
        return carry

    lax.fori_loop(0, ng, group, 0)

    @pl.when(tb == pl.num_programs(1) - 1)
    def _():
        st_ref[0] = s_ref[...]


def wkv_scan(w2, k2, b2, v, kk, r, s0):
    B, T, C = v.shape
    nquads = C // QUAD_LANES
    tc = _row_tile(T, 128)
    nt = T // tc
    fwd = lambda col: pl.BlockSpec((1, tc, C), lambda b, t: (b, t, col))
    bwd = lambda col: pl.BlockSpec((1, tc, C), lambda b, t: (b, nt - 1 - t, col))
    sspec = pl.BlockSpec((1,) + s0.shape[1:], lambda b, t: (b, 0, 0))
    kern = functools.partial(_wkv_kernel, tc=tc, nquads=nquads)
    return pl.pallas_call(
        kern,
        out_shape=(jax.ShapeDtypeStruct((B, T, C), F32), jax.ShapeDtypeStruct((B, T, C), F32),
                   jax.ShapeDtypeStruct(s0.shape, F32)),
        grid=(B, nt),
        in_specs=[fwd(0), bwd(1), fwd(0), bwd(1), fwd(0), bwd(1), fwd(0), bwd(0), fwd(0), bwd(0),
                  fwd(0), bwd(0), sspec],
        out_specs=(fwd(0), bwd(0), sspec),
        scratch_shapes=[pltpu.VMEM(s0.shape[1:], F32)],
        compiler_params=_params(2),
        name="wkv_scan",
    )(w2, w2, k2, k2, b2, b2, v, v, kk, kk, r, r, s0)


CHUNK = 64
CHUNKS_PER_STEP = 2
ATTN_ROW_GROUP = 128


def _split_bf16(x):
    hi = x.astype(BF16)
    return hi, (x - hi.astype(F32)).astype(BF16)


def _mm3(a, b, nt=False, passes=3):
    dn = (((1,), (1 if nt else 0,)), ((), ()))
    if passes == 1:
        return lax.dot_general(a.astype(BF16), b.astype(BF16), dn, preferred_element_type=F32)
    ah, al = _split_bf16(a)
    m = a.shape[0]
    if passes == 2:
        top = lax.dot_general(jnp.concatenate([ah, al], axis=0), b.astype(BF16), dn, preferred_element_type=F32)
        return top[:m] + top[m:]
    bh, bl = _split_bf16(b)
    top = lax.dot_general(jnp.concatenate([ah, al], axis=0), bh, dn, preferred_element_type=F32)
    return top[:m] + top[m:] + lax.dot_general(ah, bl, dn, preferred_element_type=F32)


def _chunk_terms(chains):
    L = CHUNK
    nch = range(len(chains))
    row = lax.broadcasted_iota(jnp.int32, (L, QUAD_LANES), 0)
    col = lax.broadcasted_iota(jnp.int32, (L, QUAD_LANES), 1) % HEAD_DIM
    tr = lax.broadcasted_iota(jnp.int32, (L, L), 0)
    tc = lax.broadcasted_iota(jnp.int32, (L, L), 1)
    masks = {True: (col > row, col >= row, (tc >= tr).astype(F32)),
             False: (col < row, col <= row, (tc <= tr).astype(F32))}
    eye_l = (col == row).astype(F32)
    ra = lax.broadcasted_iota(jnp.int32, (QUAD_LANES, QUAD_LANES), 0)
    rb = lax.broadcasted_iota(jnp.int32, (QUAD_LANES, QUAD_LANES), 1)
    bd = ra // HEAD_DIM == rb // HEAD_DIM
    eye_q = (ra == rb).astype(F32)

    def bdiag(y):
        return jnp.where(bd, jnp.concatenate([y] * (QUAD_LANES // L), axis=0), 0.0)

    def bdm(x, y, passes):
        ys = [bdiag(y[:, i:i + QUAD_LANES]) for i in range(0, y.shape[1], QUAD_LANES)]
        return _mm3(x, ys[0] if len(ys) == 1 else jnp.concatenate(ys, axis=1), passes=passes)

    lw, kd, bb, v, kk, r, back = (list(x) for x in zip(*chains))
    strict = [masks[b][0] for b in back]
    incl = [masks[b][1] for b in back]
    cs = [_mm3(masks[back[c]][2], lw[c]) for c in nch]
    g_last = [jnp.exp(cs[c][0:1] if back[c] else cs[c][L - 1:L]) for c in nch]
    at = [-kk[c] * jnp.exp(cs[c] - lw[c]) for c in nch]
    g_inv = [jnp.exp(-cs[c]) for c in nch]
    bt = [bb[c] * g_inv[c] for c in nch]
    kt = [kd[c] * g_inv[c] for c in nch]
    rt = [r[c] * jnp.exp(cs[c]) for c in nch]
    ar = [jnp.concatenate([at[c], rt[c]], axis=0) for c in nch]
    pb = [_mm3(ar[c], bdiag(bt[c]), nt=True) for c in nch]
    pk = [_mm3(ar[c], bdiag(kt[c]), nt=True) for c in nch]
    n = [jnp.where(strict[c], pb[c][:L], 0.0) for c in nch]
    m_rb = [jnp.where(incl[c], pb[c][L:], 0.0) for c in nch]
    m_ak = [jnp.where(strict[c], pk[c][:L], 0.0) for c in nch]
    m_rk = [jnp.where(incl[c], pk[c][L:], 0.0) for c in nch]
    steps = int(math.log2(L)) - 1
    p = [bdm(n[c], n[c], 3) for c in nch]
    t_inv = [eye_l + n[c] for c in nch]
    for k in range(steps):
        if k + 1 < steps:
            both = [bdm(jnp.concatenate([p[c], t_inv[c]], axis=0), p[c], 1) for c in nch]
            p = [both[c][:L] for c in nch]
            t_inv = [t_inv[c] + both[c][L:] for c in nch]
        else:
            t_inv = [t_inv[c] + bdm(t_inv[c], p[c], 1) for c in nch]
    xy = [bdm(jnp.concatenate([m_ak[c], m_rk[c]], axis=0), v[c], 1) for c in nch]
    w = [bdm(t_inv[c], jnp.concatenate([at[c], xy[c][:L]], axis=1), 2) for c in nch]
    qq = [bdm(m_rb[c], w[c], 1) for c in nch]
    q1 = [rt[c] + qq[c][:, :QUAD_LANES] for c in nch]
    q2 = [xy[c][L:] + qq[c][:, QUAD_LANES:] for c in nch]
    g = [jnp.where(bd, (eye_q + _mm3(w[c][:, :QUAD_LANES].T, bt[c], passes=2)) * g_last[c], 0.0) for c in nch]
    h = [jnp.where(bd, _mm3(jnp.concatenate([w[c][:, QUAD_LANES:], v[c]], axis=0).T,
                            jnp.concatenate([bt[c], kt[c]], axis=0), passes=2) * g_last[c], 0.0) for c in nch]
    return [(q1[c], q2[c], g[c].T, h[c].T) for c in nch]


def _wkv_chunk_kernel(lw_ref, k_ref, b_ref, v_ref, kk_ref, r_ref, q1_ref, q2_ref, gt_ref, ht_ref, *, nquads):
    keys = [(j, d, q) for j in range(CHUNKS_PER_STEP) for d in range(2) for q in range(nquads)]
    chains = []
    for j, d, q in keys:
        rows = slice(j * CHUNK, (j + 1) * CHUNK)
        sl = slice(q * QUAD_LANES, (q + 1) * QUAD_LANES)
        sd = slice(d * RW_DIM + q * QUAD_LANES, d * RW_DIM + (q + 1) * QUAD_LANES)
        chains.append((lw_ref[0, rows, sd], k_ref[0, rows, sd], b_ref[0, rows, sd],
                       v_ref[0, rows, sl], kk_ref[0, rows, sl], r_ref[0, rows, sl], d == 1))
    res = _chunk_terms(chains)
    for (j, d, q), (q1, q2, gt, ht) in zip(keys, res):
        rows = slice(j * CHUNK, (j + 1) * CHUNK)
        sl = slice(q * QUAD_LANES, (q + 1) * QUAD_LANES)
        q1_ref[0, d, rows, sl] = q1
        q2_ref[0, d, rows, sl] = q2
        gt_ref[0, d, j, q] = gt
        ht_ref[0, d, j, q] = ht


def wkv_chunk_prep(lw2, k2, b2, v, kk, r):
    B, T, C = v.shape
    nquads = C // QUAD_LANES
    nc = T // CHUNK
    cps = CHUNKS_PER_STEP
    tr = cps * CHUNK
    wide = pl.BlockSpec((1, tr, 2 * C), lambda b, c: (b, c, 0))
    narrow = pl.BlockSpec((1, tr, C), lambda b, c: (b, c, 0))
    qspec = pl.BlockSpec((1, 2, tr, C), lambda b, c: (b, 0, c, 0))
    gspec = pl.BlockSpec((1, 2, cps, nquads, QUAD_LANES, QUAD_LANES), lambda b, c: (b, 0, c, 0, 0, 0))
    qshape = jax.ShapeDtypeStruct((B, 2, T, C), F32)
    gshape = jax.ShapeDtypeStruct((B, 2, nc, nquads, QUAD_LANES, QUAD_LANES), F32)
    return pl.pallas_call(
        functools.partial(_wkv_chunk_kernel, nquads=nquads),
        out_shape=(qshape, qshape, gshape, gshape),
        grid=(B, nc // cps),
        in_specs=[wide, wide, wide, narrow, narrow, narrow],
        out_specs=(qspec, qspec, gspec, gspec),
        compiler_params=_params(2),
        name="wkv_chunk_prep",
    )(lw2, k2, b2, v, kk, r)


def _wkv_seq_kernel(q1f_ref, q1b_ref, q2f_ref, q2b_ref, gtf_ref, gtb_ref, htf_ref, htb_ref, s0_ref,
                    of_ref, ob_ref, st_ref, z_ref, *, nquads):
    c = pl.program_id(1)

    @pl.when(c == 0)
    def _():
        z_ref[...] = s0_ref[0]

    dirs = ((q1f_ref, q2f_ref, gtf_ref, htf_ref, of_ref), (q1b_ref, q2b_ref, gtb_ref, htb_ref, ob_ref))
    chains = [(d, q, slice(q * QUAD_LANES, (q + 1) * QUAD_LANES)) for d in range(2) for q in range(nquads)]
    zs = [_split_bf16(z_ref[d * nquads + q]) for d, q, sl in chains]
    gs = [_split_bf16(dirs[d][2][0, 0, 0, q]) for d, q, sl in chains]
    lhs = [jnp.concatenate([dirs[d][0][0, 0, :, sl].astype(BF16), gs[i][0], gs[i][1]], axis=0)
           for i, (d, q, sl) in enumerate(chains)]
    top = [jnp.dot(lhs[i], zs[i][0], preferred_element_type=F32) for i in range(len(chains))]
    low = [jnp.dot(gs[i][0], zs[i][1], preferred_element_type=F32) for i in range(len(chains))]
    for i, (d, q, sl) in enumerate(chains):
        dirs[d][4][0, :, sl] = top[i][:CHUNK] + dirs[d][1][0, 0, :, sl]
        z_ref[d * nquads + q] = (top[i][CHUNK:CHUNK + QUAD_LANES] + top[i][CHUNK + QUAD_LANES:] + low[i]
                                 + dirs[d][3][0, 0, 0, q])

    @pl.when(c == pl.num_programs(1) - 1)
    def _():
        st_ref[0] = z_ref[...]


def wkv_seq(q1, q2, gt, ht, s0):
    B, _, T, C = q1.shape
    nc, nquads = gt.shape[2], gt.shape[3]
    qf = pl.BlockSpec((1, 1, CHUNK, C), lambda b, c: (b, 0, c, 0))
    qb = pl.BlockSpec((1, 1, CHUNK, C), lambda b, c: (b, 1, nc - 1 - c, 0))
    gshape = (1, 1, 1, nquads, QUAD_LANES, QUAD_LANES)
    gf = pl.BlockSpec(gshape, lambda b, c: (b, 0, c, 0, 0, 0))
    gb = pl.BlockSpec(gshape, lambda b, c: (b, 1, nc - 1 - c, 0, 0, 0))
    sspec = pl.BlockSpec((1,) + s0.shape[1:], lambda b, c: (b, 0, 0, 0))
    return pl.pallas_call(
        functools.partial(_wkv_seq_kernel, nquads=nquads),
        out_shape=(jax.ShapeDtypeStruct((B, T, C), F32), jax.ShapeDtypeStruct((B, T, C), F32),
                   jax.ShapeDtypeStruct(s0.shape, F32)),
        grid=(B, nc),
        in_specs=[qf, qb, qf, qb, gf, gb, gf, gb, sspec],
        out_specs=(pl.BlockSpec((1, CHUNK, C), lambda b, c: (b, c, 0)),
                   pl.BlockSpec((1, CHUNK, C), lambda b, c: (b, nc - 1 - c, 0)), sspec),
        scratch_shapes=[pltpu.VMEM(s0.shape[1:], F32)],
        compiler_params=_params(2),
        name="wkv_seq",
    )(q1, q1, q2, q2, gt, gt, ht, ht, s0)


def wkv_chunked(lw2, k2, b2, v, kk, r, s0):
    q1, q2, gt, ht = wkv_chunk_prep(lw2, k2, b2, v, kk, r)
    return wkv_seq(q1, q2, gt, ht, s0)


def _mixer_out_kernel(of_ref, ob_ref, r_ref, v_ref, k2_ref, xg_ref, oat_ref, x_ref, gm_ref,
                      gup_ref, rk_ref, lng_ref, lnb_ref, wo_ref, o_ref):
    ones = _ones_bd(RW_DIM)
    inv = 1.0 / HEAD_DIM
    o = of_ref[0] + ob_ref[0]
    mu = _seg_sum2(o, ones) * inv
    dv = o - mu
    var = _seg_sum2(dv * dv, ones) * inv
    gn = dv * lax.rsqrt(var + GN_EPS) * lng_ref[...] + lnb_ref[...]
    k2 = k2_ref[0]
    k_bonus = (k2[:, :RW_DIM] + k2[:, RW_DIM:]) * 0.5
    bonus = _seg_sum2(r_ref[0] * k_bonus * rk_ref[...], ones) * v_ref[0]
    g = jnp.dot(jax.nn.sigmoid(xg_ref[0]), gup_ref[...], preferred_element_type=F32, precision=HIGHEST)
    y = (gn + bonus) * g
    acc = jnp.dot(oat_ref[0].astype(BF16), wo_ref[:ATTN_DIM, :], preferred_element_type=F32)
    acc = acc + jnp.dot(y.astype(BF16), wo_ref[ATTN_DIM:, :], preferred_element_type=F32)
    o_ref[0] = x_ref[0] + gm_ref[0] * acc


def mixer_out(o_f, o_b, r, v, k2, p, o_at, x, g_m, g_up, r_k, ln_x_g, ln_x_b, w_out):
    B, T, D = x.shape
    tm = _row_tile(T, 256)
    nspec = pl.BlockSpec((1, tm, RW_DIM), lambda b, i: (b, i, 0))
    full = lambda a: pl.BlockSpec(a.shape, lambda b, i: (0,) * a.ndim)
    rk2, lng2, lnb2 = r_k.reshape(1, RW_DIM), ln_x_g.reshape(1, RW_DIM), ln_x_b.reshape(1, RW_DIM)
    return pl.pallas_call(
        _mixer_out_kernel,
        out_shape=jax.ShapeDtypeStruct((B, T, D), F32),
        grid=(B, T // tm),
        in_specs=[nspec, nspec, nspec, nspec,
                  pl.BlockSpec((1, tm, 2 * RW_DIM), lambda b, i: (b, i, 0)),
                  pl.BlockSpec((1, tm, G_LORA), lambda b, i: (b, i, COL_XG // G_LORA)),
                  pl.BlockSpec((1, tm, ATTN_DIM), lambda b, i: (b, i, 0)),
                  pl.BlockSpec((1, tm, D), lambda b, i: (b, i, 0)),
                  pl.BlockSpec((1, 1, D), lambda b, i: (b, 0, 0)),
                  full(g_up), full(rk2), full(lng2), full(lnb2), full(w_out)],
        out_specs=pl.BlockSpec((1, tm, D), lambda b, i: (b, i, 0)),
        compiler_params=_params(2),
        name="mixer_out",
    )(o_f, o_b, r, v, k2, p, o_at, x, g_m, g_up, rk2, lng2, lnb2, w_out)


def _start_row_copies(idx_ref, n, src_hbm, dst_ref, sem):
    def body(r, carry):
        pltpu.make_async_copy(src_hbm.at[pl.ds(idx_ref[0, 0, r], 1)], dst_ref.at[pl.ds(r, 1)], sem).start()
        return carry
    lax.fori_loop(0, n, body, 0, unroll=32)


def _wait_row_copies(n, src_hbm, dst_ref, sem):
    pltpu.make_async_copy(src_hbm.at[pl.ds(0, n)], dst_ref.at[pl.ds(0, n)], sem).wait()


def _moe_kernel(be_ref, tok_ref, tokn_ref, h_hbm, w1f_ref, b1_ref, w2f_ref, b2_ref, o_ref, buf, sems,
                w1_ref, w2_ref, *, f, tm):
    i = pl.program_id(0)
    slot = i % 2

    @pl.when(jnp.logical_or(i == 0, be_ref[i] != be_ref[jnp.maximum(i - 1, 0)]))
    def _():
        w1_ref[...] = w1f_ref[0].astype(BF16)
        w2_ref[...] = w2f_ref[0].astype(BF16)

    @pl.when(i == 0)
    def _():
        _start_row_copies(tok_ref, tm, h_hbm, buf.at[0], sems.at[0])

    @pl.when(i + 1 < pl.num_programs(0))
    def _():
        _start_row_copies(tokn_ref, tm, h_hbm, buf.at[1 - slot], sems.at[1 - slot])

    _wait_row_copies(tm, h_hbm, buf.at[slot], sems.at[slot])
    u = jnp.dot(buf[slot].astype(BF16), w1_ref[0], preferred_element_type=F32) + b1_ref[0, 0]
    x_glu = jnp.minimum(u[:, :f], SWIGLU_LIMIT)
    x_lin = jnp.clip(u[:, f:], -SWIGLU_LIMIT, SWIGLU_LIMIT)
    act = x_glu * jax.nn.sigmoid(SWIGLU_ALPHA * x_glu) * (x_lin + 1.0)
    o_ref[...] = jnp.dot(act.astype(BF16), w2_ref[0], preferred_element_type=F32) + b2_ref[0, 0]


def moe_blocks(block_e, slot_tok, h, w1, b1, w2, b2, layer):
    N, D = h.shape
    nl, E, _, F2 = w1.shape
    f = F2 // 2
    tm = MOE_BLOCK
    rows = slot_tok.shape[0]
    nb = rows // tm
    tok3 = slot_tok.reshape(nb, 1, tm)
    kern = functools.partial(_moe_kernel, f=f, tm=tm)
    grid_spec = pltpu.PrefetchScalarGridSpec(
        num_scalar_prefetch=1,
        grid=(nb,),
        in_specs=[pl.BlockSpec((1, 1, tm), lambda i, be: (i, 0, 0), memory_space=pltpu.SMEM),
                  pl.BlockSpec((1, 1, tm), lambda i, be: (jnp.minimum(i + 1, nb - 1), 0, 0),
                               memory_space=pltpu.SMEM),
                  pl.BlockSpec(memory_space=pl.ANY),
                  pl.BlockSpec((1, 1, D, F2), lambda i, be: (layer, be[i], 0, 0)),
                  pl.BlockSpec((1, 1, 1, F2), lambda i, be: (layer, be[i], 0, 0)),
                  pl.BlockSpec((1, 1, f, D), lambda i, be: (layer, be[i], 0, 0)),
                  pl.BlockSpec((1, 1, 1, D), lambda i, be: (layer, be[i], 0, 0))],
        out_specs=pl.BlockSpec((tm, D), lambda i, be: (i, 0)),
        scratch_shapes=[pltpu.VMEM((2, tm, D), F32), pltpu.SemaphoreType.DMA((2,)),
                        pltpu.VMEM((1, D, F2), BF16), pltpu.VMEM((1, f, D), BF16)],
    )
    return pl.pallas_call(
        kern,
        out_shape=jax.ShapeDtypeStruct((rows, D), F32),
        grid_spec=grid_spec,
        compiler_params=_params(1),
        name="moe_blocks",
    )(block_e, tok3, tok3, h, w1, b1.reshape(nl, E, 1, F2), w2, b2.reshape(nl, E, 1, D))


def _moe_combine_kernel(sl_ref, sln_ref, out_hbm, gate_ref, x_ref, gf_ref, o_ref, buf, sems, *, tb):
    b, i = pl.program_id(0), pl.program_id(1)
    step = b * pl.num_programs(1) + i
    nsteps = pl.num_programs(0) * pl.num_programs(1)
    slot = step % 2
    n = TOP_K * tb

    @pl.when(step == 0)
    def _():
        _start_row_copies(sl_ref, n, out_hbm, buf.at[0], sems.at[0])

    @pl.when(step + 1 < nsteps)
    def _():
        _start_row_copies(sln_ref, n, out_hbm, buf.at[1 - slot], sems.at[1 - slot])

    _wait_row_copies(n, out_hbm, buf.at[slot], sems.at[slot])
    gate = gate_ref[0]
    y = jnp.zeros(x_ref.shape[1:], F32)
    for k in range(TOP_K):
        y = y + gate[:, k:k + 1] * buf[slot, k * tb:(k + 1) * tb, :]
    o_ref[0] = x_ref[0] + gf_ref[0] * y


def moe_combine(out, slots, gates, x, g_f):
    B, T, D = x.shape
    tb = _row_tile(T, 64)
    nt = T // tb
    n = TOP_K * tb
    sl3 = jnp.transpose(slots.reshape(B * nt, tb, TOP_K), (0, 2, 1)).reshape(B * nt, 1, n)
    last = B * nt - 1
    return pl.pallas_call(
        functools.partial(_moe_combine_kernel, tb=tb),
        out_shape=jax.ShapeDtypeStruct((B, T, D), F32),
        grid=(B, nt),
        in_specs=[pl.BlockSpec((1, 1, n), lambda b, i: (b * nt + i, 0, 0), memory_space=pltpu.SMEM),
                  pl.BlockSpec((1, 1, n), lambda b, i: (jnp.minimum(b * nt + i + 1, last), 0, 0),
                               memory_space=pltpu.SMEM),
                  pl.BlockSpec(memory_space=pl.ANY),
                  pl.BlockSpec((1, tb, LANES), lambda b, i: (b, i, 0)),
                  pl.BlockSpec((1, tb, D), lambda b, i: (b, i, 0)),
                  pl.BlockSpec((1, 1, D), lambda b, i: (b, 0, 0))],
        out_specs=pl.BlockSpec((1, tb, D), lambda b, i: (b, i, 0)),
        scratch_shapes=[pltpu.VMEM((2, n, D), F32), pltpu.SemaphoreType.DMA((2,))],
        compiler_params=_params(2),
        name="moe_combine",
    )(sl3, sl3, out, gates, x, g_f)


def _ffn_pre_kernel(x_ref, g_ref, sc_ref, sh_ref, rw_ref, rb_ref, h_ref, te_ref, tg_ref):
    h = _norm_mod(x_ref[0], g_ref[...], sc_ref[0], sh_ref[0])
    h_ref[0] = h
    cur = jnp.dot(h, rw_ref[...], preferred_element_type=F32, precision=HIGHEST) + rb_ref[...]
    n_e = cur.shape[1]
    lane_e = lax.broadcasted_iota(jnp.int32, cur.shape, 1).astype(F32)
    lane = lax.broadcasted_iota(jnp.int32, te_ref.shape[1:], 1)
    vals, idxs = [], []
    for k in range(TOP_K):
        m = jnp.max(cur, axis=1, keepdims=True)
        idx = jnp.min(jnp.where(cur == m, lane_e, float(n_e)), axis=1, keepdims=True)
        vals.append(m)
        idxs.append(idx)
        cur = jnp.where(lane_e == idx, -jnp.inf, cur)
    ex = [jnp.exp(v - vals[0]) for v in vals]
    denom = ex[0]
    for e in ex[1:]:
        denom = denom + e
    te = jnp.zeros(lane.shape, F32)
    tg = jnp.zeros(lane.shape, F32)
    for k in range(TOP_K):
        te = jnp.where(lane == k, idxs[k], te)
        tg = jnp.where(lane == k, ex[k] / denom, tg)
    te_ref[0] = te.astype(jnp.int32)
    tg_ref[0] = tg


def ffn_pre(x, g, sc, sh, router_w, router_b):
    B, T, D = x.shape
    E = router_w.shape[1]
    tm = _row_tile(T, 256)
    return pl.pallas_call(
        _ffn_pre_kernel,
        out_shape=(jax.ShapeDtypeStruct((B, T, D), F32), jax.ShapeDtypeStruct((B, T, LANES), jnp.int32),
                   jax.ShapeDtypeStruct((B, T, LANES), F32)),
        grid=(B, T // tm),
        in_specs=[pl.BlockSpec((1, tm, D), lambda b, i: (b, i, 0)),
                  pl.BlockSpec((1, D), lambda b, i: (0, 0)),
                  pl.BlockSpec((1, 1, D), lambda b, i: (b, 0, 0)),
                  pl.BlockSpec((1, 1, D), lambda b, i: (b, 0, 0)),
                  pl.BlockSpec((D, E), lambda b, i: (0, 0)),
                  pl.BlockSpec((1, E), lambda b, i: (0, 0))],
        out_specs=(pl.BlockSpec((1, tm, D), lambda b, i: (b, i, 0)),
                   pl.BlockSpec((1, tm, LANES), lambda b, i: (b, i, 0)),
                   pl.BlockSpec((1, tm, LANES), lambda b, i: (b, i, 0))),
        compiler_params=_params(2),
        name="ffn_pre",
    )(x, g.reshape(1, D), sc, sh, router_w, router_b.reshape(1, E))


def _rms_kernel(x_ref, g_ref, o_ref):
    x = x_ref[0]
    ms = jnp.mean(x * x, axis=-1, keepdims=True)
    o_ref[0] = x * lax.rsqrt(ms + NORM_EPS) * g_ref[...]


def final_norm(x, g):
    B, T, D = x.shape
    tm = _row_tile(T, 512)
    return pl.pallas_call(
        _rms_kernel,
        out_shape=jax.ShapeDtypeStruct((B, T, D), F32),
        grid=(B, T // tm),
        in_specs=[pl.BlockSpec((1, tm, D), lambda b, i: (b, i, 0)),
                  pl.BlockSpec((1, D), lambda b, i: (0, 0))],
        out_specs=pl.BlockSpec((1, tm, D), lambda b, i: (b, i, 0)),
        compiler_params=_params(2),
        name="final_norm",
    )(x, g.reshape(1, D))


def _route(top_e):
    N = top_e.shape[0]
    n4 = N * TOP_K
    flat_e = top_e.reshape(-1)
    order = jnp.argsort(flat_e)
    e_sorted = flat_e[order]
    counts = jnp.bincount(flat_e, length=N_EXPERTS)
    padded = (counts + MOE_BLOCK - 1) // MOE_BLOCK * MOE_BLOCK
    start = jnp.cumsum(counts) - counts
    pstart = jnp.cumsum(padded) - padded
    dest = (pstart[e_sorted] + jnp.arange(n4, dtype=jnp.int32) - start[e_sorted]).astype(jnp.int32)
    n_blocks = (n4 + MOE_BLOCK - 1) // MOE_BLOCK + N_EXPERTS
    rows = n_blocks * MOE_BLOCK
    ends = jnp.cumsum(padded)
    block_start = jnp.arange(n_blocks, dtype=ends.dtype) * MOE_BLOCK
    block_e = jnp.minimum(jnp.sum(ends[None, :] <= block_start[:, None], axis=1), N_EXPERTS - 1).astype(jnp.int32)
    slot = jnp.arange(rows, dtype=jnp.int32)
    slot_e = jnp.repeat(block_e, MOE_BLOCK)
    off = slot - pstart[slot_e].astype(jnp.int32)
    src = jnp.minimum(start[slot_e].astype(jnp.int32) + off, n4 - 1)
    slot_tok = jnp.where(off < counts[slot_e], order[src] // TOP_K, 0).astype(jnp.int32)
    slot_flat = dest[jnp.argsort(order)]
    return slot_tok, block_e, slot_flat.reshape(N, TOP_K)


def kernel(x, c, ctx, c_ctx, norm_mix_g, norm_ffn_g, w_mod, b_mod, w_in, w_out, q_norm_g, k_norm_g,
           conv_w, w0, w_up, a0, a_up, g_up, k_k, k_a, r_k, ln_x_g, ln_x_b,
           router_w, router_b, e_w1, e_b1, e_w2, e_b2, norm_final_g):
    B, T, D = x.shape
    C = ctx.shape[1]
    depth = w_in.shape[0]
    cos, sin = _rope_tables(T)
    rpad = -(-(B + 1) // SUBLANES) * SUBLANES
    cvecs = jnp.zeros((rpad, D), F32).at[:B].set(c).at[B].set(c_ctx)
    s_zero = jnp.zeros((B, 2 * RW_DIM // QUAD_LANES, QUAD_LANES, QUAD_LANES), F32)
    x_lat, x_ctx = x, ctx
    for l in range(depth):
        last = l == depth - 1
        mods = adaln_all(cvecs, w_mod[l], b_mod[l])
        m_lat = [m[:, None, :] for m in jnp.split(mods[:B], 6, axis=-1)]
        m_ctx = [jnp.broadcast_to(m[None, :, :], (B, 1, D)) for m in jnp.split(mods[B:B + 1], 6, axis=-1)]
        sh_m, sc_m, g_m, sh_f, sc_f, g_f = m_lat
        csh_m, csc_m, cg_m, csh_f, csc_f, cg_f = m_ctx
        w_in_l = w_in[l].astype(BF16)
        w_out_l = w_out[l].astype(BF16)
        p_lat = norm_mod_matmul(x_lat, norm_mix_g[l], sc_m, sh_m, w_in_l)
        p_ctx = norm_mod_matmul(x_ctx, norm_mix_g[l], csc_m, csh_m,
                                w_in_l[:, :STATE_COLS] if last else w_in_l)

        ktc, vc = kv_prep(p_ctx, k_norm_g[l])
        ktl, vl = kv_prep(p_lat, k_norm_g[l], cos, sin)
        o_at = attention(p_lat, q_norm_g[l], ktc, vc, ktl, vl, cos, sin)

        rw = (conv_w[l], w0[l], w_up[l], a0[l], a_up[l], k_k[l], k_a[l])
        w2c, k2c, b2c, v_c, kk_c, r_c = wkv_prep(p_ctx, *rw, has_r=not last)
        w2l, k2l, b2l, v_l, kk_l, r_l = wkv_prep(p_lat, *rw, has_r=True)
        of_c, ob_c, s_c = wkv_chunked(w2c, k2c, b2c, v_c, kk_c, r_c, s_zero)
        of_l, ob_l, _ = wkv_chunked(w2l, k2l, b2l, v_l, kk_l, r_l, s_c)

        ro = (g_up[l], r_k[l], ln_x_g[l], ln_x_b[l], w_out_l)
        x_lat = mixer_out(of_l, ob_l, r_l, v_l, k2l, p_lat, o_at, x_lat, g_m, *ro)
        h_lat, te_lat, tg_lat = ffn_pre(x_lat, norm_ffn_g[l], sc_f, sh_f, router_w[l], router_b[l])
        if last:
            slot_tok, block_e, slot_flat = _route(te_lat.reshape(B * T, LANES)[:, :TOP_K])
            out = moe_blocks(block_e, slot_tok, h_lat.reshape(B * T, D), e_w1, e_b1, e_w2, e_b2, l)
            x_lat = moe_combine(out, slot_flat.reshape(B, T, TOP_K), tg_lat, x_lat, g_f)
        else:
            o_at_c = attention(p_ctx, q_norm_g[l], ktc, vc)
            x_ctx = mixer_out(of_c, ob_c, r_c, v_c, k2c, p_ctx, o_at_c, x_ctx, cg_m, *ro)
            h_ctx, te_ctx, tg_ctx = ffn_pre(x_ctx, norm_ffn_g[l], csc_f, csh_f, router_w[l], router_b[l])
            h_all = jnp.concatenate([h_lat.reshape(B * T, D), h_ctx.reshape(B * C, D)], axis=0)
            te_all = jnp.concatenate([te_lat.reshape(B * T, LANES), te_ctx.reshape(B * C, LANES)], axis=0)
            slot_tok, block_e, slot_flat = _route(te_all[:, :TOP_K])
            out = moe_blocks(block_e, slot_tok, h_all, e_w1, e_b1, e_w2, e_b2, l)
            x_lat = moe_combine(out, slot_flat[:B * T].reshape(B, T, TOP_K), tg_lat, x_lat, g_f)
            x_ctx = moe_combine(out, slot_flat[B * T:].reshape(B, C, TOP_K), tg_ctx, x_ctx, cg_f)
    return final_norm(x_lat, norm_final_g)
```

```python
import functools
import math

import numpy as np
import jax
import jax.numpy as jnp
from jax import lax
from jax.experimental import pallas as pl
from jax.experimental.pallas import tpu as pltpu

F32 = jnp.float32
BF16 = jnp.bfloat16
HIGHEST = lax.Precision.HIGHEST

GRID_W = 64
HEAD_DIM = 64
ATTN_HEADS = 8
ATTN_KV_HEADS = 2
GROUP = ATTN_HEADS // ATTN_KV_HEADS
ATTN_DIM = ATTN_HEADS * HEAD_DIM
KV_DIM = ATTN_KV_HEADS * HEAD_DIM
RW_HEADS = 8
RW_DIM = RW_HEADS * HEAD_DIM
W_LORA = 64
A_LORA = 64
G_LORA = 128
CONV_W = 3
N_EXPERTS = 32
TOP_K = 4
SWIGLU_LIMIT = 7.0
SWIGLU_ALPHA = 1.702
ROPE_THETA = 10000.0
ROPE_PAIRS = HEAD_DIM // 4
MOE_BLOCK = 256
NORM_EPS = 1e-6
GN_EPS = 64e-5
ATTN_SCALE = HEAD_DIM ** -0.5
EXP_M05 = math.exp(-0.5)
COL_K_AT = 0
COL_V_AT = KV_DIM
COL_K_RW = 2 * KV_DIM
COL_V_RW = COL_K_RW + RW_DIM
COL_XW = COL_V_RW + RW_DIM
COL_XA = COL_XW + 2 * W_LORA
STATE_COLS = COL_XA + 2 * A_LORA
COL_Q_AT = STATE_COLS
COL_R_RW = COL_Q_AT + ATTN_DIM
COL_XG = COL_R_RW + RW_DIM

LANES = 128
SUBLANES = 8
QUAD_LANES = 256
CHUNK = 64
CHUNKS_PER_STEP = 4
ATTN_ROW_GROUP = 128
VMEM_LIMIT = 48 * 1024 * 1024


def _row_tile(n, pref):
    t = pref
    while n % t:
        t //= 2
    return t


def _params(n_axes):
    return pltpu.CompilerParams(dimension_semantics=("arbitrary",) * n_axes, vmem_limit_bytes=VMEM_LIMIT)


def _ones_bd(n):
    ra = lax.broadcasted_iota(jnp.int32, (n, n), 0) // HEAD_DIM
    rb = lax.broadcasted_iota(jnp.int32, (n, n), 1) // HEAD_DIM
    return (ra == rb).astype(BF16)


def _seg_sum(x_bf16, ones_bd):
    return jnp.dot(x_bf16, ones_bd, preferred_element_type=F32)


def _seg_sum2(x, ones_bd):
    hi = x.astype(BF16)
    lo = (x - hi.astype(F32)).astype(BF16)
    return _seg_sum(hi, ones_bd) + _seg_sum(lo, ones_bd)


def _adaln_kernel(c_ref, w_ref, b_ref, o_ref):
    c = c_ref[...]
    s = c * jax.nn.sigmoid(c)
    o_ref[...] = jnp.dot(s, w_ref[...], preferred_element_type=F32, precision=HIGHEST) + b_ref[...]


def adaln_all(cvecs, w_mod, b_mod):
    R, D = cvecs.shape
    N = w_mod.shape[1]
    tn = 512
    return pl.pallas_call(
        _adaln_kernel,
        out_shape=jax.ShapeDtypeStruct((R, N), F32),
        grid=(N // tn,),
        in_specs=[pl.BlockSpec((R, D), lambda j: (0, 0)),
                  pl.BlockSpec((D, tn), lambda j: (0, j)),
                  pl.BlockSpec((1, tn), lambda j: (0, j))],
        out_specs=pl.BlockSpec((R, tn), lambda j: (0, j)),
        compiler_params=_params(1),
        name="adaln",
    )(cvecs, w_mod, b_mod.reshape(1, N))


def _norm_mod(x, g, sc, sh):
    ms = jnp.mean(x * x, axis=-1, keepdims=True)
    y = x * lax.rsqrt(ms + NORM_EPS) * g
    return y * (1.0 + sc) + sh


def _norm_mod_matmul_kernel(x_ref, g_ref, sc_ref, sh_ref, w_ref, o_ref):
    h = _norm_mod(x_ref[0], g_ref[...], sc_ref[0], sh_ref[0])
    o_ref[0] = jnp.dot(h.astype(BF16), w_ref[...], preferred_element_type=F32)


def norm_mod_matmul(x, g, sc, sh, w):
    B, T, D = x.shape
    N = w.shape[1]
    tm = _row_tile(T, 256)
    return pl.pallas_call(
        _norm_mod_matmul_kernel,
        out_shape=jax.ShapeDtypeStruct((B, T, N), F32),
        grid=(B, T // tm),
        in_specs=[pl.BlockSpec((1, tm, D), lambda b, i: (b, i, 0)),
                  pl.BlockSpec((1, D), lambda b, i: (0, 0)),
                  pl.BlockSpec((1, 1, D), lambda b, i: (b, 0, 0)),
                  pl.BlockSpec((1, 1, D), lambda b, i: (b, 0, 0)),
                  pl.BlockSpec((D, N), lambda b, i: (0, 0))],
        out_specs=pl.BlockSpec((1, tm, N), lambda b, i: (b, i, 0)),
        compiler_params=_params(2),
        name="norm_mod_matmul",
    )(x, g.reshape(1, D), sc, sh, w)


def _pair_rms(x, g):
    lane = lax.broadcasted_iota(jnp.int32, x.shape, 1)
    first = lane < HEAD_DIM
    xx = x * x
    s0 = jnp.sum(jnp.where(first, xx, 0.0), axis=1, keepdims=True)
    s1 = jnp.sum(jnp.where(first, 0.0, xx), axis=1, keepdims=True)
    ms = jnp.where(first, s0, s1) * (1.0 / HEAD_DIM)
    return x * lax.rsqrt(ms + NORM_EPS) * g


def _pair_rope(x, cos, sin):
    lane = lax.broadcasted_iota(jnp.int32, x.shape, 1)
    partner = jnp.where(lane % (2 * ROPE_PAIRS) < ROPE_PAIRS,
                        pltpu.roll(x, LANES - ROPE_PAIRS, 1), pltpu.roll(x, ROPE_PAIRS, 1))
    return x * cos + partner * sin


def _rope_tables(T):
    rows = T // GRID_W
    row = jnp.repeat(jnp.arange(rows), GRID_W)
    col = jnp.tile(jnp.arange(GRID_W), rows)
    pos = jnp.stack([row, col], axis=-1).astype(F32)
    freqs = ROPE_THETA ** (-jnp.arange(ROPE_PAIRS, dtype=F32) / ROPE_PAIRS)
    ang = pos[:, :, None] * freqs
    cos, sin = jnp.cos(ang), jnp.sin(ang)
    cos_h = jnp.concatenate([cos[:, 0], cos[:, 0], cos[:, 1], cos[:, 1]], axis=-1)
    sin_h = jnp.concatenate([-sin[:, 0], sin[:, 0], -sin[:, 1], sin[:, 1]], axis=-1)
    return jnp.tile(cos_h, (1, 2)), jnp.tile(sin_h, (1, 2))


def _kv_prep_kernel(*refs, rope):
    if rope:
        p_ref, g_ref, cos_ref, sin_ref, kt_ref, v_ref = refs
    else:
        p_ref, g_ref, kt_ref, v_ref = refs
    x = p_ref[0]
    k = _pair_rms(x[:, :KV_DIM], g_ref[...])
    if rope:
        k = _pair_rope(k, cos_ref[...], sin_ref[...])
    v = x[:, KV_DIM:]
    lane = lax.broadcasted_iota(jnp.int32, k.shape, 1)
    first = lane < HEAD_DIM
    kr = pltpu.roll(k, HEAD_DIM, 1)
    vr = pltpu.roll(v, HEAD_DIM, 1)
    kt_ref[0, 0] = jnp.where(first, k, kr).T.astype(BF16)
    kt_ref[0, 1] = jnp.where(first, kr, k).T.astype(BF16)
    ones = jnp.ones(v.shape, F32)
    v_ref[0, 0] = jnp.concatenate([jnp.where(first, v, vr), ones], axis=1).astype(BF16)
    v_ref[0, 1] = jnp.concatenate([jnp.where(first, vr, v), ones], axis=1).astype(BF16)


def kv_prep(p, k_norm_g, cos=None, sin=None):
    B, T, _ = p.shape
    tm = _row_tile(T, 256)
    rope = cos is not None
    g2 = jnp.tile(k_norm_g, 2).reshape(1, LANES)
    in_specs = [pl.BlockSpec((1, tm, 2 * KV_DIM), lambda b, i: (b, i, 0)),
                pl.BlockSpec((1, LANES), lambda b, i: (0, 0))]
    args = [p, g2]
    if rope:
        in_specs += [pl.BlockSpec((tm, LANES), lambda b, i: (i, 0))] * 2
        args += [cos, sin]
    return pl.pallas_call(
        functools.partial(_kv_prep_kernel, rope=rope),
        out_shape=(jax.ShapeDtypeStruct((B, ATTN_KV_HEADS, LANES, T), BF16),
                   jax.ShapeDtypeStruct((B, ATTN_KV_HEADS, T, 2 * LANES), BF16)),
        grid=(B, T // tm),
        in_specs=in_specs,
        out_specs=(pl.BlockSpec((1, ATTN_KV_HEADS, LANES, tm), lambda b, i: (b, 0, 0, i)),
                   pl.BlockSpec((1, ATTN_KV_HEADS, tm, 2 * LANES), lambda b, i: (b, 0, i, 0))),
        compiler_params=_params(2),
        name="kv_prep",
    )(*args)


def _attn_kernel(*refs, tq, tkl, nkl, rope, has_lat, rg):
    refs = list(refs)
    p_ref, g_ref = refs[:2]
    refs = refs[2:]
    if rope:
        cos_ref, sin_ref = refs[:2]
        refs = refs[2:]
    ktc_ref, vc_ref = refs[:2]
    refs = refs[2:]
    if has_lat:
        ktl_ref, vl_ref = refs[:2]
        refs = refs[2:]
    o_ref, m_ref, acc_ref = refs

    q = _pair_rms(p_ref[0], g_ref[...])
    if rope:
        q = _pair_rope(q, cos_ref[...], sin_ref[...])
    q = q * ATTN_SCALE
    lane = lax.broadcasted_iota(jnp.int32, (tq, LANES), 1)
    first = lane < HEAD_DIM
    qq = jnp.concatenate([jnp.where(first, q, 0.0), jnp.where(first, 0.0, q)], axis=0).astype(BF16)

    m_ref[...] = jnp.full(m_ref.shape, -jnp.inf, F32)
    acc_ref[...] = jnp.zeros(acc_ref.shape, F32)
    groups = [slice(g * rg, (g + 1) * rg) for g in range(2 * tq // rg)]

    def chunk(kt, vv):
        s = [jnp.dot(qq[g], kt, preferred_element_type=F32) for g in groups]
        m_old = [m_ref[g] for g in groups]
        m_new = [jnp.maximum(mo, jnp.max(x, axis=-1, keepdims=True)) for mo, x in zip(m_old, s)]
        alpha = [jnp.exp(mo - mn) for mo, mn in zip(m_old, m_new)]
        p = [jnp.exp(x - jnp.concatenate([mn] * (x.shape[1] // LANES), axis=1)).astype(BF16)
             for x, mn in zip(s, m_new)]
        pv = [jnp.dot(x, vv, preferred_element_type=F32) for x in p]
        for g, al, mn, x in zip(groups, alpha, m_new, pv):
            acc_ref[g] = jnp.concatenate([al, al], axis=1) * acc_ref[g] + x
            m_ref[g] = mn

    chunk(ktc_ref[0, 0], vc_ref[0, 0])
    if has_lat:
        def body(j, carry):
            ks = pl.multiple_of(j * tkl, tkl)
            chunk(ktl_ref[0, 0, :, pl.ds(ks, tkl)], vl_ref[0, 0, pl.ds(ks, tkl), :])
            return carry
        lax.fori_loop(0, nkl, body, 0)

    acc = acc_ref[...]
    o = acc[:, :LANES] / acc[:, LANES:]
    o_ref[0] = jnp.where(first, o[:tq], o[tq:])


def attention(p, q_norm_g, ktc, vc, ktl=None, vl=None, cos=None, sin=None):
    B, T, _ = p.shape
    C = ktc.shape[-1]
    rope = cos is not None
    has_lat = ktl is not None
    tq = _row_tile(T, 256)
    pairs = ATTN_HEADS // 2
    pairs_per_kv = GROUP // 2
    qblk0 = COL_Q_AT // LANES
    g2 = jnp.tile(q_norm_g, 2).reshape(1, LANES)
    in_specs = [pl.BlockSpec((1, tq, LANES), lambda b, h, i: (b, i, qblk0 + h)),
                pl.BlockSpec((1, LANES), lambda b, h, i: (0, 0))]
    args = [p, g2]
    if rope:
        in_specs += [pl.BlockSpec((tq, LANES), lambda b, h, i: (i, 0))] * 2
        args += [cos, sin]
    in_specs += [pl.BlockSpec((1, 1, LANES, C), lambda b, h, i: (b, h // pairs_per_kv, 0, 0)),
                 pl.BlockSpec((1, 1, C, 2 * LANES), lambda b, h, i: (b, h // pairs_per_kv, 0, 0))]
    args += [ktc, vc]
    tkl = nkl = 0
    if has_lat:
        S = ktl.shape[-1]
        tkl = _row_tile(S, 2048)
        nkl = S // tkl
        in_specs += [pl.BlockSpec((1, 1, LANES, S), lambda b, h, i: (b, h // pairs_per_kv, 0, 0)),
                     pl.BlockSpec((1, 1, S, 2 * LANES), lambda b, h, i: (b, h // pairs_per_kv, 0, 0))]
        args += [ktl, vl]
    kern = functools.partial(_attn_kernel, tq=tq, tkl=tkl, nkl=nkl, rope=rope, has_lat=has_lat,
                             rg=_row_tile(tq, ATTN_ROW_GROUP))
    return pl.pallas_call(
        kern,
        out_shape=jax.ShapeDtypeStruct((B, T, ATTN_DIM), F32),
        grid=(B, pairs, T // tq),
        in_specs=in_specs,
        out_specs=pl.BlockSpec((1, tq, LANES), lambda b, h, i: (b, i, h)),
        scratch_shapes=[pltpu.VMEM((2 * tq, LANES), F32), pltpu.VMEM((2 * tq, 2 * LANES), F32)],
        compiler_params=_params(3),
        name="attention",
    )(*args)


def _wkv_prep_kernel(p_ref, pp_ref, pn_ref, cw_ref, w0_ref, wup_ref, a0_ref, aup_ref, kk_ref, ka_ref,
                     w2_ref, k2_ref, b2_ref, v_ref, kkn_ref, r_ref, *, tm, has_r):
    i = pl.program_id(1)
    n = pl.num_programs(1)
    x = p_ref[0]
    prev = pp_ref[0][SUBLANES - 1:SUBLANES, :] * (i > 0).astype(F32)
    nxt = pn_ref[0][0:1, :] * (i < n - 1).astype(F32)
    row = lax.broadcasted_iota(jnp.int32, (tm, RW_DIM), 0)

    def conv(col, wcol):
        u = x[:, col:col + RW_DIM]
        up = jnp.where(row == 0, prev[:, col:col + RW_DIM], pltpu.roll(u, 1, 0))
        un = jnp.where(row == tm - 1, nxt[:, col:col + RW_DIM], pltpu.roll(u, tm - 1, 0))
        w = cw_ref[:, wcol:wcol + RW_DIM]
        return up * w[0:1] + u * w[1:2] + un * w[2:3]

    k = conv(COL_K_RW, RW_DIM)
    v = conv(COL_V_RW, 2 * RW_DIM)
    v_ref[0] = v
    r_ref[0] = conv(COL_R_RW, 0) if has_r else jnp.zeros((tm, RW_DIM), F32)
    kk = k * kk_ref[...]
    ss = _seg_sum2(kk * kk, _ones_bd(RW_DIM))
    kk = kk * lax.rsqrt(jnp.maximum(ss, 1e-24))
    kkn_ref[0] = kk
    xw = jnp.tanh(x[:, COL_XW:COL_XW + 2 * W_LORA])
    xa = x[:, COL_XA:COL_XA + 2 * A_LORA]
    for d in range(2):
        wl = w0_ref[d:d + 1, :] + jnp.dot(xw[:, d * W_LORA:(d + 1) * W_LORA], wup_ref[d],
                                          preferred_element_type=F32, precision=HIGHEST)
        decay = -EXP_M05 * jax.nn.sigmoid(wl)
        a = jax.nn.sigmoid(a0_ref[d:d + 1, :] + jnp.dot(xa[:, d * A_LORA:(d + 1) * A_LORA], aup_ref[d],
                                                        preferred_element_type=F32, precision=HIGHEST))
        sl = slice(d * RW_DIM, (d + 1) * RW_DIM)
        w2_ref[0, :, sl] = decay
        k2_ref[0, :, sl] = k * (1.0 + (a - 1.0) * ka_ref[...])
        b2_ref[0, :, sl] = kk * a


def wkv_prep(p, conv_w, w0, w_up, a0, a_up, k_k, k_a, has_r):
    B, T, NP = p.shape
    tm = _row_tile(T, 256)
    nb8 = tm // SUBLANES
    last8 = T // SUBLANES - 1
    full = lambda a: pl.BlockSpec(a.shape, lambda b, i: (0,) * a.ndim)
    kk2, ka2 = k_k.reshape(1, RW_DIM), k_a.reshape(1, RW_DIM)
    wide = jax.ShapeDtypeStruct((B, T, 2 * RW_DIM), F32)
    narrow = jax.ShapeDtypeStruct((B, T, RW_DIM), F32)
    wspec = pl.BlockSpec((1, tm, 2 * RW_DIM), lambda b, i: (b, i, 0))
    nspec = pl.BlockSpec((1, tm, RW_DIM), lambda b, i: (b, i, 0))
    return pl.pallas_call(
        functools.partial(_wkv_prep_kernel, tm=tm, has_r=has_r),
        out_shape=(wide, wide, wide, narrow, narrow, narrow),
        grid=(B, T // tm),
        in_specs=[pl.BlockSpec((1, tm, NP), lambda b, i: (b, i, 0)),
                  pl.BlockSpec((1, SUBLANES, NP), lambda b, i: (b, jnp.maximum(i * nb8 - 1, 0), 0)),
                  pl.BlockSpec((1, SUBLANES, NP), lambda b, i: (b, jnp.minimum((i + 1) * nb8, last8), 0)),
                  full(conv_w), full(w0), full(w_up), full(a0), full(a_up), full(kk2), full(ka2)],
        out_specs=(wspec, wspec, wspec, nspec, nspec, nspec),
        compiler_params=_params(2),
        name="wkv_prep",
    )(p, p, p, conv_w, w0, w_up, a0, a_up, kk2, ka2)


def _split_bf16(x):
    hi = x.astype(BF16)
    return hi, (x - hi.astype(F32)).astype(BF16)


def _mm3(a, b, nt=False, passes=3):
    dn = (((1,), (1 if nt else 0,)), ((), ()))
    if passes == 1:
        return lax.dot_general(a.astype(BF16), b.astype(BF16), dn, preferred_element_type=F32)
    ah, al = _split_bf16(a)
    m = a.shape[0]
    if passes == 2:
        top = lax.dot_general(jnp.concatenate([ah, al], axis=0), b.astype(BF16), dn, preferred_element_type=F32)
        return top[:m] + top[m:]
    bh, bl = _split_bf16(b)
    top = lax.dot_general(jnp.concatenate([ah, al], axis=0), bh, dn, preferred_element_type=F32)
    return top[:m] + top[m:] + lax.dot_general(ah, bl, dn, preferred_element_type=F32)


def _chunk_terms(chains):
    L = CHUNK
    nch = range(len(chains))
    row = lax.broadcasted_iota(jnp.int32, (L, QUAD_LANES), 0)
    col = lax.broadcasted_iota(jnp.int32, (L, QUAD_LANES), 1) % HEAD_DIM
    tr = lax.broadcasted_iota(jnp.int32, (L, L), 0)
    tc = lax.broadcasted_iota(jnp.int32, (L, L), 1)
    masks = {True: (col > row, col >= row, (tc >= tr).astype(F32)),
             False: (col < row, col <= row, (tc <= tr).astype(F32))}
    eye_l = (col == row).astype(F32)
    ra = lax.broadcasted_iota(jnp.int32, (QUAD_LANES, QUAD_LANES), 0)
    rb = lax.broadcasted_iota(jnp.int32, (QUAD_LANES, QUAD_LANES), 1)
    bd = ra // HEAD_DIM == rb // HEAD_DIM
    eye_q = (ra == rb).astype(F32)

    def bdiag(y):
        return jnp.where(bd, jnp.concatenate([y] * (QUAD_LANES // L), axis=0), 0.0)

    def bdm(x, y, passes):
        ys = [bdiag(y[:, i:i + QUAD_LANES]) for i in range(0, y.shape[1], QUAD_LANES)]
        return _mm3(x, ys[0] if len(ys) == 1 else jnp.concatenate(ys, axis=1), passes=passes)

    lw, kd, bb, v, kk, r, back = (list(x) for x in zip(*chains))
    strict = [masks[b][0] for b in back]
    incl = [masks[b][1] for b in back]
    cs = [_mm3(masks[back[c]][2], lw[c]) for c in nch]
    g_last = [jnp.exp(cs[c][0:1] if back[c] else cs[c][L - 1:L]) for c in nch]
    at = [-kk[c] * jnp.exp(cs[c] - lw[c]) for c in nch]
    g_inv = [jnp.exp(-cs[c]) for c in nch]
    bt = [bb[c] * g_inv[c] for c in nch]
    kt = [kd[c] * g_inv[c] for c in nch]
    rt = [r[c] * jnp.exp(cs[c]) for c in nch]
    nt_dims = (((1,), (1,)), ((), ()))
    a_split = [_split_bf16(at[c]) for c in nch]
    b_split = [_split_bf16(bdiag(bt[c])) for c in nch]
    lhs = [jnp.concatenate([a_split[c][0], a_split[c][1], rt[c].astype(BF16)], axis=0) for c in nch]
    pb = [lax.dot_general(lhs[c], b_split[c][0], nt_dims, preferred_element_type=F32) for c in nch]
    pb_lo = [lax.dot_general(a_split[c][0], b_split[c][1], nt_dims, preferred_element_type=F32) for c in nch]
    pk = [lax.dot_general(lhs[c], bdiag(kt[c]).astype(BF16), nt_dims, preferred_element_type=F32) for c in nch]
    n = [jnp.where(strict[c], pb[c][:L] + pb[c][L:2 * L] + pb_lo[c], 0.0) for c in nch]
    m_rb = [jnp.where(incl[c], pb[c][2 * L:], 0.0) for c in nch]
    m_ak = [jnp.where(strict[c], pk[c][:L] + pk[c][L:2 * L], 0.0) for c in nch]
    m_rk = [jnp.where(incl[c], pk[c][2 * L:], 0.0) for c in nch]
    steps = int(math.log2(L)) - 1
    p = [bdm(n[c], n[c], 3) for c in nch]
    t_inv = [eye_l + n[c] for c in nch]
    for k in range(steps):
        if k + 1 < steps:
            both = [bdm(jnp.concatenate([p[c], t_inv[c]], axis=0), p[c], 1) for c in nch]
            p = [both[c][:L] for c in nch]
            t_inv = [t_inv[c] + both[c][L:] for c in nch]
        else:
            t_inv = [t_inv[c] + bdm(t_inv[c], p[c], 1) for c in nch]
    xy = [bdm(jnp.concatenate([m_ak[c], m_rk[c]], axis=0), v[c], 1) for c in nch]
    w = [bdm(t_inv[c], jnp.concatenate([at[c], xy[c][:L]], axis=1), 2) for c in nch]
    qq = [bdm(m_rb[c], w[c], 1) for c in nch]
    q1 = [rt[c] + qq[c][:, :QUAD_LANES] for c in nch]
    q2 = [xy[c][L:] + qq[c][:, QUAD_LANES:] for c in nch]
    g = [jnp.where(bd, (eye_q + _mm3(w[c][:, :QUAD_LANES].T, bt[c], passes=2)) * g_last[c], 0.0) for c in nch]
    h = [jnp.where(bd, _mm3(jnp.concatenate([w[c][:, QUAD_LANES:], v[c]], axis=0).T,
                            jnp.concatenate([bt[c], kt[c]], axis=0), passes=2) * g_last[c], 0.0) for c in nch]
    return [(q1[c], q2[c], g[c].T, h[c].T) for c in nch]


def _wkv_chunk_kernel(lw_ref, k_ref, b_ref, v_ref, kk_ref, r_ref, q1_ref, q2_ref, gt_ref, ht_ref, *, nquads, cps):
    keys = [(j, d, q) for j in range(cps) for d in range(2) for q in range(nquads)]
    chains = []
    for j, d, q in keys:
        rows = slice(j * CHUNK, (j + 1) * CHUNK)
        sl = slice(q * QUAD_LANES, (q + 1) * QUAD_LANES)
        sd = slice(d * RW_DIM + q * QUAD_LANES, d * RW_DIM + (q + 1) * QUAD_LANES)
        chains.append((lw_ref[0, rows, sd], k_ref[0, rows, sd], b_ref[0, rows, sd],
                       v_ref[0, rows, sl], kk_ref[0, rows, sl], r_ref[0, rows, sl], d == 1))
    res = _chunk_terms(chains)
    for (j, d, q), (q1, q2, gt, ht) in zip(keys, res):
        rows = slice(j * CHUNK, (j + 1) * CHUNK)
        sl = slice(q * QUAD_LANES, (q + 1) * QUAD_LANES)
        q1_ref[0, d, rows, sl] = q1
        q2_ref[0, d, rows, sl] = q2
        gt_ref[0, d, j, q] = gt
        ht_ref[0, d, j, q] = ht


def wkv_chunk_prep(lw2, k2, b2, v, kk, r):
    B, T, C = v.shape
    nquads = C // QUAD_LANES
    nc = T // CHUNK
    cps = math.gcd(nc, CHUNKS_PER_STEP)
    tr = cps * CHUNK
    wide = pl.BlockSpec((1, tr, 2 * C), lambda b, c: (b, c, 0))
    narrow = pl.BlockSpec((1, tr, C), lambda b, c: (b, c, 0))
    qspec = pl.BlockSpec((1, 2, tr, C), lambda b, c: (b, 0, c, 0))
    gspec = pl.BlockSpec((1, 2, cps, nquads, QUAD_LANES, QUAD_LANES), lambda b, c: (b, 0, c, 0, 0, 0))
    qshape = jax.ShapeDtypeStruct((B, 2, T, C), F32)
    gshape = jax.ShapeDtypeStruct((B, 2, nc, nquads, QUAD_LANES, QUAD_LANES), F32)
    return pl.pallas_call(
        functools.partial(_wkv_chunk_kernel, nquads=nquads, cps=cps),
        out_shape=(qshape, qshape, gshape, gshape),
        grid=(B, nc // cps),
        in_specs=[wide, wide, wide, narrow, narrow, narrow],
        out_specs=(qspec, qspec, gspec, gspec),
        compiler_params=_params(2),
        name="wkv_chunk_prep",
    )(lw2, k2, b2, v, kk, r)


def _wkv_seq_kernel(q1f_ref, q1b_ref, q2f_ref, q2b_ref, gtf_ref, gtb_ref, htf_ref, htb_ref, s0_ref,
                    of_ref, ob_ref, st_ref, z_ref, *, nquads):
    c = pl.program_id(1)

    @pl.when(c == 0)
    def _():
        z_ref[...] = s0_ref[0]

    dirs = ((q1f_ref, q2f_ref, gtf_ref, htf_ref, of_ref), (q1b_ref, q2b_ref, gtb_ref, htb_ref, ob_ref))
    chains = [(d, q, slice(q * QUAD_LANES, (q + 1) * QUAD_LANES)) for d in range(2) for q in range(nquads)]
    zs = [_split_bf16(z_ref[d * nquads + q]) for d, q, sl in chains]
    gs = [_split_bf16(dirs[d][2][0, 0, 0, q]) for d, q, sl in chains]
    lhs = [jnp.concatenate([dirs[d][0][0, 0, :, sl].astype(BF16), gs[i][0], gs[i][1]], axis=0)
           for i, (d, q, sl) in enumerate(chains)]
    top = [jnp.dot(lhs[i], zs[i][0], preferred_element_type=F32) for i in range(len(chains))]
    low = [jnp.dot(gs[i][0], zs[i][1], preferred_element_type=F32) for i in range(len(chains))]
    for i, (d, q, sl) in enumerate(chains):
        dirs[d][4][0, :, sl] = top[i][:CHUNK] + dirs[d][1][0, 0, :, sl]
        z_ref[d * nquads + q] = (top[i][CHUNK:CHUNK + QUAD_LANES] + top[i][CHUNK + QUAD_LANES:] + low[i]
                                 + dirs[d][3][0, 0, 0, q])

    @pl.when(c == pl.num_programs(1) - 1)
    def _():
        st_ref[0] = z_ref[...]


def wkv_seq(q1, q2, gt, ht, s0):
    B, _, T, C = q1.shape
    nc, nquads = gt.shape[2], gt.shape[3]
    qf = pl.BlockSpec((1, 1, CHUNK, C), lambda b, c: (b, 0, c, 0))
    qb = pl.BlockSpec((1, 1, CHUNK, C), lambda b, c: (b, 1, nc - 1 - c, 0))
    gshape = (1, 1, 1, nquads, QUAD_LANES, QUAD_LANES)
    gf = pl.BlockSpec(gshape, lambda b, c: (b, 0, c, 0, 0, 0))
    gb = pl.BlockSpec(gshape, lambda b, c: (b, 1, nc - 1 - c, 0, 0, 0))
    sspec = pl.BlockSpec((1,) + s0.shape[1:], lambda b, c: (b, 0, 0, 0))
    return pl.pallas_call(
        functools.partial(_wkv_seq_kernel, nquads=nquads),
        out_shape=(jax.ShapeDtypeStruct((B, T, C), F32), jax.ShapeDtypeStruct((B, T, C), F32),
                   jax.ShapeDtypeStruct(s0.shape, F32)),
        grid=(B, nc),
        in_specs=[qf, qb, qf, qb, gf, gb, gf, gb, sspec],
        out_specs=(pl.BlockSpec((1, CHUNK, C), lambda b, c: (b, c, 0)),
                   pl.BlockSpec((1, CHUNK, C), lambda b, c: (b, nc - 1 - c, 0)), sspec),
        scratch_shapes=[pltpu.VMEM(s0.shape[1:], F32)],
        compiler_params=_params(2),
        name="wkv_seq",
    )(q1, q1, q2, q2, gt, gt, ht, ht, s0)


def wkv_chunked(lw2, k2, b2, v, kk, r, s0):
    q1, q2, gt, ht = wkv_chunk_prep(lw2, k2, b2, v, kk, r)
    return wkv_seq(q1, q2, gt, ht, s0)


def _mixer_out_kernel(of_ref, ob_ref, r_ref, v_ref, k2_ref, xg_ref, oat_ref, x_ref, gm_ref,
                      gup_ref, rk_ref, lng_ref, lnb_ref, wo_ref, o_ref):
    ones = _ones_bd(RW_DIM)
    inv = 1.0 / HEAD_DIM
    o = of_ref[0] + ob_ref[0]
    mu = _seg_sum2(o, ones) * inv
    dv = o - mu
    var = _seg_sum2(dv * dv, ones) * inv
    gn = dv * lax.rsqrt(var + GN_EPS) * lng_ref[...] + lnb_ref[...]
    k2 = k2_ref[0]
    k_bonus = (k2[:, :RW_DIM] + k2[:, RW_DIM:]) * 0.5
    bonus = _seg_sum2(r_ref[0] * k_bonus * rk_ref[...], ones) * v_ref[0]
    g = jnp.dot(jax.nn.sigmoid(xg_ref[0]), gup_ref[...], preferred_element_type=F32, precision=HIGHEST)
    y = (gn + bonus) * g
    acc = jnp.dot(oat_ref[0].astype(BF16), wo_ref[:ATTN_DIM, :], preferred_element_type=F32)
    acc = acc + jnp.dot(y.astype(BF16), wo_ref[ATTN_DIM:, :], preferred_element_type=F32)
    o_ref[0] = x_ref[0] + gm_ref[0] * acc


def mixer_out(o_f, o_b, r, v, k2, p, o_at, x, g_m, g_up, r_k, ln_x_g, ln_x_b, w_out):
    B, T, D = x.shape
    tm = _row_tile(T, 256)
    nspec = pl.BlockSpec((1, tm, RW_DIM), lambda b, i: (b, i, 0))
    full = lambda a: pl.BlockSpec(a.shape, lambda b, i: (0,) * a.ndim)
    rk2, lng2, lnb2 = r_k.reshape(1, RW_DIM), ln_x_g.reshape(1, RW_DIM), ln_x_b.reshape(1, RW_DIM)
    return pl.pallas_call(
        _mixer_out_kernel,
        out_shape=jax.ShapeDtypeStruct((B, T, D), F32),
        grid=(B, T // tm),
        in_specs=[nspec, nspec, nspec, nspec,
                  pl.BlockSpec((1, tm, 2 * RW_DIM), lambda b, i: (b, i, 0)),
                  pl.BlockSpec((1, tm, G_LORA), lambda b, i: (b, i, COL_XG // G_LORA)),
                  pl.BlockSpec((1, tm, ATTN_DIM), lambda b, i: (b, i, 0)),
                  pl.BlockSpec((1, tm, D), lambda b, i: (b, i, 0)),
                  pl.BlockSpec((1, 1, D), lambda b, i: (b, 0, 0)),
                  full(g_up), full(rk2), full(lng2), full(lnb2), full(w_out)],
        out_specs=pl.BlockSpec((1, tm, D), lambda b, i: (b, i, 0)),
        compiler_params=_params(2),
        name="mixer_out",
    )(o_f, o_b, r, v, k2, p, o_at, x, g_m, g_up, rk2, lng2, lnb2, w_out)


def _start_row_copies(idx_ref, n, src_hbm, dst_ref, sem):
    def body(r, carry):
        pltpu.make_async_copy(src_hbm.at[pl.ds(idx_ref[0, 0, r], 1)], dst_ref.at[pl.ds(r, 1)], sem).start()
        return carry
    lax.fori_loop(0, n, body, 0, unroll=32)


def _wait_row_copies(n, src_hbm, dst_ref, sem):
    pltpu.make_async_copy(src_hbm.at[pl.ds(0, n)], dst_ref.at[pl.ds(0, n)], sem).wait()


def _moe_kernel(be_ref, tok_ref, tokn_ref, h_hbm, w1f_ref, b1_ref, w2f_ref, b2_ref, o_ref, buf, sems,
                w1_ref, w2_ref, *, f, tm):
    i = pl.program_id(0)
    slot = i % 2

    @pl.when(jnp.logical_or(i == 0, be_ref[i] != be_ref[jnp.maximum(i - 1, 0)]))
    def _():
        w1_ref[...] = w1f_ref[0].astype(BF16)
        w2_ref[...] = w2f_ref[0].astype(BF16)

    @pl.when(i == 0)
    def _():
        _start_row_copies(tok_ref, tm, h_hbm, buf.at[0], sems.at[0])

    @pl.when(i + 1 < pl.num_programs(0))
    def _():
        _start_row_copies(tokn_ref, tm, h_hbm, buf.at[1 - slot], sems.at[1 - slot])

    _wait_row_copies(tm, h_hbm, buf.at[slot], sems.at[slot])
    u = jnp.dot(buf[slot].astype(BF16), w1_ref[0], preferred_element_type=F32) + b1_ref[0, 0]
    x_glu = jnp.minimum(u[:, :f], SWIGLU_LIMIT)
    x_lin = jnp.clip(u[:, f:], -SWIGLU_LIMIT, SWIGLU_LIMIT)
    act = x_glu * jax.nn.sigmoid(SWIGLU_ALPHA * x_glu) * (x_lin + 1.0)
    o_ref[...] = jnp.dot(act.astype(BF16), w2_ref[0], preferred_element_type=F32) + b2_ref[0, 0]


def moe_blocks(block_e, slot_tok, h, w1, b1, w2, b2, layer):
    N, D = h.shape
    nl, E, _, F2 = w1.shape
    f = F2 // 2
    tm = MOE_BLOCK
    rows = slot_tok.shape[0]
    nb = rows // tm
    tok3 = slot_tok.reshape(nb, 1, tm)
    kern = functools.partial(_moe_kernel, f=f, tm=tm)
    grid_spec = pltpu.PrefetchScalarGridSpec(
        num_scalar_prefetch=1,
        grid=(nb,),
        in_specs=[pl.BlockSpec((1, 1, tm), lambda i, be: (i, 0, 0), memory_space=pltpu.SMEM),
                  pl.BlockSpec((1, 1, tm), lambda i, be: (jnp.minimum(i + 1, nb - 1), 0, 0),
                               memory_space=pltpu.SMEM),
                  pl.BlockSpec(memory_space=pl.ANY),
                  pl.BlockSpec((1, 1, D, F2), lambda i, be: (layer, be[i], 0, 0)),
                  pl.BlockSpec((1, 1, 1, F2), lambda i, be: (layer, be[i], 0, 0)),
                  pl.BlockSpec((1, 1, f, D), lambda i, be: (layer, be[i], 0, 0)),
                  pl.BlockSpec((1, 1, 1, D), lambda i, be: (layer, be[i], 0, 0))],
        out_specs=pl.BlockSpec((tm, D), lambda i, be: (i, 0)),
        scratch_shapes=[pltpu.VMEM((2, tm, D), F32), pltpu.SemaphoreType.DMA((2,)),
                        pltpu.VMEM((1, D, F2), BF16), pltpu.VMEM((1, f, D), BF16)],
    )
    return pl.pallas_call(
        kern,
        out_shape=jax.ShapeDtypeStruct((rows, D), F32),
        grid_spec=grid_spec,
        compiler_params=_params(1),
        name="moe_blocks",
    )(block_e, tok3, tok3, h, w1, b1.reshape(nl, E, 1, F2), w2, b2.reshape(nl, E, 1, D))


def _moe_combine_kernel(sl_ref, sln_ref, out_hbm, gate_ref, x_ref, gf_ref, o_ref, buf, sems, *, tb):
    b, i = pl.program_id(0), pl.program_id(1)
    step = b * pl.num_programs(1) + i
    nsteps = pl.num_programs(0) * pl.num_programs(1)
    slot = step % 2
    n = TOP_K * tb

    @pl.when(step == 0)
    def _():
        _start_row_copies(sl_ref, n, out_hbm, buf.at[0], sems.at[0])

    @pl.when(step + 1 < nsteps)
    def _():
        _start_row_copies(sln_ref, n, out_hbm, buf.at[1 - slot], sems.at[1 - slot])

    _wait_row_copies(n, out_hbm, buf.at[slot], sems.at[slot])
    gate = gate_ref[0]
    y = jnp.zeros(x_ref.shape[1:], F32)
    for k in range(TOP_K):
        y = y + gate[:, k:k + 1] * buf[slot, k * tb:(k + 1) * tb, :]
    o_ref[0] = x_ref[0] + gf_ref[0] * y


def moe_combine(out, slots, gates, x, g_f):
    B, T, D = x.shape
    tb = _row_tile(T, 64)
    nt = T // tb
    n = TOP_K * tb
    sl3 = jnp.transpose(slots.reshape(B * nt, tb, TOP_K), (0, 2, 1)).reshape(B * nt, 1, n)
    last = B * nt - 1
    return pl.pallas_call(
        functools.partial(_moe_combine_kernel, tb=tb),
        out_shape=jax.ShapeDtypeStruct((B, T, D), F32),
        grid=(B, nt),
        in_specs=[pl.BlockSpec((1, 1, n), lambda b, i: (b * nt + i, 0, 0), memory_space=pltpu.SMEM),
                  pl.BlockSpec((1, 1, n), lambda b, i: (jnp.minimum(b * nt + i + 1, last), 0, 0),
                               memory_space=pltpu.SMEM),
                  pl.BlockSpec(memory_space=pl.ANY),
                  pl.BlockSpec((1, tb, LANES), lambda b, i: (b, i, 0)),
                  pl.BlockSpec((1, tb, D), lambda b, i: (b, i, 0)),
                  pl.BlockSpec((1, 1, D), lambda b, i: (b, 0, 0))],
        out_specs=pl.BlockSpec((1, tb, D), lambda b, i: (b, i, 0)),
        scratch_shapes=[pltpu.VMEM((2, n, D), F32), pltpu.SemaphoreType.DMA((2,))],
        compiler_params=_params(2),
        name="moe_combine",
    )(sl3, sl3, out, gates, x, g_f)


def _ffn_pre_kernel(x_ref, g_ref, sc_ref, sh_ref, rw_ref, rb_ref, h_ref, te_ref, tg_ref):
    h = _norm_mod(x_ref[0], g_ref[...], sc_ref[0], sh_ref[0])
    h_ref[0] = h
    cur = jnp.dot(h, rw_ref[...], preferred_element_type=F32, precision=HIGHEST) + rb_ref[...]
    n_e = cur.shape[1]
    lane_e = lax.broadcasted_iota(jnp.int32, cur.shape, 1).astype(F32)
    lane = lax.broadcasted_iota(jnp.int32, te_ref.shape[1:], 1)
    vals, idxs = [], []
    for k in range(TOP_K):
        m = jnp.max(cur, axis=1, keepdims=True)
        idx = jnp.min(jnp.where(cur == m, lane_e, float(n_e)), axis=1, keepdims=True)
        vals.append(m)
        idxs.append(idx)
        cur = jnp.where(lane_e == idx, -jnp.inf, cur)
    ex = [jnp.exp(v - vals[0]) for v in vals]
    denom = ex[0]
    for e in ex[1:]:
        denom = denom + e
    te = jnp.zeros(lane.shape, F32)
    tg = jnp.zeros(lane.shape, F32)
    for k in range(TOP_K):
        te = jnp.where(lane == k, idxs[k], te)
        tg = jnp.where(lane == k, ex[k] / denom, tg)
    te_ref[0] = te.astype(jnp.int32)
    tg_ref[0] = tg


def ffn_pre(x, g, sc, sh, router_w, router_b):
    B, T, D = x.shape
    E = router_w.shape[1]
    tm = _row_tile(T, 256)
    return pl.pallas_call(
        _ffn_pre_kernel,
        out_shape=(jax.ShapeDtypeStruct((B, T, D), F32), jax.ShapeDtypeStruct((B, T, LANES), jnp.int32),
                   jax.ShapeDtypeStruct((B, T, LANES), F32)),
        grid=(B, T // tm),
        in_specs=[pl.BlockSpec((1, tm, D), lambda b, i: (b, i, 0)),
                  pl.BlockSpec((1, D), lambda b, i: (0, 0)),
                  pl.BlockSpec((1, 1, D), lambda b, i: (b, 0, 0)),
                  pl.BlockSpec((1, 1, D), lambda b, i: (b, 0, 0)),
                  pl.BlockSpec((D, E), lambda b, i: (0, 0)),
                  pl.BlockSpec((1, E), lambda b, i: (0, 0))],
        out_specs=(pl.BlockSpec((1, tm, D), lambda b, i: (b, i, 0)),
                   pl.BlockSpec((1, tm, LANES), lambda b, i: (b, i, 0)),
                   pl.BlockSpec((1, tm, LANES), lambda b, i: (b, i, 0))),
        compiler_params=_params(2),
        name="ffn_pre",
    )(x, g.reshape(1, D), sc, sh, router_w, router_b.reshape(1, E))


def _rms_kernel(x_ref, g_ref, o_ref):
    x = x_ref[0]
    ms = jnp.mean(x * x, axis=-1, keepdims=True)
    o_ref[0] = x * lax.rsqrt(ms + NORM_EPS) * g_ref[...]


def final_norm(x, g):
    B, T, D = x.shape
    tm = _row_tile(T, 512)
    return pl.pallas_call(
        _rms_kernel,
        out_shape=jax.ShapeDtypeStruct((B, T, D), F32),
        grid=(B, T // tm),
        in_specs=[pl.BlockSpec((1, tm, D), lambda b, i: (b, i, 0)),
                  pl.BlockSpec((1, D), lambda b, i: (0, 0))],
        out_specs=pl.BlockSpec((1, tm, D), lambda b, i: (b, i, 0)),
        compiler_params=_params(2),
        name="final_norm",
    )(x, g.reshape(1, D))


def _route(top_e):
    N = top_e.shape[0]
    n4 = N * TOP_K
    flat_e = top_e.reshape(-1)
    order = jnp.argsort(flat_e)
    e_sorted = flat_e[order]
    counts = jnp.bincount(flat_e, length=N_EXPERTS)
    padded = (counts + MOE_BLOCK - 1) // MOE_BLOCK * MOE_BLOCK
    start = jnp.cumsum(counts) - counts
    pstart = jnp.cumsum(padded) - padded
    dest = (pstart[e_sorted] + jnp.arange(n4, dtype=jnp.int32) - start[e_sorted]).astype(jnp.int32)
    n_blocks = (n4 + MOE_BLOCK - 1) // MOE_BLOCK + N_EXPERTS
    rows = n_blocks * MOE_BLOCK
    ends = jnp.cumsum(padded)
    block_start = jnp.arange(n_blocks, dtype=ends.dtype) * MOE_BLOCK
    block_e = jnp.minimum(jnp.sum(ends[None, :] <= block_start[:, None], axis=1), N_EXPERTS - 1).astype(jnp.int32)
    slot = jnp.arange(rows, dtype=jnp.int32)
    slot_e = jnp.repeat(block_e, MOE_BLOCK)
    off = slot - pstart[slot_e].astype(jnp.int32)
    src = jnp.minimum(start[slot_e].astype(jnp.int32) + off, n4 - 1)
    slot_tok = jnp.where(off < counts[slot_e], order[src] // TOP_K, 0).astype(jnp.int32)
    slot_flat = dest[jnp.argsort(order)]
    return slot_tok, block_e, slot_flat.reshape(N, TOP_K)


def kernel(x, c, ctx, c_ctx, norm_mix_g, norm_ffn_g, w_mod, b_mod, w_in, w_out, q_norm_g, k_norm_g,
           conv_w, w0, w_up, a0, a_up, g_up, k_k, k_a, r_k, ln_x_g, ln_x_b,
           router_w, router_b, e_w1, e_b1, e_w2, e_b2, norm_final_g):
    B, T, D = x.shape
    C = ctx.shape[1]
    depth = w_in.shape[0]
    cos, sin = _rope_tables(T)
    rpad = -(-(B + 1) // SUBLANES) * SUBLANES
    cvecs = jnp.zeros((rpad, D), F32).at[:B].set(c).at[B].set(c_ctx)
    s_zero = jnp.zeros((B, 2 * RW_DIM // QUAD_LANES, QUAD_LANES, QUAD_LANES), F32)
    x_lat, x_ctx = x, ctx
    for l in range(depth):
        last = l == depth - 1
        mods = adaln_all(cvecs, w_mod[l], b_mod[l])
        m_lat = [m[:, None, :] for m in jnp.split(mods[:B], 6, axis=-1)]
        m_ctx = [jnp.broadcast_to(m[None, :, :], (B, 1, D)) for m in jnp.split(mods[B:B + 1], 6, axis=-1)]
        sh_m, sc_m, g_m, sh_f, sc_f, g_f = m_lat
        csh_m, csc_m, cg_m, csh_f, csc_f, cg_f = m_ctx
        w_in_l = w_in[l].astype(BF16)
        w_out_l = w_out[l].astype(BF16)
        p_lat = norm_mod_matmul(x_lat, norm_mix_g[l], sc_m, sh_m, w_in_l)
        p_ctx = norm_mod_matmul(x_ctx, norm_mix_g[l], csc_m, csh_m,
                                w_in_l[:, :STATE_COLS] if last else w_in_l)

        ktc, vc = kv_prep(p_ctx, k_norm_g[l])
        ktl, vl = kv_prep(p_lat, k_norm_g[l], cos, sin)
        o_at = attention(p_lat, q_norm_g[l], ktc, vc, ktl, vl, cos, sin)

        rw = (conv_w[l], w0[l], w_up[l], a0[l], a_up[l], k_k[l], k_a[l])
        w2c, k2c, b2c, v_c, kk_c, r_c = wkv_prep(p_ctx, *rw, has_r=not last)
        w2l, k2l, b2l, v_l, kk_l, r_l = wkv_prep(p_lat, *rw, has_r=True)
        of_c, ob_c, s_c = wkv_chunked(w2c, k2c, b2c, v_c, kk_c, r_c, s_zero)
        of_l, ob_l, _ = wkv_chunked(w2l, k2l, b2l, v_l, kk_l, r_l, s_c)

        ro = (g_up[l], r_k[l], ln_x_g[l], ln_x_b[l], w_out_l)
        x_lat = mixer_out(of_l, ob_l, r_l, v_l, k2l, p_lat, o_at, x_lat, g_m, *ro)
        h_lat, te_lat, tg_lat = ffn_pre(x_lat, norm_ffn_g[l], sc_f, sh_f, router_w[l], router_b[l])
        if last:
            slot_tok, block_e, slot_flat = _route(te_lat.reshape(B * T, LANES)[:, :TOP_K])
            out = moe_blocks(block_e, slot_tok, h_lat.reshape(B * T, D), e_w1, e_b1, e_w2, e_b2, l)
            x_lat = moe_combine(out, slot_flat.reshape(B, T, TOP_K), tg_lat, x_lat, g_f)
        else:
            o_at_c = attention(p_ctx, q_norm_g[l], ktc, vc)
            x_ctx = mixer_out(of_c, ob_c, r_c, v_c, k2c, p_ctx, o_at_c, x_ctx, cg_m, *ro)
            h_ctx, te_ctx, tg_ctx = ffn_pre(x_ctx, norm_ffn_g[l], csc_f, csh_f, router_w[l], router_b[l])
            h_all = jnp.concatenate([h_lat.reshape(B * T, D), h_ctx.reshape(B * C, D)], axis=0)
            te_all = jnp.concatenate([te_lat.reshape(B * T, LANES), te_ctx.reshape(B * C, LANES)], axis=0)
            slot_tok, block_e, slot_flat = _route(te_all[:, :TOP_K])
            out = moe_blocks(block_e, slot_tok, h_all, e_w1, e_b1, e_w2, e_b2, l)
            x_lat = moe_combine(out, slot_flat[:B * T].reshape(B, T, TOP_K), tg_lat, x_lat, g_f)
            x_ctx = moe_combine(out, slot_flat[B * T:].reshape(B, C, TOP_K), tg_ctx, x_ctx, cg_f)
    return final_norm(x_lat, norm_final_g)
```

```python
import functools
import math

import numpy as np
import jax
import jax.numpy as jnp
from jax import lax
from jax.experimental import pallas as pl
from jax.experimental.pallas import tpu as pltpu

F32 = jnp.float32
BF16 = jnp.bfloat16
HIGHEST = lax.Precision.HIGHEST

GRID_W = 64
HEAD_DIM = 64
ATTN_HEADS = 8
ATTN_KV_HEADS = 2
GROUP = ATTN_HEADS // ATTN_KV_HEADS
ATTN_DIM = ATTN_HEADS * HEAD_DIM
KV_DIM = ATTN_KV_HEADS * HEAD_DIM
RW_HEADS = 8
RW_DIM = RW_HEADS * HEAD_DIM
W_LORA = 64
A_LORA = 64
G_LORA = 128
CONV_W = 3
N_EXPERTS = 32
TOP_K = 4
SWIGLU_LIMIT = 7.0
SWIGLU_ALPHA = 1.702
ROPE_THETA = 10000.0
ROPE_PAIRS = HEAD_DIM // 4
MOE_BLOCK = 256
NORM_EPS = 1e-6
GN_EPS = 64e-5
ATTN_SCALE = HEAD_DIM ** -0.5
EXP_M05 = math.exp(-0.5)
COL_K_AT = 0
COL_V_AT = KV_DIM
COL_K_RW = 2 * KV_DIM
COL_V_RW = COL_K_RW + RW_DIM
COL_XW = COL_V_RW + RW_DIM
COL_XA = COL_XW + 2 * W_LORA
STATE_COLS = COL_XA + 2 * A_LORA
COL_Q_AT = STATE_COLS
COL_R_RW = COL_Q_AT + ATTN_DIM
COL_XG = COL_R_RW + RW_DIM

LANES = 128
SUBLANES = 8
QUAD_LANES = 256
CHUNK = 64
CHUNKS_PER_STEP = 4
ATTN_ROW_GROUP = 128
VMEM_LIMIT = 48 * 1024 * 1024


def _row_tile(n, pref):
    t = pref
    while n % t:
        t //= 2
    return t


def _params(n_axes):
    return pltpu.CompilerParams(dimension_semantics=("arbitrary",) * n_axes, vmem_limit_bytes=VMEM_LIMIT)


def _ones_bd(n):
    ra = lax.broadcasted_iota(jnp.int32, (n, n), 0) // HEAD_DIM
    rb = lax.broadcasted_iota(jnp.int32, (n, n), 1) // HEAD_DIM
    return (ra == rb).astype(BF16)


def _seg_sum(x_bf16, ones_bd):
    return jnp.dot(x_bf16, ones_bd, preferred_element_type=F32)


def _seg_sum2(x, ones_bd):
    hi = x.astype(BF16)
    lo = (x - hi.astype(F32)).astype(BF16)
    return _seg_sum(hi, ones_bd) + _seg_sum(lo, ones_bd)


def _adaln_kernel(c_ref, w_ref, b_ref, o_ref):
    c = c_ref[...]
    s = c * jax.nn.sigmoid(c)
    o_ref[...] = jnp.dot(s, w_ref[...], preferred_element_type=F32, precision=HIGHEST) + b_ref[...]


def adaln_all(cvecs, w_mod, b_mod):
    R, D = cvecs.shape
    N = w_mod.shape[1]
    tn = 512
    return pl.pallas_call(
        _adaln_kernel,
        out_shape=jax.ShapeDtypeStruct((R, N), F32),
        grid=(N // tn,),
        in_specs=[pl.BlockSpec((R, D), lambda j: (0, 0)),
                  pl.BlockSpec((D, tn), lambda j: (0, j)),
                  pl.BlockSpec((1, tn), lambda j: (0, j))],
        out_specs=pl.BlockSpec((R, tn), lambda j: (0, j)),
        compiler_params=_params(1),
        name="adaln",
    )(cvecs, w_mod, b_mod.reshape(1, N))


def _norm_mod(x, g, sc, sh):
    ms = jnp.mean(x * x, axis=-1, keepdims=True)
    y = x * lax.rsqrt(ms + NORM_EPS) * g
    return y * (1.0 + sc) + sh


def _norm_mod_matmul_kernel(x_ref, g_ref, sc_ref, sh_ref, w_ref, o_ref):
    h = _norm_mod(x_ref[0], g_ref[...], sc_ref[0], sh_ref[0])
    o_ref[0] = jnp.dot(h.astype(BF16), w_ref[...], preferred_element_type=F32)


def norm_mod_matmul(x, g, sc, sh, w):
    B, T, D = x.shape
    N = w.shape[1]
    tm = _row_tile(T, 256)
    return pl.pallas_call(
        _norm_mod_matmul_kernel,
        out_shape=jax.ShapeDtypeStruct((B, T, N), F32),
        grid=(B, T // tm),
        in_specs=[pl.BlockSpec((1, tm, D), lambda b, i: (b, i, 0)),
                  pl.BlockSpec((1, D), lambda b, i: (0, 0)),
                  pl.BlockSpec((1, 1, D), lambda b, i: (b, 0, 0)),
                  pl.BlockSpec((1, 1, D), lambda b, i: (b, 0, 0)),
                  pl.BlockSpec((D, N), lambda b, i: (0, 0))],
        out_specs=pl.BlockSpec((1, tm, N), lambda b, i: (b, i, 0)),
        compiler_params=_params(2),
        name="norm_mod_matmul",
    )(x, g.reshape(1, D), sc, sh, w)


def _pair_rms(x, g):
    lane = lax.broadcasted_iota(jnp.int32, x.shape, 1)
    first = lane < HEAD_DIM
    xx = x * x
    s0 = jnp.sum(jnp.where(first, xx, 0.0), axis=1, keepdims=True)
    s1 = jnp.sum(jnp.where(first, 0.0, xx), axis=1, keepdims=True)
    ms = jnp.where(first, s0, s1) * (1.0 / HEAD_DIM)
    return x * lax.rsqrt(ms + NORM_EPS) * g


def _pair_rope(x, cos, sin):
    lane = lax.broadcasted_iota(jnp.int32, x.shape, 1)
    partner = jnp.where(lane % (2 * ROPE_PAIRS) < ROPE_PAIRS,
                        pltpu.roll(x, LANES - ROPE_PAIRS, 1), pltpu.roll(x, ROPE_PAIRS, 1))
    return x * cos + partner * sin


def _rope_tables(T):
    rows = T // GRID_W
    row = jnp.repeat(jnp.arange(rows), GRID_W)
    col = jnp.tile(jnp.arange(GRID_W), rows)
    pos = jnp.stack([row, col], axis=-1).astype(F32)
    freqs = ROPE_THETA ** (-jnp.arange(ROPE_PAIRS, dtype=F32) / ROPE_PAIRS)
    ang = pos[:, :, None] * freqs
    cos, sin = jnp.cos(ang), jnp.sin(ang)
    cos_h = jnp.concatenate([cos[:, 0], cos[:, 0], cos[:, 1], cos[:, 1]], axis=-1)
    sin_h = jnp.concatenate([-sin[:, 0], sin[:, 0], -sin[:, 1], sin[:, 1]], axis=-1)
    return jnp.tile(cos_h, (1, 2)), jnp.tile(sin_h, (1, 2))


def _kv_prep_kernel(*refs, rope):
    if rope:
        p_ref, g_ref, cos_ref, sin_ref, kt_ref, v_ref = refs
    else:
        p_ref, g_ref, kt_ref, v_ref = refs
    x = p_ref[0]
    k = _pair_rms(x[:, :KV_DIM], g_ref[...])
    if rope:
        k = _pair_rope(k, cos_ref[...], sin_ref[...])
    v = x[:, KV_DIM:]
    lane = lax.broadcasted_iota(jnp.int32, k.shape, 1)
    first = lane < HEAD_DIM
    kr = pltpu.roll(k, HEAD_DIM, 1)
    vr = pltpu.roll(v, HEAD_DIM, 1)
    kt_ref[0, 0] = jnp.where(first, k, kr).T.astype(BF16)
    kt_ref[0, 1] = jnp.where(first, kr, k).T.astype(BF16)
    ones = jnp.ones(v.shape, F32)
    v_ref[0, 0] = jnp.concatenate([jnp.where(first, v, vr), ones], axis=1).astype(BF16)
    v_ref[0, 1] = jnp.concatenate([jnp.where(first, vr, v), ones], axis=1).astype(BF16)


def kv_prep(p, k_norm_g, cos=None, sin=None):
    B, T, _ = p.shape
    tm = _row_tile(T, 256)
    rope = cos is not None
    g2 = jnp.tile(k_norm_g, 2).reshape(1, LANES)
    in_specs = [pl.BlockSpec((1, tm, 2 * KV_DIM), lambda b, i: (b, i, 0)),
                pl.BlockSpec((1, LANES), lambda b, i: (0, 0))]
    args = [p, g2]
    if rope:
        in_specs += [pl.BlockSpec((tm, LANES), lambda b, i: (i, 0))] * 2
        args += [cos, sin]
    return pl.pallas_call(
        functools.partial(_kv_prep_kernel, rope=rope),
        out_shape=(jax.ShapeDtypeStruct((B, ATTN_KV_HEADS, LANES, T), BF16),
                   jax.ShapeDtypeStruct((B, ATTN_KV_HEADS, T, 2 * LANES), BF16)),
        grid=(B, T // tm),
        in_specs=in_specs,
        out_specs=(pl.BlockSpec((1, ATTN_KV_HEADS, LANES, tm), lambda b, i: (b, 0, 0, i)),
                   pl.BlockSpec((1, ATTN_KV_HEADS, tm, 2 * LANES), lambda b, i: (b, 0, i, 0))),
        compiler_params=_params(2),
        name="kv_prep",
    )(*args)


def _attn_kernel(*refs, tq, tkl, nkl, rope, has_lat, rg):
    refs = list(refs)
    p_ref, g_ref = refs[:2]
    refs = refs[2:]
    if rope:
        cos_ref, sin_ref = refs[:2]
        refs = refs[2:]
    ktc_ref, vc_ref = refs[:2]
    refs = refs[2:]
    if has_lat:
        ktl_ref, vl_ref = refs[:2]
        refs = refs[2:]
    o_ref, m_ref, acc_ref = refs

    q = _pair_rms(p_ref[0], g_ref[...])
    if rope:
        q = _pair_rope(q, cos_ref[...], sin_ref[...])
    q = q * ATTN_SCALE
    lane = lax.broadcasted_iota(jnp.int32, (tq, LANES), 1)
    first = lane < HEAD_DIM
    qq = jnp.concatenate([jnp.where(first, q, 0.0), jnp.where(first, 0.0, q)], axis=0).astype(BF16)

    m_ref[...] = jnp.full(m_ref.shape, -jnp.inf, F32)
    acc_ref[...] = jnp.zeros(acc_ref.shape, F32)
    groups = [slice(g * rg, (g + 1) * rg) for g in range(2 * tq // rg)]

    def chunk(kt, vv):
        s = [jnp.dot(qq[g], kt, preferred_element_type=F32) for g in groups]
        m_old = [m_ref[g] for g in groups]
        m_new = [jnp.maximum(mo, jnp.max(x, axis=-1, keepdims=True)) for mo, x in zip(m_old, s)]
        alpha = [jnp.exp(mo - mn) for mo, mn in zip(m_old, m_new)]
        p = [jnp.exp(x - jnp.concatenate([mn] * (x.shape[1] // LANES), axis=1)).astype(BF16)
             for x, mn in zip(s, m_new)]
        pv = [jnp.dot(x, vv, preferred_element_type=F32) for x in p]
        for g, al, mn, x in zip(groups, alpha, m_new, pv):
            acc_ref[g] = jnp.concatenate([al, al], axis=1) * acc_ref[g] + x
            m_ref[g] = mn

    chunk(ktc_ref[0, 0], vc_ref[0, 0])
    if has_lat:
        def body(j, carry):
            ks = pl.multiple_of(j * tkl, tkl)
            chunk(ktl_ref[0, 0, :, pl.ds(ks, tkl)], vl_ref[0, 0, pl.ds(ks, tkl), :])
            return carry
        lax.fori_loop(0, nkl, body, 0)

    acc = acc_ref[...]
    o = acc[:, :LANES] / acc[:, LANES:]
    o_ref[0] = jnp.where(first, o[:tq], o[tq:])


def attention(p, q_norm_g, ktc, vc, ktl=None, vl=None, cos=None, sin=None):
    B, T, _ = p.shape
    C = ktc.shape[-1]
    rope = cos is not None
    has_lat = ktl is not None
    tq = _row_tile(T, 512)
    pairs = ATTN_HEADS // 2
    pairs_per_kv = GROUP // 2
    qblk0 = COL_Q_AT // LANES
    g2 = jnp.tile(q_norm_g, 2).reshape(1, LANES)
    in_specs = [pl.BlockSpec((1, tq, LANES), lambda b, h, i: (b, i, qblk0 + h)),
                pl.BlockSpec((1, LANES), lambda b, h, i: (0, 0))]
    args = [p, g2]
    if rope:
        in_specs += [pl.BlockSpec((tq, LANES), lambda b, h, i: (i, 0))] * 2
        args += [cos, sin]
    in_specs += [pl.BlockSpec((1, 1, LANES, C), lambda b, h, i: (b, h // pairs_per_kv, 0, 0)),
                 pl.BlockSpec((1, 1, C, 2 * LANES), lambda b, h, i: (b, h // pairs_per_kv, 0, 0))]
    args += [ktc, vc]
    tkl = nkl = 0
    if has_lat:
        S = ktl.shape[-1]
        tkl = _row_tile(S, 2048)
        nkl = S // tkl
        in_specs += [pl.BlockSpec((1, 1, LANES, S), lambda b, h, i: (b, h // pairs_per_kv, 0, 0)),
                     pl.BlockSpec((1, 1, S, 2 * LANES), lambda b, h, i: (b, h // pairs_per_kv, 0, 0))]
        args += [ktl, vl]
    kern = functools.partial(_attn_kernel, tq=tq, tkl=tkl, nkl=nkl, rope=rope, has_lat=has_lat,
                             rg=_row_tile(tq, ATTN_ROW_GROUP))
    return pl.pallas_call(
        kern,
        out_shape=jax.ShapeDtypeStruct((B, T, ATTN_DIM), F32),
        grid=(B, pairs, T // tq),
        in_specs=in_specs,
        out_specs=pl.BlockSpec((1, tq, LANES), lambda b, h, i: (b, i, h)),
        scratch_shapes=[pltpu.VMEM((2 * tq, LANES), F32), pltpu.VMEM((2 * tq, 2 * LANES), F32)],
        compiler_params=_params(3),
        name="attention",
    )(*args)


def _wkv_prep_kernel(p_ref, pp_ref, pn_ref, cw_ref, w0_ref, wup_ref, a0_ref, aup_ref, kk_ref, ka_ref,
                     w2_ref, k2_ref, b2_ref, v_ref, kkn_ref, r_ref, *, tm, has_r):
    i = pl.program_id(1)
    n = pl.num_programs(1)
    x = p_ref[0]
    prev = pp_ref[0][SUBLANES - 1:SUBLANES, :] * (i > 0).astype(F32)
    nxt = pn_ref[0][0:1, :] * (i < n - 1).astype(F32)
    row = lax.broadcasted_iota(jnp.int32, (tm, RW_DIM), 0)

    def conv(col, wcol):
        u = x[:, col:col + RW_DIM]
        up = jnp.where(row == 0, prev[:, col:col + RW_DIM], pltpu.roll(u, 1, 0))
        un = jnp.where(row == tm - 1, nxt[:, col:col + RW_DIM], pltpu.roll(u, tm - 1, 0))
        w = cw_ref[:, wcol:wcol + RW_DIM]
        return up * w[0:1] + u * w[1:2] + un * w[2:3]

    k = conv(COL_K_RW, RW_DIM)
    v = conv(COL_V_RW, 2 * RW_DIM)
    v_ref[0] = v
    r_ref[0] = conv(COL_R_RW, 0) if has_r else jnp.zeros((tm, RW_DIM), F32)
    kk = k * kk_ref[...]
    ss = _seg_sum2(kk * kk, _ones_bd(RW_DIM))
    kk = kk * lax.rsqrt(jnp.maximum(ss, 1e-24))
    kkn_ref[0] = kk
    xw = jnp.tanh(x[:, COL_XW:COL_XW + 2 * W_LORA])
    xa = x[:, COL_XA:COL_XA + 2 * A_LORA]
    for d in range(2):
        wl = w0_ref[d:d + 1, :] + _mm3(xw[:, d * W_LORA:(d + 1) * W_LORA], wup_ref[d])
        decay = -EXP_M05 * jax.nn.sigmoid(wl)
        a = jax.nn.sigmoid(a0_ref[d:d + 1, :] + _mm3(xa[:, d * A_LORA:(d + 1) * A_LORA], aup_ref[d]))
        sl = slice(d * RW_DIM, (d + 1) * RW_DIM)
        w2_ref[0, :, sl] = decay
        k2_ref[0, :, sl] = k * (1.0 + (a - 1.0) * ka_ref[...])
        b2_ref[0, :, sl] = kk * a


def wkv_prep(p, conv_w, w0, w_up, a0, a_up, k_k, k_a, has_r):
    B, T, NP = p.shape
    tm = _row_tile(T, 256)
    nb8 = tm // SUBLANES
    last8 = T // SUBLANES - 1
    full = lambda a: pl.BlockSpec(a.shape, lambda b, i: (0,) * a.ndim)
    kk2, ka2 = k_k.reshape(1, RW_DIM), k_a.reshape(1, RW_DIM)
    wide = jax.ShapeDtypeStruct((B, T, 2 * RW_DIM), F32)
    narrow = jax.ShapeDtypeStruct((B, T, RW_DIM), F32)
    wspec = pl.BlockSpec((1, tm, 2 * RW_DIM), lambda b, i: (b, i, 0))
    nspec = pl.BlockSpec((1, tm, RW_DIM), lambda b, i: (b, i, 0))
    return pl.pallas_call(
        functools.partial(_wkv_prep_kernel, tm=tm, has_r=has_r),
        out_shape=(wide, wide, wide, narrow, narrow, narrow),
        grid=(B, T // tm),
        in_specs=[pl.BlockSpec((1, tm, NP), lambda b, i: (b, i, 0)),
                  pl.BlockSpec((1, SUBLANES, NP), lambda b, i: (b, jnp.maximum(i * nb8 - 1, 0), 0)),
                  pl.BlockSpec((1, SUBLANES, NP), lambda b, i: (b, jnp.minimum((i + 1) * nb8, last8), 0)),
                  full(conv_w), full(w0), full(w_up), full(a0), full(a_up), full(kk2), full(ka2)],
        out_specs=(wspec, wspec, wspec, nspec, nspec, nspec),
        compiler_params=_params(2),
        name="wkv_prep",
    )(p, p, p, conv_w, w0, w_up, a0, a_up, kk2, ka2)


def _split_bf16(x):
    hi = x.astype(BF16)
    return hi, (x - hi.astype(F32)).astype(BF16)


def _mm3(a, b, nt=False, passes=3):
    dn = (((1,), (1 if nt else 0,)), ((), ()))
    if passes == 1:
        return lax.dot_general(a.astype(BF16), b.astype(BF16), dn, preferred_element_type=F32)
    ah, al = _split_bf16(a)
    m = a.shape[0]
    if passes == 2:
        top = lax.dot_general(jnp.concatenate([ah, al], axis=0), b.astype(BF16), dn, preferred_element_type=F32)
        return top[:m] + top[m:]
    bh, bl = _split_bf16(b)
    top = lax.dot_general(jnp.concatenate([ah, al], axis=0), bh, dn, preferred_element_type=F32)
    return top[:m] + top[m:] + lax.dot_general(ah, bl, dn, preferred_element_type=F32)


def _chunk_terms(chains):
    L = CHUNK
    nch = range(len(chains))
    row = lax.broadcasted_iota(jnp.int32, (L, QUAD_LANES), 0)
    col = lax.broadcasted_iota(jnp.int32, (L, QUAD_LANES), 1) % HEAD_DIM
    tr = lax.broadcasted_iota(jnp.int32, (L, L), 0)
    tc = lax.broadcasted_iota(jnp.int32, (L, L), 1)
    masks = {True: (col > row, col >= row, (tc >= tr).astype(F32)),
             False: (col < row, col <= row, (tc <= tr).astype(F32))}
    eye_l = (col == row).astype(F32)
    ra = lax.broadcasted_iota(jnp.int32, (QUAD_LANES, QUAD_LANES), 0)
    rb = lax.broadcasted_iota(jnp.int32, (QUAD_LANES, QUAD_LANES), 1)
    bd = ra // HEAD_DIM == rb // HEAD_DIM
    eye_q = (ra == rb).astype(F32)

    def bdiag(y):
        return jnp.where(bd, jnp.concatenate([y] * (QUAD_LANES // L), axis=0), 0.0)

    def bdm(x, y, passes):
        ys = [bdiag(y[:, i:i + QUAD_LANES]) for i in range(0, y.shape[1], QUAD_LANES)]
        return _mm3(x, ys[0] if len(ys) == 1 else jnp.concatenate(ys, axis=1), passes=passes)

    lw, kd, bb, v, kk, r, back = (list(x) for x in zip(*chains))
    strict = [masks[b][0] for b in back]
    incl = [masks[b][1] for b in back]
    cs = [_mm3(masks[back[c]][2], lw[c]) for c in nch]
    g_last = [jnp.exp(cs[c][0:1] if back[c] else cs[c][L - 1:L]) for c in nch]
    at = [-kk[c] * jnp.exp(cs[c] - lw[c]) for c in nch]
    g_inv = [jnp.exp(-cs[c]) for c in nch]
    bt = [bb[c] * g_inv[c] for c in nch]
    kt = [kd[c] * g_inv[c] for c in nch]
    rt = [r[c] * jnp.exp(cs[c]) for c in nch]
    nt_dims = (((1,), (1,)), ((), ()))
    a_split = [_split_bf16(at[c]) for c in nch]
    b_split = [_split_bf16(bdiag(bt[c])) for c in nch]
    lhs = [jnp.concatenate([a_split[c][0], a_split[c][1], rt[c].astype(BF16)], axis=0) for c in nch]
    pb = [lax.dot_general(lhs[c], b_split[c][0], nt_dims, preferred_element_type=F32) for c in nch]
    pb_lo = [lax.dot_general(a_split[c][0], b_split[c][1], nt_dims, preferred_element_type=F32) for c in nch]
    pk = [lax.dot_general(lhs[c], bdiag(kt[c]).astype(BF16), nt_dims, preferred_element_type=F32) for c in nch]
    n = [jnp.where(strict[c], pb[c][:L] + pb[c][L:2 * L] + pb_lo[c], 0.0) for c in nch]
    m_rb = [jnp.where(incl[c], pb[c][2 * L:], 0.0) for c in nch]
    m_ak = [jnp.where(strict[c], pk[c][:L] + pk[c][L:2 * L], 0.0) for c in nch]
    m_rk = [jnp.where(incl[c], pk[c][2 * L:], 0.0) for c in nch]
    steps = int(math.log2(L)) - 1
    p = [bdm(n[c], n[c], 3) for c in nch]
    t_inv = [eye_l + n[c] for c in nch]
    for k in range(steps):
        if k + 1 < steps:
            both = [bdm(jnp.concatenate([p[c], t_inv[c]], axis=0), p[c], 1) for c in nch]
            p = [both[c][:L] for c in nch]
            t_inv = [t_inv[c] + both[c][L:] for c in nch]
        else:
            t_inv = [t_inv[c] + bdm(t_inv[c], p[c], 1) for c in nch]
    xy = [bdm(jnp.concatenate([m_ak[c], m_rk[c]], axis=0), v[c], 1) for c in nch]
    w = [bdm(t_inv[c], jnp.concatenate([at[c], xy[c][:L]], axis=1), 2) for c in nch]
    qq = [bdm(m_rb[c], w[c], 1) for c in nch]
    q1 = [rt[c] + qq[c][:, :QUAD_LANES] for c in nch]
    q2 = [xy[c][L:] + qq[c][:, QUAD_LANES:] for c in nch]
    g = [jnp.where(bd, (eye_q + _mm3(w[c][:, :QUAD_LANES].T, bt[c], passes=2)) * g_last[c], 0.0) for c in nch]
    h = [jnp.where(bd, _mm3(jnp.concatenate([w[c][:, QUAD_LANES:], v[c]], axis=0).T,
                            jnp.concatenate([bt[c], kt[c]], axis=0), passes=2) * g_last[c], 0.0) for c in nch]
    return [(q1[c], q2[c], g[c].T, h[c].T) for c in nch]


def _wkv_chunk_kernel(lw_ref, k_ref, b_ref, v_ref, kk_ref, r_ref, q1_ref, q2_ref, gt_ref, ht_ref, *, nquads, cps):
    keys = [(j, d, q) for j in range(cps) for d in range(2) for q in range(nquads)]
    chains = []
    for j, d, q in keys:
        rows = slice(j * CHUNK, (j + 1) * CHUNK)
        sl = slice(q * QUAD_LANES, (q + 1) * QUAD_LANES)
        sd = slice(d * RW_DIM + q * QUAD_LANES, d * RW_DIM + (q + 1) * QUAD_LANES)
        chains.append((lw_ref[0, rows, sd], k_ref[0, rows, sd], b_ref[0, rows, sd],
                       v_ref[0, rows, sl], kk_ref[0, rows, sl], r_ref[0, rows, sl], d == 1))
    res = _chunk_terms(chains)
    for (j, d, q), (q1, q2, gt, ht) in zip(keys, res):
        rows = slice(j * CHUNK, (j + 1) * CHUNK)
        sl = slice(q * QUAD_LANES, (q + 1) * QUAD_LANES)
        q1_ref[0, d, rows, sl] = q1
        q2_ref[0, d, rows, sl] = q2
        gt_ref[0, d, j, q] = gt
        ht_ref[0, d, j, q] = ht


def wkv_chunk_prep(lw2, k2, b2, v, kk, r):
    B, T, C = v.shape
    nquads = C // QUAD_LANES
    nc = T // CHUNK
    cps = math.gcd(nc, CHUNKS_PER_STEP)
    tr = cps * CHUNK
    wide = pl.BlockSpec((1, tr, 2 * C), lambda b, c: (b, c, 0))
    narrow = pl.BlockSpec((1, tr, C), lambda b, c: (b, c, 0))
    qspec = pl.BlockSpec((1, 2, tr, C), lambda b, c: (b, 0, c, 0))
    gspec = pl.BlockSpec((1, 2, cps, nquads, QUAD_LANES, QUAD_LANES), lambda b, c: (b, 0, c, 0, 0, 0))
    qshape = jax.ShapeDtypeStruct((B, 2, T, C), F32)
    gshape = jax.ShapeDtypeStruct((B, 2, nc, nquads, QUAD_LANES, QUAD_LANES), F32)
    return pl.pallas_call(
        functools.partial(_wkv_chunk_kernel, nquads=nquads, cps=cps),
        out_shape=(qshape, qshape, gshape, gshape),
        grid=(B, nc // cps),
        in_specs=[wide, wide, wide, narrow, narrow, narrow],
        out_specs=(qspec, qspec, gspec, gspec),
        compiler_params=_params(2),
        name="wkv_chunk_prep",
    )(lw2, k2, b2, v, kk, r)


def _wkv_seq_kernel(q1f_ref, q1b_ref, q2f_ref, q2b_ref, gtf_ref, gtb_ref, htf_ref, htb_ref, s0_ref,
                    of_ref, ob_ref, st_ref, z_ref, *, nquads):
    c = pl.program_id(1)

    @pl.when(c == 0)
    def _():
        z_ref[...] = s0_ref[0]

    dirs = ((q1f_ref, q2f_ref, gtf_ref, htf_ref, of_ref), (q1b_ref, q2b_ref, gtb_ref, htb_ref, ob_ref))
    chains = [(d, q, slice(q * QUAD_LANES, (q + 1) * QUAD_LANES)) for d in range(2) for q in range(nquads)]
    zs = [_split_bf16(z_ref[d * nquads + q]) for d, q, sl in chains]
    gs = [_split_bf16(dirs[d][2][0, 0, 0, q]) for d, q, sl in chains]
    lhs = [jnp.concatenate([dirs[d][0][0, 0, :, sl].astype(BF16), gs[i][0], gs[i][1]], axis=0)
           for i, (d, q, sl) in enumerate(chains)]
    top = [jnp.dot(lhs[i], zs[i][0], preferred_element_type=F32) for i in range(len(chains))]
    low = [jnp.dot(gs[i][0], zs[i][1], preferred_element_type=F32) for i in range(len(chains))]
    for i, (d, q, sl) in enumerate(chains):
        dirs[d][4][0, :, sl] = top[i][:CHUNK] + dirs[d][1][0, 0, :, sl]
        z_ref[d * nquads + q] = (top[i][CHUNK:CHUNK + QUAD_LANES] + top[i][CHUNK + QUAD_LANES:] + low[i]
                                 + dirs[d][3][0, 0, 0, q])

    @pl.when(c == pl.num_programs(1) - 1)
    def _():
        st_ref[0] = z_ref[...]


def wkv_seq(q1, q2, gt, ht, s0):
    B, _, T, C = q1.shape
    nc, nquads = gt.shape[2], gt.shape[3]
    qf = pl.BlockSpec((1, 1, CHUNK, C), lambda b, c: (b, 0, c, 0))
    qb = pl.BlockSpec((1, 1, CHUNK, C), lambda b, c: (b, 1, nc - 1 - c, 0))
    gshape = (1, 1, 1, nquads, QUAD_LANES, QUAD_LANES)
    gf = pl.BlockSpec(gshape, lambda b, c: (b, 0, c, 0, 0, 0))
    gb = pl.BlockSpec(gshape, lambda b, c: (b, 1, nc - 1 - c, 0, 0, 0))
    sspec = pl.BlockSpec((1,) + s0.shape[1:], lambda b, c: (b, 0, 0, 0))
    return pl.pallas_call(
        functools.partial(_wkv_seq_kernel, nquads=nquads),
        out_shape=(jax.ShapeDtypeStruct((B, T, C), F32), jax.ShapeDtypeStruct((B, T, C), F32),
                   jax.ShapeDtypeStruct(s0.shape, F32)),
        grid=(B, nc),
        in_specs=[qf, qb, qf, qb, gf, gb, gf, gb, sspec],
        out_specs=(pl.BlockSpec((1, CHUNK, C), lambda b, c: (b, c, 0)),
                   pl.BlockSpec((1, CHUNK, C), lambda b, c: (b, nc - 1 - c, 0)), sspec),
        scratch_shapes=[pltpu.VMEM(s0.shape[1:], F32)],
        compiler_params=_params(2),
        name="wkv_seq",
    )(q1, q1, q2, q2, gt, gt, ht, ht, s0)


def wkv_chunked(lw2, k2, b2, v, kk, r, s0):
    q1, q2, gt, ht = wkv_chunk_prep(lw2, k2, b2, v, kk, r)
    return wkv_seq(q1, q2, gt, ht, s0)


def _mixer_out_kernel(of_ref, ob_ref, r_ref, v_ref, k2_ref, xg_ref, oat_ref, x_ref, gm_ref,
                      gup_ref, rk_ref, lng_ref, lnb_ref, wo_ref, o_ref):
    ones = _ones_bd(RW_DIM)
    inv = 1.0 / HEAD_DIM
    o = of_ref[0] + ob_ref[0]
    mu = _seg_sum2(o, ones) * inv
    dv = o - mu
    var = _seg_sum2(dv * dv, ones) * inv
    gn = dv * lax.rsqrt(var + GN_EPS) * lng_ref[...] + lnb_ref[...]
    k2 = k2_ref[0]
    k_bonus = (k2[:, :RW_DIM] + k2[:, RW_DIM:]) * 0.5
    bonus = _seg_sum2(r_ref[0] * k_bonus * rk_ref[...], ones) * v_ref[0]
    g = _mm3(jax.nn.sigmoid(xg_ref[0]), gup_ref[...])
    y = (gn + bonus) * g
    acc = jnp.dot(oat_ref[0].astype(BF16), wo_ref[:ATTN_DIM, :], preferred_element_type=F32)
    acc = acc + jnp.dot(y.astype(BF16), wo_ref[ATTN_DIM:, :], preferred_element_type=F32)
    o_ref[0] = x_ref[0] + gm_ref[0] * acc


def mixer_out(o_f, o_b, r, v, k2, p, o_at, x, g_m, g_up, r_k, ln_x_g, ln_x_b, w_out):
    B, T, D = x.shape
    tm = _row_tile(T, 256)
    nspec = pl.BlockSpec((1, tm, RW_DIM), lambda b, i: (b, i, 0))
    full = lambda a: pl.BlockSpec(a.shape, lambda b, i: (0,) * a.ndim)
    rk2, lng2, lnb2 = r_k.reshape(1, RW_DIM), ln_x_g.reshape(1, RW_DIM), ln_x_b.reshape(1, RW_DIM)
    return pl.pallas_call(
        _mixer_out_kernel,
        out_shape=jax.ShapeDtypeStruct((B, T, D), F32),
        grid=(B, T // tm),
        in_specs=[nspec, nspec, nspec, nspec,
                  pl.BlockSpec((1, tm, 2 * RW_DIM), lambda b, i: (b, i, 0)),
                  pl.BlockSpec((1, tm, G_LORA), lambda b, i: (b, i, COL_XG // G_LORA)),
                  pl.BlockSpec((1, tm, ATTN_DIM), lambda b, i: (b, i, 0)),
                  pl.BlockSpec((1, tm, D), lambda b, i: (b, i, 0)),
                  pl.BlockSpec((1, 1, D), lambda b, i: (b, 0, 0)),
                  full(g_up), full(rk2), full(lng2), full(lnb2), full(w_out)],
        out_specs=pl.BlockSpec((1, tm, D), lambda b, i: (b, i, 0)),
        compiler_params=_params(2),
        name="mixer_out",
    )(o_f, o_b, r, v, k2, p, o_at, x, g_m, g_up, rk2, lng2, lnb2, w_out)


def _start_row_copies(idx_ref, n, src_hbm, dst_ref, sem):
    def body(r, carry):
        pltpu.make_async_copy(src_hbm.at[pl.ds(idx_ref[0, 0, r], 1)], dst_ref.at[pl.ds(r, 1)], sem).start()
        return carry
    lax.fori_loop(0, n, body, 0, unroll=32)


def _wait_row_copies(n, src_hbm, dst_ref, sem):
    pltpu.make_async_copy(src_hbm.at[pl.ds(0, n)], dst_ref.at[pl.ds(0, n)], sem).wait()


def _moe_kernel(be_ref, tok_ref, tokn_ref, h_hbm, w1f_ref, b1_ref, w2f_ref, b2_ref, o_ref, buf, sems,
                w1_ref, w2_ref, *, f, tm):
    i = pl.program_id(0)
    slot = i % 2

    @pl.when(jnp.logical_or(i == 0, be_ref[i] != be_ref[jnp.maximum(i - 1, 0)]))
    def _():
        w1_ref[...] = w1f_ref[0].astype(BF16)
        w2_ref[...] = w2f_ref[0].astype(BF16)

    @pl.when(i == 0)
    def _():
        _start_row_copies(tok_ref, tm, h_hbm, buf.at[0], sems.at[0])

    @pl.when(i + 1 < pl.num_programs(0))
    def _():
        _start_row_copies(tokn_ref, tm, h_hbm, buf.at[1 - slot], sems.at[1 - slot])

    _wait_row_copies(tm, h_hbm, buf.at[slot], sems.at[slot])
    u = jnp.dot(buf[slot].astype(BF16), w1_ref[0], preferred_element_type=F32) + b1_ref[0, 0]
    x_glu = jnp.minimum(u[:, :f], SWIGLU_LIMIT)
    x_lin = jnp.clip(u[:, f:], -SWIGLU_LIMIT, SWIGLU_LIMIT)
    act = x_glu * jax.nn.sigmoid(SWIGLU_ALPHA * x_glu) * (x_lin + 1.0)
    o_ref[...] = jnp.dot(act.astype(BF16), w2_ref[0], preferred_element_type=F32) + b2_ref[0, 0]


def moe_blocks(block_e, slot_tok, h, w1, b1, w2, b2, layer):
    N, D = h.shape
    nl, E, _, F2 = w1.shape
    f = F2 // 2
    tm = MOE_BLOCK
    rows = slot_tok.shape[0]
    nb = rows // tm
    tok3 = slot_tok.reshape(nb, 1, tm)
    kern = functools.partial(_moe_kernel, f=f, tm=tm)
    grid_spec = pltpu.PrefetchScalarGridSpec(
        num_scalar_prefetch=1,
        grid=(nb,),
        in_specs=[pl.BlockSpec((1, 1, tm), lambda i, be: (i, 0, 0), memory_space=pltpu.SMEM),
                  pl.BlockSpec((1, 1, tm), lambda i, be: (jnp.minimum(i + 1, nb - 1), 0, 0),
                               memory_space=pltpu.SMEM),
                  pl.BlockSpec(memory_space=pl.ANY),
                  pl.BlockSpec((1, 1, D, F2), lambda i, be: (layer, be[i], 0, 0)),
                  pl.BlockSpec((1, 1, 1, F2), lambda i, be: (layer, be[i], 0, 0)),
                  pl.BlockSpec((1, 1, f, D), lambda i, be: (layer, be[i], 0, 0)),
                  pl.BlockSpec((1, 1, 1, D), lambda i, be: (layer, be[i], 0, 0))],
        out_specs=pl.BlockSpec((tm, D), lambda i, be: (i, 0)),
        scratch_shapes=[pltpu.VMEM((2, tm, D), F32), pltpu.SemaphoreType.DMA((2,)),
                        pltpu.VMEM((1, D, F2), BF16), pltpu.VMEM((1, f, D), BF16)],
    )
    return pl.pallas_call(
        kern,
        out_shape=jax.ShapeDtypeStruct((rows, D), F32),
        grid_spec=grid_spec,
        compiler_params=_params(1),
        name="moe_blocks",
    )(block_e, tok3, tok3, h, w1, b1.reshape(nl, E, 1, F2), w2, b2.reshape(nl, E, 1, D))


def _moe_combine_kernel(sl_ref, sln_ref, out_hbm, gate_ref, x_ref, gf_ref, *rest, tb, final_norm):
    if final_norm:
        ng_ref, o_ref, buf, sems = rest
    else:
        o_ref, buf, sems = rest
    b, i = pl.program_id(0), pl.program_id(1)
    step = b * pl.num_programs(1) + i
    nsteps = pl.num_programs(0) * pl.num_programs(1)
    slot = step % 2
    n = TOP_K * tb

    @pl.when(step == 0)
    def _():
        _start_row_copies(sl_ref, n, out_hbm, buf.at[0], sems.at[0])

    @pl.when(step + 1 < nsteps)
    def _():
        _start_row_copies(sln_ref, n, out_hbm, buf.at[1 - slot], sems.at[1 - slot])

    _wait_row_copies(n, out_hbm, buf.at[slot], sems.at[slot])
    gate = gate_ref[0]
    y = jnp.zeros(x_ref.shape[1:], F32)
    for k in range(TOP_K):
        y = y + gate[:, k:k + 1] * buf[slot, k * tb:(k + 1) * tb, :]
    z = x_ref[0] + gf_ref[0] * y
    if final_norm:
        z = z * lax.rsqrt(jnp.mean(z * z, axis=-1, keepdims=True) + NORM_EPS) * ng_ref[...]
    o_ref[0] = z


def moe_combine(out, slots, gates, x, g_f, final_g=None):
    B, T, D = x.shape
    final_norm = final_g is not None
    extra_specs = [pl.BlockSpec((1, D), lambda b, i: (0, 0))] if final_norm else []
    extra_args = [final_g.reshape(1, D)] if final_norm else []
    tb = _row_tile(T, 64)
    nt = T // tb
    n = TOP_K * tb
    sl3 = jnp.transpose(slots.reshape(B * nt, tb, TOP_K), (0, 2, 1)).reshape(B * nt, 1, n)
    last = B * nt - 1
    return pl.pallas_call(
        functools.partial(_moe_combine_kernel, tb=tb, final_norm=final_norm),
        out_shape=jax.ShapeDtypeStruct((B, T, D), F32),
        grid=(B, nt),
        in_specs=[pl.BlockSpec((1, 1, n), lambda b, i: (b * nt + i, 0, 0), memory_space=pltpu.SMEM),
                  pl.BlockSpec((1, 1, n), lambda b, i: (jnp.minimum(b * nt + i + 1, last), 0, 0),
                               memory_space=pltpu.SMEM),
                  pl.BlockSpec(memory_space=pl.ANY),
                  pl.BlockSpec((1, tb, LANES), lambda b, i: (b, i, 0)),
                  pl.BlockSpec((1, tb, D), lambda b, i: (b, i, 0)),
                  pl.BlockSpec((1, 1, D), lambda b, i: (b, 0, 0))] + extra_specs,
        out_specs=pl.BlockSpec((1, tb, D), lambda b, i: (b, i, 0)),
        scratch_shapes=[pltpu.VMEM((2, n, D), F32), pltpu.SemaphoreType.DMA((2,))],
        compiler_params=_params(2),
        name="moe_combine",
    )(sl3, sl3, out, gates, x, g_f, *extra_args)


def _ffn_pre_kernel(x_ref, g_ref, sc_ref, sh_ref, rw_ref, rb_ref, h_ref, te_ref, tg_ref):
    h = _norm_mod(x_ref[0], g_ref[...], sc_ref[0], sh_ref[0])
    h_ref[0] = h
    cur = _mm3(h, rw_ref[...]) + rb_ref[...]
    n_e = cur.shape[1]
    lane_e = lax.broadcasted_iota(jnp.int32, cur.shape, 1).astype(F32)
    lane = lax.broadcasted_iota(jnp.int32, te_ref.shape[1:], 1)
    vals, idxs = [], []
    for k in range(TOP_K):
        m = jnp.max(cur, axis=1, keepdims=True)
        idx = jnp.min(jnp.where(cur == m, lane_e, float(n_e)), axis=1, keepdims=True)
        vals.append(m)
        idxs.append(idx)
        cur = jnp.where(lane_e == idx, -jnp.inf, cur)
    ex = [jnp.exp(v - vals[0]) for v in vals]
    denom = ex[0]
    for e in ex[1:]:
        denom = denom + e
    te = jnp.zeros(lane.shape, F32)
    tg = jnp.zeros(lane.shape, F32)
    for k in range(TOP_K):
        te = jnp.where(lane == k, idxs[k], te)
        tg = jnp.where(lane == k, ex[k] / denom, tg)
    te_ref[0] = te.astype(jnp.int32)
    tg_ref[0] = tg


def ffn_pre(x, g, sc, sh, router_w, router_b):
    B, T, D = x.shape
    E = router_w.shape[1]
    tm = _row_tile(T, 256)
    return pl.pallas_call(
        _ffn_pre_kernel,
        out_shape=(jax.ShapeDtypeStruct((B, T, D), F32), jax.ShapeDtypeStruct((B, T, LANES), jnp.int32),
                   jax.ShapeDtypeStruct((B, T, LANES), F32)),
        grid=(B, T // tm),
        in_specs=[pl.BlockSpec((1, tm, D), lambda b, i: (b, i, 0)),
                  pl.BlockSpec((1, D), lambda b, i: (0, 0)),
                  pl.BlockSpec((1, 1, D), lambda b, i: (b, 0, 0)),
                  pl.BlockSpec((1, 1, D), lambda b, i: (b, 0, 0)),
                  pl.BlockSpec((D, E), lambda b, i: (0, 0)),
                  pl.BlockSpec((1, E), lambda b, i: (0, 0))],
        out_specs=(pl.BlockSpec((1, tm, D), lambda b, i: (b, i, 0)),
                   pl.BlockSpec((1, tm, LANES), lambda b, i: (b, i, 0)),
                   pl.BlockSpec((1, tm, LANES), lambda b, i: (b, i, 0))),
        compiler_params=_params(2),
        name="ffn_pre",
    )(x, g.reshape(1, D), sc, sh, router_w, router_b.reshape(1, E))


def _route(top_e):
    N = top_e.shape[0]
    n4 = N * TOP_K
    flat_e = top_e.reshape(-1)
    order = jnp.argsort(flat_e)
    e_sorted = flat_e[order]
    counts = jnp.bincount(flat_e, length=N_EXPERTS)
    padded = (counts + MOE_BLOCK - 1) // MOE_BLOCK * MOE_BLOCK
    start = jnp.cumsum(counts) - counts
    pstart = jnp.cumsum(padded) - padded
    dest = (pstart[e_sorted] + jnp.arange(n4, dtype=jnp.int32) - start[e_sorted]).astype(jnp.int32)
    n_blocks = (n4 + MOE_BLOCK - 1) // MOE_BLOCK + N_EXPERTS
    rows = n_blocks * MOE_BLOCK
    ends = jnp.cumsum(padded)
    block_start = jnp.arange(n_blocks, dtype=ends.dtype) * MOE_BLOCK
    block_e = jnp.minimum(jnp.sum(ends[None, :] <= block_start[:, None], axis=1), N_EXPERTS - 1).astype(jnp.int32)
    slot = jnp.arange(rows, dtype=jnp.int32)
    slot_e = jnp.repeat(block_e, MOE_BLOCK)
    off = slot - pstart[slot_e].astype(jnp.int32)
    src = jnp.minimum(start[slot_e].astype(jnp.int32) + off, n4 - 1)
    slot_tok = jnp.where(off < counts[slot_e], order[src] // TOP_K, 0).astype(jnp.int32)
    slot_flat = dest[jnp.argsort(order)]
    return slot_tok, block_e, slot_flat.reshape(N, TOP_K)


def kernel(x, c, ctx, c_ctx, norm_mix_g, norm_ffn_g, w_mod, b_mod, w_in, w_out, q_norm_g, k_norm_g,
           conv_w, w0, w_up, a0, a_up, g_up, k_k, k_a, r_k, ln_x_g, ln_x_b,
           router_w, router_b, e_w1, e_b1, e_w2, e_b2, norm_final_g):
    B, T, D = x.shape
    C = ctx.shape[1]
    depth = w_in.shape[0]
    cos, sin = _rope_tables(T)
    rpad = -(-(B + 1) // SUBLANES) * SUBLANES
    cvecs = jnp.zeros((rpad, D), F32).at[:B].set(c).at[B].set(c_ctx)
    s_zero = jnp.zeros((B, 2 * RW_DIM // QUAD_LANES, QUAD_LANES, QUAD_LANES), F32)
    x_lat, x_ctx = x, ctx
    for l in range(depth):
        last = l == depth - 1
        mods = adaln_all(cvecs, w_mod[l], b_mod[l])
        m_lat = [m[:, None, :] for m in jnp.split(mods[:B], 6, axis=-1)]
        m_ctx = [jnp.broadcast_to(m[None, :, :], (B, 1, D)) for m in jnp.split(mods[B:B + 1], 6, axis=-1)]
        sh_m, sc_m, g_m, sh_f, sc_f, g_f = m_lat
        csh_m, csc_m, cg_m, csh_f, csc_f, cg_f = m_ctx
        w_in_l = w_in[l].astype(BF16)
        w_out_l = w_out[l].astype(BF16)
        p_lat = norm_mod_matmul(x_lat, norm_mix_g[l], sc_m, sh_m, w_in_l)
        p_ctx = norm_mod_matmul(x_ctx, norm_mix_g[l], csc_m, csh_m,
                                w_in_l[:, :STATE_COLS] if last else w_in_l)

        ktc, vc = kv_prep(p_ctx, k_norm_g[l])
        ktl, vl = kv_prep(p_lat, k_norm_g[l], cos, sin)
        o_at = attention(p_lat, q_norm_g[l], ktc, vc, ktl, vl, cos, sin)

        rw = (conv_w[l], w0[l], w_up[l], a0[l], a_up[l], k_k[l], k_a[l])
        w2c, k2c, b2c, v_c, kk_c, r_c = wkv_prep(p_ctx, *rw, has_r=not last)
        w2l, k2l, b2l, v_l, kk_l, r_l = wkv_prep(p_lat, *rw, has_r=True)
        of_c, ob_c, s_c = wkv_chunked(w2c, k2c, b2c, v_c, kk_c, r_c, s_zero)
        of_l, ob_l, _ = wkv_chunked(w2l, k2l, b2l, v_l, kk_l, r_l, s_c)

        ro = (g_up[l], r_k[l], ln_x_g[l], ln_x_b[l], w_out_l)
        x_lat = mixer_out(of_l, ob_l, r_l, v_l, k2l, p_lat, o_at, x_lat, g_m, *ro)
        h_lat, te_lat, tg_lat = ffn_pre(x_lat, norm_ffn_g[l], sc_f, sh_f, router_w[l], router_b[l])
        if last:
            slot_tok, block_e, slot_flat = _route(te_lat.reshape(B * T, LANES)[:, :TOP_K])
            out = moe_blocks(block_e, slot_tok, h_lat.reshape(B * T, D), e_w1, e_b1, e_w2, e_b2, l)
            x_lat = moe_combine(out, slot_flat.reshape(B, T, TOP_K), tg_lat, x_lat, g_f, final_g=norm_final_g)
        else:
            o_at_c = attention(p_ctx, q_norm_g[l], ktc, vc)
            x_ctx = mixer_out(of_c, ob_c, r_c, v_c, k2c, p_ctx, o_at_c, x_ctx, cg_m, *ro)
            h_ctx, te_ctx, tg_ctx = ffn_pre(x_ctx, norm_ffn_g[l], csc_f, csh_f, router_w[l], router_b[l])
            h_all = jnp.concatenate([h_lat.reshape(B * T, D), h_ctx.reshape(B * C, D)], axis=0)
            te_all = jnp.concatenate([te_lat.reshape(B * T, LANES), te_ctx.reshape(B * C, LANES)], axis=0)
            slot_tok, block_e, slot_flat = _route(te_all[:, :TOP_K])
            out = moe_blocks(block_e, slot_tok, h_all, e_w1, e_b1, e_w2, e_b2, l)
            x_lat = moe_combine(out, slot_flat[:B * T].reshape(B, T, TOP_K), tg_lat, x_lat, g_f)
            x_ctx = moe_combine(out, slot_flat[B * T:].reshape(B, C, TOP_K), tg_ctx, x_ctx, cg_f)
    return x_lat
```

```python
import functools
import math

import numpy as np
import jax
import jax.numpy as jnp
from jax import lax
from jax.experimental import pallas as pl
from jax.experimental.pallas import tpu as pltpu

F32 = jnp.float32
BF16 = jnp.bfloat16
HIGHEST = lax.Precision.HIGHEST

GRID_W = 64
HEAD_DIM = 64
ATTN_HEADS = 8
ATTN_KV_HEADS = 2
GROUP = ATTN_HEADS // ATTN_KV_HEADS
ATTN_DIM = ATTN_HEADS * HEAD_DIM
KV_DIM = ATTN_KV_HEADS * HEAD_DIM
RW_HEADS = 8
RW_DIM = RW_HEADS * HEAD_DIM
W_LORA = 64
A_LORA = 64
G_LORA = 128
CONV_W = 3
N_EXPERTS = 32
TOP_K = 4
SWIGLU_LIMIT = 7.0
SWIGLU_ALPHA = 1.702
ROPE_THETA = 10000.0
ROPE_PAIRS = HEAD_DIM // 4
MOE_BLOCK = 256
NORM_EPS = 1e-6
GN_EPS = 64e-5
ATTN_SCALE = HEAD_DIM ** -0.5
EXP_M05 = math.exp(-0.5)
COL_K_AT = 0
COL_V_AT = KV_DIM
COL_K_RW = 2 * KV_DIM
COL_V_RW = COL_K_RW + RW_DIM
COL_XW = COL_V_RW + RW_DIM
COL_XA = COL_XW + 2 * W_LORA
STATE_COLS = COL_XA + 2 * A_LORA
COL_Q_AT = STATE_COLS
COL_R_RW = COL_Q_AT + ATTN_DIM
COL_XG = COL_R_RW + RW_DIM

LANES = 128
SUBLANES = 8
QUAD_LANES = 256
CHUNK = 64
CHUNKS_PER_STEP = 4
ATTN_ROW_GROUP = 128
VMEM_LIMIT = 48 * 1024 * 1024


def _row_tile(n, pref):
    t = pref
    while n % t:
        t //= 2
    return t


def _params(n_axes):
    return pltpu.CompilerParams(dimension_semantics=("arbitrary",) * n_axes, vmem_limit_bytes=VMEM_LIMIT)


def _ones_bd(n):
    ra = lax.broadcasted_iota(jnp.int32, (n, n), 0) // HEAD_DIM
    rb = lax.broadcasted_iota(jnp.int32, (n, n), 1) // HEAD_DIM
    return (ra == rb).astype(BF16)


def _seg_sum(x_bf16, ones_bd):
    return jnp.dot(x_bf16, ones_bd, preferred_element_type=F32)


def _seg_sum2(x, ones_bd):
    hi = x.astype(BF16)
    lo = (x - hi.astype(F32)).astype(BF16)
    return _seg_sum(hi, ones_bd) + _seg_sum(lo, ones_bd)


def _adaln_kernel(c_ref, w_ref, b_ref, o_ref):
    c = c_ref[...]
    s = c * jax.nn.sigmoid(c)
    o_ref[...] = jnp.dot(s, w_ref[...], preferred_element_type=F32, precision=HIGHEST) + b_ref[...]


def adaln_all(cvecs, w_mod, b_mod):
    R, D = cvecs.shape
    N = w_mod.shape[1]
    tn = 512
    return pl.pallas_call(
        _adaln_kernel,
        out_shape=jax.ShapeDtypeStruct((R, N), F32),
        grid=(N // tn,),
        in_specs=[pl.BlockSpec((R, D), lambda j: (0, 0)),
                  pl.BlockSpec((D, tn), lambda j: (0, j)),
                  pl.BlockSpec((1, tn), lambda j: (0, j))],
        out_specs=pl.BlockSpec((R, tn), lambda j: (0, j)),
        compiler_params=_params(1),
        name="adaln",
    )(cvecs, w_mod, b_mod.reshape(1, N))


def _norm_mod(x, g, sc, sh):
    ms = jnp.mean(x * x, axis=-1, keepdims=True)
    y = x * lax.rsqrt(ms + NORM_EPS) * g
    return y * (1.0 + sc) + sh


def _norm_mod_matmul_kernel(x_ref, g_ref, sc_ref, sh_ref, w_ref, o_ref):
    h = _norm_mod(x_ref[0], g_ref[...], sc_ref[0], sh_ref[0])
    o_ref[0] = jnp.dot(h.astype(BF16), w_ref[...], preferred_element_type=F32)


def norm_mod_matmul(x, g, sc, sh, w):
    B, T, D = x.shape
    N = w.shape[1]
    tm = _row_tile(T, 256)
    return pl.pallas_call(
        _norm_mod_matmul_kernel,
        out_shape=jax.ShapeDtypeStruct((B, T, N), F32),
        grid=(B, T // tm),
        in_specs=[pl.BlockSpec((1, tm, D), lambda b, i: (b, i, 0)),
                  pl.BlockSpec((1, D), lambda b, i: (0, 0)),
                  pl.BlockSpec((1, 1, D), lambda b, i: (b, 0, 0)),
                  pl.BlockSpec((1, 1, D), lambda b, i: (b, 0, 0)),
                  pl.BlockSpec((D, N), lambda b, i: (0, 0))],
        out_specs=pl.BlockSpec((1, tm, N), lambda b, i: (b, i, 0)),
        compiler_params=_params(2),
        name="norm_mod_matmul",
    )(x, g.reshape(1, D), sc, sh, w)


def _pair_rms(x, g):
    lane = lax.broadcasted_iota(jnp.int32, x.shape, 1)
    first = lane < HEAD_DIM
    xx = x * x
    s0 = jnp.sum(jnp.where(first, xx, 0.0), axis=1, keepdims=True)
    s1 = jnp.sum(jnp.where(first, 0.0, xx), axis=1, keepdims=True)
    ms = jnp.where(first, s0, s1) * (1.0 / HEAD_DIM)
    return x * lax.rsqrt(ms + NORM_EPS) * g


def _pair_rope(x, cos, sin):
    lane = lax.broadcasted_iota(jnp.int32, x.shape, 1)
    partner = jnp.where(lane % (2 * ROPE_PAIRS) < ROPE_PAIRS,
                        pltpu.roll(x, LANES - ROPE_PAIRS, 1), pltpu.roll(x, ROPE_PAIRS, 1))
    return x * cos + partner * sin


def _rope_tables(T):
    rows = T // GRID_W
    row = jnp.repeat(jnp.arange(rows), GRID_W)
    col = jnp.tile(jnp.arange(GRID_W), rows)
    pos = jnp.stack([row, col], axis=-1).astype(F32)
    freqs = ROPE_THETA ** (-jnp.arange(ROPE_PAIRS, dtype=F32) / ROPE_PAIRS)
    ang = pos[:, :, None] * freqs
    cos, sin = jnp.cos(ang), jnp.sin(ang)
    cos_h = jnp.concatenate([cos[:, 0], cos[:, 0], cos[:, 1], cos[:, 1]], axis=-1)
    sin_h = jnp.concatenate([-sin[:, 0], sin[:, 0], -sin[:, 1], sin[:, 1]], axis=-1)
    return jnp.tile(cos_h, (1, 2)), jnp.tile(sin_h, (1, 2))


def _kv_prep_kernel(*refs, rope):
    if rope:
        p_ref, g_ref, cos_ref, sin_ref, kt_ref, v_ref = refs
    else:
        p_ref, g_ref, kt_ref, v_ref = refs
    x = p_ref[0]
    k = _pair_rms(x[:, :KV_DIM], g_ref[...])
    if rope:
        k = _pair_rope(k, cos_ref[...], sin_ref[...])
    v = x[:, KV_DIM:]
    lane = lax.broadcasted_iota(jnp.int32, k.shape, 1)
    first = lane < HEAD_DIM
    kr = pltpu.roll(k, HEAD_DIM, 1)
    vr = pltpu.roll(v, HEAD_DIM, 1)
    kt_ref[0, 0] = jnp.where(first, k, kr).T.astype(BF16)
    kt_ref[0, 1] = jnp.where(first, kr, k).T.astype(BF16)
    ones = jnp.ones(v.shape, F32)
    v_ref[0, 0] = jnp.concatenate([jnp.where(first, v, vr), ones], axis=1).astype(BF16)
    v_ref[0, 1] = jnp.concatenate([jnp.where(first, vr, v), ones], axis=1).astype(BF16)


def kv_prep(p, k_norm_g, cos=None, sin=None):
    B, T, _ = p.shape
    tm = _row_tile(T, 256)
    rope = cos is not None
    g2 = jnp.tile(k_norm_g, 2).reshape(1, LANES)
    in_specs = [pl.BlockSpec((1, tm, 2 * KV_DIM), lambda b, i: (b, i, 0)),
                pl.BlockSpec((1, LANES), lambda b, i: (0, 0))]
    args = [p, g2]
    if rope:
        in_specs += [pl.BlockSpec((tm, LANES), lambda b, i: (i, 0))] * 2
        args += [cos, sin]
    return pl.pallas_call(
        functools.partial(_kv_prep_kernel, rope=rope),
        out_shape=(jax.ShapeDtypeStruct((B, ATTN_KV_HEADS, LANES, T), BF16),
                   jax.ShapeDtypeStruct((B, ATTN_KV_HEADS, T, 2 * LANES), BF16)),
        grid=(B, T // tm),
        in_specs=in_specs,
        out_specs=(pl.BlockSpec((1, ATTN_KV_HEADS, LANES, tm), lambda b, i: (b, 0, 0, i)),
                   pl.BlockSpec((1, ATTN_KV_HEADS, tm, 2 * LANES), lambda b, i: (b, 0, i, 0))),
        compiler_params=_params(2),
        name="kv_prep",
    )(*args)


def _attn_kernel(*refs, tq, tkl, nkl, rope, has_lat, rg):
    refs = list(refs)
    p_ref, g_ref = refs[:2]
    refs = refs[2:]
    if rope:
        cos_ref, sin_ref = refs[:2]
        refs = refs[2:]
    ktc_ref, vc_ref = refs[:2]
    refs = refs[2:]
    if has_lat:
        ktl_ref, vl_ref = refs[:2]
        refs = refs[2:]
    o_ref, m_ref, acc_ref = refs

    q = _pair_rms(p_ref[0], g_ref[...])
    if rope:
        q = _pair_rope(q, cos_ref[...], sin_ref[...])
    q = q * ATTN_SCALE
    lane = lax.broadcasted_iota(jnp.int32, (tq, LANES), 1)
    first = lane < HEAD_DIM
    qq = jnp.concatenate([jnp.where(first, q, 0.0), jnp.where(first, 0.0, q)], axis=0).astype(BF16)

    m_ref[...] = jnp.full(m_ref.shape, -jnp.inf, F32)
    acc_ref[...] = jnp.zeros(acc_ref.shape, F32)
    groups = [slice(g * rg, (g + 1) * rg) for g in range(2 * tq // rg)]

    def chunk(kt, vv):
        s = [jnp.dot(qq[g], kt, preferred_element_type=F32) for g in groups]
        m_old = [m_ref[g] for g in groups]
        m_new = [jnp.maximum(mo, jnp.max(x, axis=-1, keepdims=True)) for mo, x in zip(m_old, s)]
        alpha = [jnp.exp(mo - mn) for mo, mn in zip(m_old, m_new)]
        p = [jnp.exp(x - jnp.concatenate([mn] * (x.shape[1] // LANES), axis=1)).astype(BF16)
             for x, mn in zip(s, m_new)]
        pv = [jnp.dot(x, vv, preferred_element_type=F32) for x in p]
        for g, al, mn, x in zip(groups, alpha, m_new, pv):
            acc_ref[g] = jnp.concatenate([al, al], axis=1) * acc_ref[g] + x
            m_ref[g] = mn

    chunk(ktc_ref[0, 0], vc_ref[0, 0])
    if has_lat:
        def body(j, carry):
            ks = pl.multiple_of(j * tkl, tkl)
            chunk(ktl_ref[0, 0, :, pl.ds(ks, tkl)], vl_ref[0, 0, pl.ds(ks, tkl), :])
            return carry
        lax.fori_loop(0, nkl, body, 0)

    acc = acc_ref[...]
    o = acc[:, :LANES] / acc[:, LANES:]
    o_ref[0] = jnp.where(first, o[:tq], o[tq:])


def attention(p, q_norm_g, ktc, vc, ktl=None, vl=None, cos=None, sin=None):
    B, T, _ = p.shape
    C = ktc.shape[-1]
    rope = cos is not None
    has_lat = ktl is not None
    tq = _row_tile(T, 512)
    pairs = ATTN_HEADS // 2
    pairs_per_kv = GROUP // 2
    qblk0 = COL_Q_AT // LANES
    g2 = jnp.tile(q_norm_g, 2).reshape(1, LANES)
    in_specs = [pl.BlockSpec((1, tq, LANES), lambda b, h, i: (b, i, qblk0 + h)),
                pl.BlockSpec((1, LANES), lambda b, h, i: (0, 0))]
    args = [p, g2]
    if rope:
        in_specs += [pl.BlockSpec((tq, LANES), lambda b, h, i: (i, 0))] * 2
        args += [cos, sin]
    in_specs += [pl.BlockSpec((1, 1, LANES, C), lambda b, h, i: (b, h // pairs_per_kv, 0, 0)),
                 pl.BlockSpec((1, 1, C, 2 * LANES), lambda b, h, i: (b, h // pairs_per_kv, 0, 0))]
    args += [ktc, vc]
    tkl = nkl = 0
    if has_lat:
        S = ktl.shape[-1]
        tkl = _row_tile(S, 2048)
        nkl = S // tkl
        in_specs += [pl.BlockSpec((1, 1, LANES, S), lambda b, h, i: (b, h // pairs_per_kv, 0, 0)),
                     pl.BlockSpec((1, 1, S, 2 * LANES), lambda b, h, i: (b, h // pairs_per_kv, 0, 0))]
        args += [ktl, vl]
    kern = functools.partial(_attn_kernel, tq=tq, tkl=tkl, nkl=nkl, rope=rope, has_lat=has_lat,
                             rg=_row_tile(tq, ATTN_ROW_GROUP))
    return pl.pallas_call(
        kern,
        out_shape=jax.ShapeDtypeStruct((B, T, ATTN_DIM), F32),
        grid=(B, pairs, T // tq),
        in_specs=in_specs,
        out_specs=pl.BlockSpec((1, tq, LANES), lambda b, h, i: (b, i, h)),
        scratch_shapes=[pltpu.VMEM((2 * tq, LANES), F32), pltpu.VMEM((2 * tq, 2 * LANES), F32)],
        compiler_params=_params(3),
        name="attention",
    )(*args)


def _wkv_prep_kernel(p_ref, pp_ref, pn_ref, cw_ref, w0_ref, wup_ref, a0_ref, aup_ref, kk_ref, ka_ref,
                     w2_ref, k2_ref, b2_ref, v_ref, kkn_ref, r_ref, *, tm, has_r):
    i = pl.program_id(1)
    n = pl.num_programs(1)
    x = p_ref[0]
    prev = pp_ref[0][SUBLANES - 1:SUBLANES, :] * (i > 0).astype(F32)
    nxt = pn_ref[0][0:1, :] * (i < n - 1).astype(F32)
    row = lax.broadcasted_iota(jnp.int32, (tm, RW_DIM), 0)

    def conv(col, wcol):
        u = x[:, col:col + RW_DIM]
        up = jnp.where(row == 0, prev[:, col:col + RW_DIM], pltpu.roll(u, 1, 0))
        un = jnp.where(row == tm - 1, nxt[:, col:col + RW_DIM], pltpu.roll(u, tm - 1, 0))
        w = cw_ref[:, wcol:wcol + RW_DIM]
        return up * w[0:1] + u * w[1:2] + un * w[2:3]

    k = conv(COL_K_RW, RW_DIM)
    v = conv(COL_V_RW, 2 * RW_DIM)
    v_ref[0] = v
    r_ref[0] = conv(COL_R_RW, 0) if has_r else jnp.zeros((tm, RW_DIM), F32)
    kk = k * kk_ref[...]
    ss = _seg_sum2(kk * kk, _ones_bd(RW_DIM))
    kk = kk * lax.rsqrt(jnp.maximum(ss, 1e-24))
    kkn_ref[0] = kk
    xw = jnp.tanh(x[:, COL_XW:COL_XW + 2 * W_LORA])
    xa = x[:, COL_XA:COL_XA + 2 * A_LORA]
    for d in range(2):
        wl = w0_ref[d:d + 1, :] + _mm3(xw[:, d * W_LORA:(d + 1) * W_LORA], wup_ref[d])
        decay = -EXP_M05 * jax.nn.sigmoid(wl)
        a = jax.nn.sigmoid(a0_ref[d:d + 1, :] + _mm3(xa[:, d * A_LORA:(d + 1) * A_LORA], aup_ref[d]))
        sl = slice(d * RW_DIM, (d + 1) * RW_DIM)
        w2_ref[0, :, sl] = decay
        k2_ref[0, :, sl] = k * (1.0 + (a - 1.0) * ka_ref[...])
        b2_ref[0, :, sl] = kk * a


def wkv_prep(p, conv_w, w0, w_up, a0, a_up, k_k, k_a, has_r):
    B, T, NP = p.shape
    tm = _row_tile(T, 256)
    nb8 = tm // SUBLANES
    last8 = T // SUBLANES - 1
    full = lambda a: pl.BlockSpec(a.shape, lambda b, i: (0,) * a.ndim)
    kk2, ka2 = k_k.reshape(1, RW_DIM), k_a.reshape(1, RW_DIM)
    wide = jax.ShapeDtypeStruct((B, T, 2 * RW_DIM), F32)
    narrow = jax.ShapeDtypeStruct((B, T, RW_DIM), F32)
    wspec = pl.BlockSpec((1, tm, 2 * RW_DIM), lambda b, i: (b, i, 0))
    nspec = pl.BlockSpec((1, tm, RW_DIM), lambda b, i: (b, i, 0))
    return pl.pallas_call(
        functools.partial(_wkv_prep_kernel, tm=tm, has_r=has_r),
        out_shape=(wide, wide, wide, narrow, narrow, narrow),
        grid=(B, T // tm),
        in_specs=[pl.BlockSpec((1, tm, NP), lambda b, i: (b, i, 0)),
                  pl.BlockSpec((1, SUBLANES, NP), lambda b, i: (b, jnp.maximum(i * nb8 - 1, 0), 0)),
                  pl.BlockSpec((1, SUBLANES, NP), lambda b, i: (b, jnp.minimum((i + 1) * nb8, last8), 0)),
                  full(conv_w), full(w0), full(w_up), full(a0), full(a_up), full(kk2), full(ka2)],
        out_specs=(wspec, wspec, wspec, nspec, nspec, nspec),
        compiler_params=_params(2),
        name="wkv_prep",
    )(p, p, p, conv_w, w0, w_up, a0, a_up, kk2, ka2)


def _split_bf16(x):
    hi = x.astype(BF16)
    return hi, (x - hi.astype(F32)).astype(BF16)


def _mm3(a, b, nt=False, passes=3):
    dn = (((1,), (1 if nt else 0,)), ((), ()))
    if passes == 1:
        return lax.dot_general(a.astype(BF16), b.astype(BF16), dn, preferred_element_type=F32)
    ah, al = _split_bf16(a)
    m = a.shape[0]
    if passes == 2:
        top = lax.dot_general(jnp.concatenate([ah, al], axis=0), b.astype(BF16), dn, preferred_element_type=F32)
        return top[:m] + top[m:]
    bh, bl = _split_bf16(b)
    top = lax.dot_general(jnp.concatenate([ah, al], axis=0), bh, dn, preferred_element_type=F32)
    return top[:m] + top[m:] + lax.dot_general(ah, bl, dn, preferred_element_type=F32)


def _chunk_terms(chains):
    L = CHUNK
    nch = range(len(chains))
    row = lax.broadcasted_iota(jnp.int32, (L, QUAD_LANES), 0)
    col = lax.broadcasted_iota(jnp.int32, (L, QUAD_LANES), 1) % HEAD_DIM
    tr = lax.broadcasted_iota(jnp.int32, (L, L), 0)
    tc = lax.broadcasted_iota(jnp.int32, (L, L), 1)
    masks = {True: (col > row, col >= row, (tc >= tr).astype(F32)),
             False: (col < row, col <= row, (tc <= tr).astype(F32))}
    eye_l = (col == row).astype(F32)
    ra = lax.broadcasted_iota(jnp.int32, (QUAD_LANES, QUAD_LANES), 0)
    rb = lax.broadcasted_iota(jnp.int32, (QUAD_LANES, QUAD_LANES), 1)
    bd = ra // HEAD_DIM == rb // HEAD_DIM
    eye_q = (ra == rb).astype(F32)

    def bdiag(y):
        return jnp.where(bd, jnp.concatenate([y] * (QUAD_LANES // L), axis=0), 0.0)

    def bdm(x, y, passes):
        ys = [bdiag(y[:, i:i + QUAD_LANES]) for i in range(0, y.shape[1], QUAD_LANES)]
        return _mm3(x, ys[0] if len(ys) == 1 else jnp.concatenate(ys, axis=1), passes=passes)

    lw, kd, bb, v, kk, r, back = (list(x) for x in zip(*chains))
    strict = [masks[b][0] for b in back]
    incl = [masks[b][1] for b in back]
    cs = [_mm3(masks[back[c]][2], lw[c]) for c in nch]
    g_last = [jnp.exp(cs[c][0:1] if back[c] else cs[c][L - 1:L]) for c in nch]
    at = [-kk[c] * jnp.exp(cs[c] - lw[c]) for c in nch]
    g_inv = [jnp.exp(-cs[c]) for c in nch]
    bt = [bb[c] * g_inv[c] for c in nch]
    kt = [kd[c] * g_inv[c] for c in nch]
    rt = [r[c] * jnp.exp(cs[c]) for c in nch]
    nt_dims = (((1,), (1,)), ((), ()))
    a_split = [_split_bf16(at[c]) for c in nch]
    b_split = [_split_bf16(bdiag(bt[c])) for c in nch]
    lhs = [jnp.concatenate([a_split[c][0], a_split[c][1], rt[c].astype(BF16)], axis=0) for c in nch]
    pb = [lax.dot_general(lhs[c], b_split[c][0], nt_dims, preferred_element_type=F32) for c in nch]
    pb_lo = [lax.dot_general(a_split[c][0], b_split[c][1], nt_dims, preferred_element_type=F32) for c in nch]
    pk = [lax.dot_general(lhs[c], bdiag(kt[c]).astype(BF16), nt_dims, preferred_element_type=F32) for c in nch]
    n = [jnp.where(strict[c], pb[c][:L] + pb[c][L:2 * L] + pb_lo[c], 0.0) for c in nch]
    m_rb = [jnp.where(incl[c], pb[c][2 * L:], 0.0) for c in nch]
    m_ak = [jnp.where(strict[c], pk[c][:L] + pk[c][L:2 * L], 0.0) for c in nch]
    m_rk = [jnp.where(incl[c], pk[c][2 * L:], 0.0) for c in nch]
    steps = int(math.log2(L)) - 1
    p = [bdm(n[c], n[c], 3) for c in nch]
    t_inv = [eye_l + n[c] for c in nch]
    for k in range(steps):
        if k + 1 < steps:
            both = [bdm(jnp.concatenate([p[c], t_inv[c]], axis=0), p[c], 1) for c in nch]
            p = [both[c][:L] for c in nch]
            t_inv = [t_inv[c] + both[c][L:] for c in nch]
        else:
            t_inv = [t_inv[c] + bdm(t_inv[c], p[c], 1) for c in nch]
    xy = [bdm(jnp.concatenate([m_ak[c], m_rk[c]], axis=0), v[c], 1) for c in nch]
    w = [bdm(t_inv[c], jnp.concatenate([at[c], xy[c][:L]], axis=1), 2) for c in nch]
    qq = [bdm(m_rb[c], w[c], 1) for c in nch]
    q1 = [rt[c] + qq[c][:, :QUAD_LANES] for c in nch]
    q2 = [xy[c][L:] + qq[c][:, QUAD_LANES:] for c in nch]
    g = [jnp.where(bd, (eye_q + _mm3(w[c][:, :QUAD_LANES].T, bt[c], passes=2)) * g_last[c], 0.0) for c in nch]
    h = [jnp.where(bd, _mm3(jnp.concatenate([w[c][:, QUAD_LANES:], v[c]], axis=0).T,
                            jnp.concatenate([bt[c], kt[c]], axis=0), passes=2) * g_last[c], 0.0) for c in nch]
    def fold(x):
        return sum(x[i:i + HEAD_DIM] for i in range(0, QUAD_LANES, HEAD_DIM))

    return [(q1[c], q2[c], fold(g[c].T), fold(h[c].T)) for c in nch]


def _wkv_chunk_kernel(lw_ref, k_ref, b_ref, v_ref, kk_ref, r_ref, q1_ref, q2_ref, gt_ref, ht_ref, *, nquads, cps):
    keys = [(j, d, q) for j in range(cps) for d in range(2) for q in range(nquads)]
    chains = []
    for j, d, q in keys:
        rows = slice(j * CHUNK, (j + 1) * CHUNK)
        sl = slice(q * QUAD_LANES, (q + 1) * QUAD_LANES)
        sd = slice(d * RW_DIM + q * QUAD_LANES, d * RW_DIM + (q + 1) * QUAD_LANES)
        chains.append((lw_ref[0, rows, sd], k_ref[0, rows, sd], b_ref[0, rows, sd],
                       v_ref[0, rows, sl], kk_ref[0, rows, sl], r_ref[0, rows, sl], d == 1))
    res = _chunk_terms(chains)
    for (j, d, q), (q1, q2, gt, ht) in zip(keys, res):
        rows = slice(j * CHUNK, (j + 1) * CHUNK)
        sl = slice(q * QUAD_LANES, (q + 1) * QUAD_LANES)
        q1_ref[0, d, rows, sl] = q1
        q2_ref[0, d, rows, sl] = q2
        gt_ref[0, d, j, q] = gt
        ht_ref[0, d, j, q] = ht


def wkv_chunk_prep(lw2, k2, b2, v, kk, r):
    B, T, C = v.shape
    nquads = C // QUAD_LANES
    nc = T // CHUNK
    cps = math.gcd(nc, CHUNKS_PER_STEP)
    tr = cps * CHUNK
    wide = pl.BlockSpec((1, tr, 2 * C), lambda b, c: (b, c, 0))
    narrow = pl.BlockSpec((1, tr, C), lambda b, c: (b, c, 0))
    qspec = pl.BlockSpec((1, 2, tr, C), lambda b, c: (b, 0, c, 0))
    gspec = pl.BlockSpec((1, 2, cps, nquads, HEAD_DIM, QUAD_LANES), lambda b, c: (b, 0, c, 0, 0, 0))
    qshape = jax.ShapeDtypeStruct((B, 2, T, C), F32)
    gshape = jax.ShapeDtypeStruct((B, 2, nc, nquads, HEAD_DIM, QUAD_LANES), F32)
    return pl.pallas_call(
        functools.partial(_wkv_chunk_kernel, nquads=nquads, cps=cps),
        out_shape=(qshape, qshape, gshape, gshape),
        grid=(B, nc // cps),
        in_specs=[wide, wide, wide, narrow, narrow, narrow],
        out_specs=(qspec, qspec, gspec, gspec),
        compiler_params=_params(2),
        name="wkv_chunk_prep",
    )(lw2, k2, b2, v, kk, r)


def _wkv_seq_kernel(q1f_ref, q1b_ref, q2f_ref, q2b_ref, gtf_ref, gtb_ref, htf_ref, htb_ref, s0_ref,
                    of_ref, ob_ref, st_ref, z_ref, *, nquads):
    c = pl.program_id(1)

    @pl.when(c == 0)
    def _():
        z_ref[...] = s0_ref[0]

    dirs = ((q1f_ref, q2f_ref, gtf_ref, htf_ref, of_ref), (q1b_ref, q2b_ref, gtb_ref, htb_ref, ob_ref))
    chains = [(d, q, slice(q * QUAD_LANES, (q + 1) * QUAD_LANES)) for d in range(2) for q in range(nquads)]
    ra = lax.broadcasted_iota(jnp.int32, (QUAD_LANES, QUAD_LANES), 0) // HEAD_DIM
    rb = lax.broadcasted_iota(jnp.int32, (QUAD_LANES, QUAD_LANES), 1) // HEAD_DIM

    def unfold(x):
        return jnp.where(ra == rb, jnp.concatenate([x] * (QUAD_LANES // HEAD_DIM), axis=0), 0.0)

    zs = [_split_bf16(z_ref[d * nquads + q]) for d, q, sl in chains]
    gs = [_split_bf16(unfold(dirs[d][2][0, 0, 0, q])) for d, q, sl in chains]
    lhs = [jnp.concatenate([dirs[d][0][0, 0, :, sl].astype(BF16), gs[i][0], gs[i][1]], axis=0)
           for i, (d, q, sl) in enumerate(chains)]
    top = [jnp.dot(lhs[i], zs[i][0], preferred_element_type=F32) for i in range(len(chains))]
    low = [jnp.dot(gs[i][0], zs[i][1], preferred_element_type=F32) for i in range(len(chains))]
    for i, (d, q, sl) in enumerate(chains):
        dirs[d][4][0, :, sl] = top[i][:CHUNK] + dirs[d][1][0, 0, :, sl]
        z_ref[d * nquads + q] = (top[i][CHUNK:CHUNK + QUAD_LANES] + top[i][CHUNK + QUAD_LANES:] + low[i]
                                 + unfold(dirs[d][3][0, 0, 0, q]))

    @pl.when(c == pl.num_programs(1) - 1)
    def _():
        st_ref[0] = z_ref[...]


def wkv_seq(q1, q2, gt, ht, s0):
    B, _, T, C = q1.shape
    nc, nquads = gt.shape[2], gt.shape[3]
    qf = pl.BlockSpec((1, 1, CHUNK, C), lambda b, c: (b, 0, c, 0))
    qb = pl.BlockSpec((1, 1, CHUNK, C), lambda b, c: (b, 1, nc - 1 - c, 0))
    gshape = (1, 1, 1, nquads, HEAD_DIM, QUAD_LANES)
    gf = pl.BlockSpec(gshape, lambda b, c: (b, 0, c, 0, 0, 0))
    gb = pl.BlockSpec(gshape, lambda b, c: (b, 1, nc - 1 - c, 0, 0, 0))
    sspec = pl.BlockSpec((1,) + s0.shape[1:], lambda b, c: (b, 0, 0, 0))
    return pl.pallas_call(
        functools.partial(_wkv_seq_kernel, nquads=nquads),
        out_shape=(jax.ShapeDtypeStruct((B, T, C), F32), jax.ShapeDtypeStruct((B, T, C), F32),
                   jax.ShapeDtypeStruct(s0.shape, F32)),
        grid=(B, nc),
        in_specs=[qf, qb, qf, qb, gf, gb, gf, gb, sspec],
        out_specs=(pl.BlockSpec((1, CHUNK, C), lambda b, c: (b, c, 0)),
                   pl.BlockSpec((1, CHUNK, C), lambda b, c: (b, nc - 1 - c, 0)), sspec),
        scratch_shapes=[pltpu.VMEM(s0.shape[1:], F32)],
        compiler_params=_params(2),
        name="wkv_seq",
    )(q1, q1, q2, q2, gt, gt, ht, ht, s0)


def wkv_chunked(lw2, k2, b2, v, kk, r, s0):
    q1, q2, gt, ht = wkv_chunk_prep(lw2, k2, b2, v, kk, r)
    return wkv_seq(q1, q2, gt, ht, s0)


def _mixer_out_kernel(of_ref, ob_ref, r_ref, v_ref, k2_ref, xg_ref, oat_ref, x_ref, gm_ref,
                      gup_ref, rk_ref, lng_ref, lnb_ref, wo_ref, o_ref):
    ones = _ones_bd(RW_DIM)
    inv = 1.0 / HEAD_DIM
    o = of_ref[0] + ob_ref[0]
    mu = _seg_sum2(o, ones) * inv
    dv = o - mu
    var = _seg_sum2(dv * dv, ones) * inv
    gn = dv * lax.rsqrt(var + GN_EPS) * lng_ref[...] + lnb_ref[...]
    k2 = k2_ref[0]
    k_bonus = (k2[:, :RW_DIM] + k2[:, RW_DIM:]) * 0.5
    bonus = _seg_sum2(r_ref[0] * k_bonus * rk_ref[...], ones) * v_ref[0]
    g = _mm3(jax.nn.sigmoid(xg_ref[0]), gup_ref[...])
    y = (gn + bonus) * g
    acc = jnp.dot(oat_ref[0].astype(BF16), wo_ref[:ATTN_DIM, :], preferred_element_type=F32)
    acc = acc + jnp.dot(y.astype(BF16), wo_ref[ATTN_DIM:, :], preferred_element_type=F32)
    o_ref[0] = x_ref[0] + gm_ref[0] * acc


def mixer_out(o_f, o_b, r, v, k2, p, o_at, x, g_m, g_up, r_k, ln_x_g, ln_x_b, w_out):
    B, T, D = x.shape
    tm = _row_tile(T, 256)
    nspec = pl.BlockSpec((1, tm, RW_DIM), lambda b, i: (b, i, 0))
    full = lambda a: pl.BlockSpec(a.shape, lambda b, i: (0,) * a.ndim)
    rk2, lng2, lnb2 = r_k.reshape(1, RW_DIM), ln_x_g.reshape(1, RW_DIM), ln_x_b.reshape(1, RW_DIM)
    return pl.pallas_call(
        _mixer_out_kernel,
        out_shape=jax.ShapeDtypeStruct((B, T, D), F32),
        grid=(B, T // tm),
        in_specs=[nspec, nspec, nspec, nspec,
                  pl.BlockSpec((1, tm, 2 * RW_DIM), lambda b, i: (b, i, 0)),
                  pl.BlockSpec((1, tm, G_LORA), lambda b, i: (b, i, COL_XG // G_LORA)),
                  pl.BlockSpec((1, tm, ATTN_DIM), lambda b, i: (b, i, 0)),
                  pl.BlockSpec((1, tm, D), lambda b, i: (b, i, 0)),
                  pl.BlockSpec((1, 1, D), lambda b, i: (b, 0, 0)),
                  full(g_up), full(rk2), full(lng2), full(lnb2), full(w_out)],
        out_specs=pl.BlockSpec((1, tm, D), lambda b, i: (b, i, 0)),
        compiler_params=_params(2),
        name="mixer_out",
    )(o_f, o_b, r, v, k2, p, o_at, x, g_m, g_up, rk2, lng2, lnb2, w_out)


def _start_row_copies(idx_ref, n, src_hbm, dst_ref, sem):
    def body(r, carry):
        pltpu.make_async_copy(src_hbm.at[pl.ds(idx_ref[0, 0, r], 1)], dst_ref.at[pl.ds(r, 1)], sem).start()
        return carry
    lax.fori_loop(0, n, body, 0, unroll=32)


def _wait_row_copies(n, src_hbm, dst_ref, sem):
    pltpu.make_async_copy(src_hbm.at[pl.ds(0, n)], dst_ref.at[pl.ds(0, n)], sem).wait()


def _moe_kernel(be_ref, tok_ref, tokn_ref, h_hbm, w1f_ref, b1_ref, w2f_ref, b2_ref, o_ref, buf, sems,
                w1_ref, w2_ref, *, f, tm):
    i = pl.program_id(0)
    slot = i % 2

    @pl.when(jnp.logical_or(i == 0, be_ref[i] != be_ref[jnp.maximum(i - 1, 0)]))
    def _():
        w1_ref[...] = w1f_ref[0].astype(BF16)
        w2_ref[...] = w2f_ref[0].astype(BF16)

    @pl.when(i == 0)
    def _():
        _start_row_copies(tok_ref, tm, h_hbm, buf.at[0], sems.at[0])

    @pl.when(i + 1 < pl.num_programs(0))
    def _():
        _start_row_copies(tokn_ref, tm, h_hbm, buf.at[1 - slot], sems.at[1 - slot])

    _wait_row_copies(tm, h_hbm, buf.at[slot], sems.at[slot])
    u = jnp.dot(buf[slot].astype(BF16), w1_ref[0], preferred_element_type=F32) + b1_ref[0, 0]
    x_glu = jnp.minimum(u[:, :f], SWIGLU_LIMIT)
    x_lin = jnp.clip(u[:, f:], -SWIGLU_LIMIT, SWIGLU_LIMIT)
    act = x_glu * jax.nn.sigmoid(SWIGLU_ALPHA * x_glu) * (x_lin + 1.0)
    o_ref[...] = jnp.dot(act.astype(BF16), w2_ref[0], preferred_element_type=F32) + b2_ref[0, 0]


def moe_blocks(block_e, slot_tok, h, w1, b1, w2, b2, layer):
    N, D = h.shape
    nl, E, _, F2 = w1.shape
    f = F2 // 2
    tm = MOE_BLOCK
    rows = slot_tok.shape[0]
    nb = rows // tm
    tok3 = slot_tok.reshape(nb, 1, tm)
    kern = functools.partial(_moe_kernel, f=f, tm=tm)
    grid_spec = pltpu.PrefetchScalarGridSpec(
        num_scalar_prefetch=1,
        grid=(nb,),
        in_specs=[pl.BlockSpec((1, 1, tm), lambda i, be: (i, 0, 0), memory_space=pltpu.SMEM),
                  pl.BlockSpec((1, 1, tm), lambda i, be: (jnp.minimum(i + 1, nb - 1), 0, 0),
                               memory_space=pltpu.SMEM),
                  pl.BlockSpec(memory_space=pl.ANY),
                  pl.BlockSpec((1, 1, D, F2), lambda i, be: (layer, be[i], 0, 0)),
                  pl.BlockSpec((1, 1, 1, F2), lambda i, be: (layer, be[i], 0, 0)),
                  pl.BlockSpec((1, 1, f, D), lambda i, be: (layer, be[i], 0, 0)),
                  pl.BlockSpec((1, 1, 1, D), lambda i, be: (layer, be[i], 0, 0))],
        out_specs=pl.BlockSpec((tm, D), lambda i, be: (i, 0)),
        scratch_shapes=[pltpu.VMEM((2, tm, D), F32), pltpu.SemaphoreType.DMA((2,)),
                        pltpu.VMEM((1, D, F2), BF16), pltpu.VMEM((1, f, D), BF16)],
    )
    return pl.pallas_call(
        kern,
        out_shape=jax.ShapeDtypeStruct((rows, D), F32),
        grid_spec=grid_spec,
        compiler_params=_params(1),
        name="moe_blocks",
    )(block_e, tok3, tok3, h, w1, b1.reshape(nl, E, 1, F2), w2, b2.reshape(nl, E, 1, D))


def _moe_combine_kernel(sl_ref, sln_ref, out_hbm, gate_ref, x_ref, gf_ref, *rest, tb, final_norm):
    if final_norm:
        ng_ref, o_ref, buf, sems = rest
    else:
        o_ref, buf, sems = rest
    b, i = pl.program_id(0), pl.program_id(1)
    step = b * pl.num_programs(1) + i
    nsteps = pl.num_programs(0) * pl.num_programs(1)
    slot = step % 2
    n = TOP_K * tb

    @pl.when(step == 0)
    def _():
        _start_row_copies(sl_ref, n, out_hbm, buf.at[0], sems.at[0])

    @pl.when(step + 1 < nsteps)
    def _():
        _start_row_copies(sln_ref, n, out_hbm, buf.at[1 - slot], sems.at[1 - slot])

    _wait_row_copies(n, out_hbm, buf.at[slot], sems.at[slot])
    gate = gate_ref[0]
    y = jnp.zeros(x_ref.shape[1:], F32)
    for k in range(TOP_K):
        y = y + gate[:, k:k + 1] * buf[slot, k * tb:(k + 1) * tb, :]
    z = x_ref[0] + gf_ref[0] * y
    if final_norm:
        z = z * lax.rsqrt(jnp.mean(z * z, axis=-1, keepdims=True) + NORM_EPS) * ng_ref[...]
    o_ref[0] = z


def moe_combine(out, slots, gates, x, g_f, final_g=None):
    B, T, D = x.shape
    final_norm = final_g is not None
    extra_specs = [pl.BlockSpec((1, D), lambda b, i: (0, 0))] if final_norm else []
    extra_args = [final_g.reshape(1, D)] if final_norm else []
    tb = _row_tile(T, 64)
    nt = T // tb
    n = TOP_K * tb
    sl3 = jnp.transpose(slots.reshape(B * nt, tb, TOP_K), (0, 2, 1)).reshape(B * nt, 1, n)
    last = B * nt - 1
    return pl.pallas_call(
        functools.partial(_moe_combine_kernel, tb=tb, final_norm=final_norm),
        out_shape=jax.ShapeDtypeStruct((B, T, D), F32),
        grid=(B, nt),
        in_specs=[pl.BlockSpec((1, 1, n), lambda b, i: (b * nt + i, 0, 0), memory_space=pltpu.SMEM),
                  pl.BlockSpec((1, 1, n), lambda b, i: (jnp.minimum(b * nt + i + 1, last), 0, 0),
                               memory_space=pltpu.SMEM),
                  pl.BlockSpec(memory_space=pl.ANY),
                  pl.BlockSpec((1, tb, LANES), lambda b, i: (b, i, 0)),
                  pl.BlockSpec((1, tb, D), lambda b, i: (b, i, 0)),
                  pl.BlockSpec((1, 1, D), lambda b, i: (b, 0, 0))] + extra_specs,
        out_specs=pl.BlockSpec((1, tb, D), lambda b, i: (b, i, 0)),
        scratch_shapes=[pltpu.VMEM((2, n, D), F32), pltpu.SemaphoreType.DMA((2,))],
        compiler_params=_params(2),
        name="moe_combine",
    )(sl3, sl3, out, gates, x, g_f, *extra_args)


def _ffn_pre_kernel(x_ref, g_ref, sc_ref, sh_ref, rw_ref, rb_ref, h_ref, te_ref, tg_ref):
    h = _norm_mod(x_ref[0], g_ref[...], sc_ref[0], sh_ref[0])
    h_ref[0] = h
    cur = _mm3(h, rw_ref[...]) + rb_ref[...]
    n_e = cur.shape[1]
    lane_e = lax.broadcasted_iota(jnp.int32, cur.shape, 1).astype(F32)
    lane = lax.broadcasted_iota(jnp.int32, te_ref.shape[1:], 1)
    vals, idxs = [], []
    for k in range(TOP_K):
        m = jnp.max(cur, axis=1, keepdims=True)
        idx = jnp.min(jnp.where(cur == m, lane_e, float(n_e)), axis=1, keepdims=True)
        vals.append(m)
        idxs.append(idx)
        cur = jnp.where(lane_e == idx, -jnp.inf, cur)
    ex = [jnp.exp(v - vals[0]) for v in vals]
    denom = ex[0]
    for e in ex[1:]:
        denom = denom + e
    te = jnp.zeros(lane.shape, F32)
    tg = jnp.zeros(lane.shape, F32)
    for k in range(TOP_K):
        te = jnp.where(lane == k, idxs[k], te)
        tg = jnp.where(lane == k, ex[k] / denom, tg)
    te_ref[0] = te.astype(jnp.int32)
    tg_ref[0] = tg


def ffn_pre(x, g, sc, sh, router_w, router_b):
    B, T, D = x.shape
    E = router_w.shape[1]
    tm = _row_tile(T, 256)
    return pl.pallas_call(
        _ffn_pre_kernel,
        out_shape=(jax.ShapeDtypeStruct((B, T, D), F32), jax.ShapeDtypeStruct((B, T, LANES), jnp.int32),
                   jax.ShapeDtypeStruct((B, T, LANES), F32)),
        grid=(B, T // tm),
        in_specs=[pl.BlockSpec((1, tm, D), lambda b, i: (b, i, 0)),
                  pl.BlockSpec((1, D), lambda b, i: (0, 0)),
                  pl.BlockSpec((1, 1, D), lambda b, i: (b, 0, 0)),
                  pl.BlockSpec((1, 1, D), lambda b, i: (b, 0, 0)),
                  pl.BlockSpec((D, E), lambda b, i: (0, 0)),
                  pl.BlockSpec((1, E), lambda b, i: (0, 0))],
        out_specs=(pl.BlockSpec((1, tm, D), lambda b, i: (b, i, 0)),
                   pl.BlockSpec((1, tm, LANES), lambda b, i: (b, i, 0)),
                   pl.BlockSpec((1, tm, LANES), lambda b, i: (b, i, 0))),
        compiler_params=_params(2),
        name="ffn_pre",
    )(x, g.reshape(1, D), sc, sh, router_w, router_b.reshape(1, E))


def _route(top_e):
    N = top_e.shape[0]
    n4 = N * TOP_K
    flat_e = top_e.reshape(-1)
    order = jnp.argsort(flat_e)
    e_sorted = flat_e[order]
    counts = jnp.bincount(flat_e, length=N_EXPERTS)
    padded = (counts + MOE_BLOCK - 1) // MOE_BLOCK * MOE_BLOCK
    start = jnp.cumsum(counts) - counts
    pstart = jnp.cumsum(padded) - padded
    dest = (pstart[e_sorted] + jnp.arange(n4, dtype=jnp.int32) - start[e_sorted]).astype(jnp.int32)
    n_blocks = (n4 + MOE_BLOCK - 1) // MOE_BLOCK + N_EXPERTS
    rows = n_blocks * MOE_BLOCK
    ends = jnp.cumsum(padded)
    block_start = jnp.arange(n_blocks, dtype=ends.dtype) * MOE_BLOCK
    block_e = jnp.minimum(jnp.sum(ends[None, :] <= block_start[:, None], axis=1), N_EXPERTS - 1).astype(jnp.int32)
    slot = jnp.arange(rows, dtype=jnp.int32)
    slot_e = jnp.repeat(block_e, MOE_BLOCK)
    off = slot - pstart[slot_e].astype(jnp.int32)
    src = jnp.minimum(start[slot_e].astype(jnp.int32) + off, n4 - 1)
    slot_tok = jnp.where(off < counts[slot_e], order[src] // TOP_K, 0).astype(jnp.int32)
    slot_flat = dest[jnp.argsort(order)]
    return slot_tok, block_e, slot_flat.reshape(N, TOP_K)


def kernel(x, c, ctx, c_ctx, norm_mix_g, norm_ffn_g, w_mod, b_mod, w_in, w_out, q_norm_g, k_norm_g,
           conv_w, w0, w_up, a0, a_up, g_up, k_k, k_a, r_k, ln_x_g, ln_x_b,
           router_w, router_b, e_w1, e_b1, e_w2, e_b2, norm_final_g):
    B, T, D = x.shape
    C = ctx.shape[1]
    depth = w_in.shape[0]
    cos, sin = _rope_tables(T)
    rpad = -(-(B + 1) // SUBLANES) * SUBLANES
    cvecs = jnp.zeros((rpad, D), F32).at[:B].set(c).at[B].set(c_ctx)
    s_zero = jnp.zeros((B, 2 * RW_DIM // QUAD_LANES, QUAD_LANES, QUAD_LANES), F32)
    x_lat, x_ctx = x, ctx
    for l in range(depth):
        last = l == depth - 1
        mods = adaln_all(cvecs, w_mod[l], b_mod[l])
        m_lat = [m[:, None, :] for m in jnp.split(mods[:B], 6, axis=-1)]
        m_ctx = [jnp.broadcast_to(m[None, :, :], (B, 1, D)) for m in jnp.split(mods[B:B + 1], 6, axis=-1)]
        sh_m, sc_m, g_m, sh_f, sc_f, g_f = m_lat
        csh_m, csc_m, cg_m, csh_f, csc_f, cg_f = m_ctx
        w_in_l = w_in[l].astype(BF16)
        w_out_l = w_out[l].astype(BF16)
        p_lat = norm_mod_matmul(x_lat, norm_mix_g[l], sc_m, sh_m, w_in_l)
        p_ctx = norm_mod_matmul(x_ctx, norm_mix_g[l], csc_m, csh_m,
                                w_in_l[:, :STATE_COLS] if last else w_in_l)

        ktc, vc = kv_prep(p_ctx, k_norm_g[l])
        ktl, vl = kv_prep(p_lat, k_norm_g[l], cos, sin)
        o_at = attention(p_lat, q_norm_g[l], ktc, vc, ktl, vl, cos, sin)

        rw = (conv_w[l], w0[l], w_up[l], a0[l], a_up[l], k_k[l], k_a[l])
        w2c, k2c, b2c, v_c, kk_c, r_c = wkv_prep(p_ctx, *rw, has_r=not last)
        w2l, k2l, b2l, v_l, kk_l, r_l = wkv_prep(p_lat, *rw, has_r=True)
        of_c, ob_c, s_c = wkv_chunked(w2c, k2c, b2c, v_c, kk_c, r_c, s_zero)
        of_l, ob_l, _ = wkv_chunked(w2l, k2l, b2l, v_l, kk_l, r_l, s_c)

        ro = (g_up[l], r_k[l], ln_x_g[l], ln_x_b[l], w_out_l)
        x_lat = mixer_out(of_l, ob_l, r_l, v_l, k2l, p_lat, o_at, x_lat, g_m, *ro)
        h_lat, te_lat, tg_lat = ffn_pre(x_lat, norm_ffn_g[l], sc_f, sh_f, router_w[l], router_b[l])
        if last:
            slot_tok, block_e, slot_flat = _route(te_lat.reshape(B * T, LANES)[:, :TOP_K])
            out = moe_blocks(block_e, slot_tok, h_lat.reshape(B * T, D), e_w1, e_b1, e_w2, e_b2, l)
            x_lat = moe_combine(out, slot_flat.reshape(B, T, TOP_K), tg_lat, x_lat, g_f, final_g=norm_final_g)
        else:
            o_at_c = attention(p_ctx, q_norm_g[l], ktc, vc)
            x_ctx = mixer_out(of_c, ob_c, r_c, v_c, k2c, p_ctx, o_at_c, x_ctx, cg_m, *ro)
            h_ctx, te_ctx, tg_ctx = ffn_pre(x_ctx, norm_ffn_g[l], csc_f, csh_f, router_w[l], router_b[l])
            h_all = jnp.concatenate([h_lat.reshape(B * T, D), h_ctx.reshape(B * C, D)], axis=0)
            te_all = jnp.concatenate([te_lat.reshape(B * T, LANES), te_ctx.reshape(B * C, LANES)], axis=0)
            slot_tok, block_e, slot_flat = _route(te_all[:, :TOP_K])
            out = moe_blocks(block_e, slot_tok, h_all, e_w1, e_b1, e_w2, e_b2, l)
            x_lat = moe_combine(out, slot_flat[:B * T].reshape(B, T, TOP_K), tg_lat, x_lat, g_f)
            x_ctx = moe_combine(out, slot_flat[B * T:].reshape(B, C, TOP_K), tg_ctx, x_ctx, cg_f)
    return x_lat
```

```python
import functools
import math

import numpy as np
import jax
import jax.numpy as jnp
from jax import lax
from jax.experimental import pallas as pl
from jax.experimental.pallas import tpu as pltpu

F32 = jnp.float32
BF16 = jnp.bfloat16
HIGHEST = lax.Precision.HIGHEST

GRID_W = 64
HEAD_DIM = 64
ATTN_HEADS = 8
ATTN_KV_HEADS = 2
GROUP = ATTN_HEADS // ATTN_KV_HEADS
ATTN_DIM = ATTN_HEADS * HEAD_DIM
KV_DIM = ATTN_KV_HEADS * HEAD_DIM
RW_HEADS = 8
RW_DIM = RW_HEADS * HEAD_DIM
W_LORA = 64
A_LORA = 64
G_LORA = 128
CONV_W = 3
N_EXPERTS = 32
TOP_K = 4
SWIGLU_LIMIT = 7.0
SWIGLU_ALPHA = 1.702
ROPE_THETA = 10000.0
ROPE_PAIRS = HEAD_DIM // 4
MOE_BLOCK = 256
NORM_EPS = 1e-6
GN_EPS = 64e-5
ATTN_SCALE = HEAD_DIM ** -0.5
EXP_M05 = math.exp(-0.5)
COL_K_AT = 0
COL_V_AT = KV_DIM
COL_K_RW = 2 * KV_DIM
COL_V_RW = COL_K_RW + RW_DIM
COL_XW = COL_V_RW + RW_DIM
COL_XA = COL_XW + 2 * W_LORA
STATE_COLS = COL_XA + 2 * A_LORA
COL_Q_AT = STATE_COLS
COL_R_RW = COL_Q_AT + ATTN_DIM
COL_XG = COL_R_RW + RW_DIM

LANES = 128
SUBLANES = 8
QUAD_LANES = 256
CHUNK = 64
CHUNKS_PER_STEP = 4
ATTN_ROW_GROUP = 256
VMEM_LIMIT = 48 * 1024 * 1024


def _row_tile(n, pref):
    t = pref
    while n % t:
        t //= 2
    return t


def _params(n_axes):
    return pltpu.CompilerParams(dimension_semantics=("arbitrary",) * n_axes, vmem_limit_bytes=VMEM_LIMIT)


def _ones_bd(n):
    ra = lax.broadcasted_iota(jnp.int32, (n, n), 0) // HEAD_DIM
    rb = lax.broadcasted_iota(jnp.int32, (n, n), 1) // HEAD_DIM
    return (ra == rb).astype(BF16)


def _seg_sum(x_bf16, ones_bd):
    return jnp.dot(x_bf16, ones_bd, preferred_element_type=F32)


def _seg_sum2(x, ones_bd):
    hi = x.astype(BF16)
    lo = (x - hi.astype(F32)).astype(BF16)
    return _seg_sum(hi, ones_bd) + _seg_sum(lo, ones_bd)


def _adaln_kernel(c_ref, w_ref, b_ref, o_ref):
    c = c_ref[...]
    s = c * jax.nn.sigmoid(c)
    o_ref[...] = jnp.dot(s, w_ref[...], preferred_element_type=F32, precision=HIGHEST) + b_ref[...]


def adaln_all(cvecs, w_mod, b_mod):
    R, D = cvecs.shape
    N = w_mod.shape[1]
    tn = 512
    return pl.pallas_call(
        _adaln_kernel,
        out_shape=jax.ShapeDtypeStruct((R, N), F32),
        grid=(N // tn,),
        in_specs=[pl.BlockSpec((R, D), lambda j: (0, 0)),
                  pl.BlockSpec((D, tn), lambda j: (0, j)),
                  pl.BlockSpec((1, tn), lambda j: (0, j))],
        out_specs=pl.BlockSpec((R, tn), lambda j: (0, j)),
        compiler_params=_params(1),
        name="adaln",
    )(cvecs, w_mod, b_mod.reshape(1, N))


def _norm_mod(x, g, sc, sh):
    ms = jnp.mean(x * x, axis=-1, keepdims=True)
    y = x * lax.rsqrt(ms + NORM_EPS) * g
    return y * (1.0 + sc) + sh


def _norm_mod_matmul_kernel(x_ref, g_ref, sc_ref, sh_ref, w_ref, o_ref):
    h = _norm_mod(x_ref[0], g_ref[...], sc_ref[0], sh_ref[0])
    o_ref[0] = jnp.dot(h.astype(BF16), w_ref[...], preferred_element_type=F32)


def norm_mod_matmul(x, g, sc, sh, w):
    B, T, D = x.shape
    N = w.shape[1]
    tm = _row_tile(T, 256)
    return pl.pallas_call(
        _norm_mod_matmul_kernel,
        out_shape=jax.ShapeDtypeStruct((B, T, N), F32),
        grid=(B, T // tm),
        in_specs=[pl.BlockSpec((1, tm, D), lambda b, i: (b, i, 0)),
                  pl.BlockSpec((1, D), lambda b, i: (0, 0)),
                  pl.BlockSpec((1, 1, D), lambda b, i: (b, 0, 0)),
                  pl.BlockSpec((1, 1, D), lambda b, i: (b, 0, 0)),
                  pl.BlockSpec((D, N), lambda b, i: (0, 0))],
        out_specs=pl.BlockSpec((1, tm, N), lambda b, i: (b, i, 0)),
        compiler_params=_params(2),
        name="norm_mod_matmul",
    )(x, g.reshape(1, D), sc, sh, w)


def _pair_rms(x, g):
    lane = lax.broadcasted_iota(jnp.int32, x.shape, 1)
    first = lane < HEAD_DIM
    xx = x * x
    s0 = jnp.sum(jnp.where(first, xx, 0.0), axis=1, keepdims=True)
    s1 = jnp.sum(jnp.where(first, 0.0, xx), axis=1, keepdims=True)
    ms = jnp.where(first, s0, s1) * (1.0 / HEAD_DIM)
    return x * lax.rsqrt(ms + NORM_EPS) * g


def _pair_rope(x, cos, sin):
    lane = lax.broadcasted_iota(jnp.int32, x.shape, 1)
    partner = jnp.where(lane % (2 * ROPE_PAIRS) < ROPE_PAIRS,
                        pltpu.roll(x, LANES - ROPE_PAIRS, 1), pltpu.roll(x, ROPE_PAIRS, 1))
    return x * cos + partner * sin


def _rope_tables(T):
    rows = T // GRID_W
    row = jnp.repeat(jnp.arange(rows), GRID_W)
    col = jnp.tile(jnp.arange(GRID_W), rows)
    pos = jnp.stack([row, col], axis=-1).astype(F32)
    freqs = ROPE_THETA ** (-jnp.arange(ROPE_PAIRS, dtype=F32) / ROPE_PAIRS)
    ang = pos[:, :, None] * freqs
    cos, sin = jnp.cos(ang), jnp.sin(ang)
    cos_h = jnp.concatenate([cos[:, 0], cos[:, 0], cos[:, 1], cos[:, 1]], axis=-1)
    sin_h = jnp.concatenate([-sin[:, 0], sin[:, 0], -sin[:, 1], sin[:, 1]], axis=-1)
    return jnp.tile(cos_h, (1, 2)), jnp.tile(sin_h, (1, 2))


def _kv_prep_kernel(*refs, rope):
    if rope:
        p_ref, g_ref, cos_ref, sin_ref, kt_ref, v_ref = refs
    else:
        p_ref, g_ref, kt_ref, v_ref = refs
    x = p_ref[0]
    k = _pair_rms(x[:, :KV_DIM], g_ref[...])
    if rope:
        k = _pair_rope(k, cos_ref[...], sin_ref[...])
    v = x[:, KV_DIM:]
    lane = lax.broadcasted_iota(jnp.int32, k.shape, 1)
    first = lane < HEAD_DIM
    kr = pltpu.roll(k, HEAD_DIM, 1)
    vr = pltpu.roll(v, HEAD_DIM, 1)
    kt_ref[0, 0] = jnp.where(first, k, kr).T.astype(BF16)
    kt_ref[0, 1] = jnp.where(first, kr, k).T.astype(BF16)
    ones = jnp.ones(v.shape, F32)
    v_ref[0, 0] = jnp.concatenate([jnp.where(first, v, vr), ones], axis=1).astype(BF16)
    v_ref[0, 1] = jnp.concatenate([jnp.where(first, vr, v), ones], axis=1).astype(BF16)


def kv_prep(p, k_norm_g, cos=None, sin=None):
    B, T, _ = p.shape
    tm = _row_tile(T, 256)
    rope = cos is not None
    g2 = jnp.tile(k_norm_g, 2).reshape(1, LANES)
    in_specs = [pl.BlockSpec((1, tm, 2 * KV_DIM), lambda b, i: (b, i, 0)),
                pl.BlockSpec((1, LANES), lambda b, i: (0, 0))]
    args = [p, g2]
    if rope:
        in_specs += [pl.BlockSpec((tm, LANES), lambda b, i: (i, 0))] * 2
        args += [cos, sin]
    return pl.pallas_call(
        functools.partial(_kv_prep_kernel, rope=rope),
        out_shape=(jax.ShapeDtypeStruct((B, ATTN_KV_HEADS, LANES, T), BF16),
                   jax.ShapeDtypeStruct((B, ATTN_KV_HEADS, T, 2 * LANES), BF16)),
        grid=(B, T // tm),
        in_specs=in_specs,
        out_specs=(pl.BlockSpec((1, ATTN_KV_HEADS, LANES, tm), lambda b, i: (b, 0, 0, i)),
                   pl.BlockSpec((1, ATTN_KV_HEADS, tm, 2 * LANES), lambda b, i: (b, 0, i, 0))),
        compiler_params=_params(2),
        name="kv_prep",
    )(*args)


def _attn_kernel(*refs, tq, tkl, nkl, rope, has_lat, rg):
    refs = list(refs)
    p_ref, g_ref = refs[:2]
    refs = refs[2:]
    if rope:
        cos_ref, sin_ref = refs[:2]
        refs = refs[2:]
    ktc_ref, vc_ref = refs[:2]
    refs = refs[2:]
    if has_lat:
        ktl_ref, vl_ref = refs[:2]
        refs = refs[2:]
    o_ref, m_ref, acc_ref = refs

    q = _pair_rms(p_ref[0], g_ref[...])
    if rope:
        q = _pair_rope(q, cos_ref[...], sin_ref[...])
    q = q * ATTN_SCALE
    lane = lax.broadcasted_iota(jnp.int32, (tq, LANES), 1)
    first = lane < HEAD_DIM
    qq = jnp.concatenate([jnp.where(first, q, 0.0), jnp.where(first, 0.0, q)], axis=0).astype(BF16)

    m_ref[...] = jnp.full(m_ref.shape, -jnp.inf, F32)
    acc_ref[...] = jnp.zeros(acc_ref.shape, F32)
    groups = [slice(g * rg, (g + 1) * rg) for g in range(2 * tq // rg)]

    def chunk(kt, vv):
        s = [jnp.dot(qq[g], kt, preferred_element_type=F32) for g in groups]
        m_old = [m_ref[g] for g in groups]
        m_new = [jnp.maximum(mo, jnp.max(x, axis=-1, keepdims=True)) for mo, x in zip(m_old, s)]
        alpha = [jnp.exp(mo - mn) for mo, mn in zip(m_old, m_new)]
        p = [jnp.exp(x - jnp.concatenate([mn] * (x.shape[1] // LANES), axis=1)).astype(BF16)
             for x, mn in zip(s, m_new)]
        pv = [jnp.dot(x, vv, preferred_element_type=F32) for x in p]
        for g, al, mn, x in zip(groups, alpha, m_new, pv):
            acc_ref[g] = jnp.concatenate([al, al], axis=1) * acc_ref[g] + x
            m_ref[g] = mn

    chunk(ktc_ref[0, 0], vc_ref[0, 0])
    if has_lat:
        def body(j, carry):
            ks = pl.multiple_of(j * tkl, tkl)
            chunk(ktl_ref[0, 0, :, pl.ds(ks, tkl)], vl_ref[0, 0, pl.ds(ks, tkl), :])
            return carry
        lax.fori_loop(0, nkl, body, 0)

    acc = acc_ref[...]
    o = acc[:, :LANES] / acc[:, LANES:]
    o_ref[0] = jnp.where(first, o[:tq], o[tq:])


def attention(p, q_norm_g, ktc, vc, ktl=None, vl=None, cos=None, sin=None):
    B, T, _ = p.shape
    C = ktc.shape[-1]
    rope = cos is not None
    has_lat = ktl is not None
    tq = _row_tile(T, 512)
    pairs = ATTN_HEADS // 2
    pairs_per_kv = GROUP // 2
    qblk0 = COL_Q_AT // LANES
    g2 = jnp.tile(q_norm_g, 2).reshape(1, LANES)
    in_specs = [pl.BlockSpec((1, tq, LANES), lambda b, h, i: (b, i, qblk0 + h)),
                pl.BlockSpec((1, LANES), lambda b, h, i: (0, 0))]
    args = [p, g2]
    if rope:
        in_specs += [pl.BlockSpec((tq, LANES), lambda b, h, i: (i, 0))] * 2
        args += [cos, sin]
    in_specs += [pl.BlockSpec((1, 1, LANES, C), lambda b, h, i: (b, h // pairs_per_kv, 0, 0)),
                 pl.BlockSpec((1, 1, C, 2 * LANES), lambda b, h, i: (b, h // pairs_per_kv, 0, 0))]
    args += [ktc, vc]
    tkl = nkl = 0
    if has_lat:
        S = ktl.shape[-1]
        tkl = _row_tile(S, 2048)
        nkl = S // tkl
        in_specs += [pl.BlockSpec((1, 1, LANES, S), lambda b, h, i: (b, h // pairs_per_kv, 0, 0)),
                     pl.BlockSpec((1, 1, S, 2 * LANES), lambda b, h, i: (b, h // pairs_per_kv, 0, 0))]
        args += [ktl, vl]
    kern = functools.partial(_attn_kernel, tq=tq, tkl=tkl, nkl=nkl, rope=rope, has_lat=has_lat,
                             rg=_row_tile(tq, ATTN_ROW_GROUP))
    return pl.pallas_call(
        kern,
        out_shape=jax.ShapeDtypeStruct((B, T, ATTN_DIM), F32),
        grid=(B, pairs, T // tq),
        in_specs=in_specs,
        out_specs=pl.BlockSpec((1, tq, LANES), lambda b, h, i: (b, i, h)),
        scratch_shapes=[pltpu.VMEM((2 * tq, LANES), F32), pltpu.VMEM((2 * tq, 2 * LANES), F32)],
        compiler_params=_params(3),
        name="attention",
    )(*args)


def _wkv_prep_kernel(p_ref, pp_ref, pn_ref, cw_ref, w0_ref, wup_ref, a0_ref, aup_ref, kk_ref, ka_ref,
                     w2_ref, k2_ref, b2_ref, v_ref, kkn_ref, r_ref, *, tm, has_r):
    i = pl.program_id(1)
    n = pl.num_programs(1)
    x = p_ref[0]
    prev = pp_ref[0][SUBLANES - 1:SUBLANES, :] * (i > 0).astype(F32)
    nxt = pn_ref[0][0:1, :] * (i < n - 1).astype(F32)
    row = lax.broadcasted_iota(jnp.int32, (tm, RW_DIM), 0)

    def conv(col, wcol):
        u = x[:, col:col + RW_DIM]
        up = jnp.where(row == 0, prev[:, col:col + RW_DIM], pltpu.roll(u, 1, 0))
        un = jnp.where(row == tm - 1, nxt[:, col:col + RW_DIM], pltpu.roll(u, tm - 1, 0))
        w = cw_ref[:, wcol:wcol + RW_DIM]
        return up * w[0:1] + u * w[1:2] + un * w[2:3]

    k = conv(COL_K_RW, RW_DIM)
    v = conv(COL_V_RW, 2 * RW_DIM)
    v_ref[0] = v
    r_ref[0] = conv(COL_R_RW, 0) if has_r else jnp.zeros((tm, RW_DIM), F32)
    kk = k * kk_ref[...]
    ss = _seg_sum2(kk * kk, _ones_bd(RW_DIM))
    kk = kk * lax.rsqrt(jnp.maximum(ss, 1e-24))
    kkn_ref[0] = kk
    xw = jnp.tanh(x[:, COL_XW:COL_XW + 2 * W_LORA])
    xa = x[:, COL_XA:COL_XA + 2 * A_LORA]
    for d in range(2):
        wl = w0_ref[d:d + 1, :] + _mm3(xw[:, d * W_LORA:(d + 1) * W_LORA], wup_ref[d])
        decay = -EXP_M05 * jax.nn.sigmoid(wl)
        a = jax.nn.sigmoid(a0_ref[d:d + 1, :] + _mm3(xa[:, d * A_LORA:(d + 1) * A_LORA], aup_ref[d]))
        sl = slice(d * RW_DIM, (d + 1) * RW_DIM)
        w2_ref[0, :, sl] = decay
        k2_ref[0, :, sl] = k * (1.0 + (a - 1.0) * ka_ref[...])
        b2_ref[0, :, sl] = kk * a


def wkv_prep(p, conv_w, w0, w_up, a0, a_up, k_k, k_a, has_r):
    B, T, NP = p.shape
    tm = _row_tile(T, 256)
    nb8 = tm // SUBLANES
    last8 = T // SUBLANES - 1
    full = lambda a: pl.BlockSpec(a.shape, lambda b, i: (0,) * a.ndim)
    kk2, ka2 = k_k.reshape(1, RW_DIM), k_a.reshape(1, RW_DIM)
    wide = jax.ShapeDtypeStruct((B, T, 2 * RW_DIM), F32)
    narrow = jax.ShapeDtypeStruct((B, T, RW_DIM), F32)
    wspec = pl.BlockSpec((1, tm, 2 * RW_DIM), lambda b, i: (b, i, 0))
    nspec = pl.BlockSpec((1, tm, RW_DIM), lambda b, i: (b, i, 0))
    return pl.pallas_call(
        functools.partial(_wkv_prep_kernel, tm=tm, has_r=has_r),
        out_shape=(wide, wide, wide, narrow, narrow, narrow),
        grid=(B, T // tm),
        in_specs=[pl.BlockSpec((1, tm, NP), lambda b, i: (b, i, 0)),
                  pl.BlockSpec((1, SUBLANES, NP), lambda b, i: (b, jnp.maximum(i * nb8 - 1, 0), 0)),
                  pl.BlockSpec((1, SUBLANES, NP), lambda b, i: (b, jnp.minimum((i + 1) * nb8, last8), 0)),
                  full(conv_w), full(w0), full(w_up), full(a0), full(a_up), full(kk2), full(ka2)],
        out_specs=(wspec, wspec, wspec, nspec, nspec, nspec),
        compiler_params=_params(2),
        name="wkv_prep",
    )(p, p, p, conv_w, w0, w_up, a0, a_up, kk2, ka2)


def _split_bf16(x):
    hi = x.astype(BF16)
    return hi, (x - hi.astype(F32)).astype(BF16)


def _mm3(a, b, nt=False, passes=3):
    dn = (((1,), (1 if nt else 0,)), ((), ()))
    if passes == 1:
        return lax.dot_general(a.astype(BF16), b.astype(BF16), dn, preferred_element_type=F32)
    ah, al = _split_bf16(a)
    m = a.shape[0]
    if passes == 2:
        top = lax.dot_general(jnp.concatenate([ah, al], axis=0), b.astype(BF16), dn, preferred_element_type=F32)
        return top[:m] + top[m:]
    bh, bl = _split_bf16(b)
    top = lax.dot_general(jnp.concatenate([ah, al], axis=0), bh, dn, preferred_element_type=F32)
    return top[:m] + top[m:] + lax.dot_general(ah, bl, dn, preferred_element_type=F32)


def _chunk_terms(chains):
    L = CHUNK
    nch = range(len(chains))
    row = lax.broadcasted_iota(jnp.int32, (L, QUAD_LANES), 0)
    col = lax.broadcasted_iota(jnp.int32, (L, QUAD_LANES), 1) % HEAD_DIM
    tr = lax.broadcasted_iota(jnp.int32, (L, L), 0)
    tc = lax.broadcasted_iota(jnp.int32, (L, L), 1)
    masks = {True: (col > row, col >= row, (tc >= tr).astype(F32)),
             False: (col < row, col <= row, (tc <= tr).astype(F32))}
    eye_l = (col == row).astype(F32)
    ra = lax.broadcasted_iota(jnp.int32, (QUAD_LANES, QUAD_LANES), 0)
    rb = lax.broadcasted_iota(jnp.int32, (QUAD_LANES, QUAD_LANES), 1)
    bd = ra // HEAD_DIM == rb // HEAD_DIM
    eye_q = (ra == rb).astype(F32)

    def bdiag(y):
        return jnp.where(bd, jnp.concatenate([y] * (QUAD_LANES // L), axis=0), 0.0)

    def bdm(x, y, passes):
        ys = [bdiag(y[:, i:i + QUAD_LANES]) for i in range(0, y.shape[1], QUAD_LANES)]
        return _mm3(x, ys[0] if len(ys) == 1 else jnp.concatenate(ys, axis=1), passes=passes)

    lw, kd, bb, v, kk, r, back = (list(x) for x in zip(*chains))
    strict = [masks[b][0] for b in back]
    incl = [masks[b][1] for b in back]
    cs = [_mm3(masks[back[c]][2], lw[c]) for c in nch]
    g_last = [jnp.exp(cs[c][0:1] if back[c] else cs[c][L - 1:L]) for c in nch]
    at = [-kk[c] * jnp.exp(cs[c] - lw[c]) for c in nch]
    g_inv = [jnp.exp(-cs[c]) for c in nch]
    bt = [bb[c] * g_inv[c] for c in nch]
    kt = [kd[c] * g_inv[c] for c in nch]
    rt = [r[c] * jnp.exp(cs[c]) for c in nch]
    nt_dims = (((1,), (1,)), ((), ()))
    a_split = [_split_bf16(at[c]) for c in nch]
    b_split = [_split_bf16(bdiag(bt[c])) for c in nch]
    lhs = [jnp.concatenate([a_split[c][0], a_split[c][1], rt[c].astype(BF16)], axis=0) for c in nch]
    pb = [lax.dot_general(lhs[c], b_split[c][0], nt_dims, preferred_element_type=F32) for c in nch]
    pb_lo = [lax.dot_general(a_split[c][0], b_split[c][1], nt_dims, preferred_element_type=F32) for c in nch]
    pk = [lax.dot_general(lhs[c], bdiag(kt[c]).astype(BF16), nt_dims, preferred_element_type=F32) for c in nch]
    n = [jnp.where(strict[c], pb[c][:L] + pb[c][L:2 * L] + pb_lo[c], 0.0) for c in nch]
    m_rb = [jnp.where(incl[c], pb[c][2 * L:], 0.0) for c in nch]
    m_ak = [jnp.where(strict[c], pk[c][:L] + pk[c][L:2 * L], 0.0) for c in nch]
    m_rk = [jnp.where(incl[c], pk[c][2 * L:], 0.0) for c in nch]
    steps = int(math.log2(L)) - 1
    p = [bdm(n[c], n[c], 3) for c in nch]
    t_inv = [eye_l + n[c] for c in nch]
    for k in range(steps):
        if k + 1 < steps:
            both = [bdm(jnp.concatenate([p[c], t_inv[c]], axis=0), p[c], 1) for c in nch]
            p = [both[c][:L] for c in nch]
            t_inv = [t_inv[c] + both[c][L:] for c in nch]
        else:
            t_inv = [t_inv[c] + bdm(t_inv[c], p[c], 1) for c in nch]
    xy = [bdm(jnp.concatenate([m_ak[c], m_rk[c]], axis=0), v[c], 1) for c in nch]
    w = [bdm(t_inv[c], jnp.concatenate([at[c], xy[c][:L]], axis=1), 2) for c in nch]
    qq = [bdm(m_rb[c], w[c], 1) for c in nch]
    q1 = [rt[c] + qq[c][:, :QUAD_LANES] for c in nch]
    q2 = [xy[c][L:] + qq[c][:, QUAD_LANES:] for c in nch]
    g = [jnp.where(bd, (eye_q + _mm3(w[c][:, :QUAD_LANES].T, bt[c], passes=2)) * g_last[c], 0.0) for c in nch]
    h = [jnp.where(bd, _mm3(jnp.concatenate([w[c][:, QUAD_LANES:], v[c]], axis=0).T,
                            jnp.concatenate([bt[c], kt[c]], axis=0), passes=2) * g_last[c], 0.0) for c in nch]
    def fold(x):
        return sum(x[i:i + HEAD_DIM] for i in range(0, QUAD_LANES, HEAD_DIM))

    return [(q1[c], q2[c], fold(g[c].T), fold(h[c].T)) for c in nch]


def _wkv_chunk_kernel(lw_ref, k_ref, b_ref, v_ref, kk_ref, r_ref, q1_ref, q2_ref, gt_ref, ht_ref, *, nquads, cps):
    keys = [(j, d, q) for j in range(cps) for d in range(2) for q in range(nquads)]
    chains = []
    for j, d, q in keys:
        rows = slice(j * CHUNK, (j + 1) * CHUNK)
        sl = slice(q * QUAD_LANES, (q + 1) * QUAD_LANES)
        sd = slice(d * RW_DIM + q * QUAD_LANES, d * RW_DIM + (q + 1) * QUAD_LANES)
        chains.append((lw_ref[0, rows, sd], k_ref[0, rows, sd], b_ref[0, rows, sd],
                       v_ref[0, rows, sl], kk_ref[0, rows, sl], r_ref[0, rows, sl], d == 1))
    res = _chunk_terms(chains)
    for (j, d, q), (q1, q2, gt, ht) in zip(keys, res):
        rows = slice(j * CHUNK, (j + 1) * CHUNK)
        sl = slice(q * QUAD_LANES, (q + 1) * QUAD_LANES)
        q1_ref[0, d, rows, sl] = q1
        q2_ref[0, d, rows, sl] = q2
        gt_ref[0, d, j, q] = gt
        ht_ref[0, d, j, q] = ht


def wkv_chunk_prep(lw2, k2, b2, v, kk, r):
    B, T, C = v.shape
    nquads = C // QUAD_LANES
    nc = T // CHUNK
    cps = math.gcd(nc, CHUNKS_PER_STEP)
    tr = cps * CHUNK
    wide = pl.BlockSpec((1, tr, 2 * C), lambda b, c: (b, c, 0))
    narrow = pl.BlockSpec((1, tr, C), lambda b, c: (b, c, 0))
    qspec = pl.BlockSpec((1, 2, tr, C), lambda b, c: (b, 0, c, 0))
    gspec = pl.BlockSpec((1, 2, cps, nquads, HEAD_DIM, QUAD_LANES), lambda b, c: (b, 0, c, 0, 0, 0))
    qshape = jax.ShapeDtypeStruct((B, 2, T, C), F32)
    gshape = jax.ShapeDtypeStruct((B, 2, nc, nquads, HEAD_DIM, QUAD_LANES), F32)
    return pl.pallas_call(
        functools.partial(_wkv_chunk_kernel, nquads=nquads, cps=cps),
        out_shape=(qshape, qshape, gshape, gshape),
        grid=(B, nc // cps),
        in_specs=[wide, wide, wide, narrow, narrow, narrow],
        out_specs=(qspec, qspec, gspec, gspec),
        compiler_params=_params(2),
        name="wkv_chunk_prep",
    )(lw2, k2, b2, v, kk, r)


def _wkv_seq_kernel(q1f_ref, q1b_ref, q2f_ref, q2b_ref, gtf_ref, gtb_ref, htf_ref, htb_ref, s0_ref,
                    of_ref, ob_ref, st_ref, z_ref, *, nquads):
    c = pl.program_id(1)

    @pl.when(c == 0)
    def _():
        z_ref[...] = s0_ref[0]

    dirs = ((q1f_ref, q2f_ref, gtf_ref, htf_ref, of_ref), (q1b_ref, q2b_ref, gtb_ref, htb_ref, ob_ref))
    chains = [(d, q, slice(q * QUAD_LANES, (q + 1) * QUAD_LANES)) for d in range(2) for q in range(nquads)]
    ra = lax.broadcasted_iota(jnp.int32, (QUAD_LANES, QUAD_LANES), 0) // HEAD_DIM
    rb = lax.broadcasted_iota(jnp.int32, (QUAD_LANES, QUAD_LANES), 1) // HEAD_DIM

    def unfold(x):
        return jnp.where(ra == rb, jnp.concatenate([x] * (QUAD_LANES // HEAD_DIM), axis=0), 0.0)

    zs = [_split_bf16(z_ref[d * nquads + q]) for d, q, sl in chains]
    gs = [_split_bf16(unfold(dirs[d][2][0, 0, 0, q])) for d, q, sl in chains]
    lhs = [jnp.concatenate([dirs[d][0][0, 0, :, sl].astype(BF16), gs[i][0], gs[i][1]], axis=0)
           for i, (d, q, sl) in enumerate(chains)]
    top = [jnp.dot(lhs[i], zs[i][0], preferred_element_type=F32) for i in range(len(chains))]
    low = [jnp.dot(gs[i][0], zs[i][1], preferred_element_type=F32) for i in range(len(chains))]
    for i, (d, q, sl) in enumerate(chains):
        dirs[d][4][0, :, sl] = top[i][:CHUNK] + dirs[d][1][0, 0, :, sl]
        z_ref[d * nquads + q] = (top[i][CHUNK:CHUNK + QUAD_LANES] + top[i][CHUNK + QUAD_LANES:] + low[i]
                                 + unfold(dirs[d][3][0, 0, 0, q]))

    @pl.when(c == pl.num_programs(1) - 1)
    def _():
        st_ref[0] = z_ref[...]


def wkv_seq(q1, q2, gt, ht, s0):
    B, _, T, C = q1.shape
    nc, nquads = gt.shape[2], gt.shape[3]
    qf = pl.BlockSpec((1, 1, CHUNK, C), lambda b, c: (b, 0, c, 0))
    qb = pl.BlockSpec((1, 1, CHUNK, C), lambda b, c: (b, 1, nc - 1 - c, 0))
    gshape = (1, 1, 1, nquads, HEAD_DIM, QUAD_LANES)
    gf = pl.BlockSpec(gshape, lambda b, c: (b, 0, c, 0, 0, 0))
    gb = pl.BlockSpec(gshape, lambda b, c: (b, 1, nc - 1 - c, 0, 0, 0))
    sspec = pl.BlockSpec((1,) + s0.shape[1:], lambda b, c: (b, 0, 0, 0))
    return pl.pallas_call(
        functools.partial(_wkv_seq_kernel, nquads=nquads),
        out_shape=(jax.ShapeDtypeStruct((B, T, C), F32), jax.ShapeDtypeStruct((B, T, C), F32),
                   jax.ShapeDtypeStruct(s0.shape, F32)),
        grid=(B, nc),
        in_specs=[qf, qb, qf, qb, gf, gb, gf, gb, sspec],
        out_specs=(pl.BlockSpec((1, CHUNK, C), lambda b, c: (b, c, 0)),
                   pl.BlockSpec((1, CHUNK, C), lambda b, c: (b, nc - 1 - c, 0)), sspec),
        scratch_shapes=[pltpu.VMEM(s0.shape[1:], F32)],
        compiler_params=_params(2),
        name="wkv_seq",
    )(q1, q1, q2, q2, gt, gt, ht, ht, s0)


def wkv_chunked(lw2, k2, b2, v, kk, r, s0):
    q1, q2, gt, ht = wkv_chunk_prep(lw2, k2, b2, v, kk, r)
    return wkv_seq(q1, q2, gt, ht, s0)


def _mixer_out_kernel(of_ref, ob_ref, r_ref, v_ref, k2_ref, xg_ref, oat_ref, x_ref, gm_ref,
                      gup_ref, rk_ref, lng_ref, lnb_ref, wo_ref, o_ref):
    ones = _ones_bd(RW_DIM)
    inv = 1.0 / HEAD_DIM
    o = of_ref[0] + ob_ref[0]
    mu = _seg_sum2(o, ones) * inv
    dv = o - mu
    var = _seg_sum2(dv * dv, ones) * inv
    gn = dv * lax.rsqrt(var + GN_EPS) * lng_ref[...] + lnb_ref[...]
    k2 = k2_ref[0]
    k_bonus = (k2[:, :RW_DIM] + k2[:, RW_DIM:]) * 0.5
    bonus = _seg_sum2(r_ref[0] * k_bonus * rk_ref[...], ones) * v_ref[0]
    g = _mm3(jax.nn.sigmoid(xg_ref[0]), gup_ref[...])
    y = (gn + bonus) * g
    acc = jnp.dot(oat_ref[0].astype(BF16), wo_ref[:ATTN_DIM, :], preferred_element_type=F32)
    acc = acc + jnp.dot(y.astype(BF16), wo_ref[ATTN_DIM:, :], preferred_element_type=F32)
    o_ref[0] = x_ref[0] + gm_ref[0] * acc


def mixer_out(o_f, o_b, r, v, k2, p, o_at, x, g_m, g_up, r_k, ln_x_g, ln_x_b, w_out):
    B, T, D = x.shape
    tm = _row_tile(T, 256)
    nspec = pl.BlockSpec((1, tm, RW_DIM), lambda b, i: (b, i, 0))
    full = lambda a: pl.BlockSpec(a.shape, lambda b, i: (0,) * a.ndim)
    rk2, lng2, lnb2 = r_k.reshape(1, RW_DIM), ln_x_g.reshape(1, RW_DIM), ln_x_b.reshape(1, RW_DIM)
    return pl.pallas_call(
        _mixer_out_kernel,
        out_shape=jax.ShapeDtypeStruct((B, T, D), F32),
        grid=(B, T // tm),
        in_specs=[nspec, nspec, nspec, nspec,
                  pl.BlockSpec((1, tm, 2 * RW_DIM), lambda b, i: (b, i, 0)),
                  pl.BlockSpec((1, tm, G_LORA), lambda b, i: (b, i, COL_XG // G_LORA)),
                  pl.BlockSpec((1, tm, ATTN_DIM), lambda b, i: (b, i, 0)),
                  pl.BlockSpec((1, tm, D), lambda b, i: (b, i, 0)),
                  pl.BlockSpec((1, 1, D), lambda b, i: (b, 0, 0)),
                  full(g_up), full(rk2), full(lng2), full(lnb2), full(w_out)],
        out_specs=pl.BlockSpec((1, tm, D), lambda b, i: (b, i, 0)),
        compiler_params=_params(2),
        name="mixer_out",
    )(o_f, o_b, r, v, k2, p, o_at, x, g_m, g_up, rk2, lng2, lnb2, w_out)


def _start_row_copies(idx_ref, n, src_hbm, dst_ref, sem):
    def body(r, carry):
        pltpu.make_async_copy(src_hbm.at[pl.ds(idx_ref[0, 0, r], 1)], dst_ref.at[pl.ds(r, 1)], sem).start()
        return carry
    lax.fori_loop(0, n, body, 0, unroll=32)


def _wait_row_copies(n, src_hbm, dst_ref, sem):
    pltpu.make_async_copy(src_hbm.at[pl.ds(0, n)], dst_ref.at[pl.ds(0, n)], sem).wait()


def _moe_kernel(be_ref, tok_ref, tokn_ref, h_hbm, w1f_ref, b1_ref, w2f_ref, b2_ref, o_ref, buf, sems,
                w1_ref, w2_ref, *, f, tm):
    i = pl.program_id(0)
    slot = i % 2

    @pl.when(jnp.logical_or(i == 0, be_ref[i] != be_ref[jnp.maximum(i - 1, 0)]))
    def _():
        w1_ref[...] = w1f_ref[0].astype(BF16)
        w2_ref[...] = w2f_ref[0].astype(BF16)

    @pl.when(i == 0)
    def _():
        _start_row_copies(tok_ref, tm, h_hbm, buf.at[0], sems.at[0])

    @pl.when(i + 1 < pl.num_programs(0))
    def _():
        _start_row_copies(tokn_ref, tm, h_hbm, buf.at[1 - slot], sems.at[1 - slot])

    _wait_row_copies(tm, h_hbm, buf.at[slot], sems.at[slot])
    u = jnp.dot(buf[slot].astype(BF16), w1_ref[0], preferred_element_type=F32) + b1_ref[0, 0]
    x_glu = jnp.minimum(u[:, :f], SWIGLU_LIMIT)
    x_lin = jnp.clip(u[:, f:], -SWIGLU_LIMIT, SWIGLU_LIMIT)
    act = x_glu * jax.nn.sigmoid(SWIGLU_ALPHA * x_glu) * (x_lin + 1.0)
    o_ref[...] = jnp.dot(act.astype(BF16), w2_ref[0], preferred_element_type=F32) + b2_ref[0, 0]


def moe_blocks(block_e, slot_tok, h, w1, b1, w2, b2, layer):
    N, D = h.shape
    nl, E, _, F2 = w1.shape
    f = F2 // 2
    tm = MOE_BLOCK
    rows = slot_tok.shape[0]
    nb = rows // tm
    tok3 = slot_tok.reshape(nb, 1, tm)
    kern = functools.partial(_moe_kernel, f=f, tm=tm)
    grid_spec = pltpu.PrefetchScalarGridSpec(
        num_scalar_prefetch=1,
        grid=(nb,),
        in_specs=[pl.BlockSpec((1, 1, tm), lambda i, be: (i, 0, 0), memory_space=pltpu.SMEM),
                  pl.BlockSpec((1, 1, tm), lambda i, be: (jnp.minimum(i + 1, nb - 1), 0, 0),
                               memory_space=pltpu.SMEM),
                  pl.BlockSpec(memory_space=pl.ANY),
                  pl.BlockSpec((1, 1, D, F2), lambda i, be: (layer, be[i], 0, 0)),
                  pl.BlockSpec((1, 1, 1, F2), lambda i, be: (layer, be[i], 0, 0)),
                  pl.BlockSpec((1, 1, f, D), lambda i, be: (layer, be[i], 0, 0)),
                  pl.BlockSpec((1, 1, 1, D), lambda i, be: (layer, be[i], 0, 0))],
        out_specs=pl.BlockSpec((tm, D), lambda i, be: (i, 0)),
        scratch_shapes=[pltpu.VMEM((2, tm, D), F32), pltpu.SemaphoreType.DMA((2,)),
                        pltpu.VMEM((1, D, F2), BF16), pltpu.VMEM((1, f, D), BF16)],
    )
    return pl.pallas_call(
        kern,
        out_shape=jax.ShapeDtypeStruct((rows, D), F32),
        grid_spec=grid_spec,
        compiler_params=_params(1),
        name="moe_blocks",
    )(block_e, tok3, tok3, h, w1, b1.reshape(nl, E, 1, F2), w2, b2.reshape(nl, E, 1, D))


def _moe_combine_kernel(sl_ref, sln_ref, out_hbm, gate_ref, x_ref, gf_ref, *rest, tb, final_norm):
    if final_norm:
        ng_ref, o_ref, buf, sems = rest
    else:
        o_ref, buf, sems = rest
    b, i = pl.program_id(0), pl.program_id(1)
    step = b * pl.num_programs(1) + i
    nsteps = pl.num_programs(0) * pl.num_programs(1)
    slot = step % 2
    n = TOP_K * tb

    @pl.when(step == 0)
    def _():
        _start_row_copies(sl_ref, n, out_hbm, buf.at[0], sems.at[0])

    @pl.when(step + 1 < nsteps)
    def _():
        _start_row_copies(sln_ref, n, out_hbm, buf.at[1 - slot], sems.at[1 - slot])

    _wait_row_copies(n, out_hbm, buf.at[slot], sems.at[slot])
    gate = gate_ref[0]
    y = jnp.zeros(x_ref.shape[1:], F32)
    for k in range(TOP_K):
        y = y + gate[:, k:k + 1] * buf[slot, k * tb:(k + 1) * tb, :]
    z = x_ref[0] + gf_ref[0] * y
    if final_norm:
        z = z * lax.rsqrt(jnp.mean(z * z, axis=-1, keepdims=True) + NORM_EPS) * ng_ref[...]
    o_ref[0] = z


def moe_combine(out, slots, gates, x, g_f, final_g=None):
    B, T, D = x.shape
    final_norm = final_g is not None
    extra_specs = [pl.BlockSpec((1, D), lambda b, i: (0, 0))] if final_norm else []
    extra_args = [final_g.reshape(1, D)] if final_norm else []
    tb = _row_tile(T, 64)
    nt = T // tb
    n = TOP_K * tb
    sl3 = jnp.transpose(slots.reshape(B * nt, tb, TOP_K), (0, 2, 1)).reshape(B * nt, 1, n)
    last = B * nt - 1
    return pl.pallas_call(
        functools.partial(_moe_combine_kernel, tb=tb, final_norm=final_norm),
        out_shape=jax.ShapeDtypeStruct((B, T, D), F32),
        grid=(B, nt),
        in_specs=[pl.BlockSpec((1, 1, n), lambda b, i: (b * nt + i, 0, 0), memory_space=pltpu.SMEM),
                  pl.BlockSpec((1, 1, n), lambda b, i: (jnp.minimum(b * nt + i + 1, last), 0, 0),
                               memory_space=pltpu.SMEM),
                  pl.BlockSpec(memory_space=pl.ANY),
                  pl.BlockSpec((1, tb, LANES), lambda b, i: (b, i, 0)),
                  pl.BlockSpec((1, tb, D), lambda b, i: (b, i, 0)),
                  pl.BlockSpec((1, 1, D), lambda b, i: (b, 0, 0))] + extra_specs,
        out_specs=pl.BlockSpec((1, tb, D), lambda b, i: (b, i, 0)),
        scratch_shapes=[pltpu.VMEM((2, n, D), F32), pltpu.SemaphoreType.DMA((2,))],
        compiler_params=_params(2),
        name="moe_combine",
    )(sl3, sl3, out, gates, x, g_f, *extra_args)


def _ffn_pre_kernel(x_ref, g_ref, sc_ref, sh_ref, rw_ref, rb_ref, h_ref, te_ref, tg_ref):
    h = _norm_mod(x_ref[0], g_ref[...], sc_ref[0], sh_ref[0])
    h_ref[0] = h
    cur = _mm3(h, rw_ref[...]) + rb_ref[...]
    n_e = cur.shape[1]
    lane_e = lax.broadcasted_iota(jnp.int32, cur.shape, 1).astype(F32)
    lane = lax.broadcasted_iota(jnp.int32, te_ref.shape[1:], 1)
    vals, idxs = [], []
    for k in range(TOP_K):
        m = jnp.max(cur, axis=1, keepdims=True)
        idx = jnp.min(jnp.where(cur == m, lane_e, float(n_e)), axis=1, keepdims=True)
        vals.append(m)
        idxs.append(idx)
        cur = jnp.where(lane_e == idx, -jnp.inf, cur)
    ex = [jnp.exp(v - vals[0]) for v in vals]
    denom = ex[0]
    for e in ex[1:]:
        denom = denom + e
    te = jnp.zeros(lane.shape, F32)
    tg = jnp.zeros(lane.shape, F32)
    for k in range(TOP_K):
        te = jnp.where(lane == k, idxs[k], te)
        tg = jnp.where(lane == k, ex[k] / denom, tg)
    te_ref[0] = te.astype(jnp.int32)
    tg_ref[0] = tg


def ffn_pre(x, g, sc, sh, router_w, router_b):
    B, T, D = x.shape
    E = router_w.shape[1]
    tm = _row_tile(T, 256)
    return pl.pallas_call(
        _ffn_pre_kernel,
        out_shape=(jax.ShapeDtypeStruct((B, T, D), F32), jax.ShapeDtypeStruct((B, T, LANES), jnp.int32),
                   jax.ShapeDtypeStruct((B, T, LANES), F32)),
        grid=(B, T // tm),
        in_specs=[pl.BlockSpec((1, tm, D), lambda b, i: (b, i, 0)),
                  pl.BlockSpec((1, D), lambda b, i: (0, 0)),
                  pl.BlockSpec((1, 1, D), lambda b, i: (b, 0, 0)),
                  pl.BlockSpec((1, 1, D), lambda b, i: (b, 0, 0)),
                  pl.BlockSpec((D, E), lambda b, i: (0, 0)),
                  pl.BlockSpec((1, E), lambda b, i: (0, 0))],
        out_specs=(pl.BlockSpec((1, tm, D), lambda b, i: (b, i, 0)),
                   pl.BlockSpec((1, tm, LANES), lambda b, i: (b, i, 0)),
                   pl.BlockSpec((1, tm, LANES), lambda b, i: (b, i, 0))),
        compiler_params=_params(2),
        name="ffn_pre",
    )(x, g.reshape(1, D), sc, sh, router_w, router_b.reshape(1, E))


def _route(top_e):
    N = top_e.shape[0]
    n4 = N * TOP_K
    flat_e = top_e.reshape(-1)
    order = jnp.argsort(flat_e)
    e_sorted = flat_e[order]
    counts = jnp.bincount(flat_e, length=N_EXPERTS)
    padded = (counts + MOE_BLOCK - 1) // MOE_BLOCK * MOE_BLOCK
    start = jnp.cumsum(counts) - counts
    pstart = jnp.cumsum(padded) - padded
    dest = (pstart[e_sorted] + jnp.arange(n4, dtype=jnp.int32) - start[e_sorted]).astype(jnp.int32)
    n_blocks = (n4 + MOE_BLOCK - 1) // MOE_BLOCK + N_EXPERTS
    rows = n_blocks * MOE_BLOCK
    ends = jnp.cumsum(padded)
    block_start = jnp.arange(n_blocks, dtype=ends.dtype) * MOE_BLOCK
    block_e = jnp.minimum(jnp.sum(ends[None, :] <= block_start[:, None], axis=1), N_EXPERTS - 1).astype(jnp.int32)
    slot = jnp.arange(rows, dtype=jnp.int32)
    slot_e = jnp.repeat(block_e, MOE_BLOCK)
    off = slot - pstart[slot_e].astype(jnp.int32)
    src = jnp.minimum(start[slot_e].astype(jnp.int32) + off, n4 - 1)
    slot_tok = jnp.where(off < counts[slot_e], order[src] // TOP_K, 0).astype(jnp.int32)
    slot_flat = dest[jnp.argsort(order)]
    return slot_tok, block_e, slot_flat.reshape(N, TOP_K)


def kernel(x, c, ctx, c_ctx, norm_mix_g, norm_ffn_g, w_mod, b_mod, w_in, w_out, q_norm_g, k_norm_g,
           conv_w, w0, w_up, a0, a_up, g_up, k_k, k_a, r_k, ln_x_g, ln_x_b,
           router_w, router_b, e_w1, e_b1, e_w2, e_b2, norm_final_g):
    B, T, D = x.shape
    C = ctx.shape[1]
    depth = w_in.shape[0]
    cos, sin = _rope_tables(T)
    rpad = -(-(B + 1) // SUBLANES) * SUBLANES
    cvecs = jnp.zeros((rpad, D), F32).at[:B].set(c).at[B].set(c_ctx)
    s_zero = jnp.zeros((B, 2 * RW_DIM // QUAD_LANES, QUAD_LANES, QUAD_LANES), F32)
    x_lat, x_ctx = x, ctx
    for l in range(depth):
        last = l == depth - 1
        mods = adaln_all(cvecs, w_mod[l], b_mod[l])
        m_lat = [m[:, None, :] for m in jnp.split(mods[:B], 6, axis=-1)]
        m_ctx = [jnp.broadcast_to(m[None, :, :], (B, 1, D)) for m in jnp.split(mods[B:B + 1], 6, axis=-1)]
        sh_m, sc_m, g_m, sh_f, sc_f, g_f = m_lat
        csh_m, csc_m, cg_m, csh_f, csc_f, cg_f = m_ctx
        w_in_l = w_in[l].astype(BF16)
        w_out_l = w_out[l].astype(BF16)
        p_lat = norm_mod_matmul(x_lat, norm_mix_g[l], sc_m, sh_m, w_in_l)
        p_ctx = norm_mod_matmul(x_ctx, norm_mix_g[l], csc_m, csh_m,
                                w_in_l[:, :STATE_COLS] if last else w_in_l)

        ktc, vc = kv_prep(p_ctx, k_norm_g[l])
        ktl, vl = kv_prep(p_lat, k_norm_g[l], cos, sin)
        o_at = attention(p_lat, q_norm_g[l], ktc, vc, ktl, vl, cos, sin)

        rw = (conv_w[l], w0[l], w_up[l], a0[l], a_up[l], k_k[l], k_a[l])
        w2c, k2c, b2c, v_c, kk_c, r_c = wkv_prep(p_ctx, *rw, has_r=not last)
        w2l, k2l, b2l, v_l, kk_l, r_l = wkv_prep(p_lat, *rw, has_r=True)
        of_c, ob_c, s_c = wkv_chunked(w2c, k2c, b2c, v_c, kk_c, r_c, s_zero)
        of_l, ob_l, _ = wkv_chunked(w2l, k2l, b2l, v_l, kk_l, r_l, s_c)

        ro = (g_up[l], r_k[l], ln_x_g[l], ln_x_b[l], w_out_l)
        x_lat = mixer_out(of_l, ob_l, r_l, v_l, k2l, p_lat, o_at, x_lat, g_m, *ro)
        h_lat, te_lat, tg_lat = ffn_pre(x_lat, norm_ffn_g[l], sc_f, sh_f, router_w[l], router_b[l])
        if last:
            slot_tok, block_e, slot_flat = _route(te_lat.reshape(B * T, LANES)[:, :TOP_K])
            out = moe_blocks(block_e, slot_tok, h_lat.reshape(B * T, D), e_w1, e_b1, e_w2, e_b2, l)
            x_lat = moe_combine(out, slot_flat.reshape(B, T, TOP_K), tg_lat, x_lat, g_f, final_g=norm_final_g)
        else:
            o_at_c = attention(p_ctx, q_norm_g[l], ktc, vc)
            x_ctx = mixer_out(of_c, ob_c, r_c, v_c, k2c, p_ctx, o_at_c, x_ctx, cg_m, *ro)
            h_ctx, te_ctx, tg_ctx = ffn_pre(x_ctx, norm_ffn_g[l], csc_f, csh_f, router_w[l], router_b[l])
            h_all = jnp.concatenate([h_lat.reshape(B * T, D), h_ctx.reshape(B * C, D)], axis=0)
            te_all = jnp.concatenate([te_lat.reshape(B * T, LANES), te_ctx.reshape(B * C, LANES)], axis=0)
            slot_tok, block_e, slot_flat = _route(te_all[:, :TOP_K])
            out = moe_blocks(block_e, slot_tok, h_all, e_w1, e_b1, e_w2, e_b2, l)
            x_lat = moe_combine(out, slot_flat[:B * T].reshape(B, T, TOP_K), tg_lat, x_lat, g_f)
            x_ctx = moe_combine(out, slot_flat[B * T:].reshape(B, C, TOP_K), tg_ctx, x_ctx, cg_f)
    return x_lat
```
